```python
import math
import jax
import jax.numpy as jnp
from jax import lax
import numpy as np

D_MODEL = 2048
BATCH = 4
SEQ = 4096
DEPTH = 4

CTX_LEN = 256
GRID_W = 64
EPS = 1e-6
N_MOD = 6
D_MIX = D_MODEL
HY_W = D_MIX // 2
HY_GROUPS = 8
HY_ORDER = 2
HY_PROJ = (HY_ORDER + 1) * HY_W
HY_CONV = 3
HY_POS_EMB = 33
HY_FILT_H = 64
HY_TARGET = 1e-2
HY_FAST = 0.3
HY_SLOW = 1.5
SSD_W = D_MIX - HY_W
SSD_HEADDIM = 64
SSD_HEADS = SSD_W // SSD_HEADDIM
SSD_GROUPS = 2
SSD_HPG = SSD_HEADS // SSD_GROUPS
SSD_STATE = 128
SSD_CONV = 3
SSD_CHUNK = 128
SSD_XBC = SSD_W + 2 * SSD_GROUPS * SSD_STATE
SSD_DT = 2 * SSD_HEADS
PROJ_W = HY_PROJ + SSD_XBC + SSD_DT + SSD_W
D_FF = 4 * D_MODEL

kernel_name = 'hyena_ssd_parallel_prefix_dit'


def rmsnorm(x, w):
    xf = x.astype(jnp.float32)
    y = xf * lax.rsqrt(jnp.mean(xf * xf, axis=-1, keepdims=True) + EPS)
    return (y * w).astype(x.dtype)


def centred_dwconv(u, w, bias, grid):
    b, L, C = u.shape
    if grid is not None:
        u = u.reshape(b, grid[0], grid[1], C)
    K = w.shape[0]
    pad = K // 2
    T = u.shape[-2]
    up = jnp.pad(u, [(0, 0)] * (u.ndim - 2) + [(pad, pad), (0, 0)])
    y = bias + w[0] * up[..., 0:T, :]
    for k in range(1, K):
        y = y + w[k] * up[..., k:k + T, :]
    return y.reshape(b, L, C)


def hyena_kernel_fft(L, fw1, fb1, freq, fw2, fb2, fw3):
    f32 = jnp.float32
    t = jnp.linspace(0.0, 1.0, L, dtype=f32)[:, None]
    bands = (HY_POS_EMB - 1) // 2
    w = 2.0 * math.pi * jnp.arange(L, dtype=f32)[:, None] / L
    f = jnp.linspace(1e-4, bands - 1, bands, dtype=f32)[None, :]
    feats = jnp.concatenate([t, jnp.cos(f * w), -jnp.sin(f * w)], axis=-1)
    hdn = jnp.sin(freq * (feats @ fw1 + fb1))
    hdn = jnp.sin(freq * (hdn @ fw2 + fb2))
    h = (hdn @ fw3).astype(f32).reshape(L, HY_ORDER, 2, HY_W)
    deltas = jnp.linspace(math.log(HY_TARGET) / HY_SLOW, math.log(HY_TARGET) / HY_FAST, HY_W, dtype=f32)
    h = h * jnp.exp(-t[:, :, None, None] * jnp.abs(deltas))
    k = jnp.concatenate([h[:, :, 0], jnp.zeros((1, HY_ORDER, HY_W), f32), h[:0:-1, :, 1]], axis=0)
    k = k / (jnp.sum(jnp.abs(k), axis=0, keepdims=True) + EPS)
    return jnp.fft.rfft(k, axis=0)


def fft_conv(z, kf):
    L = z.shape[1]
    zf = jnp.fft.rfft(z, n=2 * L, axis=1)
    return jnp.fft.irfft(zf * kf[None], n=2 * L, axis=1)[:, :L]


def segsum(a):
    T = a.shape[-1]
    cs = jnp.cumsum(a, axis=-1)
    diff = cs[..., :, None] - cs[..., None, :]
    mask = jnp.tril(jnp.ones((T, T), dtype=bool))
    return jnp.where(mask, diff, -jnp.inf)


def ssd_scan(xdt, da, bm, cm, init, want_y):
    b, l, g, r, p = xdt.shape
    nc = l // SSD_CHUNK
    xdt = xdt.reshape(b, nc, SSD_CHUNK, g, r, p)
    bc = bm.reshape(b, nc, SSD_CHUNK, g, -1)
    a = da.reshape(b, nc, SSD_CHUNK, g, r).transpose(0, 3, 4, 1, 2)
    a_cs = jnp.cumsum(a, axis=-1)
    decay_to_end = jnp.exp(a_cs[..., -1:] - a_cs)
    states = jnp.einsum('bcsgn,bgrcs,bcsgrp->bcgrpn', bc, decay_to_end, xdt)
    chunk_a = jnp.pad(a_cs[..., -1], ((0, 0), (0, 0), (0, 0), (1, 0)))
    states_all = jnp.concatenate([init[:, None], states], axis=1)
    carried = jnp.einsum('bgrzc,bcgrpn->bzgrpn', jnp.exp(segsum(chunk_a)), states_all)
    final = carried[:, -1]
    if not want_y:
        return final
    cc = cm.reshape(b, nc, SSD_CHUNK, g, -1)
    cb = jnp.einsum('bclgn,bcsgn->bgcls', cc, bc)
    y_diag = jnp.einsum('bgcls,bgrcls,bcsgrp->bclgrp', cb, jnp.exp(segsum(a)), xdt)
    y_off = jnp.einsum('bclgn,bcgrpn,bgrcl->bclgrp', cc, carried[:, :-1], jnp.exp(a_cs))
    y = (y_diag + y_off).reshape(b, l, g, r, p)
    return y, final


def ssd_branch(h, init_f, init_b, grid, lp, want_y):
    b, L, _ = h.shape
    f32 = jnp.float32
    w_in = lp['w_in']
    o0 = HY_PROJ
    o1 = o0 + SSD_XBC
    o2 = o1 + SSD_DT
    xbc = jax.nn.silu(centred_dwconv(h @ w_in[:, o0:o1], lp['ssd_conv_w'], lp['ssd_conv_b'], grid)).astype(f32)
    gn = SSD_GROUPS * SSD_STATE
    xs = xbc[..., :SSD_W].reshape(b, L, SSD_GROUPS, SSD_HPG, SSD_HEADDIM)
    bm = xbc[..., SSD_W:SSD_W + gn].reshape(b, L, SSD_GROUPS, SSD_STATE)
    cm = xbc[..., SSD_W + gn:].reshape(b, L, SSD_GROUPS, SSD_STATE)
    dt = jax.nn.softplus((h @ w_in[:, o1:o2]).astype(f32).reshape(b, L, 2, SSD_HEADS) + lp['dt_bias'])
    dt = dt.reshape(b, L, 2, SSD_GROUPS, SSD_HPG)
    a = (-jnp.exp(lp['a_log'].astype(f32))).reshape(2, SSD_GROUPS, SSD_HPG)

    def run(d, init, flip):
        dt_d = dt[:, :, d]
        seqs = (xs * dt_d[..., None], dt_d * a[d], bm, cm)
        if flip:
            seqs = tuple(s[:, ::-1] for s in seqs)
        return ssd_scan(seqs[0], seqs[1], seqs[2], seqs[3], init, want_y)

    if not want_y:
        return run(0, init_f, False), run(1, init_b, True)
    y_f, s_f = run(0, init_f, False)
    y_b, s_b = run(1, init_b, True)
    y = y_f + y_b[:, ::-1] + lp['ssd_d'].reshape(SSD_GROUPS, SSD_HPG)[..., None] * xs
    zg = jax.nn.silu((h @ w_in[:, o2:]).astype(f32))
    y = (y.reshape(b, L, SSD_W) * zg).reshape(b, L, SSD_GROUPS, SSD_W // SSD_GROUPS)
    y = rmsnorm(y, lp['ssd_norm_w'].reshape(SSD_GROUPS, -1)).reshape(b, L, SSD_W)
    return y.astype(h.dtype), s_f, s_b


def token_mixer(h, init_f, init_b, grid, lp):
    b, L, _ = h.shape
    u = centred_dwconv(h @ lp['w_in'][:, :HY_PROJ], lp['hy_conv_w'], lp['hy_conv_b'], grid).astype(jnp.float32)
    v, *gates = jnp.split(u, HY_ORDER + 1, axis=-1)
    kf = hyena_kernel_fft(L, lp['filt_w1'], lp['filt_b1'], lp['filt_freq'], lp['filt_w2'], lp['filt_b2'], lp['filt_w3'])
    z = v
    for o, gate in enumerate(gates):
        z = gate * (fft_conv(z, kf[:, o]) + z * lp['hy_bias'][o])
    y_hy = rmsnorm(z.reshape(b, L, HY_GROUPS, HY_W // HY_GROUPS), lp['hy_norm_w'].reshape(HY_GROUPS, -1))
    y_hy = y_hy.reshape(b, L, HY_W).astype(h.dtype)
    y_ssd, s_f, s_b = ssd_branch(h, init_f, init_b, grid, lp, True)
    out = jnp.concatenate([y_hy, y_ssd], axis=-1) @ lp['w_out']
    return out, s_f, s_b


def sqrelu_mlp(h, w1, w2):
    return jnp.square(jax.nn.relu(h @ w1)) @ w2


def setup_inputs(seed: int = 0) -> dict:
    key = jax.random.key(seed)
    ks = jax.random.split(key, 32)

    def nrm(k, shape, scale):
        return scale * jax.random.normal(k, shape, jnp.float32)

    D = D_MODEL
    dt0 = jnp.exp(jax.random.uniform(ks[20], (DEPTH, 2, SSD_HEADS), jnp.float32, math.log(1e-3), math.log(1e-1)))
    return {
        'x': nrm(ks[0], (BATCH, SEQ, D), 1.0),
        'c': nrm(ks[1], (BATCH, D), 1.0),
        'ctx': nrm(ks[2], (BATCH, CTX_LEN, D), 1.0),
        'c_ctx': nrm(ks[3], (D,), 1.0),
        'w_ada': nrm(ks[4], (DEPTH, D, N_MOD * D), 0.5 * D ** -0.5),
        'b_ada': nrm(ks[5], (DEPTH, N_MOD * D), 0.02),
        'norm1_w': 1.0 + nrm(ks[6], (DEPTH, D), 0.05),
        'w_in': nrm(ks[7], (DEPTH, D, PROJ_W), D ** -0.5),
        'hy_conv_w': nrm(ks[8], (DEPTH, HY_CONV, HY_PROJ), HY_CONV ** -0.5),
        'hy_conv_b': nrm(ks[9], (DEPTH, HY_PROJ), 0.02),
        'filt_w1': nrm(ks[10], (DEPTH, HY_POS_EMB, HY_FILT_H), HY_POS_EMB ** -0.5),
        'filt_b1': nrm(ks[11], (DEPTH, HY_FILT_H), 0.1),
        'filt_freq': 1.0 + nrm(ks[12], (DEPTH, HY_FILT_H), 0.1),
        'filt_w2': nrm(ks[13], (DEPTH, HY_FILT_H, HY_FILT_H), HY_FILT_H ** -0.5),
        'filt_b2': nrm(ks[14], (DEPTH, HY_FILT_H), 0.1),
        'filt_w3': nrm(ks[15], (DEPTH, HY_FILT_H, HY_ORDER * 2 * HY_W), HY_FILT_H ** -0.5),
        'hy_bias': nrm(ks[16], (DEPTH, HY_ORDER, HY_W), 0.1),
        'hy_norm_w': 1.0 + nrm(ks[17], (DEPTH, HY_W), 0.05),
        'ssd_conv_w': nrm(ks[18], (DEPTH, SSD_CONV, SSD_XBC), SSD_CONV ** -0.5),
        'ssd_conv_b': nrm(ks[19], (DEPTH, SSD_XBC), 0.02),
        'dt_bias': dt0 + jnp.log(-jnp.expm1(-dt0)),
        'a_log': jnp.log(jax.random.uniform(ks[21], (DEPTH, 2, SSD_HEADS), jnp.float32, 1.0, 16.0)),
        'ssd_d': 1.0 + nrm(ks[22], (DEPTH, SSD_HEADS), 0.05),
        'ssd_norm_w': 1.0 + nrm(ks[23], (DEPTH, SSD_W), 0.05),
        'w_out': nrm(ks[24], (DEPTH, D_MIX, D), D_MIX ** -0.5),
        'norm2_w': 1.0 + nrm(ks[25], (DEPTH, D), 0.05),
        'w_mlp1': nrm(ks[26], (DEPTH, D, D_FF), D ** -0.5),
        'w_mlp2': nrm(ks[27], (DEPTH, D_FF, D), D_FF ** -0.5),
        'final_norm_w': 1.0 + nrm(ks[28], (D,), 0.05),
    }


def reference(x, c, ctx, c_ctx, w_ada, b_ada, norm1_w, w_in, hy_conv_w, hy_conv_b, filt_w1, filt_b1,
              filt_freq, filt_w2, filt_b2, filt_w3, hy_bias, hy_norm_w, ssd_conv_w, ssd_conv_b, dt_bias,
              a_log, ssd_d, ssd_norm_w, w_out, norm2_w, w_mlp1, w_mlp2, final_norm_w):
    rows = x.shape[1] // GRID_W
    grid = (rows, GRID_W)
    zero_state = jnp.zeros((x.shape[0], SSD_GROUPS, SSD_HPG, SSD_HEADDIM, SSD_STATE), jnp.float32)
    h_ctx = ctx
    for i in range(DEPTH):
        lp = dict(w_in=w_in[i], hy_conv_w=hy_conv_w[i], hy_conv_b=hy_conv_b[i], filt_w1=filt_w1[i],
                  filt_b1=filt_b1[i], filt_freq=filt_freq[i], filt_w2=filt_w2[i], filt_b2=filt_b2[i],
                  filt_w3=filt_w3[i], hy_bias=hy_bias[i], hy_norm_w=hy_norm_w[i], ssd_conv_w=ssd_conv_w[i],
                  ssd_conv_b=ssd_conv_b[i], dt_bias=dt_bias[i], a_log=a_log[i], ssd_d=ssd_d[i],
                  ssd_norm_w=ssd_norm_w[i], w_out=w_out[i])
        mod = (jax.nn.silu(c) @ w_ada[i] + b_ada[i])[:, None, :]
        mod_c = jax.nn.silu(c_ctx) @ w_ada[i] + b_ada[i]
        sh1, sc1, g1, sh2, sc2, g2 = jnp.split(mod, N_MOD, axis=-1)
        csh1, csc1, cg1, csh2, csc2, cg2 = jnp.split(mod_c, N_MOD, axis=-1)
        hc = rmsnorm(h_ctx, norm1_w[i]) * (1.0 + csc1) + csh1
        if i < DEPTH - 1:
            out_c, s_f, s_b = token_mixer(hc, zero_state, zero_state, None, lp)
            h_ctx = h_ctx + cg1 * out_c
            hc2 = rmsnorm(h_ctx, norm2_w[i]) * (1.0 + csc2) + csh2
            h_ctx = h_ctx + cg2 * sqrelu_mlp(hc2, w_mlp1[i], w_mlp2[i])
        else:
            s_f, s_b = ssd_branch(hc, zero_state, zero_state, None, lp, False)
        hl = rmsnorm(x, norm1_w[i]) * (1.0 + sc1) + sh1
        out_l, _, _ = token_mixer(hl, s_f, s_b, grid, lp)
        x = x + g1 * out_l
        hl2 = rmsnorm(x, norm2_w[i]) * (1.0 + sc2) + sh2
        x = x + g2 * sqrelu_mlp(hl2, w_mlp1[i], w_mlp2[i])
    return rmsnorm(x, final_norm_w)
```

```python
import functools
import math

import jax
import jax.numpy as jnp
from jax import lax
from jax.experimental import pallas as pl
from jax.experimental.pallas import tpu as pltpu

EPS = 1e-6
N_MOD = 6
GRID_W = 64
HY_GROUPS = 8
HY_ORDER = 2
HY_POS_EMB = 33
HY_TARGET = 1e-2
HY_FAST = 0.3
HY_SLOW = 1.5
SSD_HEADDIM = 64
SSD_GROUPS = 2
SSD_STATE = 128
SSD_CHUNK = 128

LANE = 128
VMEM_LIMIT = 56 * 1024 * 1024


def _cparams(*sem):
    return pltpu.CompilerParams(dimension_semantics=sem, vmem_limit_bytes=VMEM_LIMIT)


def _ada_kernel(c_ref, w_ref, b_ref, o_ref):
    c = c_ref[...]
    a = (c * jax.nn.sigmoid(c)).astype(jnp.bfloat16)
    w = w_ref[0].astype(jnp.bfloat16)
    o_ref[0] = jnp.dot(a, w, preferred_element_type=jnp.float32) + b_ref[0]


def ada_mod(c_all, w_ada, b_ada, tn=1024):
    depth, d, n = w_ada.shape
    r = c_all.shape[0]
    return pl.pallas_call(
        _ada_kernel,
        out_shape=jax.ShapeDtypeStruct((depth, r, n), jnp.float32),
        grid=(depth, n // tn),
        in_specs=[
            pl.BlockSpec((r, d), lambda i, j: (0, 0)),
            pl.BlockSpec((1, d, tn), lambda i, j: (i, 0, j)),
            pl.BlockSpec((1, 1, tn), lambda i, j: (i, 0, j)),
        ],
        out_specs=pl.BlockSpec((1, r, tn), lambda i, j: (i, 0, j)),
        compiler_params=_cparams("parallel", "parallel"),
        name="ada_mod",
    )(c_all, w_ada, b_ada.reshape(depth, 1, n))


def _normmod(x, nw, sc, sh):
    ms = jnp.mean(x * x, axis=-1, keepdims=True)
    return (x * lax.rsqrt(ms + EPS) * nw) * (1.0 + sc) + sh


def _nm_matmul_kernel(x_ref, nw_ref, sc_ref, sh_ref, w_ref, o_ref, h_ref):
    @pl.when(pl.program_id(2) == 0)
    def _():
        h_ref[...] = _normmod(x_ref[0], nw_ref[...], sc_ref[0], sh_ref[0]).astype(jnp.bfloat16)

    o_ref[0] = jnp.dot(h_ref[...], w_ref[...], preferred_element_type=jnp.float32)


def normmod_matmul(x, nw, sc, sh, w, tm, tn):
    b, l, d = x.shape
    n = w.shape[1]
    return pl.pallas_call(
        _nm_matmul_kernel,
        out_shape=jax.ShapeDtypeStruct((b, l, n), jnp.float32),
        grid=(b, l // tm, n // tn),
        in_specs=[
            pl.BlockSpec((1, tm, d), lambda i, j, k: (i, j, 0)),
            pl.BlockSpec((1, d), lambda i, j, k: (0, 0)),
            pl.BlockSpec((1, 1, d), lambda i, j, k: (i, 0, 0)),
            pl.BlockSpec((1, 1, d), lambda i, j, k: (i, 0, 0)),
            pl.BlockSpec((d, tn), lambda i, j, k: (0, k)),
        ],
        out_specs=pl.BlockSpec((1, tm, tn), lambda i, j, k: (i, j, k)),
        scratch_shapes=[pltpu.VMEM((tm, d), jnp.bfloat16)],
        compiler_params=_cparams("parallel", "parallel", "arbitrary"),
        name="normmod_matmul",
    )(x, nw.reshape(1, d), sc.reshape(b, 1, d), sh.reshape(b, 1, d), w)


def _proj_res_kernel(y_ref, w_ref, x_ref, g_ref, o_ref):
    acc = jnp.dot(y_ref[0].astype(jnp.bfloat16), w_ref[...], preferred_element_type=jnp.float32)
    o_ref[0] = x_ref[0] + g_ref[0] * acc


def proj_residual(y, w, x, g, tm):
    b, l, k = y.shape
    n = w.shape[1]
    return pl.pallas_call(
        _proj_res_kernel,
        out_shape=jax.ShapeDtypeStruct((b, l, n), jnp.float32),
        grid=(b, l // tm),
        in_specs=[
            pl.BlockSpec((1, tm, k), lambda i, j: (i, j, 0)),
            pl.BlockSpec((k, n), lambda i, j: (0, 0)),
            pl.BlockSpec((1, tm, n), lambda i, j: (i, j, 0)),
            pl.BlockSpec((1, 1, n), lambda i, j: (i, 0, 0)),
        ],
        out_specs=pl.BlockSpec((1, tm, n), lambda i, j: (i, j, 0)),
        compiler_params=_cparams("parallel", "parallel"),
        name="proj_residual",
    )(y, w, x, g.reshape(b, 1, n))


def _mlp_kernel(x_ref, nw_ref, sc_ref, sh_ref, g_ref, w1_ref, w2_ref, fw_ref, o_ref, h_ref, *, final_norm):
    f = pl.program_id(2)

    @pl.when(f == 0)
    def _():
        h_ref[...] = _normmod(x_ref[0], nw_ref[...], sc_ref[0], sh_ref[0]).astype(jnp.bfloat16)

    a = jnp.dot(h_ref[...], w1_ref[...], preferred_element_type=jnp.float32)
    a = jnp.square(jnp.maximum(a, 0.0)).astype(jnp.bfloat16)
    part = jnp.dot(a, w2_ref[...], preferred_element_type=jnp.float32)

    @pl.when(f == 0)
    def _():
        o_ref[0] = part

    @pl.when(f > 0)
    def _():
        o_ref[0] += part

    @pl.when(f == pl.num_programs(2) - 1)
    def _():
        y = x_ref[0] + g_ref[0] * o_ref[0]
        if final_norm:
            ms = jnp.mean(y * y, axis=-1, keepdims=True)
            y = y * lax.rsqrt(ms + EPS) * fw_ref[...]
        o_ref[0] = y


def mlp_residual(x, nw, sc, sh, g, w1, w2, fw, tm, tf, final_norm):
    b, l, d = x.shape
    dff = w1.shape[1]
    vec = pl.BlockSpec((1, 1, d), lambda i, j, k: (i, 0, 0))
    row = pl.BlockSpec((1, d), lambda i, j, k: (0, 0))
    return pl.pallas_call(
        functools.partial(_mlp_kernel, final_norm=final_norm),
        out_shape=jax.ShapeDtypeStruct((b, l, d), jnp.float32),
        grid=(b, l // tm, dff // tf),
        in_specs=[
            pl.BlockSpec((1, tm, d), lambda i, j, k: (i, j, 0)),
            row, vec, vec, vec,
            pl.BlockSpec((d, tf), lambda i, j, k: (0, k)),
            pl.BlockSpec((tf, d), lambda i, j, k: (k, 0)),
            row,
        ],
        out_specs=pl.BlockSpec((1, tm, d), lambda i, j, k: (i, j, 0)),
        scratch_shapes=[pltpu.VMEM((tm, d), jnp.bfloat16)],
        compiler_params=_cparams("parallel", "parallel", "arbitrary"),
        name="mlp_residual",
    )(x, nw.reshape(1, d), sc.reshape(b, 1, d), sh.reshape(b, 1, d), g.reshape(b, 1, d), w1, w2,
      fw.reshape(1, d))


def _rmsnorm(x, w):
    return x * lax.rsqrt(jnp.mean(x * x, axis=-1, keepdims=True) + EPS) * w


def _dwconv(u, w, bias, grid):
    b, L, C = u.shape
    if grid is not None:
        u = u.reshape(b, grid[0], grid[1], C)
    K = w.shape[0]
    pad = K // 2
    T = u.shape[-2]
    up = jnp.pad(u, [(0, 0)] * (u.ndim - 2) + [(pad, pad), (0, 0)])
    y = bias + w[0] * up[..., 0:T, :]
    for k in range(1, K):
        y = y + w[k] * up[..., k:k + T, :]
    return y.reshape(b, L, C)


def _hyena_filter_fft(L, fw1, fb1, freq, fw2, fb2, fw3):
    f32 = jnp.float32
    hy_w = fw3.shape[1] // (2 * HY_ORDER)
    t = jnp.linspace(0.0, 1.0, L, dtype=f32)[:, None]
    bands = (HY_POS_EMB - 1) // 2
    w = 2.0 * math.pi * jnp.arange(L, dtype=f32)[:, None] / L
    f = jnp.linspace(1e-4, bands - 1, bands, dtype=f32)[None, :]
    feats = jnp.concatenate([t, jnp.cos(f * w), -jnp.sin(f * w)], axis=-1)
    hdn = jnp.sin(freq * (feats @ fw1 + fb1))
    hdn = jnp.sin(freq * (hdn @ fw2 + fb2))
    h = (hdn @ fw3).astype(f32).reshape(L, HY_ORDER, 2, hy_w)
    deltas = jnp.linspace(math.log(HY_TARGET) / HY_SLOW, math.log(HY_TARGET) / HY_FAST, hy_w, dtype=f32)
    h = h * jnp.exp(-t[:, :, None, None] * jnp.abs(deltas))
    k = jnp.concatenate([h[:, :, 0], jnp.zeros((1, HY_ORDER, hy_w), f32), h[:0:-1, :, 1]], axis=0)
    k = k / (jnp.sum(jnp.abs(k), axis=0, keepdims=True) + EPS)
    return jnp.fft.rfft(k, axis=0)


def _fft_conv(z, kf):
    L = z.shape[1]
    zf = jnp.fft.rfft(z, n=2 * L, axis=1)
    return jnp.fft.irfft(zf * kf[None], n=2 * L, axis=1)[:, :L]


def _segsum(a):
    T = a.shape[-1]
    cs = jnp.cumsum(a, axis=-1)
    diff = cs[..., :, None] - cs[..., None, :]
    mask = jnp.tril(jnp.ones((T, T), dtype=bool))
    return jnp.where(mask, diff, -jnp.inf)


def _ssd_scan(xdt, da, bm, cm, init, want_y):
    b, l, g, r, p = xdt.shape
    nc = l // SSD_CHUNK
    xdt = xdt.reshape(b, nc, SSD_CHUNK, g, r, p)
    bc = bm.reshape(b, nc, SSD_CHUNK, g, -1)
    a = da.reshape(b, nc, SSD_CHUNK, g, r).transpose(0, 3, 4, 1, 2)
    a_cs = jnp.cumsum(a, axis=-1)
    decay_to_end = jnp.exp(a_cs[..., -1:] - a_cs)
    states = jnp.einsum('bcsgn,bgrcs,bcsgrp->bcgrpn', bc, decay_to_end, xdt)
    chunk_a = jnp.pad(a_cs[..., -1], ((0, 0), (0, 0), (0, 0), (1, 0)))
    states_all = jnp.concatenate([init[:, None], states], axis=1)
    carried = jnp.einsum('bgrzc,bcgrpn->bzgrpn', jnp.exp(_segsum(chunk_a)), states_all)
    final = carried[:, -1]
    if not want_y:
        return final
    cc = cm.reshape(b, nc, SSD_CHUNK, g, -1)
    cb = jnp.einsum('bclgn,bcsgn->bgcls', cc, bc)
    y_diag = jnp.einsum('bgcls,bgrcls,bcsgrp->bclgrp', cb, jnp.exp(_segsum(a)), xdt)
    y_off = jnp.einsum('bclgn,bcgrpn,bgrcl->bclgrp', cc, carried[:, :-1], jnp.exp(a_cs))
    y = (y_diag + y_off).reshape(b, l, g, r, p)
    return y, final


def _ssd_branch(xbc_raw, dt_raw, z_raw, init_f, init_b, grid, lp, want_y):
    b, L, _ = xbc_raw.shape
    f32 = jnp.float32
    ssd_w = lp['ssd_norm_w'].shape[0]
    heads = ssd_w // SSD_HEADDIM
    hpg = heads // SSD_GROUPS
    xbc = jax.nn.silu(_dwconv(xbc_raw, lp['ssd_conv_w'], lp['ssd_conv_b'], grid))
    gn = SSD_GROUPS * SSD_STATE
    xs = xbc[..., :ssd_w].reshape(b, L, SSD_GROUPS, hpg, SSD_HEADDIM)
    bm = xbc[..., ssd_w:ssd_w + gn].reshape(b, L, SSD_GROUPS, SSD_STATE)
    cm = xbc[..., ssd_w + gn:].reshape(b, L, SSD_GROUPS, SSD_STATE)
    dt = jax.nn.softplus(dt_raw.reshape(b, L, 2, heads) + lp['dt_bias'])
    dt = dt.reshape(b, L, 2, SSD_GROUPS, hpg)
    a = (-jnp.exp(lp['a_log'].astype(f32))).reshape(2, SSD_GROUPS, hpg)

    def run(d, init, flip):
        dt_d = dt[:, :, d]
        seqs = (xs * dt_d[..., None], dt_d * a[d], bm, cm)
        if flip:
            seqs = tuple(s[:, ::-1] for s in seqs)
        return _ssd_scan(seqs[0], seqs[1], seqs[2], seqs[3], init, want_y)

    if not want_y:
        return run(0, init_f, False), run(1, init_b, True)
    y_f, s_f = run(0, init_f, False)
    y_b, s_b = run(1, init_b, True)
    y = y_f + y_b[:, ::-1] + lp['ssd_d'].reshape(SSD_GROUPS, hpg)[..., None] * xs
    zg = jax.nn.silu(z_raw)
    y = (y.reshape(b, L, ssd_w) * zg).reshape(b, L, SSD_GROUPS, ssd_w // SSD_GROUPS)
    y = _rmsnorm(y, lp['ssd_norm_w'].reshape(SSD_GROUPS, -1)).reshape(b, L, ssd_w)
    return y, s_f, s_b


def _hyena_branch(u_raw, grid, lp):
    b, L, _ = u_raw.shape
    hy_w = lp['hy_norm_w'].shape[0]
    u = _dwconv(u_raw, lp['hy_conv_w'], lp['hy_conv_b'], grid)
    v, *gates = jnp.split(u, HY_ORDER + 1, axis=-1)
    kf = _hyena_filter_fft(L, lp['filt_w1'], lp['filt_b1'], lp['filt_freq'], lp['filt_w2'], lp['filt_b2'],
                           lp['filt_w3'])
    z = v
    for o, gate in enumerate(gates):
        z = gate * (_fft_conv(z, kf[:, o]) + z * lp['hy_bias'][o])
    y = _rmsnorm(z.reshape(b, L, HY_GROUPS, hy_w // HY_GROUPS), lp['hy_norm_w'].reshape(HY_GROUPS, -1))
    return y.reshape(b, L, hy_w)


def _pick(l, pref):
    return pref if l % pref == 0 else l


def kernel(x, c, ctx, c_ctx, w_ada, b_ada, norm1_w, w_in, hy_conv_w, hy_conv_b, filt_w1, filt_b1, filt_freq,
           filt_w2, filt_b2, filt_w3, hy_bias, hy_norm_w, ssd_conv_w, ssd_conv_b, dt_bias, a_log, ssd_d,
           ssd_norm_w, w_out, norm2_w, w_mlp1, w_mlp2, final_norm_w):
    depth = w_in.shape[0]
    bsz, seq, d = x.shape
    hy_w = hy_norm_w.shape[1]
    hy_proj = hy_conv_w.shape[2]
    ssd_w = ssd_norm_w.shape[1]
    ssd_xbc = ssd_conv_w.shape[2]
    heads = ssd_w // SSD_HEADDIM
    hpg = heads // SSD_GROUPS
    ssd_dt = 2 * heads
    o0, o1 = hy_proj, hy_proj + ssd_xbc
    o2 = o1 + ssd_dt
    dt_pad = (-ssd_dt) % LANE
    bf16 = jnp.bfloat16

    grid = (seq // GRID_W, GRID_W)
    zero_state = jnp.zeros((bsz, SSD_GROUPS, hpg, SSD_HEADDIM, SSD_STATE), jnp.float32)

    rows = 8
    c_all = jnp.zeros((rows, d), jnp.float32).at[:bsz].set(c).at[bsz].set(c_ctx)
    mod_all = ada_mod(c_all, w_ada, b_ada)

    h_ctx = ctx
    tm_l = _pick(seq, 512)
    tm_c = _pick(ctx.shape[1], 256)
    for i in range(depth):
        lp = dict(hy_conv_w=hy_conv_w[i], hy_conv_b=hy_conv_b[i], filt_w1=filt_w1[i], filt_b1=filt_b1[i],
                  filt_freq=filt_freq[i], filt_w2=filt_w2[i], filt_b2=filt_b2[i], filt_w3=filt_w3[i],
                  hy_bias=hy_bias[i], hy_norm_w=hy_norm_w[i], ssd_conv_w=ssd_conv_w[i],
                  ssd_conv_b=ssd_conv_b[i], dt_bias=dt_bias[i], a_log=a_log[i], ssd_d=ssd_d[i],
                  ssd_norm_w=ssd_norm_w[i])
        wi = w_in[i]
        w_cat = jnp.concatenate(
            [wi[:, :o1], wi[:, o2:], wi[:, o1:o2], jnp.zeros((d, dt_pad), wi.dtype)], axis=1).astype(bf16)
        n_cat = w_cat.shape[1]
        tn = n_cat // 3 if (n_cat % 3 == 0 and (n_cat // 3) % LANE == 0) else n_cat
        w_o = w_out[i].astype(bf16)
        w1 = w_mlp1[i].astype(bf16)
        w2 = w_mlp2[i].astype(bf16)

        mod = mod_all[i, :bsz].reshape(bsz, N_MOD, d)
        mod_c = jnp.broadcast_to(mod_all[i, bsz].reshape(1, N_MOD, d), (bsz, N_MOD, d))
        z0, z1 = o1, o1 + ssd_w

        def split(p):
            return p[..., :o0], p[..., o0:o1], p[..., z0:z1], p[..., z1:z1 + ssd_dt]

        pc = normmod_matmul(h_ctx, norm1_w[i], mod_c[:, 1], mod_c[:, 0], w_cat, tm_c, tn)
        u_c, xbc_c, zz_c, dt_c = split(pc)
        if i < depth - 1:
            y_hy = _hyena_branch(u_c, None, lp)
            y_ssd, s_f, s_b = _ssd_branch(xbc_c, dt_c, zz_c, zero_state, zero_state, None, lp, True)
            ycat = jnp.concatenate([y_hy, y_ssd], axis=-1)
            h_ctx = proj_residual(ycat, w_o, h_ctx, mod_c[:, 2], tm_c)
            h_ctx = mlp_residual(h_ctx, norm2_w[i], mod_c[:, 4], mod_c[:, 3], mod_c[:, 5], w1, w2,
                                 final_norm_w, tm_c, 512, False)
        else:
            s_f, s_b = _ssd_branch(xbc_c, dt_c, zz_c, zero_state, zero_state, None, lp, False)
        pl_ = normmod_matmul(x, norm1_w[i], mod[:, 1], mod[:, 0], w_cat, tm_l, tn)
        u_l, xbc_l, zz_l, dt_l = split(pl_)
        y_hy = _hyena_branch(u_l, grid, lp)
        y_ssd, _, _ = _ssd_branch(xbc_l, dt_l, zz_l, s_f, s_b, grid, lp, True)
        ycat = jnp.concatenate([y_hy, y_ssd], axis=-1)
        x = proj_residual(ycat, w_o, x, mod[:, 2], tm_l)
        x = mlp_residual(x, norm2_w[i], mod[:, 4], mod[:, 3], mod[:, 5], w1, w2, final_norm_w, tm_l, 512,
                         i == depth - 1)
    return x
```

```python
import functools
import math

import jax
import jax.numpy as jnp
from jax import lax
from jax.experimental import pallas as pl
from jax.experimental.pallas import tpu as pltpu

EPS = 1e-6
N_MOD = 6
GRID_W = 64
HY_GROUPS = 8
HY_ORDER = 2
HY_POS_EMB = 33
HY_TARGET = 1e-2
HY_FAST = 0.3
HY_SLOW = 1.5
SSD_HEADDIM = 64
SSD_GROUPS = 2
SSD_STATE = 128
SSD_CHUNK = 128

LANE = 128
VMEM_LIMIT = 56 * 1024 * 1024


def _cparams(*sem):
    return pltpu.CompilerParams(dimension_semantics=sem, vmem_limit_bytes=VMEM_LIMIT)


def _ada_kernel(c_ref, w_ref, b_ref, o_ref):
    c = c_ref[...]
    a = (c * jax.nn.sigmoid(c)).astype(jnp.bfloat16)
    w = w_ref[0].astype(jnp.bfloat16)
    o_ref[0] = jnp.dot(a, w, preferred_element_type=jnp.float32) + b_ref[0]


def ada_mod(c_all, w_ada, b_ada, tn=1024):
    depth, d, n = w_ada.shape
    r = c_all.shape[0]
    return pl.pallas_call(
        _ada_kernel,
        out_shape=jax.ShapeDtypeStruct((depth, r, n), jnp.float32),
        grid=(depth, n // tn),
        in_specs=[
            pl.BlockSpec((r, d), lambda i, j: (0, 0)),
            pl.BlockSpec((1, d, tn), lambda i, j: (i, 0, j)),
            pl.BlockSpec((1, 1, tn), lambda i, j: (i, 0, j)),
        ],
        out_specs=pl.BlockSpec((1, r, tn), lambda i, j: (i, 0, j)),
        compiler_params=_cparams("parallel", "parallel"),
        name="ada_mod",
    )(c_all, w_ada, b_ada.reshape(depth, 1, n))


def _normmod(x, nw, sc, sh):
    ms = jnp.mean(x * x, axis=-1, keepdims=True)
    return (x * lax.rsqrt(ms + EPS) * nw) * (1.0 + sc) + sh


def _nm_matmul_kernel(x_ref, nw_ref, sc_ref, sh_ref, w_ref, o_ref, h_ref):
    @pl.when(pl.program_id(2) == 0)
    def _():
        h_ref[...] = _normmod(x_ref[0], nw_ref[...], sc_ref[0], sh_ref[0]).astype(jnp.bfloat16)

    o_ref[0] = jnp.dot(h_ref[...], w_ref[...], preferred_element_type=jnp.float32)


def normmod_matmul(x, nw, sc, sh, w, tm, tn):
    b, l, d = x.shape
    n = w.shape[1]
    return pl.pallas_call(
        _nm_matmul_kernel,
        out_shape=jax.ShapeDtypeStruct((b, l, n), jnp.float32),
        grid=(b, l // tm, n // tn),
        in_specs=[
            pl.BlockSpec((1, tm, d), lambda i, j, k: (i, j, 0)),
            pl.BlockSpec((1, d), lambda i, j, k: (0, 0)),
            pl.BlockSpec((1, 1, d), lambda i, j, k: (i, 0, 0)),
            pl.BlockSpec((1, 1, d), lambda i, j, k: (i, 0, 0)),
            pl.BlockSpec((d, tn), lambda i, j, k: (0, k)),
        ],
        out_specs=pl.BlockSpec((1, tm, tn), lambda i, j, k: (i, j, k)),
        scratch_shapes=[pltpu.VMEM((tm, d), jnp.bfloat16)],
        compiler_params=_cparams("parallel", "parallel", "arbitrary"),
        name="normmod_matmul",
    )(x, nw.reshape(1, d), sc.reshape(b, 1, d), sh.reshape(b, 1, d), w)


def _proj_res_kernel(y_ref, w_ref, x_ref, g_ref, o_ref):
    acc = jnp.dot(y_ref[0].astype(jnp.bfloat16), w_ref[...], preferred_element_type=jnp.float32)
    o_ref[0] = x_ref[0] + g_ref[0] * acc


def proj_residual(y, w, x, g, tm):
    b, l, k = y.shape
    n = w.shape[1]
    return pl.pallas_call(
        _proj_res_kernel,
        out_shape=jax.ShapeDtypeStruct((b, l, n), jnp.float32),
        grid=(b, l // tm),
        in_specs=[
            pl.BlockSpec((1, tm, k), lambda i, j: (i, j, 0)),
            pl.BlockSpec((k, n), lambda i, j: (0, 0)),
            pl.BlockSpec((1, tm, n), lambda i, j: (i, j, 0)),
            pl.BlockSpec((1, 1, n), lambda i, j: (i, 0, 0)),
        ],
        out_specs=pl.BlockSpec((1, tm, n), lambda i, j: (i, j, 0)),
        compiler_params=_cparams("parallel", "parallel"),
        name="proj_residual",
    )(y, w, x, g.reshape(b, 1, n))


def _mlp_kernel(x_ref, nw_ref, sc_ref, sh_ref, g_ref, w1_ref, w2_ref, fw_ref, o_ref, h_ref, *, final_norm):
    f = pl.program_id(2)

    @pl.when(f == 0)
    def _():
        h_ref[...] = _normmod(x_ref[0], nw_ref[...], sc_ref[0], sh_ref[0]).astype(jnp.bfloat16)

    a = jnp.dot(h_ref[...], w1_ref[...], preferred_element_type=jnp.float32)
    a = jnp.square(jnp.maximum(a, 0.0)).astype(jnp.bfloat16)
    part = jnp.dot(a, w2_ref[...], preferred_element_type=jnp.float32)

    @pl.when(f == 0)
    def _():
        o_ref[0] = part

    @pl.when(f > 0)
    def _():
        o_ref[0] += part

    @pl.when(f == pl.num_programs(2) - 1)
    def _():
        y = x_ref[0] + g_ref[0] * o_ref[0]
        if final_norm:
            ms = jnp.mean(y * y, axis=-1, keepdims=True)
            y = y * lax.rsqrt(ms + EPS) * fw_ref[...]
        o_ref[0] = y


def mlp_residual(x, nw, sc, sh, g, w1, w2, fw, tm, tf, final_norm):
    b, l, d = x.shape
    dff = w1.shape[1]
    vec = pl.BlockSpec((1, 1, d), lambda i, j, k: (i, 0, 0))
    row = pl.BlockSpec((1, d), lambda i, j, k: (0, 0))
    return pl.pallas_call(
        functools.partial(_mlp_kernel, final_norm=final_norm),
        out_shape=jax.ShapeDtypeStruct((b, l, d), jnp.float32),
        grid=(b, l // tm, dff // tf),
        in_specs=[
            pl.BlockSpec((1, tm, d), lambda i, j, k: (i, j, 0)),
            row, vec, vec, vec,
            pl.BlockSpec((d, tf), lambda i, j, k: (0, k)),
            pl.BlockSpec((tf, d), lambda i, j, k: (k, 0)),
            row,
        ],
        out_specs=pl.BlockSpec((1, tm, d), lambda i, j, k: (i, j, 0)),
        scratch_shapes=[pltpu.VMEM((tm, d), jnp.bfloat16)],
        compiler_params=_cparams("parallel", "parallel", "arbitrary"),
        name="mlp_residual",
    )(x, nw.reshape(1, d), sc.reshape(b, 1, d), sh.reshape(b, 1, d), g.reshape(b, 1, d), w1, w2,
      fw.reshape(1, d))


def _split3(v):
    f32, bf16 = jnp.float32, jnp.bfloat16
    hi = v.astype(bf16)
    r1 = v - hi.astype(f32)
    mid = r1.astype(bf16)
    lo = (r1 - mid.astype(f32)).astype(bf16)
    return jnp.concatenate([hi, mid, lo], axis=1)


def _lane_repeat(v, rep):
    h = v.shape[1]
    row = lax.broadcasted_iota(jnp.int32, (3 * h, h * rep), 0) % h
    col = lax.broadcasted_iota(jnp.int32, (3 * h, h * rep), 1) // rep
    e = (row == col).astype(jnp.bfloat16)
    return jnp.dot(_split3(v), e, preferred_element_type=jnp.float32)


def _silu(v):
    return v * jax.nn.sigmoid(v)


def _softplus(v):
    return jnp.maximum(v, 0.0) + jnp.log1p(jnp.exp(-jnp.abs(v)))


def _token_conv3(u, w, bias, period):
    t, c = u.shape
    pos = lax.broadcasted_iota(jnp.int32, (t, c), 0) % period
    up = jnp.where(pos == 0, 0.0, pltpu.roll(u, 1, 0))
    dn = jnp.where(pos == period - 1, 0.0, pltpu.roll(u, t - 1, 0))
    return bias + w[0:1] * up + w[1:2] * u + w[2:3] * dn


def _ssd_chunk(xs, bm, cm, dtr, s_ref, a_row, dtb_row, reverse, heads, hpg):
    f32, bf16 = jnp.float32, jnp.bfloat16
    q = xs.shape[0]
    p = SSD_HEADDIM
    n = SSD_STATE
    gw = hpg * p
    dt = _softplus(dtr + dtb_row)
    a = dt * a_row
    ri = lax.broadcasted_iota(jnp.int32, (q, q), 0)
    ci = lax.broadcasted_iota(jnp.int32, (q, q), 1)
    keep = (ci >= ri) if reverse else (ci <= ri)
    a3 = jnp.dot(keep.astype(bf16), _split3(a), preferred_element_type=f32)
    cs = a3[:, :heads] + a3[:, heads:2 * heads] + a3[:, 2 * heads:]
    dt_rep = _lane_repeat(dt, p)
    cs_rep = _lane_repeat(cs, p)
    cs_wide = _lane_repeat(cs, q)
    end = 0 if reverse else q - 1
    cs_end = cs_rep[end:end + 1]
    xdt = xs * dt_rep
    xw = (xdt * jnp.exp(cs_end - cs_rep)).astype(bf16)
    ecs = jnp.exp(cs_rep)
    chunk_decay = jnp.exp(cs_end)
    xdt_b = xdt.astype(bf16)
    lane = lax.broadcasted_iota(jnp.int32, (q, 2 * p), 1)
    ys = []
    for g in range(SSD_GROUPS):
        bg = bm[:, g * n:(g + 1) * n].astype(bf16)
        cg = cm[:, g * n:(g + 1) * n].astype(bf16)
        cb = lax.dot_general(cg, bg, (((1,), (1,)), ((), ())), preferred_element_type=f32)
        s_old = s_ref[0, g]
        y_off = jnp.dot(cg, s_old.astype(bf16), preferred_element_type=f32) * ecs[:, g * gw:(g + 1) * gw]
        for pr in range(hpg // 2):
            xpair = xdt_b[:, g * gw + pr * 2 * p:g * gw + (pr + 1) * 2 * p]
            acc = None
            for k in range(2):
                h = g * hpg + pr * 2 + k
                csr = cs_wide[:, h * q:(h + 1) * q]
                seg = csr - csr.T
                gmat = (cb * jnp.where(keep, jnp.exp(seg), 0.0)).astype(bf16)
                xh = jnp.where((lane // p) == k, xpair, jnp.zeros_like(xpair))
                part = jnp.dot(gmat, xh, preferred_element_type=f32)
                acc = part if acc is None else acc + part
            lo = pr * 2 * p
            ys.append(acc + y_off[:, lo:lo + 2 * p])
        upd = jnp.dot(bg.T, xw[:, g * gw:(g + 1) * gw], preferred_element_type=f32)
        s_ref[0, g] = s_old * chunk_decay[:, g * gw:(g + 1) * gw] + upd
    return jnp.concatenate(ys, axis=1)


def _ssd_kernel(z_ref, xs_ref, bc_ref, dt_ref, cwx_ref, cbx_ref, cwb_ref, cbb_ref, dtb_ref, alog_ref, drep_ref,
                nw_ref, initf_ref, initb_ref, y_ref, sf_ref, sb_ref, yb_ref, *, nb, tb, period, heads, hpg):
    j = pl.program_id(1)
    q = SSD_CHUNK
    n = SSD_STATE
    gn = SSD_GROUPS * n
    nchunk = tb // q
    ssd_w = heads * SSD_HEADDIM

    @pl.when(j == 0)
    def _():
        sb_ref[...] = initb_ref[...]

    @pl.when(j == nb)
    def _():
        sf_ref[...] = initf_ref[...]

    xs_all = _silu(_token_conv3(xs_ref[0], cwx_ref[...], cbx_ref[...], period))
    bc_all = _silu(_token_conv3(bc_ref[0], cwb_ref[...], cbb_ref[...], period))
    a_all = -jnp.exp(alog_ref[...])

    def run(reverse):
        d = 1 if reverse else 0
        blk = (nb - 1 - j) if reverse else (j - nb)
        order = range(nchunk - 1, -1, -1) if reverse else range(nchunk)
        for ci in order:
            sl = slice(ci * q, (ci + 1) * q)
            y = _ssd_chunk(xs_all[sl], bc_all[sl, :gn], bc_all[sl, gn:], dt_ref[0, sl, d * heads:(d + 1) * heads],
                           sb_ref if reverse else sf_ref, a_all[d:d + 1], dtb_ref[d:d + 1], reverse, heads, hpg)
            row0 = pl.multiple_of(blk * tb + ci * q, q)
            if reverse:
                yb_ref[pl.ds(row0, q), :] = y
            else:
                y = y + yb_ref[pl.ds(row0, q), :] + drep_ref[...] * xs_all[sl]
                y = y * _silu(z_ref[0, sl, :])
                gw = ssd_w // SSD_GROUPS
                outs = []
                for g in range(SSD_GROUPS):
                    yg = y[:, g * gw:(g + 1) * gw]
                    ms = jnp.mean(yg * yg, axis=-1, keepdims=True)
                    outs.append(yg * lax.rsqrt(ms + EPS) * nw_ref[:, g * gw:(g + 1) * gw])
                y_ref[0, sl, :] = jnp.concatenate(outs, axis=1)

    @pl.when(j < nb)
    def _():
        run(True)

    @pl.when(j >= nb)
    def _():
        run(False)


def ssd_mixer(p, cols, conv_w_x, conv_b_x, conv_w_bc, conv_b_bc, dt_bias, a_log, ssd_d, norm_w, init_f, init_b,
              period, tb):
    b, l, _ = p.shape
    heads = dt_bias.shape[1]
    hpg = heads // SSD_GROUPS
    ssd_w = heads * SSD_HEADDIM
    gn = SSD_GROUPS * SSD_STATE
    nb = l // tb
    oz, ox, obc, odt = cols
    f32 = jnp.float32

    def tok(width, off):
        blk_idx = off // width
        return pl.BlockSpec((1, tb, width),
                            lambda i, j: (i, jnp.where(j < nb, nb - 1 - j, j - nb), blk_idx))

    def tok_fwd(width, off):
        blk_idx = off // width
        return pl.BlockSpec((1, tb, width), lambda i, j: (i, jnp.where(j < nb, 0, j - nb), blk_idx))

    def whole(shape):
        return pl.BlockSpec(shape, lambda i, j: (0,) * len(shape))

    st_shape = (b, SSD_GROUPS, SSD_STATE, hpg * SSD_HEADDIM)
    st_spec = pl.BlockSpec((1,) + st_shape[1:], lambda i, j: (i, 0, 0, 0))
    drep = jnp.repeat(ssd_d, SSD_HEADDIM).reshape(1, ssd_w)
    kern = functools.partial(_ssd_kernel, nb=nb, tb=tb, period=period, heads=heads, hpg=hpg)
    return pl.pallas_call(
        kern,
        out_shape=(jax.ShapeDtypeStruct((b, l, ssd_w), f32), jax.ShapeDtypeStruct(st_shape, f32),
                   jax.ShapeDtypeStruct(st_shape, f32)),
        grid=(b, 2 * nb),
        in_specs=[
            tok_fwd(ssd_w, oz), tok(ssd_w, ox), tok(2 * gn, obc), tok(LANE, odt),
            whole((3, ssd_w)), whole((1, ssd_w)), whole((3, 2 * gn)), whole((1, 2 * gn)),
            whole((2, heads)), whole((2, heads)), whole((1, ssd_w)), whole((1, ssd_w)),
            st_spec, st_spec,
        ],
        out_specs=(pl.BlockSpec((1, tb, ssd_w), lambda i, j: (i, jnp.where(j < nb, 0, j - nb), 0)),
                   st_spec, st_spec),
        scratch_shapes=[pltpu.VMEM((l, ssd_w), f32)],
        compiler_params=_cparams("parallel", "arbitrary"),
        name="ssd_mixer",
    )(p, p, p, p, conv_w_x, conv_b_x.reshape(1, -1), conv_w_bc, conv_b_bc.reshape(1, -1), dt_bias, a_log, drep,
      norm_w.reshape(1, ssd_w), init_f, init_b)


def _rmsnorm(x, w):
    return x * lax.rsqrt(jnp.mean(x * x, axis=-1, keepdims=True) + EPS) * w


def _dwconv(u, w, bias, grid):
    b, L, C = u.shape
    if grid is not None:
        u = u.reshape(b, grid[0], grid[1], C)
    K = w.shape[0]
    pad = K // 2
    T = u.shape[-2]
    up = jnp.pad(u, [(0, 0)] * (u.ndim - 2) + [(pad, pad), (0, 0)])
    y = bias + w[0] * up[..., 0:T, :]
    for k in range(1, K):
        y = y + w[k] * up[..., k:k + T, :]
    return y.reshape(b, L, C)


def _hyena_filter_fft(L, fw1, fb1, freq, fw2, fb2, fw3):
    f32 = jnp.float32
    hy_w = fw3.shape[1] // (2 * HY_ORDER)
    t = jnp.linspace(0.0, 1.0, L, dtype=f32)[:, None]
    bands = (HY_POS_EMB - 1) // 2
    w = 2.0 * math.pi * jnp.arange(L, dtype=f32)[:, None] / L
    f = jnp.linspace(1e-4, bands - 1, bands, dtype=f32)[None, :]
    feats = jnp.concatenate([t, jnp.cos(f * w), -jnp.sin(f * w)], axis=-1)
    hdn = jnp.sin(freq * (feats @ fw1 + fb1))
    hdn = jnp.sin(freq * (hdn @ fw2 + fb2))
    h = (hdn @ fw3).astype(f32).reshape(L, HY_ORDER, 2, hy_w)
    deltas = jnp.linspace(math.log(HY_TARGET) / HY_SLOW, math.log(HY_TARGET) / HY_FAST, hy_w, dtype=f32)
    h = h * jnp.exp(-t[:, :, None, None] * jnp.abs(deltas))
    k = jnp.concatenate([h[:, :, 0], jnp.zeros((1, HY_ORDER, hy_w), f32), h[:0:-1, :, 1]], axis=0)
    k = k / (jnp.sum(jnp.abs(k), axis=0, keepdims=True) + EPS)
    return jnp.fft.rfft(k, axis=0)


def _fft_conv(z, kf):
    L = z.shape[1]
    zf = jnp.fft.rfft(z, n=2 * L, axis=1)
    return jnp.fft.irfft(zf * kf[None], n=2 * L, axis=1)[:, :L]


def _hyena_branch(u_raw, grid, lp):
    b, L, _ = u_raw.shape
    hy_w = lp['hy_norm_w'].shape[0]
    u = _dwconv(u_raw, lp['hy_conv_w'], lp['hy_conv_b'], grid)
    v, *gates = jnp.split(u, HY_ORDER + 1, axis=-1)
    kf = _hyena_filter_fft(L, lp['filt_w1'], lp['filt_b1'], lp['filt_freq'], lp['filt_w2'], lp['filt_b2'],
                           lp['filt_w3'])
    z = v
    for o, gate in enumerate(gates):
        z = gate * (_fft_conv(z, kf[:, o]) + z * lp['hy_bias'][o])
    y = _rmsnorm(z.reshape(b, L, HY_GROUPS, hy_w // HY_GROUPS), lp['hy_norm_w'].reshape(HY_GROUPS, -1))
    return y.reshape(b, L, hy_w)


def _pick(l, pref):
    return pref if l % pref == 0 else l


def kernel(x, c, ctx, c_ctx, w_ada, b_ada, norm1_w, w_in, hy_conv_w, hy_conv_b, filt_w1, filt_b1, filt_freq,
           filt_w2, filt_b2, filt_w3, hy_bias, hy_norm_w, ssd_conv_w, ssd_conv_b, dt_bias, a_log, ssd_d,
           ssd_norm_w, w_out, norm2_w, w_mlp1, w_mlp2, final_norm_w):
    depth = w_in.shape[0]
    bsz, seq, d = x.shape
    hy_w = hy_norm_w.shape[1]
    hy_proj = hy_conv_w.shape[2]
    ssd_w = ssd_norm_w.shape[1]
    ssd_xbc = ssd_conv_w.shape[2]
    heads = ssd_w // SSD_HEADDIM
    hpg = heads // SSD_GROUPS
    ssd_dt = 2 * heads
    gn2 = ssd_xbc - ssd_w
    o0, o1 = hy_proj, hy_proj + ssd_xbc
    o2 = o1 + ssd_dt
    dt_pad = (-ssd_dt) % LANE
    bf16 = jnp.bfloat16
    c_z, c_x, c_hy = 0, ssd_w, 2 * ssd_w
    c_bc = c_hy + hy_proj
    c_dt = c_bc + gn2
    ssd_cols = (c_z, c_x, c_bc, c_dt)

    grid = (seq // GRID_W, GRID_W)
    zero_state = jnp.zeros((bsz, SSD_GROUPS, SSD_STATE, hpg * SSD_HEADDIM), jnp.float32)

    rows = 8
    c_all = jnp.zeros((rows, d), jnp.float32).at[:bsz].set(c).at[bsz].set(c_ctx)
    mod_all = ada_mod(c_all, w_ada, b_ada)

    h_ctx = ctx
    tm_l = _pick(seq, 512)
    tm_c = _pick(ctx.shape[1], 256)
    for i in range(depth):
        lp = dict(hy_conv_w=hy_conv_w[i], hy_conv_b=hy_conv_b[i], filt_w1=filt_w1[i], filt_b1=filt_b1[i],
                  filt_freq=filt_freq[i], filt_w2=filt_w2[i], filt_b2=filt_b2[i], filt_w3=filt_w3[i],
                  hy_bias=hy_bias[i], hy_norm_w=hy_norm_w[i], ssd_conv_w=ssd_conv_w[i],
                  ssd_conv_b=ssd_conv_b[i], dt_bias=dt_bias[i], a_log=a_log[i], ssd_d=ssd_d[i],
                  ssd_norm_w=ssd_norm_w[i])
        wi = w_in[i]
        w_cat = jnp.concatenate(
            [wi[:, o2:], wi[:, o0:o0 + ssd_w], wi[:, :o0], wi[:, o0 + ssd_w:o1], wi[:, o1:o2],
             jnp.zeros((d, dt_pad), wi.dtype)], axis=1).astype(bf16)
        n_cat = w_cat.shape[1]
        tn = n_cat // 3 if (n_cat % 3 == 0 and (n_cat // 3) % LANE == 0) else n_cat
        w_o = w_out[i].astype(bf16)
        w1 = w_mlp1[i].astype(bf16)
        w2 = w_mlp2[i].astype(bf16)

        mod = mod_all[i, :bsz].reshape(bsz, N_MOD, d)
        mod_c = jnp.broadcast_to(mod_all[i, bsz].reshape(1, N_MOD, d), (bsz, N_MOD, d))
        def ssd(p, init_f, init_b, period):
            cw, cb = lp['ssd_conv_w'], lp['ssd_conv_b']
            return ssd_mixer(p, ssd_cols, cw[:, :ssd_w], cb[:ssd_w], cw[:, ssd_w:], cb[ssd_w:], lp['dt_bias'],
                             lp['a_log'], lp['ssd_d'], lp['ssd_norm_w'], init_f, init_b, period,
                             _pick(p.shape[1], 256))

        pc = normmod_matmul(h_ctx, norm1_w[i], mod_c[:, 1], mod_c[:, 0], w_cat, tm_c, tn)
        y_ssd, s_f, s_b = ssd(pc, zero_state, zero_state, pc.shape[1])
        if i < depth - 1:
            y_hy = _hyena_branch(pc[..., c_hy:c_hy + hy_proj], None, lp)
            ycat = jnp.concatenate([y_hy, y_ssd], axis=-1)
            h_ctx = proj_residual(ycat, w_o, h_ctx, mod_c[:, 2], tm_c)
            h_ctx = mlp_residual(h_ctx, norm2_w[i], mod_c[:, 4], mod_c[:, 3], mod_c[:, 5], w1, w2,
                                 final_norm_w, tm_c, 512, False)
        pl_ = normmod_matmul(x, norm1_w[i], mod[:, 1], mod[:, 0], w_cat, tm_l, tn)
        y_hy = _hyena_branch(pl_[..., c_hy:c_hy + hy_proj], grid, lp)
        y_ssd, _, _ = ssd(pl_, s_f, s_b, GRID_W)
        ycat = jnp.concatenate([y_hy, y_ssd], axis=-1)
        x = proj_residual(ycat, w_o, x, mod[:, 2], tm_l)
        x = mlp_residual(x, norm2_w[i], mod[:, 4], mod[:, 3], mod[:, 5], w1, w2, final_norm_w, tm_l, 512,
                         i == depth - 1)
    return x
```

```python
import functools
import math

import numpy as np
import jax
import jax.numpy as jnp
from jax import lax
from jax.experimental import pallas as pl
from jax.experimental.pallas import tpu as pltpu

EPS = 1e-6
N_MOD = 6
GRID_W = 64
HY_GROUPS = 8
HY_ORDER = 2
HY_POS_EMB = 33
HY_TARGET = 1e-2
HY_FAST = 0.3
HY_SLOW = 1.5
SSD_HEADDIM = 64
SSD_GROUPS = 2
SSD_STATE = 128
SSD_CHUNK = 128

LANE = 128
VMEM_LIMIT = 56 * 1024 * 1024


def _cparams(*sem):
    return pltpu.CompilerParams(dimension_semantics=sem, vmem_limit_bytes=VMEM_LIMIT)


def _ada_kernel(c_ref, w_ref, b_ref, o_ref):
    c = c_ref[...]
    a = (c * jax.nn.sigmoid(c)).astype(jnp.bfloat16)
    w = w_ref[0].astype(jnp.bfloat16)
    o_ref[0] = jnp.dot(a, w, preferred_element_type=jnp.float32) + b_ref[0]


def ada_mod(c_all, w_ada, b_ada, tn=1024):
    depth, d, n = w_ada.shape
    r = c_all.shape[0]
    return pl.pallas_call(
        _ada_kernel,
        out_shape=jax.ShapeDtypeStruct((depth, r, n), jnp.float32),
        grid=(depth, n // tn),
        in_specs=[
            pl.BlockSpec((r, d), lambda i, j: (0, 0)),
            pl.BlockSpec((1, d, tn), lambda i, j: (i, 0, j)),
            pl.BlockSpec((1, 1, tn), lambda i, j: (i, 0, j)),
        ],
        out_specs=pl.BlockSpec((1, r, tn), lambda i, j: (i, 0, j)),
        compiler_params=_cparams("parallel", "parallel"),
        name="ada_mod",
    )(c_all, w_ada, b_ada.reshape(depth, 1, n))


def _normmod(x, nw, sc, sh):
    ms = jnp.mean(x * x, axis=-1, keepdims=True)
    return (x * lax.rsqrt(ms + EPS) * nw) * (1.0 + sc) + sh


def _nm_matmul_kernel(x_ref, nw_ref, sc_ref, sh_ref, w_ref, o_ref, h_ref):
    @pl.when(pl.program_id(2) == 0)
    def _():
        h_ref[...] = _normmod(x_ref[0], nw_ref[...], sc_ref[0], sh_ref[0]).astype(jnp.bfloat16)

    o_ref[0] = jnp.dot(h_ref[...], w_ref[...], preferred_element_type=jnp.float32)


def normmod_matmul(x, nw, sc, sh, w, tm, tn):
    b, l, d = x.shape
    n = w.shape[1]
    return pl.pallas_call(
        _nm_matmul_kernel,
        out_shape=jax.ShapeDtypeStruct((b, l, n), jnp.float32),
        grid=(b, l // tm, n // tn),
        in_specs=[
            pl.BlockSpec((1, tm, d), lambda i, j, k: (i, j, 0)),
            pl.BlockSpec((1, d), lambda i, j, k: (0, 0)),
            pl.BlockSpec((1, 1, d), lambda i, j, k: (i, 0, 0)),
            pl.BlockSpec((1, 1, d), lambda i, j, k: (i, 0, 0)),
            pl.BlockSpec((d, tn), lambda i, j, k: (0, k)),
        ],
        out_specs=pl.BlockSpec((1, tm, tn), lambda i, j, k: (i, j, k)),
        scratch_shapes=[pltpu.VMEM((tm, d), jnp.bfloat16)],
        compiler_params=_cparams("parallel", "parallel", "arbitrary"),
        name="normmod_matmul",
    )(x, nw.reshape(1, d), sc.reshape(b, 1, d), sh.reshape(b, 1, d), w)


def _proj_res_kernel(zh_ref, ys_ref, nwh_ref, wh_ref, ws_ref, x_ref, g_ref, o_ref, *, groups):
    bf16 = jnp.bfloat16
    slabs, hy_w = zh_ref.shape[1], zh_ref.shape[2]
    gs = hy_w // groups
    rows = []
    for s in range(slabs):
        cols = []
        for gi in range(groups):
            zg = zh_ref[0, s, gi * gs:(gi + 1) * gs, :]
            ms = jnp.mean(zg * zg, axis=0, keepdims=True)
            cols.append((zg * lax.rsqrt(ms + EPS)).T)
        rows.append(jnp.concatenate(cols, axis=1))
    yh = (jnp.concatenate(rows, axis=0) * nwh_ref[...]).astype(bf16)
    acc = jnp.dot(yh, wh_ref[...], preferred_element_type=jnp.float32)
    acc += jnp.dot(ys_ref[0].astype(bf16), ws_ref[...], preferred_element_type=jnp.float32)
    o_ref[0] = x_ref[0] + g_ref[0] * acc


def proj_residual(zh_t, y_ssd, hy_norm_w, w_hy, w_ssd, x, g, tm, groups):
    b, n1, hy_w, _ = zh_t.shape
    l, ssd_w = y_ssd.shape[1], y_ssd.shape[2]
    n = w_hy.shape[1]
    return pl.pallas_call(
        functools.partial(_proj_res_kernel, groups=groups),
        out_shape=jax.ShapeDtypeStruct((b, l, n), jnp.float32),
        grid=(b, l // tm),
        in_specs=[
            pl.BlockSpec((1, tm // LANE, hy_w, LANE), lambda i, j: (i, j, 0, 0)),
            pl.BlockSpec((1, tm, ssd_w), lambda i, j: (i, j, 0)),
            pl.BlockSpec((1, hy_w), lambda i, j: (0, 0)),
            pl.BlockSpec((hy_w, n), lambda i, j: (0, 0)),
            pl.BlockSpec((ssd_w, n), lambda i, j: (0, 0)),
            pl.BlockSpec((1, tm, n), lambda i, j: (i, j, 0)),
            pl.BlockSpec((1, 1, n), lambda i, j: (i, 0, 0)),
        ],
        out_specs=pl.BlockSpec((1, tm, n), lambda i, j: (i, j, 0)),
        compiler_params=_cparams("parallel", "parallel"),
        name="proj_residual",
    )(zh_t, y_ssd, hy_norm_w.reshape(1, hy_w), w_hy, w_ssd, x, g.reshape(b, 1, n))


def _hypre_kernel(u_ref, w_ref, b_ref, o_ref, *, period):
    u = _token_conv3(u_ref[0], w_ref[...], b_ref[...], period)
    tb, cw = u.shape
    for s in range(tb // LANE):
        for j in range(cw // LANE):
            o_ref[0, s, j * LANE:(j + 1) * LANE, :] = u[s * LANE:(s + 1) * LANE, j * LANE:(j + 1) * LANE].T


def hyena_pre(p, col0, conv_w, conv_b, period, tb, cw):
    b, l, _ = p.shape
    c = conv_w.shape[1]
    blk0 = col0 // cw
    return pl.pallas_call(
        functools.partial(_hypre_kernel, period=period),
        out_shape=jax.ShapeDtypeStruct((b, l // LANE, c, LANE), jnp.float32),
        grid=(b, l // tb, c // cw),
        in_specs=[
            pl.BlockSpec((1, tb, cw), lambda i, j, k: (i, j, blk0 + k)),
            pl.BlockSpec((3, cw), lambda i, j, k: (0, k)),
            pl.BlockSpec((1, cw), lambda i, j, k: (0, k)),
        ],
        out_specs=pl.BlockSpec((1, tb // LANE, cw, LANE), lambda i, j, k: (i, j, k, 0)),
        compiler_params=_cparams("parallel", "parallel", "parallel"),
        name="hyena_pre",
    )(p, conv_w, conv_b.reshape(1, c))


def _fft_constants(bsz, n1):
    m1 = 2 * n1
    n = m1 * LANE
    pairs = bsz // 2
    half = pairs * m1
    r = bsz * n1
    t1 = np.arange(n1)[:, None]
    f1 = np.arange(m1)[None, :]
    th = 2.0 * np.pi * t1 * f1 / m1
    w1 = np.zeros((r, 2 * half))
    for pr in range(pairs):
        re = slice(pr * m1, (pr + 1) * m1)
        im = slice(half + pr * m1, half + (pr + 1) * m1)
        ra = slice((2 * pr) * n1, (2 * pr + 1) * n1)
        rb = slice((2 * pr + 1) * n1, (2 * pr + 2) * n1)
        w1[ra, re], w1[ra, im] = np.cos(th), -np.sin(th)
        w1[rb, re], w1[rb, im] = np.sin(th), np.cos(th)
    w4 = w1.T / n
    t2 = np.arange(LANE)[:, None]
    ps = 2.0 * np.pi * t2 * np.arange(LANE)[None, :] / LANE
    w2 = np.block([[np.cos(ps), -np.sin(ps)], [np.sin(ps), np.cos(ps)]])
    ph = 2.0 * np.pi * t2 * f1 / n
    tc = np.tile(np.cos(ph), (1, pairs))
    ts = np.tile(np.sin(ph), (1, pairs))
    mats = [jnp.asarray(m, jnp.bfloat16) for m in (w1, w2, w2.T, w4)]
    tabs = [jnp.asarray(m, jnp.float32) for m in (tc, ts, tc.T, ts.T)]
    return mats, tabs, half


def _mxu(a, w):
    return jnp.dot(a.astype(jnp.bfloat16), w, preferred_element_type=jnp.float32)


def _hyconv_kernel(z_ref, g_ref, kre_ref, kim_ref, brep_ref, w1_ref, w2_ref, w2i_ref, w4_ref, tc_ref, ts_ref,
                   tct_ref, tst_ref, o_ref, *, half):
    x0 = z_ref[...]
    cw = x0.shape[1] // LANE
    a1 = _mxu(x0.T, w1_ref[...])
    tc, ts, tct, tst = tc_ref[...], ts_ref[...], tct_ref[...], tst_ref[...]
    lhs2 = []
    for c in range(cw):
        ar = a1[c * LANE:(c + 1) * LANE, :half]
        ai = a1[c * LANE:(c + 1) * LANE, half:]
        lhs2.append(jnp.concatenate([(ar * tc + ai * ts).T, (ai * tc - ar * ts).T], axis=1))
    lhs2 = jnp.concatenate(lhs2, axis=0)
    s = _mxu(lhs2, w2_ref[...])
    sr, si = s[:, :LANE], s[:, LANE:]
    kr, ki = kre_ref[...], kim_ref[...]
    y2 = jnp.concatenate([sr * kr - si * ki, sr * ki + si * kr], axis=1)
    bq = _mxu(y2, w2i_ref[...])
    lhs4 = []
    for c in range(cw):
        br = bq[c * half:(c + 1) * half, :LANE]
        bi = bq[c * half:(c + 1) * half, LANE:]
        lhs4.append(jnp.concatenate([(br * tct - bi * tst).T, (bi * tct + br * tst).T], axis=1))
    lhs4 = jnp.concatenate(lhs4, axis=0)
    y = _mxu(lhs4, w4_ref[...]).T
    o_ref[...] = g_ref[...] * (y + x0 * brep_ref[...])


def hyena_conv(ut, z_src, z_col, g_col, kre, kim, bias, bsz, cw):
    r = ut.shape[0]
    n1 = r // bsz
    c = bias.shape[0]
    mats, tabs, half = _fft_constants(bsz, n1)
    blk = cw * LANE
    zb, gb = z_col // cw, g_col // cw
    brep = jnp.repeat(bias, LANE).reshape(1, c * LANE)

    def const(a):
        return pl.BlockSpec(a.shape, lambda j: (0, 0))

    return pl.pallas_call(
        functools.partial(_hyconv_kernel, half=half),
        out_shape=jax.ShapeDtypeStruct((r, c * LANE), jnp.float32),
        grid=(c // cw,),
        in_specs=[
            pl.BlockSpec((r, blk), lambda j: (0, zb + j)),
            pl.BlockSpec((r, blk), lambda j: (0, gb + j)),
            pl.BlockSpec((cw * half, LANE), lambda j: (j, 0)),
            pl.BlockSpec((cw * half, LANE), lambda j: (j, 0)),
            pl.BlockSpec((1, blk), lambda j: (0, j)),
        ] + [const(a) for a in mats + tabs],
        out_specs=pl.BlockSpec((r, blk), lambda j: (0, j)),
        compiler_params=_cparams("parallel"),
        name="hyena_conv",
    )(z_src, ut, kre, kim, brep, *mats, *tabs)


def _mlp_kernel(x_ref, nw_ref, sc_ref, sh_ref, g_ref, w1_ref, w2_ref, fw_ref, o_ref, h_ref, *, final_norm):
    f = pl.program_id(2)

    @pl.when(f == 0)
    def _():
        h_ref[...] = _normmod(x_ref[0], nw_ref[...], sc_ref[0], sh_ref[0]).astype(jnp.bfloat16)

    a = jnp.dot(h_ref[...], w1_ref[...], preferred_element_type=jnp.float32)
    a = jnp.square(jnp.maximum(a, 0.0)).astype(jnp.bfloat16)
    part = jnp.dot(a, w2_ref[...], preferred_element_type=jnp.float32)

    @pl.when(f == 0)
    def _():
        o_ref[0] = part

    @pl.when(f > 0)
    def _():
        o_ref[0] += part

    @pl.when(f == pl.num_programs(2) - 1)
    def _():
        y = x_ref[0] + g_ref[0] * o_ref[0]
        if final_norm:
            ms = jnp.mean(y * y, axis=-1, keepdims=True)
            y = y * lax.rsqrt(ms + EPS) * fw_ref[...]
        o_ref[0] = y


def mlp_residual(x, nw, sc, sh, g, w1, w2, fw, tm, tf, final_norm):
    b, l, d = x.shape
    dff = w1.shape[1]
    vec = pl.BlockSpec((1, 1, d), lambda i, j, k: (i, 0, 0))
    row = pl.BlockSpec((1, d), lambda i, j, k: (0, 0))
    return pl.pallas_call(
        functools.partial(_mlp_kernel, final_norm=final_norm),
        out_shape=jax.ShapeDtypeStruct((b, l, d), jnp.float32),
        grid=(b, l // tm, dff // tf),
        in_specs=[
            pl.BlockSpec((1, tm, d), lambda i, j, k: (i, j, 0)),
            row, vec, vec, vec,
            pl.BlockSpec((d, tf), lambda i, j, k: (0, k)),
            pl.BlockSpec((tf, d), lambda i, j, k: (k, 0)),
            row,
        ],
        out_specs=pl.BlockSpec((1, tm, d), lambda i, j, k: (i, j, 0)),
        scratch_shapes=[pltpu.VMEM((tm, d), jnp.bfloat16)],
        compiler_params=_cparams("parallel", "parallel", "arbitrary"),
        name="mlp_residual",
    )(x, nw.reshape(1, d), sc.reshape(b, 1, d), sh.reshape(b, 1, d), g.reshape(b, 1, d), w1, w2,
      fw.reshape(1, d))


def _split3(v):
    f32, bf16 = jnp.float32, jnp.bfloat16
    hi = v.astype(bf16)
    r1 = v - hi.astype(f32)
    mid = r1.astype(bf16)
    lo = (r1 - mid.astype(f32)).astype(bf16)
    return jnp.concatenate([hi, mid, lo], axis=1)


def _lane_repeat(v, rep):
    h = v.shape[1]
    row = lax.broadcasted_iota(jnp.int32, (3 * h, h * rep), 0) % h
    col = lax.broadcasted_iota(jnp.int32, (3 * h, h * rep), 1) // rep
    e = (row == col).astype(jnp.bfloat16)
    return jnp.dot(_split3(v), e, preferred_element_type=jnp.float32)


def _silu(v):
    return v * jax.nn.sigmoid(v)


def _softplus(v):
    return jnp.maximum(v, 0.0) + jnp.log1p(jnp.exp(-jnp.abs(v)))


def _token_conv3(u, w, bias, period):
    t, c = u.shape
    pos = lax.broadcasted_iota(jnp.int32, (t, c), 0) % period
    up = jnp.where(pos == 0, 0.0, pltpu.roll(u, 1, 0))
    dn = jnp.where(pos == period - 1, 0.0, pltpu.roll(u, t - 1, 0))
    return bias + w[0:1] * up + w[1:2] * u + w[2:3] * dn


def _ssd_chunk(xs, bm, cm, dtr, s_ref, a_row, dtb_row, reverse, heads, hpg):
    f32, bf16 = jnp.float32, jnp.bfloat16
    q = xs.shape[0]
    p = SSD_HEADDIM
    n = SSD_STATE
    gw = hpg * p
    dt = _softplus(dtr + dtb_row)
    a = dt * a_row
    ri = lax.broadcasted_iota(jnp.int32, (q, q), 0)
    ci = lax.broadcasted_iota(jnp.int32, (q, q), 1)
    keep = (ci >= ri) if reverse else (ci <= ri)
    a3 = jnp.dot(keep.astype(bf16), _split3(a), preferred_element_type=f32)
    cs = a3[:, :heads] + a3[:, heads:2 * heads] + a3[:, 2 * heads:]
    dt_rep = _lane_repeat(dt, p)
    cs_rep = _lane_repeat(cs, p)
    cs_wide = _lane_repeat(cs, q)
    end = 0 if reverse else q - 1
    cs_end = cs_rep[end:end + 1]
    xdt = xs * dt_rep
    xw = (xdt * jnp.exp(cs_end - cs_rep)).astype(bf16)
    ecs = jnp.exp(cs_rep)
    chunk_decay = jnp.exp(cs_end)
    xdt_b = xdt.astype(bf16)
    lane = lax.broadcasted_iota(jnp.int32, (q, 2 * p), 1)
    ys = []
    for g in range(SSD_GROUPS):
        bg = bm[:, g * n:(g + 1) * n].astype(bf16)
        cg = cm[:, g * n:(g + 1) * n].astype(bf16)
        cb = lax.dot_general(cg, bg, (((1,), (1,)), ((), ())), preferred_element_type=f32)
        s_old = s_ref[0, g]
        y_off = jnp.dot(cg, s_old.astype(bf16), preferred_element_type=f32) * ecs[:, g * gw:(g + 1) * gw]
        for pr in range(hpg // 2):
            xpair = xdt_b[:, g * gw + pr * 2 * p:g * gw + (pr + 1) * 2 * p]
            acc = None
            for k in range(2):
                h = g * hpg + pr * 2 + k
                csr = cs_wide[:, h * q:(h + 1) * q]
                seg = csr - csr.T
                gmat = (cb * jnp.where(keep, jnp.exp(seg), 0.0)).astype(bf16)
                xh = jnp.where((lane // p) == k, xpair, jnp.zeros_like(xpair))
                part = jnp.dot(gmat, xh, preferred_element_type=f32)
                acc = part if acc is None else acc + part
            lo = pr * 2 * p
            ys.append(acc + y_off[:, lo:lo + 2 * p])
        upd = jnp.dot(bg.T, xw[:, g * gw:(g + 1) * gw], preferred_element_type=f32)
        s_ref[0, g] = s_old * chunk_decay[:, g * gw:(g + 1) * gw] + upd
    return jnp.concatenate(ys, axis=1)


def _ssd_kernel(z_ref, xs_ref, bc_ref, dt_ref, cwx_ref, cbx_ref, cwb_ref, cbb_ref, dtb_ref, alog_ref, drep_ref,
                nw_ref, initf_ref, initb_ref, y_ref, sf_ref, sb_ref, yb_ref, *, nb, tb, period, heads, hpg):
    j = pl.program_id(1)
    q = SSD_CHUNK
    n = SSD_STATE
    gn = SSD_GROUPS * n
    nchunk = tb // q
    ssd_w = heads * SSD_HEADDIM

    @pl.when(j == 0)
    def _():
        sb_ref[...] = initb_ref[...]

    @pl.when(j == nb)
    def _():
        sf_ref[...] = initf_ref[...]

    xs_all = _silu(_token_conv3(xs_ref[0], cwx_ref[...], cbx_ref[...], period))
    bc_all = _silu(_token_conv3(bc_ref[0], cwb_ref[...], cbb_ref[...], period))
    a_all = -jnp.exp(alog_ref[...])

    def run(reverse):
        d = 1 if reverse else 0
        blk = (nb - 1 - j) if reverse else (j - nb)
        order = range(nchunk - 1, -1, -1) if reverse else range(nchunk)
        for ci in order:
            sl = slice(ci * q, (ci + 1) * q)
            y = _ssd_chunk(xs_all[sl], bc_all[sl, :gn], bc_all[sl, gn:], dt_ref[0, sl, d * heads:(d + 1) * heads],
                           sb_ref if reverse else sf_ref, a_all[d:d + 1], dtb_ref[d:d + 1], reverse, heads, hpg)
            row0 = pl.multiple_of(blk * tb + ci * q, q)
            if reverse:
                yb_ref[pl.ds(row0, q), :] = y
            else:
                y = y + yb_ref[pl.ds(row0, q), :] + drep_ref[...] * xs_all[sl]
                y = y * _silu(z_ref[0, sl, :])
                gw = ssd_w // SSD_GROUPS
                outs = []
                for g in range(SSD_GROUPS):
                    yg = y[:, g * gw:(g + 1) * gw]
                    ms = jnp.mean(yg * yg, axis=-1, keepdims=True)
                    outs.append(yg * lax.rsqrt(ms + EPS) * nw_ref[:, g * gw:(g + 1) * gw])
                y_ref[0, sl, :] = jnp.concatenate(outs, axis=1)

    @pl.when(j < nb)
    def _():
        run(True)

    @pl.when(j >= nb)
    def _():
        run(False)


def ssd_mixer(p, cols, conv_w_x, conv_b_x, conv_w_bc, conv_b_bc, dt_bias, a_log, ssd_d, norm_w, init_f, init_b,
              period, tb):
    b, l, _ = p.shape
    heads = dt_bias.shape[1]
    hpg = heads // SSD_GROUPS
    ssd_w = heads * SSD_HEADDIM
    gn = SSD_GROUPS * SSD_STATE
    nb = l // tb
    oz, ox, obc, odt = cols
    f32 = jnp.float32

    def tok(width, off):
        blk_idx = off // width
        return pl.BlockSpec((1, tb, width),
                            lambda i, j: (i, jnp.where(j < nb, nb - 1 - j, j - nb), blk_idx))

    def tok_fwd(width, off):
        blk_idx = off // width
        return pl.BlockSpec((1, tb, width), lambda i, j: (i, jnp.where(j < nb, 0, j - nb), blk_idx))

    def whole(shape):
        return pl.BlockSpec(shape, lambda i, j: (0,) * len(shape))

    st_shape = (b, SSD_GROUPS, SSD_STATE, hpg * SSD_HEADDIM)
    st_spec = pl.BlockSpec((1,) + st_shape[1:], lambda i, j: (i, 0, 0, 0))
    drep = jnp.repeat(ssd_d, SSD_HEADDIM).reshape(1, ssd_w)
    kern = functools.partial(_ssd_kernel, nb=nb, tb=tb, period=period, heads=heads, hpg=hpg)
    return pl.pallas_call(
        kern,
        out_shape=(jax.ShapeDtypeStruct((b, l, ssd_w), f32), jax.ShapeDtypeStruct(st_shape, f32),
                   jax.ShapeDtypeStruct(st_shape, f32)),
        grid=(b, 2 * nb),
        in_specs=[
            tok_fwd(ssd_w, oz), tok(ssd_w, ox), tok(2 * gn, obc), tok(LANE, odt),
            whole((3, ssd_w)), whole((1, ssd_w)), whole((3, 2 * gn)), whole((1, 2 * gn)),
            whole((2, heads)), whole((2, heads)), whole((1, ssd_w)), whole((1, ssd_w)),
            st_spec, st_spec,
        ],
        out_specs=(pl.BlockSpec((1, tb, ssd_w), lambda i, j: (i, jnp.where(j < nb, 0, j - nb), 0)),
                   st_spec, st_spec),
        scratch_shapes=[pltpu.VMEM((l, ssd_w), f32)],
        compiler_params=_cparams("parallel", "arbitrary"),
        name="ssd_mixer",
    )(p, p, p, p, conv_w_x, conv_b_x.reshape(1, -1), conv_w_bc, conv_b_bc.reshape(1, -1), dt_bias, a_log, drep,
      norm_w.reshape(1, ssd_w), init_f, init_b)


def _rmsnorm(x, w):
    return x * lax.rsqrt(jnp.mean(x * x, axis=-1, keepdims=True) + EPS) * w


def _dwconv(u, w, bias, grid):
    b, L, C = u.shape
    if grid is not None:
        u = u.reshape(b, grid[0], grid[1], C)
    K = w.shape[0]
    pad = K // 2
    T = u.shape[-2]
    up = jnp.pad(u, [(0, 0)] * (u.ndim - 2) + [(pad, pad), (0, 0)])
    y = bias + w[0] * up[..., 0:T, :]
    for k in range(1, K):
        y = y + w[k] * up[..., k:k + T, :]
    return y.reshape(b, L, C)


def _hyena_filter(L, fw1, fb1, freq, fw2, fb2, fw3):
    f32 = jnp.float32
    hy_w = fw3.shape[1] // (2 * HY_ORDER)
    t = jnp.linspace(0.0, 1.0, L, dtype=f32)[:, None]
    bands = (HY_POS_EMB - 1) // 2
    w = 2.0 * math.pi * jnp.arange(L, dtype=f32)[:, None] / L
    f = jnp.linspace(1e-4, bands - 1, bands, dtype=f32)[None, :]
    feats = jnp.concatenate([t, jnp.cos(f * w), -jnp.sin(f * w)], axis=-1)
    hdn = jnp.sin(freq * (feats @ fw1 + fb1))
    hdn = jnp.sin(freq * (hdn @ fw2 + fb2))
    h = (hdn @ fw3).astype(f32).reshape(L, HY_ORDER, 2, hy_w)
    deltas = jnp.linspace(math.log(HY_TARGET) / HY_SLOW, math.log(HY_TARGET) / HY_FAST, hy_w, dtype=f32)
    h = h * jnp.exp(-t[:, :, None, None] * jnp.abs(deltas))
    k = jnp.concatenate([h[:, :, 0], jnp.zeros((1, HY_ORDER, hy_w), f32), h[:0:-1, :, 1]], axis=0)
    return k / (jnp.sum(jnp.abs(k), axis=0, keepdims=True) + EPS)


def _filter_spectrum(k, bsz):
    n, order, c = k.shape
    m1 = n // LANE
    pairs = bsz // 2
    kf = jnp.fft.fft(k, axis=0).reshape(LANE, m1, order, c).transpose(2, 3, 1, 0)
    kf = jnp.broadcast_to(kf[:, :, None], (order, c, pairs, m1, LANE)).reshape(order, c * pairs * m1, LANE)
    return jnp.real(kf), jnp.imag(kf)


def _fft_conv(z, kf):
    L = z.shape[1]
    zf = jnp.fft.rfft(z, n=2 * L, axis=1)
    return jnp.fft.irfft(zf * kf[None], n=2 * L, axis=1)[:, :L]


def _hyena_branch(u_raw, grid, lp):
    b, L, _ = u_raw.shape
    hy_w = lp['hy_norm_w'].shape[0]
    u = _dwconv(u_raw, lp['hy_conv_w'], lp['hy_conv_b'], grid)
    v, *gates = jnp.split(u, HY_ORDER + 1, axis=-1)
    kf = jnp.fft.rfft(_hyena_filter(L, lp['filt_w1'], lp['filt_b1'], lp['filt_freq'], lp['filt_w2'],
                                    lp['filt_b2'], lp['filt_w3']), axis=0)
    z = v
    for o, gate in enumerate(gates):
        z = gate * (_fft_conv(z, kf[:, o]) + z * lp['hy_bias'][o])
    return z.reshape(b, L // LANE, LANE, hy_w).transpose(0, 1, 3, 2)


def _pick(l, pref):
    return pref if l % pref == 0 else l


def kernel(x, c, ctx, c_ctx, w_ada, b_ada, norm1_w, w_in, hy_conv_w, hy_conv_b, filt_w1, filt_b1, filt_freq,
           filt_w2, filt_b2, filt_w3, hy_bias, hy_norm_w, ssd_conv_w, ssd_conv_b, dt_bias, a_log, ssd_d,
           ssd_norm_w, w_out, norm2_w, w_mlp1, w_mlp2, final_norm_w):
    depth = w_in.shape[0]
    bsz, seq, d = x.shape
    hy_w = hy_norm_w.shape[1]
    hy_proj = hy_conv_w.shape[2]
    ssd_w = ssd_norm_w.shape[1]
    ssd_xbc = ssd_conv_w.shape[2]
    heads = ssd_w // SSD_HEADDIM
    hpg = heads // SSD_GROUPS
    ssd_dt = 2 * heads
    gn2 = ssd_xbc - ssd_w
    o0, o1 = hy_proj, hy_proj + ssd_xbc
    o2 = o1 + ssd_dt
    dt_pad = (-ssd_dt) % LANE
    bf16 = jnp.bfloat16
    c_z, c_x, c_hy = 0, ssd_w, 2 * ssd_w
    c_bc = c_hy + hy_proj
    c_dt = c_bc + gn2
    ssd_cols = (c_z, c_x, c_bc, c_dt)

    n1 = seq // LANE
    zero_state = jnp.zeros((bsz, SSD_GROUPS, SSD_STATE, hpg * SSD_HEADDIM), jnp.float32)

    rows = 8
    c_all = jnp.zeros((rows, d), jnp.float32).at[:bsz].set(c).at[bsz].set(c_ctx)
    mod_all = ada_mod(c_all, w_ada, b_ada)

    h_ctx = ctx
    tm_l = _pick(seq, 512)
    tm_c = _pick(ctx.shape[1], 256)
    for i in range(depth):
        lp = dict(hy_conv_w=hy_conv_w[i], hy_conv_b=hy_conv_b[i], filt_w1=filt_w1[i], filt_b1=filt_b1[i],
                  filt_freq=filt_freq[i], filt_w2=filt_w2[i], filt_b2=filt_b2[i], filt_w3=filt_w3[i],
                  hy_bias=hy_bias[i], hy_norm_w=hy_norm_w[i], ssd_conv_w=ssd_conv_w[i],
                  ssd_conv_b=ssd_conv_b[i], dt_bias=dt_bias[i], a_log=a_log[i], ssd_d=ssd_d[i],
                  ssd_norm_w=ssd_norm_w[i])
        wi = w_in[i]
        w_cat = jnp.concatenate(
            [wi[:, o2:], wi[:, o0:o0 + ssd_w], wi[:, :o0], wi[:, o0 + ssd_w:o1], wi[:, o1:o2],
             jnp.zeros((d, dt_pad), wi.dtype)], axis=1).astype(bf16)
        n_cat = w_cat.shape[1]
        tn = n_cat // 3 if (n_cat % 3 == 0 and (n_cat // 3) % LANE == 0) else n_cat
        w_o_hy = w_out[i, :hy_w].astype(bf16)
        w_o_ssd = w_out[i, hy_w:].astype(bf16)
        w1 = w_mlp1[i].astype(bf16)
        w2 = w_mlp2[i].astype(bf16)

        mod = mod_all[i, :bsz].reshape(bsz, N_MOD, d)
        mod_c = jnp.broadcast_to(mod_all[i, bsz].reshape(1, N_MOD, d), (bsz, N_MOD, d))
        def ssd(p, init_f, init_b, period):
            cw, cb = lp['ssd_conv_w'], lp['ssd_conv_b']
            return ssd_mixer(p, ssd_cols, cw[:, :ssd_w], cb[:ssd_w], cw[:, ssd_w:], cb[ssd_w:], lp['dt_bias'],
                             lp['a_log'], lp['ssd_d'], lp['ssd_norm_w'], init_f, init_b, period,
                             _pick(p.shape[1], 256))

        pc = normmod_matmul(h_ctx, norm1_w[i], mod_c[:, 1], mod_c[:, 0], w_cat, tm_c, tn)
        y_ssd, s_f, s_b = ssd(pc, zero_state, zero_state, pc.shape[1])
        if i < depth - 1:
            zh_t = _hyena_branch(pc[..., c_hy:c_hy + hy_proj], None, lp)
            h_ctx = proj_residual(zh_t, y_ssd, lp['hy_norm_w'], w_o_hy, w_o_ssd, h_ctx, mod_c[:, 2], tm_c,
                                  HY_GROUPS)
            h_ctx = mlp_residual(h_ctx, norm2_w[i], mod_c[:, 4], mod_c[:, 3], mod_c[:, 5], w1, w2,
                                 final_norm_w, tm_c, 512, False)
        pl_ = normmod_matmul(x, norm1_w[i], mod[:, 1], mod[:, 0], w_cat, tm_l, tn)
        ut = hyena_pre(pl_, c_hy, lp['hy_conv_w'], lp['hy_conv_b'], GRID_W, 512, 512)
        ut = ut.reshape(bsz * n1, hy_proj * LANE)
        kre, kim = _filter_spectrum(_hyena_filter(seq, lp['filt_w1'], lp['filt_b1'], lp['filt_freq'],
                                                  lp['filt_w2'], lp['filt_b2'], lp['filt_w3']), bsz)
        zt = ut
        for o in range(HY_ORDER):
            zt = hyena_conv(ut, zt, 0, (o + 1) * hy_w, kre[o], kim[o], lp['hy_bias'][o], bsz, 16)
        zh_t = zt.reshape(bsz, n1, hy_w, LANE)
        y_ssd, _, _ = ssd(pl_, s_f, s_b, GRID_W)
        x = proj_residual(zh_t, y_ssd, lp['hy_norm_w'], w_o_hy, w_o_ssd, x, mod[:, 2], tm_l, HY_GROUPS)
        x = mlp_residual(x, norm2_w[i], mod[:, 4], mod[:, 3], mod[:, 5], w1, w2, final_norm_w, tm_l, 512,
                         i == depth - 1)
    return x
```

```python
import functools
import math

import numpy as np
import jax
import jax.numpy as jnp
from jax import lax
from jax.experimental import pallas as pl
from jax.experimental.pallas import tpu as pltpu

EPS = 1e-6
N_MOD = 6
GRID_W = 64
HY_GROUPS = 8
HY_ORDER = 2
HY_POS_EMB = 33
HY_TARGET = 1e-2
HY_FAST = 0.3
HY_SLOW = 1.5
SSD_HEADDIM = 64
SSD_GROUPS = 2
SSD_STATE = 128
SSD_CHUNK = 128

LANE = 128
VMEM_LIMIT = 56 * 1024 * 1024


def _cparams(*sem):
    return pltpu.CompilerParams(dimension_semantics=sem, vmem_limit_bytes=VMEM_LIMIT)


def _ada_kernel(c_ref, w_ref, b_ref, o_ref):
    c = c_ref[...]
    a = (c * jax.nn.sigmoid(c)).astype(jnp.bfloat16)
    w = w_ref[0].astype(jnp.bfloat16)
    o_ref[0] = jnp.dot(a, w, preferred_element_type=jnp.float32) + b_ref[0]


def ada_mod(c_all, w_ada, b_ada, tn=1024):
    depth, d, n = w_ada.shape
    r = c_all.shape[0]
    return pl.pallas_call(
        _ada_kernel,
        out_shape=jax.ShapeDtypeStruct((depth, r, n), jnp.float32),
        grid=(depth, n // tn),
        in_specs=[
            pl.BlockSpec((r, d), lambda i, j: (0, 0)),
            pl.BlockSpec((1, d, tn), lambda i, j: (i, 0, j)),
            pl.BlockSpec((1, 1, tn), lambda i, j: (i, 0, j)),
        ],
        out_specs=pl.BlockSpec((1, r, tn), lambda i, j: (i, 0, j)),
        compiler_params=_cparams("parallel", "parallel"),
        name="ada_mod",
    )(c_all, w_ada, b_ada.reshape(depth, 1, n))


def _normmod(x, nw, sc, sh):
    ms = jnp.mean(x * x, axis=-1, keepdims=True)
    return (x * lax.rsqrt(ms + EPS) * nw) * (1.0 + sc) + sh


def _nm_matmul_kernel(x_ref, nw_ref, sc_ref, sh_ref, wa_ref, wb_ref, oa_ref, ob_ref, h_ref, *, na):
    k = pl.program_id(2)

    @pl.when(k == 0)
    def _():
        h_ref[...] = _normmod(x_ref[0], nw_ref[...], sc_ref[0], sh_ref[0]).astype(jnp.bfloat16)

    @pl.when(k < na)
    def _():
        oa_ref[0] = jnp.dot(h_ref[...], wa_ref[...], preferred_element_type=jnp.float32)

    @pl.when(k >= na)
    def _():
        ob_ref[0] = jnp.dot(h_ref[...], wb_ref[...], preferred_element_type=jnp.float32)


def normmod_matmul(x, nw, sc, sh, wa, wb, tm, tn):
    b, l, d = x.shape
    n_a, n_b = wa.shape[1], wb.shape[1]
    na = n_a // tn
    last = na - 1
    return pl.pallas_call(
        functools.partial(_nm_matmul_kernel, na=na),
        out_shape=(jax.ShapeDtypeStruct((b, l, n_a), jnp.float32), jax.ShapeDtypeStruct((b, l, n_b), jnp.float32)),
        grid=(b, l // tm, na + 1),
        in_specs=[
            pl.BlockSpec((1, tm, d), lambda i, j, k: (i, j, 0)),
            pl.BlockSpec((1, d), lambda i, j, k: (0, 0)),
            pl.BlockSpec((1, 1, d), lambda i, j, k: (i, 0, 0)),
            pl.BlockSpec((1, 1, d), lambda i, j, k: (i, 0, 0)),
            pl.BlockSpec((d, tn), lambda i, j, k: (0, jnp.minimum(k, last))),
            pl.BlockSpec((d, n_b), lambda i, j, k: (0, 0)),
        ],
        out_specs=(pl.BlockSpec((1, tm, tn), lambda i, j, k: (i, j, jnp.minimum(k, last))),
                   pl.BlockSpec((1, tm, n_b), lambda i, j, k: (i, j, 0))),
        scratch_shapes=[pltpu.VMEM((tm, d), jnp.bfloat16)],
        compiler_params=_cparams("parallel", "parallel", "arbitrary"),
        name="normmod_matmul",
    )(x, nw.reshape(1, d), sc.reshape(b, 1, d), sh.reshape(b, 1, d), wa, wb)


def _proj_res_kernel(zh_ref, ys_ref, nwh_ref, wh_ref, ws_ref, x_ref, g_ref, o_ref, *, groups):
    bf16 = jnp.bfloat16
    slabs, hy_w = zh_ref.shape[1], zh_ref.shape[2]
    gs = hy_w // groups
    rows = []
    for s in range(slabs):
        cols = []
        for gi in range(groups):
            zg = zh_ref[0, s, gi * gs:(gi + 1) * gs, :]
            ms = jnp.mean(zg * zg, axis=0, keepdims=True)
            cols.append((zg * lax.rsqrt(ms + EPS)).T)
        rows.append(jnp.concatenate(cols, axis=1))
    yh = (jnp.concatenate(rows, axis=0) * nwh_ref[...]).astype(bf16)
    acc = jnp.dot(yh, wh_ref[...], preferred_element_type=jnp.float32)
    acc += jnp.dot(ys_ref[0].astype(bf16), ws_ref[...], preferred_element_type=jnp.float32)
    o_ref[0] = x_ref[0] + g_ref[0] * acc


def proj_residual(zh_t, y_ssd, hy_norm_w, w_hy, w_ssd, x, g, tm, groups):
    b, n1, hy_w, _ = zh_t.shape
    l, ssd_w = y_ssd.shape[1], y_ssd.shape[2]
    n = w_hy.shape[1]
    return pl.pallas_call(
        functools.partial(_proj_res_kernel, groups=groups),
        out_shape=jax.ShapeDtypeStruct((b, l, n), jnp.float32),
        grid=(b, l // tm),
        in_specs=[
            pl.BlockSpec((1, tm // LANE, hy_w, LANE), lambda i, j: (i, j, 0, 0)),
            pl.BlockSpec((1, tm, ssd_w), lambda i, j: (i, j, 0)),
            pl.BlockSpec((1, hy_w), lambda i, j: (0, 0)),
            pl.BlockSpec((hy_w, n), lambda i, j: (0, 0)),
            pl.BlockSpec((ssd_w, n), lambda i, j: (0, 0)),
            pl.BlockSpec((1, tm, n), lambda i, j: (i, j, 0)),
            pl.BlockSpec((1, 1, n), lambda i, j: (i, 0, 0)),
        ],
        out_specs=pl.BlockSpec((1, tm, n), lambda i, j: (i, j, 0)),
        compiler_params=_cparams("parallel", "parallel"),
        name="proj_residual",
    )(zh_t, y_ssd, hy_norm_w.reshape(1, hy_w), w_hy, w_ssd, x, g.reshape(b, 1, n))


def _hypre_kernel(u_ref, w_ref, b_ref, o_ref, *, period):
    u = _token_conv3(u_ref[0], w_ref[...], b_ref[...], period)
    tb, cw = u.shape
    for s in range(tb // LANE):
        for j in range(cw // LANE):
            o_ref[0, s, j * LANE:(j + 1) * LANE, :] = u[s * LANE:(s + 1) * LANE, j * LANE:(j + 1) * LANE].T


def hyena_pre(p, col0, conv_w, conv_b, period, tb, cw):
    b, l, _ = p.shape
    c = conv_w.shape[1]
    blk0 = col0 // cw
    return pl.pallas_call(
        functools.partial(_hypre_kernel, period=period),
        out_shape=jax.ShapeDtypeStruct((b, l // LANE, c, LANE), jnp.float32),
        grid=(b, l // tb, c // cw),
        in_specs=[
            pl.BlockSpec((1, tb, cw), lambda i, j, k: (i, j, blk0 + k)),
            pl.BlockSpec((3, cw), lambda i, j, k: (0, k)),
            pl.BlockSpec((1, cw), lambda i, j, k: (0, k)),
        ],
        out_specs=pl.BlockSpec((1, tb // LANE, cw, LANE), lambda i, j, k: (i, j, k, 0)),
        compiler_params=_cparams("parallel", "parallel", "parallel"),
        name="hyena_pre",
    )(p, conv_w, conv_b.reshape(1, c))


def _fft_constants(bsz, n1):
    m1 = 2 * n1
    n = m1 * LANE
    pairs = bsz // 2
    half = pairs * m1
    r = bsz * n1
    t1 = np.arange(n1)[:, None]
    f1 = np.arange(m1)[None, :]
    th = 2.0 * np.pi * t1 * f1 / m1
    w1 = np.zeros((r, 2 * half))
    for pr in range(pairs):
        re = slice(pr * m1, (pr + 1) * m1)
        im = slice(half + pr * m1, half + (pr + 1) * m1)
        ra = slice((2 * pr) * n1, (2 * pr + 1) * n1)
        rb = slice((2 * pr + 1) * n1, (2 * pr + 2) * n1)
        w1[ra, re], w1[ra, im] = np.cos(th), -np.sin(th)
        w1[rb, re], w1[rb, im] = np.sin(th), np.cos(th)
    w4 = w1.T / n
    t2 = np.arange(LANE)[:, None]
    ps = 2.0 * np.pi * t2 * np.arange(LANE)[None, :] / LANE
    w2 = np.block([[np.cos(ps), -np.sin(ps)], [np.sin(ps), np.cos(ps)]])
    ph = 2.0 * np.pi * t2 * f1 / n
    tc = np.tile(np.cos(ph), (1, pairs))
    ts = np.tile(np.sin(ph), (1, pairs))
    orders = half // m1
    tf = 2.0 * np.pi * np.arange(m1)[:, None] * f1 / m1
    w1f = np.zeros((orders * m1, 2 * half))
    for o in range(orders):
        w1f[o * m1:(o + 1) * m1, o * m1:(o + 1) * m1] = np.cos(tf)
        w1f[o * m1:(o + 1) * m1, half + o * m1:half + (o + 1) * m1] = -np.sin(tf)
    mats = [jnp.asarray(m, jnp.bfloat16) for m in (w1, w2, w2.T, w4)]
    tabs = [jnp.asarray(m, jnp.float32) for m in (tc, ts, tc.T, ts.T)]
    return mats, tabs, half, jnp.asarray(w1f, jnp.bfloat16)


def _mxu(a, w):
    return jnp.dot(a.astype(jnp.bfloat16), w, preferred_element_type=jnp.float32)


def _fft_forward(x0, w1, w2, tc, ts, half):
    cw = x0.shape[1] // LANE
    a1 = _mxu(x0.T, w1)
    lhs2 = []
    for c in range(cw):
        ar = a1[c * LANE:(c + 1) * LANE, :half]
        ai = a1[c * LANE:(c + 1) * LANE, half:]
        lhs2.append(jnp.concatenate([(ar * tc + ai * ts).T, (ai * tc - ar * ts).T], axis=1))
    return _mxu(jnp.concatenate(lhs2, axis=0), w2)


def _hyconv_kernel(z_ref, g_ref, kre_ref, kim_ref, brep_ref, w1_ref, w2_ref, w2i_ref, w4_ref, tc_ref, ts_ref,
                   tct_ref, tst_ref, o_ref, *, half):
    x0 = z_ref[...]
    cw = x0.shape[1] // LANE
    tct, tst = tct_ref[...], tst_ref[...]
    s = _fft_forward(x0, w1_ref[...], w2_ref[...], tc_ref[...], ts_ref[...], half)
    sr, si = s[:, :LANE], s[:, LANE:]
    m1 = kre_ref.shape[2]
    pairs = half // m1
    kr = jnp.concatenate([kre_ref[:, 0]] * pairs, axis=1).reshape(cw * half, LANE)
    ki = jnp.concatenate([kim_ref[:, 0]] * pairs, axis=1).reshape(cw * half, LANE)
    y2 = jnp.concatenate([sr * kr - si * ki, sr * ki + si * kr], axis=1)
    bq = _mxu(y2, w2i_ref[...])
    lhs4 = []
    for c in range(cw):
        br = bq[c * half:(c + 1) * half, :LANE]
        bi = bq[c * half:(c + 1) * half, LANE:]
        lhs4.append(jnp.concatenate([(br * tct - bi * tst).T, (bi * tct + br * tst).T], axis=1))
    lhs4 = jnp.concatenate(lhs4, axis=0)
    y = _mxu(lhs4, w4_ref[...]).T
    o_ref[...] = g_ref[...] * (y + x0 * brep_ref[...])


def hyena_conv(ut, z_src, z_col, g_col, kre, kim, order, bias, bsz, cw):
    r = ut.shape[0]
    n1 = r // bsz
    c = bias.shape[0]
    mats, tabs, half, _ = _fft_constants(bsz, n1)
    blk = cw * LANE
    zb, gb = z_col // cw, g_col // cw
    brep = jnp.repeat(bias, LANE).reshape(1, c * LANE)
    kspec = pl.BlockSpec((cw, 1, 2 * n1, LANE), lambda j: (j, order, 0, 0))

    def const(a):
        return pl.BlockSpec(a.shape, lambda j: (0, 0))

    return pl.pallas_call(
        functools.partial(_hyconv_kernel, half=half),
        out_shape=jax.ShapeDtypeStruct((r, c * LANE), jnp.float32),
        grid=(c // cw,),
        in_specs=[
            pl.BlockSpec((r, blk), lambda j: (0, zb + j)),
            pl.BlockSpec((r, blk), lambda j: (0, gb + j)),
            kspec, kspec,
            pl.BlockSpec((1, blk), lambda j: (0, j)),
        ] + [const(a) for a in mats + tabs],
        out_specs=pl.BlockSpec((r, blk), lambda j: (0, j)),
        compiler_params=_cparams("parallel"),
        name="hyena_conv",
    )(z_src, ut, kre, kim, brep, *mats, *tabs)


_HI = lax.Precision.HIGHEST


def _filtgen_kernel(a0_ref, wc_ref, ws_ref, fb1_ref, freq_ref, fw2t_ref, fb2_ref, fw3t_ref, dl_ref, o_ref, *,
                    seq, bands):
    f32 = jnp.float32
    s = pl.program_id(0)
    t = s * LANE + lax.broadcasted_iota(jnp.int32, (1, LANE), 1)
    pos = jnp.where(t < seq, t, 2 * seq - t).astype(f32)
    tt = pos / (seq - 1.0)
    ang = (2.0 * math.pi / seq) * pos
    j = lax.broadcasted_iota(jnp.int32, (bands, 1), 0).astype(f32)
    fj = 1e-4 + j * ((bands - 1.0 - 1e-4) / (bands - 1.0))
    fa = fj * ang
    freq = freq_ref[...]
    pre = a0_ref[...] * tt + jnp.dot(wc_ref[...], jnp.cos(fa), precision=_HI, preferred_element_type=f32) \
        - jnp.dot(ws_ref[...], jnp.sin(fa), precision=_HI, preferred_element_type=f32) + fb1_ref[...]
    h1 = jnp.sin(freq * pre)
    h2 = jnp.sin(freq * (jnp.dot(fw2t_ref[...], h1, precision=_HI, preferred_element_type=f32) + fb2_ref[...]))
    orders, _, c, hid = fw3t_ref.shape
    w3 = fw3t_ref[:, 0].reshape(orders * c, hid)
    h = jnp.dot(w3, h2, precision=_HI, preferred_element_type=f32)
    h = h * jnp.exp(-tt * dl_ref[...])
    h = jnp.where(t == seq, 0.0, h)
    o_ref[:, 0] = h.reshape(orders, c, LANE)


def hyena_filter_taps(seq, fw1, fb1, freq, fw2, fb2, fw3, hy_w):
    hid = fw2.shape[0]
    bands = (HY_POS_EMB - 1) // 2
    orders = fw3.shape[1] // (2 * hy_w)
    slabs = 2 * seq // LANE
    col = lambda v: v.reshape(-1, 1)
    fw3t = fw3.T.reshape(orders, 2, hy_w, hid)
    deltas = jnp.abs(jnp.linspace(math.log(HY_TARGET) / HY_SLOW, math.log(HY_TARGET) / HY_FAST, hy_w,
                                  dtype=jnp.float32))
    dl = jnp.tile(deltas, orders).reshape(-1, 1)

    def whole(a):
        return pl.BlockSpec(a.shape, lambda s: (0,) * a.ndim)

    args = [col(fw1[0]), fw1[1:1 + bands].T, fw1[1 + bands:].T, col(fb1), col(freq), fw2.T, col(fb2)]
    return pl.pallas_call(
        functools.partial(_filtgen_kernel, seq=seq, bands=bands),
        out_shape=jax.ShapeDtypeStruct((orders, slabs, hy_w, LANE), jnp.float32),
        grid=(slabs,),
        in_specs=[whole(a) for a in args] + [
            pl.BlockSpec((orders, 1, hy_w, hid), lambda s: (0, s // (slabs // 2), 0, 0)),
            whole(dl),
        ],
        out_specs=pl.BlockSpec((orders, 1, hy_w, LANE), lambda s: (0, s, 0, 0)),
        compiler_params=_cparams("parallel"),
        name="hyena_filter_taps",
    )(*args, fw3t, dl)


def _lane_block_abs_norm(x, orders):
    rows_per = x.shape[0] // orders
    cw = x.shape[1] // LANE
    out = []
    for o in range(orders):
        xo = x[o * rows_per:(o + 1) * rows_per]
        a = jnp.sum(jnp.abs(xo), axis=0, keepdims=True)
        inv = [jnp.broadcast_to(1.0 / (jnp.sum(a[:, c * LANE:(c + 1) * LANE], axis=1, keepdims=True) + EPS),
                                (1, LANE)) for c in range(cw)]
        out.append(xo * jnp.concatenate(inv, axis=1))
    return jnp.concatenate(out, axis=0)


def _filtspec_kernel(k_ref, w1_ref, w2_ref, tc_ref, ts_ref, re_ref, im_ref, *, half, orders):
    m1 = half // orders
    x0 = _lane_block_abs_norm(k_ref[...], orders)
    s = _fft_forward(x0, w1_ref[...], w2_ref[...], tc_ref[...], ts_ref[...], half)
    cw = x0.shape[1] // LANE
    re_ref[...] = s[:, :LANE].reshape(cw, orders, m1, LANE)
    im_ref[...] = s[:, LANE:].reshape(cw, orders, m1, LANE)


def hyena_filter_spectrum(taps, bsz, cw):
    orders, m1, c, _ = taps.shape
    mats, tabs, half, w1f = _fft_constants(bsz, m1 // 2)
    assert half == orders * m1
    consts = [w1f, mats[1], tabs[0], tabs[1]]
    out = jax.ShapeDtypeStruct((c, orders, m1, LANE), jnp.float32)
    ospec = pl.BlockSpec((cw, orders, m1, LANE), lambda j: (j, 0, 0, 0))
    return pl.pallas_call(
        functools.partial(_filtspec_kernel, half=half, orders=orders),
        out_shape=(out, out),
        grid=(c // cw,),
        in_specs=[pl.BlockSpec((orders * m1, cw * LANE), lambda j: (0, j))] +
                 [pl.BlockSpec(a.shape, lambda j: (0, 0)) for a in consts],
        out_specs=(ospec, ospec),
        compiler_params=_cparams("parallel"),
        name="hyena_filter_spectrum",
    )(taps.reshape(orders * m1, c * LANE), *consts)


def _dense_dft_constants(seq):
    n = 2 * seq
    t = np.arange(seq)[:, None]
    f = np.arange(n)[None, :]
    ps = 2.0 * np.pi * t * f / n
    fwd = np.block([[np.cos(ps), -np.sin(ps)], [np.sin(ps), np.cos(ps)]])
    inv = fwd.T / n
    tk = np.arange(n)[:, None]
    pk = 2.0 * np.pi * tk * f / n
    fk = np.concatenate([np.cos(pk), -np.sin(pk)], axis=1)
    return [jnp.asarray(m, jnp.bfloat16) for m in (fwd, inv, fk)]


def _ctxspec_kernel(k_ref, fk_ref, re_ref, im_ref):
    orders, slabs = k_ref.shape[0], k_ref.shape[1]
    n = slabs * LANE
    for o in range(orders):
        x = jnp.concatenate([k_ref[o, s] for s in range(slabs)], axis=1)
        x = x / (jnp.sum(jnp.abs(x), axis=1, keepdims=True) + EPS)
        kf = _mxu(x, fk_ref[...])
        re_ref[o] = kf[:, :n]
        im_ref[o] = kf[:, n:]


def ctx_filter_spectrum(taps, cw):
    orders, slabs, c, _ = taps.shape
    n = slabs * LANE
    fk = _dense_dft_constants(n // 2)[2]
    out = jax.ShapeDtypeStruct((orders, c, n), jnp.float32)
    ospec = pl.BlockSpec((orders, cw, n), lambda j: (0, j, 0))
    return pl.pallas_call(
        _ctxspec_kernel,
        out_shape=(out, out),
        grid=(c // cw,),
        in_specs=[pl.BlockSpec((orders, slabs, cw, LANE), lambda j: (0, 0, j, 0)),
                  pl.BlockSpec(fk.shape, lambda j: (0, 0))],
        out_specs=(ospec, ospec),
        compiler_params=_cparams("parallel"),
        name="ctx_filter_spectrum",
    )(taps, fk)


def _ctxconv_kernel(z_ref, g_ref, kre_ref, kim_ref, b_ref, fwd_ref, inv_ref, o_ref):
    bsz, slabs, cw = z_ref.shape[0], z_ref.shape[1], z_ref.shape[2]
    seq = slabs * LANE
    zs = [jnp.concatenate([z_ref[b, s] for s in range(slabs)], axis=1) for b in range(bsz)]
    lhs = jnp.concatenate([jnp.concatenate([zs[2 * p], zs[2 * p + 1]], axis=1) for p in range(bsz // 2)], axis=0)
    s = _mxu(lhs, fwd_ref[...])
    n = 2 * seq
    sr, si = s[:, :n], s[:, n:]
    kr = jnp.concatenate([kre_ref[0]] * (bsz // 2), axis=0)
    ki = jnp.concatenate([kim_ref[0]] * (bsz // 2), axis=0)
    y = _mxu(jnp.concatenate([sr * kr - si * ki, sr * ki + si * kr], axis=1), inv_ref[...])
    bias = jnp.concatenate([b_ref[...]] * slabs, axis=1)
    for b in range(bsz):
        p, m = b // 2, b % 2
        yb = y[p * cw:(p + 1) * cw, m * seq:(m + 1) * seq]
        gate = jnp.concatenate([g_ref[b, s] for s in range(slabs)], axis=1)
        res = gate * (yb + zs[b] * bias)
        for sl in range(slabs):
            o_ref[b, sl] = res[:, sl * LANE:(sl + 1) * LANE]


def ctx_hyena_conv(ut, z_src, z_col, g_col, kre, kim, order, bias, cw):
    bsz, slabs, _, _ = ut.shape
    c = bias.shape[0]
    n = 2 * slabs * LANE
    fwd, inv, _ = _dense_dft_constants(slabs * LANE)
    zb, gb = z_col // cw, g_col // cw
    bb = jnp.broadcast_to(bias[:, None], (c, LANE))
    kspec = pl.BlockSpec((1, cw, n), lambda j: (order, j, 0))
    return pl.pallas_call(
        _ctxconv_kernel,
        out_shape=jax.ShapeDtypeStruct((bsz, slabs, c, LANE), jnp.float32),
        grid=(c // cw,),
        in_specs=[
            pl.BlockSpec((bsz, slabs, cw, LANE), lambda j: (0, 0, zb + j, 0)),
            pl.BlockSpec((bsz, slabs, cw, LANE), lambda j: (0, 0, gb + j, 0)),
            kspec, kspec,
            pl.BlockSpec((cw, LANE), lambda j: (j, 0)),
            pl.BlockSpec(fwd.shape, lambda j: (0, 0)),
            pl.BlockSpec(inv.shape, lambda j: (0, 0)),
        ],
        out_specs=pl.BlockSpec((bsz, slabs, cw, LANE), lambda j: (0, 0, j, 0)),
        compiler_params=_cparams("parallel"),
        name="ctx_hyena_conv",
    )(z_src, ut, kre, kim, bb, fwd, inv)


def _mlp_kernel(x_ref, nw_ref, sc_ref, sh_ref, g_ref, w1_ref, w2_ref, fw_ref, o_ref, h_ref, *, final_norm):
    f = pl.program_id(2)

    @pl.when(f == 0)
    def _():
        h_ref[...] = _normmod(x_ref[0], nw_ref[...], sc_ref[0], sh_ref[0]).astype(jnp.bfloat16)

    a = jnp.dot(h_ref[...], w1_ref[...], preferred_element_type=jnp.float32)
    a = jnp.square(jnp.maximum(a, 0.0)).astype(jnp.bfloat16)
    part = jnp.dot(a, w2_ref[...], preferred_element_type=jnp.float32)

    @pl.when(f == 0)
    def _():
        o_ref[0] = part

    @pl.when(f > 0)
    def _():
        o_ref[0] += part

    @pl.when(f == pl.num_programs(2) - 1)
    def _():
        y = x_ref[0] + g_ref[0] * o_ref[0]
        if final_norm:
            ms = jnp.mean(y * y, axis=-1, keepdims=True)
            y = y * lax.rsqrt(ms + EPS) * fw_ref[...]
        o_ref[0] = y


def mlp_residual(x, nw, sc, sh, g, w1, w2, fw, tm, tf, final_norm):
    b, l, d = x.shape
    dff = w1.shape[1]
    vec = pl.BlockSpec((1, 1, d), lambda i, j, k: (i, 0, 0))
    row = pl.BlockSpec((1, d), lambda i, j, k: (0, 0))
    return pl.pallas_call(
        functools.partial(_mlp_kernel, final_norm=final_norm),
        out_shape=jax.ShapeDtypeStruct((b, l, d), jnp.float32),
        grid=(b, l // tm, dff // tf),
        in_specs=[
            pl.BlockSpec((1, tm, d), lambda i, j, k: (i, j, 0)),
            row, vec, vec, vec,
            pl.BlockSpec((d, tf), lambda i, j, k: (0, k)),
            pl.BlockSpec((tf, d), lambda i, j, k: (k, 0)),
            row,
        ],
        out_specs=pl.BlockSpec((1, tm, d), lambda i, j, k: (i, j, 0)),
        scratch_shapes=[pltpu.VMEM((tm, d), jnp.bfloat16)],
        compiler_params=_cparams("parallel", "parallel", "arbitrary"),
        name="mlp_residual",
    )(x, nw.reshape(1, d), sc.reshape(b, 1, d), sh.reshape(b, 1, d), g.reshape(b, 1, d), w1, w2,
      fw.reshape(1, d))


def _split3(v):
    f32, bf16 = jnp.float32, jnp.bfloat16
    hi = v.astype(bf16)
    r1 = v - hi.astype(f32)
    mid = r1.astype(bf16)
    lo = (r1 - mid.astype(f32)).astype(bf16)
    return jnp.concatenate([hi, mid, lo], axis=1)


def _lane_repeat(v, rep):
    h = v.shape[1]
    row = lax.broadcasted_iota(jnp.int32, (3 * h, h * rep), 0) % h
    col = lax.broadcasted_iota(jnp.int32, (3 * h, h * rep), 1) // rep
    e = (row == col).astype(jnp.bfloat16)
    return jnp.dot(_split3(v), e, preferred_element_type=jnp.float32)


def _silu(v):
    return v * jax.nn.sigmoid(v)


def _softplus(v):
    return jnp.maximum(v, 0.0) + jnp.log1p(jnp.exp(-jnp.abs(v)))


def _token_conv3(u, w, bias, period):
    t, c = u.shape
    pos = lax.broadcasted_iota(jnp.int32, (t, c), 0) % period
    up = jnp.where(pos == 0, 0.0, pltpu.roll(u, 1, 0))
    dn = jnp.where(pos == period - 1, 0.0, pltpu.roll(u, t - 1, 0))
    return bias + w[0:1] * up + w[1:2] * u + w[2:3] * dn


def _ssd_chunk(xs, bm, cm, dtr, s_ref, a_row, dtb_row, reverse, heads, hpg):
    f32, bf16 = jnp.float32, jnp.bfloat16
    q = xs.shape[0]
    p = SSD_HEADDIM
    n = SSD_STATE
    gw = hpg * p
    dt = _softplus(dtr + dtb_row)
    a = dt * a_row
    ri = lax.broadcasted_iota(jnp.int32, (q, q), 0)
    ci = lax.broadcasted_iota(jnp.int32, (q, q), 1)
    keep = (ci >= ri) if reverse else (ci <= ri)
    a3 = jnp.dot(keep.astype(bf16), _split3(a), preferred_element_type=f32)
    cs = a3[:, :heads] + a3[:, heads:2 * heads] + a3[:, 2 * heads:]
    dt_rep = _lane_repeat(dt, p)
    cs_rep = _lane_repeat(cs, p)
    cs_wide = _lane_repeat(cs, q)
    end = 0 if reverse else q - 1
    cs_end = cs_rep[end:end + 1]
    xdt = xs * dt_rep
    xw = (xdt * jnp.exp(cs_end - cs_rep)).astype(bf16)
    ecs = jnp.exp(cs_rep)
    chunk_decay = jnp.exp(cs_end)
    xdt_b = xdt.astype(bf16)
    lane = lax.broadcasted_iota(jnp.int32, (q, 2 * p), 1)
    ys = []
    for g in range(SSD_GROUPS):
        bg = bm[:, g * n:(g + 1) * n].astype(bf16)
        cg = cm[:, g * n:(g + 1) * n].astype(bf16)
        cb = lax.dot_general(cg, bg, (((1,), (1,)), ((), ())), preferred_element_type=f32)
        s_old = s_ref[0, g]
        y_off = jnp.dot(cg, s_old.astype(bf16), preferred_element_type=f32) * ecs[:, g * gw:(g + 1) * gw]
        for pr in range(hpg // 2):
            xpair = xdt_b[:, g * gw + pr * 2 * p:g * gw + (pr + 1) * 2 * p]
            acc = None
            for k in range(2):
                h = g * hpg + pr * 2 + k
                csr = cs_wide[:, h * q:(h + 1) * q]
                seg = csr - csr.T
                gmat = (cb * jnp.where(keep, jnp.exp(seg), 0.0)).astype(bf16)
                xh = jnp.where((lane // p) == k, xpair, jnp.zeros_like(xpair))
                part = jnp.dot(gmat, xh, preferred_element_type=f32)
                acc = part if acc is None else acc + part
            lo = pr * 2 * p
            ys.append(acc + y_off[:, lo:lo + 2 * p])
        upd = jnp.dot(bg.T, xw[:, g * gw:(g + 1) * gw], preferred_element_type=f32)
        s_ref[0, g] = s_old * chunk_decay[:, g * gw:(g + 1) * gw] + upd
    return jnp.concatenate(ys, axis=1)


def _ssd_kernel(z_ref, xs_ref, bc_ref, dt_ref, cwx_ref, cbx_ref, cwb_ref, cbb_ref, dtb_ref, alog_ref, drep_ref,
                nw_ref, initf_ref, initb_ref, y_ref, sf_ref, sb_ref, yb_ref, *, nb, tb, period, heads, hpg):
    j = pl.program_id(1)
    q = SSD_CHUNK
    n = SSD_STATE
    gn = SSD_GROUPS * n
    nchunk = tb // q
    ssd_w = heads * SSD_HEADDIM

    @pl.when(j == 0)
    def _():
        sb_ref[...] = initb_ref[...]

    @pl.when(j == nb)
    def _():
        sf_ref[...] = initf_ref[...]

    xs_all = _silu(_token_conv3(xs_ref[0], cwx_ref[...], cbx_ref[...], period))
    bc_all = _silu(_token_conv3(bc_ref[0], cwb_ref[...], cbb_ref[...], period))
    a_all = -jnp.exp(alog_ref[...])

    def run(reverse):
        d = 1 if reverse else 0
        blk = (nb - 1 - j) if reverse else (j - nb)
        order = range(nchunk - 1, -1, -1) if reverse else range(nchunk)
        for ci in order:
            sl = slice(ci * q, (ci + 1) * q)
            y = _ssd_chunk(xs_all[sl], bc_all[sl, :gn], bc_all[sl, gn:], dt_ref[0, sl, d * heads:(d + 1) * heads],
                           sb_ref if reverse else sf_ref, a_all[d:d + 1], dtb_ref[d:d + 1], reverse, heads, hpg)
            row0 = pl.multiple_of(blk * tb + ci * q, q)
            if reverse:
                yb_ref[pl.ds(row0, q), :] = y
            else:
                y = y + yb_ref[pl.ds(row0, q), :] + drep_ref[...] * xs_all[sl]
                y = y * _silu(z_ref[0, sl, :])
                gw = ssd_w // SSD_GROUPS
                outs = []
                for g in range(SSD_GROUPS):
                    yg = y[:, g * gw:(g + 1) * gw]
                    ms = jnp.mean(yg * yg, axis=-1, keepdims=True)
                    outs.append(yg * lax.rsqrt(ms + EPS) * nw_ref[:, g * gw:(g + 1) * gw])
                y_ref[0, sl, :] = jnp.concatenate(outs, axis=1)

    @pl.when(j < nb)
    def _():
        run(True)

    @pl.when(j >= nb)
    def _():
        run(False)


def ssd_mixer(srcs, cols, conv_w_x, conv_b_x, conv_w_bc, conv_b_bc, dt_bias, a_log, ssd_d, norm_w, init_f, init_b,
              period, tb):
    b, l, _ = srcs[0].shape
    heads = dt_bias.shape[1]
    hpg = heads // SSD_GROUPS
    ssd_w = heads * SSD_HEADDIM
    gn = SSD_GROUPS * SSD_STATE
    nb = l // tb
    oz, ox, obc, odt = cols
    f32 = jnp.float32

    def tok(width, off):
        blk_idx = off // width
        return pl.BlockSpec((1, tb, width),
                            lambda i, j: (i, jnp.where(j < nb, nb - 1 - j, j - nb), blk_idx))

    def tok_fwd(width, off):
        blk_idx = off // width
        return pl.BlockSpec((1, tb, width), lambda i, j: (i, jnp.where(j < nb, 0, j - nb), blk_idx))

    def whole(shape):
        return pl.BlockSpec(shape, lambda i, j: (0,) * len(shape))

    st_shape = (b, SSD_GROUPS, SSD_STATE, hpg * SSD_HEADDIM)
    st_spec = pl.BlockSpec((1,) + st_shape[1:], lambda i, j: (i, 0, 0, 0))
    drep = jnp.repeat(ssd_d, SSD_HEADDIM).reshape(1, ssd_w)
    kern = functools.partial(_ssd_kernel, nb=nb, tb=tb, period=period, heads=heads, hpg=hpg)
    return pl.pallas_call(
        kern,
        out_shape=(jax.ShapeDtypeStruct((b, l, ssd_w), f32), jax.ShapeDtypeStruct(st_shape, f32),
                   jax.ShapeDtypeStruct(st_shape, f32)),
        grid=(b, 2 * nb),
        in_specs=[
            tok_fwd(ssd_w, oz), tok(ssd_w, ox), tok(2 * gn, obc), tok(LANE, odt),
            whole((3, ssd_w)), whole((1, ssd_w)), whole((3, 2 * gn)), whole((1, 2 * gn)),
            whole((2, heads)), whole((2, heads)), whole((1, ssd_w)), whole((1, ssd_w)),
            st_spec, st_spec,
        ],
        out_specs=(pl.BlockSpec((1, tb, ssd_w), lambda i, j: (i, jnp.where(j < nb, 0, j - nb), 0)),
                   st_spec, st_spec),
        scratch_shapes=[pltpu.VMEM((l, ssd_w), f32)],
        compiler_params=_cparams("parallel", "arbitrary"),
        name="ssd_mixer",
    )(*srcs, conv_w_x, conv_b_x.reshape(1, -1), conv_w_bc, conv_b_bc.reshape(1, -1), dt_bias, a_log, drep,
      norm_w.reshape(1, ssd_w), init_f, init_b)


def _pick(l, pref):
    return pref if l % pref == 0 else l


def kernel(x, c, ctx, c_ctx, w_ada, b_ada, norm1_w, w_in, hy_conv_w, hy_conv_b, filt_w1, filt_b1, filt_freq,
           filt_w2, filt_b2, filt_w3, hy_bias, hy_norm_w, ssd_conv_w, ssd_conv_b, dt_bias, a_log, ssd_d,
           ssd_norm_w, w_out, norm2_w, w_mlp1, w_mlp2, final_norm_w):
    depth = w_in.shape[0]
    bsz, seq, d = x.shape
    hy_w = hy_norm_w.shape[1]
    hy_proj = hy_conv_w.shape[2]
    ssd_w = ssd_norm_w.shape[1]
    ssd_xbc = ssd_conv_w.shape[2]
    heads = ssd_w // SSD_HEADDIM
    hpg = heads // SSD_GROUPS
    ssd_dt = 2 * heads
    o0, o1 = hy_proj, hy_proj + ssd_xbc
    o2 = o1 + ssd_dt
    dt_pad = (-ssd_dt) % LANE
    bf16 = jnp.bfloat16
    ssd_cols = (0, o0, o0 + ssd_w, ssd_w)
    tn = o1 // 3

    n1 = seq // LANE
    n1c = ctx.shape[1] // LANE
    zero_state = jnp.zeros((bsz, SSD_GROUPS, SSD_STATE, hpg * SSD_HEADDIM), jnp.float32)

    rows = 8
    c_all = jnp.zeros((rows, d), jnp.float32).at[:bsz].set(c).at[bsz].set(c_ctx)
    mod_all = ada_mod(c_all, w_ada, b_ada)

    h_ctx = ctx
    tm_l = _pick(seq, 512)
    tm_c = _pick(ctx.shape[1], 256)
    for i in range(depth):
        lp = dict(hy_conv_w=hy_conv_w[i], hy_conv_b=hy_conv_b[i], filt_w1=filt_w1[i], filt_b1=filt_b1[i],
                  filt_freq=filt_freq[i], filt_w2=filt_w2[i], filt_b2=filt_b2[i], filt_w3=filt_w3[i],
                  hy_bias=hy_bias[i], hy_norm_w=hy_norm_w[i], ssd_conv_w=ssd_conv_w[i],
                  ssd_conv_b=ssd_conv_b[i], dt_bias=dt_bias[i], a_log=a_log[i], ssd_d=ssd_d[i],
                  ssd_norm_w=ssd_norm_w[i])
        wi = w_in[i]
        w_a = wi[:, :o1].astype(bf16)
        w_b = jnp.concatenate([wi[:, o2:], wi[:, o1:o2], jnp.zeros((d, dt_pad), wi.dtype)], axis=1).astype(bf16)
        w_o_hy = w_out[i, :hy_w].astype(bf16)
        w_o_ssd = w_out[i, hy_w:].astype(bf16)
        w1 = w_mlp1[i].astype(bf16)
        w2 = w_mlp2[i].astype(bf16)

        mod = mod_all[i, :bsz].reshape(bsz, N_MOD, d)
        mod_c = jnp.broadcast_to(mod_all[i, bsz].reshape(1, N_MOD, d), (bsz, N_MOD, d))
        def ssd(pa, pb, init_f, init_b, period):
            cw, cb = lp['ssd_conv_w'], lp['ssd_conv_b']
            return ssd_mixer((pb, pa, pa, pb), ssd_cols, cw[:, :ssd_w], cb[:ssd_w], cw[:, ssd_w:], cb[ssd_w:],
                             lp['dt_bias'], lp['a_log'], lp['ssd_d'], lp['ssd_norm_w'], init_f, init_b, period,
                             _pick(pa.shape[1], 256))

        def filt_taps(length):
            return hyena_filter_taps(length, lp['filt_w1'], lp['filt_b1'], lp['filt_freq'], lp['filt_w2'],
                                     lp['filt_b2'], lp['filt_w3'], hy_w)

        pca, pcb = normmod_matmul(h_ctx, norm1_w[i], mod_c[:, 1], mod_c[:, 0], w_a, w_b, tm_c, tn)
        y_ssd, s_f, s_b = ssd(pca, pcb, zero_state, zero_state, pca.shape[1])
        if i < depth - 1:
            utc = hyena_pre(pca, 0, lp['hy_conv_w'], lp['hy_conv_b'], pca.shape[1], pca.shape[1], 512)
            kre, kim = ctx_filter_spectrum(filt_taps(pca.shape[1]), 128)
            zt = utc
            for o in range(HY_ORDER):
                zt = ctx_hyena_conv(utc, zt, 0, (o + 1) * hy_w, kre, kim, o, lp['hy_bias'][o], 128)
            h_ctx = proj_residual(zt, y_ssd, lp['hy_norm_w'], w_o_hy, w_o_ssd, h_ctx, mod_c[:, 2], tm_c,
                                  HY_GROUPS)
            h_ctx = mlp_residual(h_ctx, norm2_w[i], mod_c[:, 4], mod_c[:, 3], mod_c[:, 5], w1, w2,
                                 final_norm_w, tm_c, 512, False)
        pla, plb = normmod_matmul(x, norm1_w[i], mod[:, 1], mod[:, 0], w_a, w_b, tm_l, tn)
        ut = hyena_pre(pla, 0, lp['hy_conv_w'], lp['hy_conv_b'], GRID_W, 512, 512)
        ut = ut.reshape(bsz * n1, hy_proj * LANE)
        kre, kim = hyena_filter_spectrum(filt_taps(seq), bsz, 16)
        zt = ut
        for o in range(HY_ORDER):
            zt = hyena_conv(ut, zt, 0, (o + 1) * hy_w, kre, kim, o, lp['hy_bias'][o], bsz, 16)
        zh_t = zt.reshape(bsz, n1, hy_w, LANE)
        y_ssd, _, _ = ssd(pla, plb, s_f, s_b, GRID_W)
        x = proj_residual(zh_t, y_ssd, lp['hy_norm_w'], w_o_hy, w_o_ssd, x, mod[:, 2], tm_l, HY_GROUPS)
        x = mlp_residual(x, norm2_w[i], mod[:, 4], mod[:, 3], mod[:, 5], w1, w2, final_norm_w, tm_l, 512,
                         i == depth - 1)
    return x
```

```python
import functools
import math

import numpy as np
import jax
import jax.numpy as jnp
from jax import lax
from jax.experimental import pallas as pl
from jax.experimental.pallas import tpu as pltpu

EPS = 1e-6
N_MOD = 6
GRID_W = 64
HY_GROUPS = 8
HY_ORDER = 2
HY_POS_EMB = 33
HY_TARGET = 1e-2
HY_FAST = 0.3
HY_SLOW = 1.5
SSD_HEADDIM = 64
SSD_GROUPS = 2
SSD_STATE = 128
SSD_CHUNK = 128

LANE = 128
VMEM_LIMIT = 56 * 1024 * 1024


def _cparams(*sem):
    return pltpu.CompilerParams(dimension_semantics=sem, vmem_limit_bytes=VMEM_LIMIT)


def _ada_kernel(c_ref, w_ref, b_ref, o_ref):
    c = c_ref[...]
    a = (c * jax.nn.sigmoid(c)).astype(jnp.bfloat16)
    w = w_ref[0].astype(jnp.bfloat16)
    o_ref[0] = jnp.dot(a, w, preferred_element_type=jnp.float32) + b_ref[0]


def ada_mod(c_all, w_ada, b_ada, tn=1024):
    depth, d, n = w_ada.shape
    r = c_all.shape[0]
    return pl.pallas_call(
        _ada_kernel,
        out_shape=jax.ShapeDtypeStruct((depth, r, n), jnp.float32),
        grid=(depth, n // tn),
        in_specs=[
            pl.BlockSpec((r, d), lambda i, j: (0, 0)),
            pl.BlockSpec((1, d, tn), lambda i, j: (i, 0, j)),
            pl.BlockSpec((1, 1, tn), lambda i, j: (i, 0, j)),
        ],
        out_specs=pl.BlockSpec((1, r, tn), lambda i, j: (i, 0, j)),
        compiler_params=_cparams("parallel", "parallel"),
        name="ada_mod",
    )(c_all, w_ada, b_ada.reshape(depth, 1, n))


def _wcast_kernel(w_ref, a_ref, b_ref, *, na, c0, c1):
    bf16 = jnp.bfloat16
    w = w_ref[0]
    a_ref[0] = w[:, :na].astype(bf16)
    nb = b_ref.shape[2]
    pad = nb - (c1 - c0) - (c0 - na)
    b_ref[0] = jnp.concatenate([w[:, c0:c1], w[:, na:c0], jnp.zeros((w.shape[0], pad), w.dtype)], axis=1).astype(bf16)


def cast_in_proj(w_in, na, c0, c1, nb, tr=256):
    depth, d, n = w_in.shape
    return pl.pallas_call(
        functools.partial(_wcast_kernel, na=na, c0=c0, c1=c1),
        out_shape=(jax.ShapeDtypeStruct((depth, d, na), jnp.bfloat16),
                   jax.ShapeDtypeStruct((depth, d, nb), jnp.bfloat16)),
        grid=(depth, d // tr),
        in_specs=[pl.BlockSpec((1, tr, n), lambda i, j: (i, j, 0))],
        out_specs=(pl.BlockSpec((1, tr, na), lambda i, j: (i, j, 0)),
                   pl.BlockSpec((1, tr, nb), lambda i, j: (i, j, 0))),
        compiler_params=_cparams("parallel", "parallel"),
        name="cast_in_proj",
    )(w_in)


NORM_ROWS = 16


def _normmod_store(h_ref, x_ref, nw, sc, sh):
    w = nw * (1.0 + sc)

    def body(i, carry):
        r0 = pl.multiple_of(i * NORM_ROWS, NORM_ROWS)
        xb = x_ref[0, pl.ds(r0, NORM_ROWS), :]
        ms = jnp.mean(xb * xb, axis=-1, keepdims=True)
        h_ref[pl.ds(r0, NORM_ROWS), :] = ((xb * lax.rsqrt(ms + EPS)) * w + sh).astype(h_ref.dtype)
        return carry

    lax.fori_loop(0, h_ref.shape[0] // NORM_ROWS, body, 0, unroll=8)


def _nm_matmul_kernel(x_ref, nw_ref, sc_ref, sh_ref, wa_ref, wb_ref, oa_ref, ob_ref, h_ref, *, na):
    k = pl.program_id(2)

    @pl.when(k == 0)
    def _():
        _normmod_store(h_ref, x_ref, nw_ref[...], sc_ref[0], sh_ref[0])

    @pl.when(k < na)
    def _():
        oa_ref[0] = jnp.dot(h_ref[...], wa_ref[...], preferred_element_type=jnp.float32)

    @pl.when(k >= na)
    def _():
        ob_ref[0] = jnp.dot(h_ref[...], wb_ref[...], preferred_element_type=jnp.float32)


def normmod_matmul(x, nw, sc, sh, wa, wb, tm, tn):
    b, l, d = x.shape
    n_a, n_b = wa.shape[1], wb.shape[1]
    na = n_a // tn
    last = na - 1
    return pl.pallas_call(
        functools.partial(_nm_matmul_kernel, na=na),
        out_shape=(jax.ShapeDtypeStruct((b, l, n_a), jnp.float32), jax.ShapeDtypeStruct((b, l, n_b), jnp.float32)),
        grid=(b, l // tm, na + 1),
        in_specs=[
            pl.BlockSpec((1, tm, d), lambda i, j, k: (i, j, 0)),
            pl.BlockSpec((1, d), lambda i, j, k: (0, 0)),
            pl.BlockSpec((1, 1, d), lambda i, j, k: (i, 0, 0)),
            pl.BlockSpec((1, 1, d), lambda i, j, k: (i, 0, 0)),
            pl.BlockSpec((d, tn), lambda i, j, k: (0, jnp.minimum(k, last))),
            pl.BlockSpec((d, n_b), lambda i, j, k: (0, 0)),
        ],
        out_specs=(pl.BlockSpec((1, tm, tn), lambda i, j, k: (i, j, jnp.minimum(k, last))),
                   pl.BlockSpec((1, tm, n_b), lambda i, j, k: (i, j, 0))),
        scratch_shapes=[pltpu.VMEM((tm, d), jnp.bfloat16)],
        compiler_params=_cparams("parallel", "parallel", "arbitrary"),
        name="normmod_matmul",
    )(x, nw.reshape(1, d), sc.reshape(b, 1, d), sh.reshape(b, 1, d), wa, wb)


def _proj_res_kernel(zh_ref, ys_ref, nwh_ref, wh_ref, ws_ref, x_ref, g_ref, o_ref, *, groups):
    bf16 = jnp.bfloat16
    slabs, hy_w = zh_ref.shape[1], zh_ref.shape[2]
    gs = hy_w // groups
    rows = []
    for s in range(slabs):
        cols = []
        for gi in range(groups):
            zg = zh_ref[0, s, gi * gs:(gi + 1) * gs, :]
            ms = jnp.mean(zg * zg, axis=0, keepdims=True)
            cols.append((zg * lax.rsqrt(ms + EPS)).T)
        rows.append(jnp.concatenate(cols, axis=1))
    yh = (jnp.concatenate(rows, axis=0) * nwh_ref[...]).astype(bf16)
    acc = jnp.dot(yh, wh_ref[...], preferred_element_type=jnp.float32)
    acc += jnp.dot(ys_ref[0].astype(bf16), ws_ref[...], preferred_element_type=jnp.float32)
    o_ref[0] = x_ref[0] + g_ref[0] * acc


def proj_residual(zh_t, y_ssd, hy_norm_w, w_hy, w_ssd, x, g, tm, groups):
    b, n1, hy_w, _ = zh_t.shape
    l, ssd_w = y_ssd.shape[1], y_ssd.shape[2]
    n = w_hy.shape[1]
    return pl.pallas_call(
        functools.partial(_proj_res_kernel, groups=groups),
        out_shape=jax.ShapeDtypeStruct((b, l, n), jnp.float32),
        grid=(b, l // tm),
        in_specs=[
            pl.BlockSpec((1, tm // LANE, hy_w, LANE), lambda i, j: (i, j, 0, 0)),
            pl.BlockSpec((1, tm, ssd_w), lambda i, j: (i, j, 0)),
            pl.BlockSpec((1, hy_w), lambda i, j: (0, 0)),
            pl.BlockSpec((hy_w, n), lambda i, j: (0, 0)),
            pl.BlockSpec((ssd_w, n), lambda i, j: (0, 0)),
            pl.BlockSpec((1, tm, n), lambda i, j: (i, j, 0)),
            pl.BlockSpec((1, 1, n), lambda i, j: (i, 0, 0)),
        ],
        out_specs=pl.BlockSpec((1, tm, n), lambda i, j: (i, j, 0)),
        compiler_params=_cparams("parallel", "parallel"),
        name="proj_residual",
    )(zh_t, y_ssd, hy_norm_w.reshape(1, hy_w), w_hy, w_ssd, x, g.reshape(b, 1, n))


def _hypre_kernel(u_ref, w_ref, b_ref, o_ref, *, period):
    u = _token_conv3(u_ref[0], w_ref[...], b_ref[...], period)
    tb, cw = u.shape
    for s in range(tb // LANE):
        for j in range(cw // LANE):
            o_ref[0, s, j * LANE:(j + 1) * LANE, :] = u[s * LANE:(s + 1) * LANE, j * LANE:(j + 1) * LANE].T


def hyena_pre(p, col0, conv_w, conv_b, period, tb, cw):
    b, l, _ = p.shape
    c = conv_w.shape[1]
    blk0 = col0 // cw
    return pl.pallas_call(
        functools.partial(_hypre_kernel, period=period),
        out_shape=jax.ShapeDtypeStruct((b, l // LANE, c, LANE), jnp.float32),
        grid=(b, l // tb, c // cw),
        in_specs=[
            pl.BlockSpec((1, tb, cw), lambda i, j, k: (i, j, blk0 + k)),
            pl.BlockSpec((3, cw), lambda i, j, k: (0, k)),
            pl.BlockSpec((1, cw), lambda i, j, k: (0, k)),
        ],
        out_specs=pl.BlockSpec((1, tb // LANE, cw, LANE), lambda i, j, k: (i, j, k, 0)),
        compiler_params=_cparams("parallel", "parallel", "parallel"),
        name="hyena_pre",
    )(p, conv_w, conv_b.reshape(1, c))


def _fft_constants(bsz, n1):
    m1 = 2 * n1
    n = m1 * LANE
    pairs = bsz // 2
    half = pairs * m1
    r = bsz * n1
    t1 = np.arange(n1)[:, None]
    f1 = np.arange(m1)[None, :]
    th = 2.0 * np.pi * t1 * f1 / m1
    w1 = np.zeros((r, 2 * half))
    for pr in range(pairs):
        re = slice(pr * m1, (pr + 1) * m1)
        im = slice(half + pr * m1, half + (pr + 1) * m1)
        ra = slice((2 * pr) * n1, (2 * pr + 1) * n1)
        rb = slice((2 * pr + 1) * n1, (2 * pr + 2) * n1)
        w1[ra, re], w1[ra, im] = np.cos(th), -np.sin(th)
        w1[rb, re], w1[rb, im] = np.sin(th), np.cos(th)
    w4 = w1.T / n
    t2 = np.arange(LANE)[:, None]
    ps = 2.0 * np.pi * t2 * np.arange(LANE)[None, :] / LANE
    w2 = np.block([[np.cos(ps), -np.sin(ps)], [np.sin(ps), np.cos(ps)]])
    ph = 2.0 * np.pi * t2 * f1 / n
    tc = np.tile(np.cos(ph), (1, pairs))
    ts = np.tile(np.sin(ph), (1, pairs))
    orders = half // m1
    tf = 2.0 * np.pi * np.arange(m1)[:, None] * f1 / m1
    w1f = np.zeros((orders * m1, 2 * half))
    for o in range(orders):
        w1f[o * m1:(o + 1) * m1, o * m1:(o + 1) * m1] = np.cos(tf)
        w1f[o * m1:(o + 1) * m1, half + o * m1:half + (o + 1) * m1] = -np.sin(tf)
    mats = [jnp.asarray(m, jnp.bfloat16) for m in (w1, w2, w2.T, w4)]
    tabs = [jnp.asarray(m, jnp.float32) for m in (tc, ts, tc.T, ts.T)]
    return mats, tabs, half, jnp.asarray(w1f, jnp.bfloat16)


def _mxu(a, w):
    return jnp.dot(a.astype(jnp.bfloat16), w, preferred_element_type=jnp.float32)


def _fft_forward(x0, w1, w2, tc, ts, half):
    cw = x0.shape[1] // LANE
    a1 = _mxu(x0.T, w1)
    lhs2 = []
    for c in range(cw):
        ar = a1[c * LANE:(c + 1) * LANE, :half]
        ai = a1[c * LANE:(c + 1) * LANE, half:]
        lhs2.append(jnp.concatenate([(ar * tc + ai * ts).T, (ai * tc - ar * ts).T], axis=1))
    return _mxu(jnp.concatenate(lhs2, axis=0), w2)


def _hyconv_kernel(z_ref, g_ref, kre_ref, kim_ref, brep_ref, w1_ref, w2_ref, w2i_ref, w4_ref, tc_ref, ts_ref,
                   tct_ref, tst_ref, o_ref, *, half):
    x0 = z_ref[...]
    cw = x0.shape[1] // LANE
    tct, tst = tct_ref[...], tst_ref[...]
    s = _fft_forward(x0, w1_ref[...], w2_ref[...], tc_ref[...], ts_ref[...], half)
    sr, si = s[:, :LANE], s[:, LANE:]
    m1 = kre_ref.shape[2]
    pairs = half // m1
    kr = jnp.concatenate([kre_ref[:, 0]] * pairs, axis=1).reshape(cw * half, LANE)
    ki = jnp.concatenate([kim_ref[:, 0]] * pairs, axis=1).reshape(cw * half, LANE)
    y2 = jnp.concatenate([sr * kr - si * ki, sr * ki + si * kr], axis=1)
    bq = _mxu(y2, w2i_ref[...])
    lhs4 = []
    for c in range(cw):
        br = bq[c * half:(c + 1) * half, :LANE]
        bi = bq[c * half:(c + 1) * half, LANE:]
        lhs4.append(jnp.concatenate([(br * tct - bi * tst).T, (bi * tct + br * tst).T], axis=1))
    lhs4 = jnp.concatenate(lhs4, axis=0)
    y = _mxu(lhs4, w4_ref[...]).T
    o_ref[...] = g_ref[...] * (y + x0 * brep_ref[...])


def hyena_conv(ut, z_src, z_col, g_col, kre, kim, order, bias, bsz, cw):
    r = ut.shape[0]
    n1 = r // bsz
    c = bias.shape[0]
    mats, tabs, half, _ = _fft_constants(bsz, n1)
    blk = cw * LANE
    zb, gb = z_col // cw, g_col // cw
    brep = jnp.repeat(bias, LANE).reshape(1, c * LANE)
    kspec = pl.BlockSpec((cw, 1, 2 * n1, LANE), lambda j: (j, order, 0, 0))

    def const(a):
        return pl.BlockSpec(a.shape, lambda j: (0, 0))

    return pl.pallas_call(
        functools.partial(_hyconv_kernel, half=half),
        out_shape=jax.ShapeDtypeStruct((r, c * LANE), jnp.float32),
        grid=(c // cw,),
        in_specs=[
            pl.BlockSpec((r, blk), lambda j: (0, zb + j)),
            pl.BlockSpec((r, blk), lambda j: (0, gb + j)),
            kspec, kspec,
            pl.BlockSpec((1, blk), lambda j: (0, j)),
        ] + [const(a) for a in mats + tabs],
        out_specs=pl.BlockSpec((r, blk), lambda j: (0, j)),
        compiler_params=_cparams("parallel"),
        name="hyena_conv",
    )(z_src, ut, kre, kim, brep, *mats, *tabs)


_HI = lax.Precision.HIGHEST


def _filtgen_kernel(a0_ref, wc_ref, ws_ref, fb1_ref, freq_ref, fw2t_ref, fb2_ref, fw3t_ref, dl_ref, o_ref, *,
                    seq, bands):
    f32 = jnp.float32
    s = pl.program_id(0)
    t = s * LANE + lax.broadcasted_iota(jnp.int32, (1, LANE), 1)
    pos = jnp.where(t < seq, t, 2 * seq - t).astype(f32)
    tt = pos / (seq - 1.0)
    ang = (2.0 * math.pi / seq) * pos
    j = lax.broadcasted_iota(jnp.int32, (bands, 1), 0).astype(f32)
    fj = 1e-4 + j * ((bands - 1.0 - 1e-4) / (bands - 1.0))
    fa = fj * ang
    freq = freq_ref[...]
    pre = a0_ref[...] * tt + jnp.dot(wc_ref[...], jnp.cos(fa), precision=_HI, preferred_element_type=f32) \
        - jnp.dot(ws_ref[...], jnp.sin(fa), precision=_HI, preferred_element_type=f32) + fb1_ref[...]
    h1 = jnp.sin(freq * pre)
    h2 = jnp.sin(freq * (jnp.dot(fw2t_ref[...], h1, precision=_HI, preferred_element_type=f32) + fb2_ref[...]))
    orders, _, c, hid = fw3t_ref.shape
    w3 = fw3t_ref[:, 0].reshape(orders * c, hid)
    h = jnp.dot(w3, h2, precision=_HI, preferred_element_type=f32)
    h = h * jnp.exp(-tt * dl_ref[...])
    h = jnp.where(t == seq, 0.0, h)
    o_ref[:, 0] = h.reshape(orders, c, LANE)


def hyena_filter_taps(seq, fw1, fb1, freq, fw2, fb2, fw3, hy_w):
    hid = fw2.shape[0]
    bands = (HY_POS_EMB - 1) // 2
    orders = fw3.shape[1] // (2 * hy_w)
    slabs = 2 * seq // LANE
    col = lambda v: v.reshape(-1, 1)
    fw3t = fw3.T.reshape(orders, 2, hy_w, hid)
    deltas = jnp.abs(jnp.linspace(math.log(HY_TARGET) / HY_SLOW, math.log(HY_TARGET) / HY_FAST, hy_w,
                                  dtype=jnp.float32))
    dl = jnp.tile(deltas, orders).reshape(-1, 1)

    def whole(a):
        return pl.BlockSpec(a.shape, lambda s: (0,) * a.ndim)

    args = [col(fw1[0]), fw1[1:1 + bands].T, fw1[1 + bands:].T, col(fb1), col(freq), fw2.T, col(fb2)]
    return pl.pallas_call(
        functools.partial(_filtgen_kernel, seq=seq, bands=bands),
        out_shape=jax.ShapeDtypeStruct((orders, slabs, hy_w, LANE), jnp.float32),
        grid=(slabs,),
        in_specs=[whole(a) for a in args] + [
            pl.BlockSpec((orders, 1, hy_w, hid), lambda s: (0, s // (slabs // 2), 0, 0)),
            whole(dl),
        ],
        out_specs=pl.BlockSpec((orders, 1, hy_w, LANE), lambda s: (0, s, 0, 0)),
        compiler_params=_cparams("parallel"),
        name="hyena_filter_taps",
    )(*args, fw3t, dl)


def _lane_block_abs_norm(x, orders):
    rows_per = x.shape[0] // orders
    cw = x.shape[1] // LANE
    out = []
    for o in range(orders):
        xo = x[o * rows_per:(o + 1) * rows_per]
        a = jnp.sum(jnp.abs(xo), axis=0, keepdims=True)
        inv = [jnp.broadcast_to(1.0 / (jnp.sum(a[:, c * LANE:(c + 1) * LANE], axis=1, keepdims=True) + EPS),
                                (1, LANE)) for c in range(cw)]
        out.append(xo * jnp.concatenate(inv, axis=1))
    return jnp.concatenate(out, axis=0)


def _filtspec_kernel(k_ref, w1_ref, w2_ref, tc_ref, ts_ref, re_ref, im_ref, *, half, orders):
    m1 = half // orders
    x0 = _lane_block_abs_norm(k_ref[...], orders)
    s = _fft_forward(x0, w1_ref[...], w2_ref[...], tc_ref[...], ts_ref[...], half)
    cw = x0.shape[1] // LANE
    re_ref[...] = s[:, :LANE].reshape(cw, orders, m1, LANE)
    im_ref[...] = s[:, LANE:].reshape(cw, orders, m1, LANE)


def hyena_filter_spectrum(taps, bsz, cw):
    orders, m1, c, _ = taps.shape
    mats, tabs, half, w1f = _fft_constants(bsz, m1 // 2)
    assert half == orders * m1
    consts = [w1f, mats[1], tabs[0], tabs[1]]
    out = jax.ShapeDtypeStruct((c, orders, m1, LANE), jnp.float32)
    ospec = pl.BlockSpec((cw, orders, m1, LANE), lambda j: (j, 0, 0, 0))
    return pl.pallas_call(
        functools.partial(_filtspec_kernel, half=half, orders=orders),
        out_shape=(out, out),
        grid=(c // cw,),
        in_specs=[pl.BlockSpec((orders * m1, cw * LANE), lambda j: (0, j))] +
                 [pl.BlockSpec(a.shape, lambda j: (0, 0)) for a in consts],
        out_specs=(ospec, ospec),
        compiler_params=_cparams("parallel"),
        name="hyena_filter_spectrum",
    )(taps.reshape(orders * m1, c * LANE), *consts)


def _dense_dft_constants(seq):
    n = 2 * seq
    t = np.arange(seq)[:, None]
    f = np.arange(n)[None, :]
    ps = 2.0 * np.pi * t * f / n
    fwd = np.block([[np.cos(ps), -np.sin(ps)], [np.sin(ps), np.cos(ps)]])
    inv = fwd.T / n
    tk = np.arange(n)[:, None]
    pk = 2.0 * np.pi * tk * f / n
    fk = np.concatenate([np.cos(pk), -np.sin(pk)], axis=1)
    return [jnp.asarray(m, jnp.bfloat16) for m in (fwd, inv, fk)]


def _ctxspec_kernel(k_ref, fk_ref, re_ref, im_ref):
    orders, slabs = k_ref.shape[0], k_ref.shape[1]
    n = slabs * LANE
    for o in range(orders):
        x = jnp.concatenate([k_ref[o, s] for s in range(slabs)], axis=1)
        x = x / (jnp.sum(jnp.abs(x), axis=1, keepdims=True) + EPS)
        kf = _mxu(x, fk_ref[...])
        re_ref[o] = kf[:, :n]
        im_ref[o] = kf[:, n:]


def ctx_filter_spectrum(taps, cw):
    orders, slabs, c, _ = taps.shape
    n = slabs * LANE
    fk = _dense_dft_constants(n // 2)[2]
    out = jax.ShapeDtypeStruct((orders, c, n), jnp.float32)
    ospec = pl.BlockSpec((orders, cw, n), lambda j: (0, j, 0))
    return pl.pallas_call(
        _ctxspec_kernel,
        out_shape=(out, out),
        grid=(c // cw,),
        in_specs=[pl.BlockSpec((orders, slabs, cw, LANE), lambda j: (0, 0, j, 0)),
                  pl.BlockSpec(fk.shape, lambda j: (0, 0))],
        out_specs=(ospec, ospec),
        compiler_params=_cparams("parallel"),
        name="ctx_filter_spectrum",
    )(taps, fk)


def _ctxconv_kernel(z_ref, g_ref, kre_ref, kim_ref, b_ref, fwd_ref, inv_ref, o_ref):
    bsz, slabs, cw = z_ref.shape[0], z_ref.shape[1], z_ref.shape[2]
    seq = slabs * LANE
    zs = [jnp.concatenate([z_ref[b, s] for s in range(slabs)], axis=1) for b in range(bsz)]
    lhs = jnp.concatenate([jnp.concatenate([zs[2 * p], zs[2 * p + 1]], axis=1) for p in range(bsz // 2)], axis=0)
    s = _mxu(lhs, fwd_ref[...])
    n = 2 * seq
    sr, si = s[:, :n], s[:, n:]
    kr = jnp.concatenate([kre_ref[0]] * (bsz // 2), axis=0)
    ki = jnp.concatenate([kim_ref[0]] * (bsz // 2), axis=0)
    y = _mxu(jnp.concatenate([sr * kr - si * ki, sr * ki + si * kr], axis=1), inv_ref[...])
    bias = jnp.concatenate([b_ref[...]] * slabs, axis=1)
    for b in range(bsz):
        p, m = b // 2, b % 2
        yb = y[p * cw:(p + 1) * cw, m * seq:(m + 1) * seq]
        gate = jnp.concatenate([g_ref[b, s] for s in range(slabs)], axis=1)
        res = gate * (yb + zs[b] * bias)
        for sl in range(slabs):
            o_ref[b, sl] = res[:, sl * LANE:(sl + 1) * LANE]


def ctx_hyena_conv(ut, z_src, z_col, g_col, kre, kim, order, bias, cw):
    bsz, slabs, _, _ = ut.shape
    c = bias.shape[0]
    n = 2 * slabs * LANE
    fwd, inv, _ = _dense_dft_constants(slabs * LANE)
    zb, gb = z_col // cw, g_col // cw
    bb = jnp.broadcast_to(bias[:, None], (c, LANE))
    kspec = pl.BlockSpec((1, cw, n), lambda j: (order, j, 0))
    return pl.pallas_call(
        _ctxconv_kernel,
        out_shape=jax.ShapeDtypeStruct((bsz, slabs, c, LANE), jnp.float32),
        grid=(c // cw,),
        in_specs=[
            pl.BlockSpec((bsz, slabs, cw, LANE), lambda j: (0, 0, zb + j, 0)),
            pl.BlockSpec((bsz, slabs, cw, LANE), lambda j: (0, 0, gb + j, 0)),
            kspec, kspec,
            pl.BlockSpec((cw, LANE), lambda j: (j, 0)),
            pl.BlockSpec(fwd.shape, lambda j: (0, 0)),
            pl.BlockSpec(inv.shape, lambda j: (0, 0)),
        ],
        out_specs=pl.BlockSpec((bsz, slabs, cw, LANE), lambda j: (0, 0, j, 0)),
        compiler_params=_cparams("parallel"),
        name="ctx_hyena_conv",
    )(z_src, ut, kre, kim, bb, fwd, inv)


def _mlp_kernel(x_ref, nw_ref, sc_ref, sh_ref, g_ref, w1_ref, w2_ref, fw_ref, o_ref, h_ref, *, final_norm):
    f = pl.program_id(2)

    @pl.when(f == 0)
    def _():
        _normmod_store(h_ref, x_ref, nw_ref[...], sc_ref[0], sh_ref[0])
        o_ref[...] = jnp.zeros_like(o_ref)

    a = jnp.dot(h_ref[...], w1_ref[...], preferred_element_type=jnp.float32)
    a = jnp.square(jnp.maximum(a, 0.0)).astype(jnp.bfloat16)
    o_ref[0] += jnp.dot(a, w2_ref[...], preferred_element_type=jnp.float32)

    @pl.when(f == pl.num_programs(2) - 1)
    def _():
        y = x_ref[0] + g_ref[0] * o_ref[0]
        if final_norm:
            ms = jnp.mean(y * y, axis=-1, keepdims=True)
            y = y * lax.rsqrt(ms + EPS) * fw_ref[...]
        o_ref[0] = y


def mlp_residual(x, nw, sc, sh, g, w1, w2, fw, tm, tf, final_norm):
    b, l, d = x.shape
    dff = w1.shape[1]
    vec = pl.BlockSpec((1, 1, d), lambda i, j, k: (i, 0, 0))
    row = pl.BlockSpec((1, d), lambda i, j, k: (0, 0))
    return pl.pallas_call(
        functools.partial(_mlp_kernel, final_norm=final_norm),
        out_shape=jax.ShapeDtypeStruct((b, l, d), jnp.float32),
        grid=(b, l // tm, dff // tf),
        in_specs=[
            pl.BlockSpec((1, tm, d), lambda i, j, k: (i, j, 0)),
            row, vec, vec, vec,
            pl.BlockSpec((d, tf), lambda i, j, k: (0, k)),
            pl.BlockSpec((tf, d), lambda i, j, k: (k, 0)),
            row,
        ],
        out_specs=pl.BlockSpec((1, tm, d), lambda i, j, k: (i, j, 0)),
        scratch_shapes=[pltpu.VMEM((tm, d), jnp.bfloat16)],
        compiler_params=_cparams("parallel", "parallel", "arbitrary"),
        name="mlp_residual",
    )(x, nw.reshape(1, d), sc.reshape(b, 1, d), sh.reshape(b, 1, d), g.reshape(b, 1, d), w1, w2,
      fw.reshape(1, d))


def _split3(v):
    f32, bf16 = jnp.float32, jnp.bfloat16
    hi = v.astype(bf16)
    r1 = v - hi.astype(f32)
    mid = r1.astype(bf16)
    lo = (r1 - mid.astype(f32)).astype(bf16)
    return jnp.concatenate([hi, mid, lo], axis=1)


def _lane_repeat(v, rep):
    h = v.shape[1]
    row = lax.broadcasted_iota(jnp.int32, (3 * h, h * rep), 0) % h
    col = lax.broadcasted_iota(jnp.int32, (3 * h, h * rep), 1) // rep
    e = (row == col).astype(jnp.bfloat16)
    return jnp.dot(_split3(v), e, preferred_element_type=jnp.float32)


def _silu(v):
    return v * jax.nn.sigmoid(v)


def _softplus(v):
    return jnp.maximum(v, 0.0) + jnp.log1p(jnp.exp(-jnp.abs(v)))


def _token_conv3(u, w, bias, period):
    t, c = u.shape
    pos = lax.broadcasted_iota(jnp.int32, (t, c), 0) % period
    up = jnp.where(pos == 0, 0.0, pltpu.roll(u, 1, 0))
    dn = jnp.where(pos == period - 1, 0.0, pltpu.roll(u, t - 1, 0))
    return bias + w[0:1] * up + w[1:2] * u + w[2:3] * dn


def _ssd_chunk(xs, bm, cm, dtr, s_ref, a_row, dtb_row, reverse, heads, hpg):
    f32, bf16 = jnp.float32, jnp.bfloat16
    q = xs.shape[0]
    p = SSD_HEADDIM
    n = SSD_STATE
    gw = hpg * p
    dt = _softplus(dtr + dtb_row)
    a = dt * a_row
    ri = lax.broadcasted_iota(jnp.int32, (q, q), 0)
    ci = lax.broadcasted_iota(jnp.int32, (q, q), 1)
    keep = (ci >= ri) if reverse else (ci <= ri)
    a3 = jnp.dot(keep.astype(bf16), _split3(a), preferred_element_type=f32)
    cs = a3[:, :heads] + a3[:, heads:2 * heads] + a3[:, 2 * heads:]
    dt_rep = _lane_repeat(dt, p)
    cs_rep = _lane_repeat(cs, p)
    cs_wide = _lane_repeat(cs, q)
    end = 0 if reverse else q - 1
    cs_end = cs_rep[end:end + 1]
    xdt = xs * dt_rep
    xw = (xdt * jnp.exp(cs_end - cs_rep)).astype(bf16)
    ecs = jnp.exp(cs_rep)
    chunk_decay = jnp.exp(cs_end)
    xdt_b = xdt.astype(bf16)
    lane = lax.broadcasted_iota(jnp.int32, (q, 2 * p), 1)
    ys = []
    for g in range(SSD_GROUPS):
        bg = bm[:, g * n:(g + 1) * n].astype(bf16)
        cg = cm[:, g * n:(g + 1) * n].astype(bf16)
        cb = lax.dot_general(cg, bg, (((1,), (1,)), ((), ())), preferred_element_type=f32)
        s_old = s_ref[0, g]
        y_off = jnp.dot(cg, s_old.astype(bf16), preferred_element_type=f32) * ecs[:, g * gw:(g + 1) * gw]
        for pr in range(hpg // 2):
            xpair = xdt_b[:, g * gw + pr * 2 * p:g * gw + (pr + 1) * 2 * p]
            acc = None
            for k in range(2):
                h = g * hpg + pr * 2 + k
                csr = cs_wide[:, h * q:(h + 1) * q]
                seg = csr - csr.T
                gmat = (cb * jnp.where(keep, jnp.exp(seg), 0.0)).astype(bf16)
                xh = jnp.where((lane // p) == k, xpair, jnp.zeros_like(xpair))
                part = jnp.dot(gmat, xh, preferred_element_type=f32)
                acc = part if acc is None else acc + part
            lo = pr * 2 * p
            ys.append(acc + y_off[:, lo:lo + 2 * p])
        upd = jnp.dot(bg.T, xw[:, g * gw:(g + 1) * gw], preferred_element_type=f32)
        s_ref[0, g] = s_old * chunk_decay[:, g * gw:(g + 1) * gw] + upd
    return jnp.concatenate(ys, axis=1)


def _ssd_kernel(z_ref, xs_ref, bc_ref, dt_ref, cwx_ref, cbx_ref, cwb_ref, cbb_ref, dtb_ref, alog_ref, drep_ref,
                nw_ref, initf_ref, initb_ref, y_ref, sf_ref, sb_ref, yb_ref, *, nb, tb, period, heads, hpg):
    j = pl.program_id(1)
    q = SSD_CHUNK
    n = SSD_STATE
    gn = SSD_GROUPS * n
    nchunk = tb // q
    ssd_w = heads * SSD_HEADDIM

    @pl.when(j == 0)
    def _():
        sb_ref[...] = initb_ref[...]

    @pl.when(j == nb)
    def _():
        sf_ref[...] = initf_ref[...]

    xs_all = _silu(_token_conv3(xs_ref[0], cwx_ref[...], cbx_ref[...], period))
    bc_all = _silu(_token_conv3(bc_ref[0], cwb_ref[...], cbb_ref[...], period))
    a_all = -jnp.exp(alog_ref[...])

    def run(reverse):
        d = 1 if reverse else 0
        blk = (nb - 1 - j) if reverse else (j - nb)
        order = range(nchunk - 1, -1, -1) if reverse else range(nchunk)
        for ci in order:
            sl = slice(ci * q, (ci + 1) * q)
            y = _ssd_chunk(xs_all[sl], bc_all[sl, :gn], bc_all[sl, gn:], dt_ref[0, sl, d * heads:(d + 1) * heads],
                           sb_ref if reverse else sf_ref, a_all[d:d + 1], dtb_ref[d:d + 1], reverse, heads, hpg)
            row0 = pl.multiple_of(blk * tb + ci * q, q)
            if reverse:
                yb_ref[pl.ds(row0, q), :] = y
            else:
                y = y + yb_ref[pl.ds(row0, q), :] + drep_ref[...] * xs_all[sl]
                y = y * _silu(z_ref[0, sl, :])
                gw = ssd_w // SSD_GROUPS
                outs = []
                for g in range(SSD_GROUPS):
                    yg = y[:, g * gw:(g + 1) * gw]
                    ms = jnp.mean(yg * yg, axis=-1, keepdims=True)
                    outs.append(yg * lax.rsqrt(ms + EPS) * nw_ref[:, g * gw:(g + 1) * gw])
                y_ref[0, sl, :] = jnp.concatenate(outs, axis=1)

    @pl.when(j < nb)
    def _():
        run(True)

    @pl.when(j >= nb)
    def _():
        run(False)


def ssd_mixer(srcs, cols, conv_w_x, conv_b_x, conv_w_bc, conv_b_bc, dt_bias, a_log, ssd_d, norm_w, init_f, init_b,
              period, tb):
    b, l, _ = srcs[0].shape
    heads = dt_bias.shape[1]
    hpg = heads // SSD_GROUPS
    ssd_w = heads * SSD_HEADDIM
    gn = SSD_GROUPS * SSD_STATE
    nb = l // tb
    oz, ox, obc, odt = cols
    f32 = jnp.float32

    def tok(width, off):
        blk_idx = off // width
        return pl.BlockSpec((1, tb, width),
                            lambda i, j: (i, jnp.where(j < nb, nb - 1 - j, j - nb), blk_idx))

    def tok_fwd(width, off):
        blk_idx = off // width
        return pl.BlockSpec((1, tb, width), lambda i, j: (i, jnp.where(j < nb, 0, j - nb), blk_idx))

    def whole(shape):
        return pl.BlockSpec(shape, lambda i, j: (0,) * len(shape))

    st_shape = (b, SSD_GROUPS, SSD_STATE, hpg * SSD_HEADDIM)
    st_spec = pl.BlockSpec((1,) + st_shape[1:], lambda i, j: (i, 0, 0, 0))
    drep = jnp.repeat(ssd_d, SSD_HEADDIM).reshape(1, ssd_w)
    kern = functools.partial(_ssd_kernel, nb=nb, tb=tb, period=period, heads=heads, hpg=hpg)
    return pl.pallas_call(
        kern,
        out_shape=(jax.ShapeDtypeStruct((b, l, ssd_w), f32), jax.ShapeDtypeStruct(st_shape, f32),
                   jax.ShapeDtypeStruct(st_shape, f32)),
        grid=(b, 2 * nb),
        in_specs=[
            tok_fwd(ssd_w, oz), tok(ssd_w, ox), tok(2 * gn, obc), tok(LANE, odt),
            whole((3, ssd_w)), whole((1, ssd_w)), whole((3, 2 * gn)), whole((1, 2 * gn)),
            whole((2, heads)), whole((2, heads)), whole((1, ssd_w)), whole((1, ssd_w)),
            st_spec, st_spec,
        ],
        out_specs=(pl.BlockSpec((1, tb, ssd_w), lambda i, j: (i, jnp.where(j < nb, 0, j - nb), 0)),
                   st_spec, st_spec),
        scratch_shapes=[pltpu.VMEM((l, ssd_w), f32)],
        compiler_params=_cparams("parallel", "arbitrary"),
        name="ssd_mixer",
    )(*srcs, conv_w_x, conv_b_x.reshape(1, -1), conv_w_bc, conv_b_bc.reshape(1, -1), dt_bias, a_log, drep,
      norm_w.reshape(1, ssd_w), init_f, init_b)


def _pick(l, pref):
    return pref if l % pref == 0 else l


def kernel(x, c, ctx, c_ctx, w_ada, b_ada, norm1_w, w_in, hy_conv_w, hy_conv_b, filt_w1, filt_b1, filt_freq,
           filt_w2, filt_b2, filt_w3, hy_bias, hy_norm_w, ssd_conv_w, ssd_conv_b, dt_bias, a_log, ssd_d,
           ssd_norm_w, w_out, norm2_w, w_mlp1, w_mlp2, final_norm_w):
    depth = w_in.shape[0]
    bsz, seq, d = x.shape
    hy_w = hy_norm_w.shape[1]
    hy_proj = hy_conv_w.shape[2]
    ssd_w = ssd_norm_w.shape[1]
    ssd_xbc = ssd_conv_w.shape[2]
    heads = ssd_w // SSD_HEADDIM
    hpg = heads // SSD_GROUPS
    ssd_dt = 2 * heads
    o0, o1 = hy_proj, hy_proj + ssd_xbc
    o2 = o1 + ssd_dt
    dt_pad = (-ssd_dt) % LANE
    bf16 = jnp.bfloat16
    ssd_cols = (0, o0, o0 + ssd_w, ssd_w)
    tn = o1 // 3

    n1 = seq // LANE
    n1c = ctx.shape[1] // LANE
    zero_state = jnp.zeros((bsz, SSD_GROUPS, SSD_STATE, hpg * SSD_HEADDIM), jnp.float32)

    rows = 8
    c_all = jnp.zeros((rows, d), jnp.float32).at[:bsz].set(c).at[bsz].set(c_ctx)
    mod_all = ada_mod(c_all, w_ada, b_ada)

    w_a_all, w_b_all = cast_in_proj(w_in, o1, o2, w_in.shape[2], ssd_w + ssd_dt + dt_pad)

    h_ctx = ctx
    tm_l = _pick(seq, 512)
    tm_c = _pick(ctx.shape[1], 256)
    for i in range(depth):
        lp = dict(hy_conv_w=hy_conv_w[i], hy_conv_b=hy_conv_b[i], filt_w1=filt_w1[i], filt_b1=filt_b1[i],
                  filt_freq=filt_freq[i], filt_w2=filt_w2[i], filt_b2=filt_b2[i], filt_w3=filt_w3[i],
                  hy_bias=hy_bias[i], hy_norm_w=hy_norm_w[i], ssd_conv_w=ssd_conv_w[i],
                  ssd_conv_b=ssd_conv_b[i], dt_bias=dt_bias[i], a_log=a_log[i], ssd_d=ssd_d[i],
                  ssd_norm_w=ssd_norm_w[i])
        w_a, w_b = w_a_all[i], w_b_all[i]
        w_o_hy = w_out[i, :hy_w].astype(bf16)
        w_o_ssd = w_out[i, hy_w:].astype(bf16)
        w1 = w_mlp1[i].astype(bf16)
        w2 = w_mlp2[i].astype(bf16)

        mod = mod_all[i, :bsz].reshape(bsz, N_MOD, d)
        mod_c = jnp.broadcast_to(mod_all[i, bsz].reshape(1, N_MOD, d), (bsz, N_MOD, d))
        def ssd(pa, pb, init_f, init_b, period):
            cw, cb = lp['ssd_conv_w'], lp['ssd_conv_b']
            return ssd_mixer((pb, pa, pa, pb), ssd_cols, cw[:, :ssd_w], cb[:ssd_w], cw[:, ssd_w:], cb[ssd_w:],
                             lp['dt_bias'], lp['a_log'], lp['ssd_d'], lp['ssd_norm_w'], init_f, init_b, period,
                             _pick(pa.shape[1], 256))

        def filt_taps(length):
            return hyena_filter_taps(length, lp['filt_w1'], lp['filt_b1'], lp['filt_freq'], lp['filt_w2'],
                                     lp['filt_b2'], lp['filt_w3'], hy_w)

        pca, pcb = normmod_matmul(h_ctx, norm1_w[i], mod_c[:, 1], mod_c[:, 0], w_a, w_b, tm_c, tn)
        y_ssd, s_f, s_b = ssd(pca, pcb, zero_state, zero_state, pca.shape[1])
        if i < depth - 1:
            utc = hyena_pre(pca, 0, lp['hy_conv_w'], lp['hy_conv_b'], pca.shape[1], pca.shape[1], 512)
            kre, kim = ctx_filter_spectrum(filt_taps(pca.shape[1]), 128)
            zt = utc
            for o in range(HY_ORDER):
                zt = ctx_hyena_conv(utc, zt, 0, (o + 1) * hy_w, kre, kim, o, lp['hy_bias'][o], 128)
            h_ctx = proj_residual(zt, y_ssd, lp['hy_norm_w'], w_o_hy, w_o_ssd, h_ctx, mod_c[:, 2], tm_c,
                                  HY_GROUPS)
            h_ctx = mlp_residual(h_ctx, norm2_w[i], mod_c[:, 4], mod_c[:, 3], mod_c[:, 5], w1, w2,
                                 final_norm_w, tm_c, 512, False)
        pla, plb = normmod_matmul(x, norm1_w[i], mod[:, 1], mod[:, 0], w_a, w_b, tm_l, tn)
        ut = hyena_pre(pla, 0, lp['hy_conv_w'], lp['hy_conv_b'], GRID_W, 512, 512)
        ut = ut.reshape(bsz * n1, hy_proj * LANE)
        kre, kim = hyena_filter_spectrum(filt_taps(seq), bsz, 16)
        zt = ut
        for o in range(HY_ORDER):
            zt = hyena_conv(ut, zt, 0, (o + 1) * hy_w, kre, kim, o, lp['hy_bias'][o], bsz, 16)
        zh_t = zt.reshape(bsz, n1, hy_w, LANE)
        y_ssd, _, _ = ssd(pla, plb, s_f, s_b, GRID_W)
        x = proj_residual(zh_t, y_ssd, lp['hy_norm_w'], w_o_hy, w_o_ssd, x, mod[:, 2], tm_l, HY_GROUPS)
        x = mlp_residual(x, norm2_w[i], mod[:, 4], mod[:, 3], mod[:, 5], w1, w2, final_norm_w, tm_l, 512,
                         i == depth - 1)
    return x
```

```python
import functools
import math

import numpy as np
import jax
import jax.numpy as jnp
from jax import lax
from jax.experimental import pallas as pl
from jax.experimental.pallas import tpu as pltpu

EPS = 1e-6
N_MOD = 6
GRID_W = 64
HY_GROUPS = 8
HY_ORDER = 2
HY_POS_EMB = 33
HY_TARGET = 1e-2
HY_FAST = 0.3
HY_SLOW = 1.5
SSD_HEADDIM = 64
SSD_GROUPS = 2
SSD_STATE = 128
SSD_CHUNK = 128

LANE = 128
VMEM_LIMIT = 56 * 1024 * 1024


def _cparams(*sem):
    return pltpu.CompilerParams(dimension_semantics=sem, vmem_limit_bytes=VMEM_LIMIT)


def _ada_kernel(c_ref, w_ref, b_ref, o_ref):
    c = c_ref[...]
    a = (c * jax.nn.sigmoid(c)).astype(jnp.bfloat16)
    w = w_ref[0].astype(jnp.bfloat16)
    o_ref[0] = jnp.dot(a, w, preferred_element_type=jnp.float32) + b_ref[0]


def ada_mod(c_all, w_ada, b_ada, tn=1024):
    depth, d, n = w_ada.shape
    r = c_all.shape[0]
    return pl.pallas_call(
        _ada_kernel,
        out_shape=jax.ShapeDtypeStruct((depth, r, n), jnp.float32),
        grid=(depth, n // tn),
        in_specs=[
            pl.BlockSpec((r, d), lambda i, j: (0, 0)),
            pl.BlockSpec((1, d, tn), lambda i, j: (i, 0, j)),
            pl.BlockSpec((1, 1, tn), lambda i, j: (i, 0, j)),
        ],
        out_specs=pl.BlockSpec((1, r, tn), lambda i, j: (i, 0, j)),
        compiler_params=_cparams("parallel", "parallel"),
        name="ada_mod",
    )(c_all, w_ada, b_ada.reshape(depth, 1, n))


def _wcast_kernel(w_ref, a_ref, b_ref, *, na, c0, c1):
    bf16 = jnp.bfloat16
    w = w_ref[0]
    a_ref[0] = w[:, :na].astype(bf16)
    nb = b_ref.shape[2]
    pad = nb - (c1 - c0) - (c0 - na)
    b_ref[0] = jnp.concatenate([w[:, c0:c1], w[:, na:c0], jnp.zeros((w.shape[0], pad), w.dtype)], axis=1).astype(bf16)


def cast_in_proj(w_in, na, c0, c1, nb, tr=256):
    depth, d, n = w_in.shape
    return pl.pallas_call(
        functools.partial(_wcast_kernel, na=na, c0=c0, c1=c1),
        out_shape=(jax.ShapeDtypeStruct((depth, d, na), jnp.bfloat16),
                   jax.ShapeDtypeStruct((depth, d, nb), jnp.bfloat16)),
        grid=(depth, d // tr),
        in_specs=[pl.BlockSpec((1, tr, n), lambda i, j: (i, j, 0))],
        out_specs=(pl.BlockSpec((1, tr, na), lambda i, j: (i, j, 0)),
                   pl.BlockSpec((1, tr, nb), lambda i, j: (i, j, 0))),
        compiler_params=_cparams("parallel", "parallel"),
        name="cast_in_proj",
    )(w_in)


NORM_ROWS = 16


def _normmod_store(h_ref, x_ref, nw, sc, sh):
    w = nw * (1.0 + sc)

    def body(i, carry):
        r0 = pl.multiple_of(i * NORM_ROWS, NORM_ROWS)
        xb = x_ref[0, pl.ds(r0, NORM_ROWS), :]
        ms = jnp.mean(xb * xb, axis=-1, keepdims=True)
        h_ref[pl.ds(r0, NORM_ROWS), :] = ((xb * lax.rsqrt(ms + EPS)) * w + sh).astype(h_ref.dtype)
        return carry

    lax.fori_loop(0, h_ref.shape[0] // NORM_ROWS, body, 0, unroll=8)


def _nm_matmul_kernel(x_ref, nw_ref, sc_ref, sh_ref, wa_ref, wb_ref, oa_ref, ob_ref, h_ref, *, na):
    k = pl.program_id(2)

    @pl.when(k == 0)
    def _():
        _normmod_store(h_ref, x_ref, nw_ref[...], sc_ref[0], sh_ref[0])

    @pl.when(k < na)
    def _():
        oa_ref[0] = jnp.dot(h_ref[...], wa_ref[...], preferred_element_type=jnp.float32)

    @pl.when(k >= na)
    def _():
        ob_ref[0] = jnp.dot(h_ref[...], wb_ref[...], preferred_element_type=jnp.float32)


def normmod_matmul(x, nw, sc, sh, wa, wb, tm, tn):
    b, l, d = x.shape
    n_a, n_b = wa.shape[1], wb.shape[1]
    na = n_a // tn
    last = na - 1
    return pl.pallas_call(
        functools.partial(_nm_matmul_kernel, na=na),
        out_shape=(jax.ShapeDtypeStruct((b, l, n_a), jnp.float32), jax.ShapeDtypeStruct((b, l, n_b), jnp.float32)),
        grid=(b, l // tm, na + 1),
        in_specs=[
            pl.BlockSpec((1, tm, d), lambda i, j, k: (i, j, 0)),
            pl.BlockSpec((1, d), lambda i, j, k: (0, 0)),
            pl.BlockSpec((1, 1, d), lambda i, j, k: (i, 0, 0)),
            pl.BlockSpec((1, 1, d), lambda i, j, k: (i, 0, 0)),
            pl.BlockSpec((d, tn), lambda i, j, k: (0, jnp.minimum(k, last))),
            pl.BlockSpec((d, n_b), lambda i, j, k: (0, 0)),
        ],
        out_specs=(pl.BlockSpec((1, tm, tn), lambda i, j, k: (i, j, jnp.minimum(k, last))),
                   pl.BlockSpec((1, tm, n_b), lambda i, j, k: (i, j, 0))),
        scratch_shapes=[pltpu.VMEM((tm, d), jnp.bfloat16)],
        compiler_params=_cparams("parallel", "parallel", "arbitrary"),
        name="normmod_matmul",
    )(x, nw.reshape(1, d), sc.reshape(b, 1, d), sh.reshape(b, 1, d), wa, wb)


def _proj_res_kernel(zh_ref, ys_ref, nwh_ref, wh_ref, ws_ref, x_ref, g_ref, o_ref, *, groups):
    bf16 = jnp.bfloat16
    slabs, hy_w = zh_ref.shape[1], zh_ref.shape[2]
    gs = hy_w // groups
    rows = []
    for s in range(slabs):
        cols = []
        for gi in range(groups):
            zg = zh_ref[0, s, gi * gs:(gi + 1) * gs, :]
            ms = jnp.mean(zg * zg, axis=0, keepdims=True)
            cols.append((zg * lax.rsqrt(ms + EPS)).T)
        rows.append(jnp.concatenate(cols, axis=1))
    yh = (jnp.concatenate(rows, axis=0) * nwh_ref[...]).astype(bf16)
    acc = jnp.dot(yh, wh_ref[...], preferred_element_type=jnp.float32)
    acc += jnp.dot(ys_ref[0].astype(bf16), ws_ref[...], preferred_element_type=jnp.float32)
    o_ref[0] = x_ref[0] + g_ref[0] * acc


def proj_residual(zh_t, y_ssd, hy_norm_w, w_hy, w_ssd, x, g, tm, groups):
    b, n1, hy_w, _ = zh_t.shape
    l, ssd_w = y_ssd.shape[1], y_ssd.shape[2]
    n = w_hy.shape[1]
    return pl.pallas_call(
        functools.partial(_proj_res_kernel, groups=groups),
        out_shape=jax.ShapeDtypeStruct((b, l, n), jnp.float32),
        grid=(b, l // tm),
        in_specs=[
            pl.BlockSpec((1, tm // LANE, hy_w, LANE), lambda i, j: (i, j, 0, 0)),
            pl.BlockSpec((1, tm, ssd_w), lambda i, j: (i, j, 0)),
            pl.BlockSpec((1, hy_w), lambda i, j: (0, 0)),
            pl.BlockSpec((hy_w, n), lambda i, j: (0, 0)),
            pl.BlockSpec((ssd_w, n), lambda i, j: (0, 0)),
            pl.BlockSpec((1, tm, n), lambda i, j: (i, j, 0)),
            pl.BlockSpec((1, 1, n), lambda i, j: (i, 0, 0)),
        ],
        out_specs=pl.BlockSpec((1, tm, n), lambda i, j: (i, j, 0)),
        compiler_params=_cparams("parallel", "parallel"),
        name="proj_residual",
    )(zh_t, y_ssd, hy_norm_w.reshape(1, hy_w), w_hy, w_ssd, x, g.reshape(b, 1, n))


def _hypre_kernel(u_ref, w_ref, b_ref, o_ref, *, period):
    u = _token_conv3(u_ref[0], w_ref[...], b_ref[...], period)
    tb, cw = u.shape
    for s in range(tb // LANE):
        for j in range(cw // LANE):
            o_ref[0, s, j * LANE:(j + 1) * LANE, :] = u[s * LANE:(s + 1) * LANE, j * LANE:(j + 1) * LANE].T


def hyena_pre(p, col0, conv_w, conv_b, period, tb, cw):
    b, l, _ = p.shape
    c = conv_w.shape[1]
    blk0 = col0 // cw
    return pl.pallas_call(
        functools.partial(_hypre_kernel, period=period),
        out_shape=jax.ShapeDtypeStruct((b, l // LANE, c, LANE), jnp.float32),
        grid=(b, l // tb, c // cw),
        in_specs=[
            pl.BlockSpec((1, tb, cw), lambda i, j, k: (i, j, blk0 + k)),
            pl.BlockSpec((3, cw), lambda i, j, k: (0, k)),
            pl.BlockSpec((1, cw), lambda i, j, k: (0, k)),
        ],
        out_specs=pl.BlockSpec((1, tb // LANE, cw, LANE), lambda i, j, k: (i, j, k, 0)),
        compiler_params=_cparams("parallel", "parallel", "parallel"),
        name="hyena_pre",
    )(p, conv_w, conv_b.reshape(1, c))


def _hypre_rows_kernel(u_ref, w_ref, b_ref, o_ref, s_ref, *, period):
    u = _token_conv3(u_ref[0], w_ref[...], b_ref[...], period)
    l, cw = u.shape
    n1 = l // LANE
    for s in range(n1):
        for j in range(cw // LANE):
            s_ref[s * cw + j * LANE:s * cw + (j + 1) * LANE, :] = u[s * LANE:(s + 1) * LANE,
                                                                    j * LANE:(j + 1) * LANE].T
    for c in range(cw):
        o_ref[:, c * LANE:(c + 1) * LANE] = s_ref[pl.ds(c, n1, stride=cw), :]


def hyena_pre_rows(p, col0, conv_w, conv_b, period, cw):
    b, l, _ = p.shape
    c = conv_w.shape[1]
    n1 = l // LANE
    blk0 = col0 // cw
    return pl.pallas_call(
        functools.partial(_hypre_rows_kernel, period=period),
        out_shape=jax.ShapeDtypeStruct((b * n1, c * LANE), jnp.float32),
        grid=(b, c // cw),
        in_specs=[
            pl.BlockSpec((1, l, cw), lambda i, k: (i, 0, blk0 + k)),
            pl.BlockSpec((3, cw), lambda i, k: (0, k)),
            pl.BlockSpec((1, cw), lambda i, k: (0, k)),
        ],
        out_specs=pl.BlockSpec((n1, cw * LANE), lambda i, k: (i, k)),
        scratch_shapes=[pltpu.VMEM((n1 * cw, LANE), jnp.float32)],
        compiler_params=_cparams("parallel", "parallel"),
        name="hyena_pre_rows",
    )(p, conv_w, conv_b.reshape(1, c))


def _fft_constants(bsz, n1):
    m1 = 2 * n1
    n = m1 * LANE
    pairs = bsz // 2
    half = pairs * m1
    r = bsz * n1
    t1 = np.arange(n1)[:, None]
    f1 = np.arange(m1)[None, :]
    th = 2.0 * np.pi * t1 * f1 / m1
    w1 = np.zeros((r, 2 * half))
    for pr in range(pairs):
        re = slice(pr * m1, (pr + 1) * m1)
        im = slice(half + pr * m1, half + (pr + 1) * m1)
        ra = slice((2 * pr) * n1, (2 * pr + 1) * n1)
        rb = slice((2 * pr + 1) * n1, (2 * pr + 2) * n1)
        w1[ra, re], w1[ra, im] = np.cos(th), -np.sin(th)
        w1[rb, re], w1[rb, im] = np.sin(th), np.cos(th)
    w4 = w1.T / n
    t2 = np.arange(LANE)[:, None]
    ps = 2.0 * np.pi * t2 * np.arange(LANE)[None, :] / LANE
    w2 = np.block([[np.cos(ps), -np.sin(ps)], [np.sin(ps), np.cos(ps)]])
    ph = 2.0 * np.pi * t2 * f1 / n
    tc = np.tile(np.cos(ph), (1, pairs))
    ts = np.tile(np.sin(ph), (1, pairs))
    orders = half // m1
    tf = 2.0 * np.pi * np.arange(m1)[:, None] * f1 / m1
    w1f = np.zeros((orders * m1, 2 * half))
    for o in range(orders):
        w1f[o * m1:(o + 1) * m1, o * m1:(o + 1) * m1] = np.cos(tf)
        w1f[o * m1:(o + 1) * m1, half + o * m1:half + (o + 1) * m1] = -np.sin(tf)
    mats = [jnp.asarray(m, jnp.bfloat16) for m in (w1, w2, w2.T, w4)]
    tabs = [jnp.asarray(m, jnp.float32) for m in (tc, ts, tc.T, ts.T)]
    return mats, tabs, half, jnp.asarray(w1f, jnp.bfloat16)


def _mxu(a, w):
    return jnp.dot(a.astype(jnp.bfloat16), w, preferred_element_type=jnp.float32)


def _fft_forward(x0, w1, w2, tc, ts, half):
    cw = x0.shape[1] // LANE
    a1 = _mxu(x0.T, w1)
    lhs2 = []
    for c in range(cw):
        ar = a1[c * LANE:(c + 1) * LANE, :half]
        ai = a1[c * LANE:(c + 1) * LANE, half:]
        lhs2.append(jnp.concatenate([(ar * tc + ai * ts).T, (ai * tc - ar * ts).T], axis=1))
    return _mxu(jnp.concatenate(lhs2, axis=0), w2)


def _hyconv_kernel(z_ref, g_ref, kre_ref, kim_ref, brep_ref, w1_ref, w2_ref, w2i_ref, w4_ref, tc_ref, ts_ref,
                   tct_ref, tst_ref, o_ref, *, half):
    x0 = z_ref[...]
    cw = x0.shape[1] // LANE
    tct, tst = tct_ref[...], tst_ref[...]
    s = _fft_forward(x0, w1_ref[...], w2_ref[...], tc_ref[...], ts_ref[...], half)
    sr, si = s[:, :LANE], s[:, LANE:]
    m1 = kre_ref.shape[2]
    pairs = half // m1
    kr = jnp.concatenate([kre_ref[:, 0]] * pairs, axis=1).reshape(cw * half, LANE)
    ki = jnp.concatenate([kim_ref[:, 0]] * pairs, axis=1).reshape(cw * half, LANE)
    y2 = jnp.concatenate([sr * kr - si * ki, sr * ki + si * kr], axis=1)
    bq = _mxu(y2, w2i_ref[...])
    lhs4 = []
    for c in range(cw):
        br = bq[c * half:(c + 1) * half, :LANE]
        bi = bq[c * half:(c + 1) * half, LANE:]
        lhs4.append(jnp.concatenate([(br * tct - bi * tst).T, (bi * tct + br * tst).T], axis=1))
    lhs4 = jnp.concatenate(lhs4, axis=0)
    yt = _mxu(lhs4, w4_ref[...])
    if len(o_ref.shape) == 2:
        o_ref[...] = g_ref[...] * (yt.T + x0 * brep_ref[...])
    else:
        for c in range(cw):
            sl = slice(c * LANE, (c + 1) * LANE)
            o_ref[:, c, :] = g_ref[:, sl] * (yt[sl].T + x0[:, sl] * brep_ref[:, sl])


def hyena_conv(ut, z_src, z_col, g_col, kre, kim, order, bias, bsz, cw, channel_tiles):
    r = ut.shape[0]
    n1 = r // bsz
    c = bias.shape[0]
    mats, tabs, half, _ = _fft_constants(bsz, n1)
    blk = cw * LANE
    zb, gb = z_col // cw, g_col // cw
    brep = jnp.repeat(bias, LANE).reshape(1, c * LANE)
    kspec = pl.BlockSpec((cw, 1, 2 * n1, LANE), lambda j: (j, order, 0, 0))

    def const(a):
        return pl.BlockSpec(a.shape, lambda j: (0, 0))

    return pl.pallas_call(
        functools.partial(_hyconv_kernel, half=half),
        out_shape=jax.ShapeDtypeStruct((r, c, LANE) if channel_tiles else (r, c * LANE), jnp.float32),
        grid=(c // cw,),
        in_specs=[
            pl.BlockSpec((r, blk), lambda j: (0, zb + j)),
            pl.BlockSpec((r, blk), lambda j: (0, gb + j)),
            kspec, kspec,
            pl.BlockSpec((1, blk), lambda j: (0, j)),
        ] + [const(a) for a in mats + tabs],
        out_specs=(pl.BlockSpec((r, cw, LANE), lambda j: (0, j, 0)) if channel_tiles
                   else pl.BlockSpec((r, blk), lambda j: (0, j))),
        compiler_params=_cparams("parallel"),
        name="hyena_conv",
    )(z_src, ut, kre, kim, brep, *mats, *tabs)


_HI = lax.Precision.HIGHEST


def _filtgen_kernel(a0_ref, wc_ref, ws_ref, fb1_ref, freq_ref, fw2t_ref, fb2_ref, fw3t_ref, dl_ref, o_ref, *,
                    seq, bands):
    f32 = jnp.float32
    s = pl.program_id(0)
    t = s * LANE + lax.broadcasted_iota(jnp.int32, (1, LANE), 1)
    pos = jnp.where(t < seq, t, 2 * seq - t).astype(f32)
    tt = pos / (seq - 1.0)
    ang = (2.0 * math.pi / seq) * pos
    j = lax.broadcasted_iota(jnp.int32, (bands, 1), 0).astype(f32)
    fj = 1e-4 + j * ((bands - 1.0 - 1e-4) / (bands - 1.0))
    fa = fj * ang
    freq = freq_ref[...]
    pre = a0_ref[...] * tt + jnp.dot(wc_ref[...], jnp.cos(fa), precision=_HI, preferred_element_type=f32) \
        - jnp.dot(ws_ref[...], jnp.sin(fa), precision=_HI, preferred_element_type=f32) + fb1_ref[...]
    h1 = jnp.sin(freq * pre)
    h2 = jnp.sin(freq * (jnp.dot(fw2t_ref[...], h1, precision=_HI, preferred_element_type=f32) + fb2_ref[...]))
    orders, _, c, hid = fw3t_ref.shape
    w3 = fw3t_ref[:, 0].reshape(orders * c, hid)
    h = jnp.dot(w3, h2, precision=_HI, preferred_element_type=f32)
    h = h * jnp.exp(-tt * dl_ref[...])
    h = jnp.where(t == seq, 0.0, h)
    o_ref[:, 0] = h.reshape(orders, c, LANE)


def hyena_filter_taps(seq, fw1, fb1, freq, fw2, fb2, fw3, hy_w):
    hid = fw2.shape[0]
    bands = (HY_POS_EMB - 1) // 2
    orders = fw3.shape[1] // (2 * hy_w)
    slabs = 2 * seq // LANE
    col = lambda v: v.reshape(-1, 1)
    fw3t = fw3.T.reshape(orders, 2, hy_w, hid)
    deltas = jnp.abs(jnp.linspace(math.log(HY_TARGET) / HY_SLOW, math.log(HY_TARGET) / HY_FAST, hy_w,
                                  dtype=jnp.float32))
    dl = jnp.tile(deltas, orders).reshape(-1, 1)

    def whole(a):
        return pl.BlockSpec(a.shape, lambda s: (0,) * a.ndim)

    args = [col(fw1[0]), fw1[1:1 + bands].T, fw1[1 + bands:].T, col(fb1), col(freq), fw2.T, col(fb2)]
    return pl.pallas_call(
        functools.partial(_filtgen_kernel, seq=seq, bands=bands),
        out_shape=jax.ShapeDtypeStruct((orders, slabs, hy_w, LANE), jnp.float32),
        grid=(slabs,),
        in_specs=[whole(a) for a in args] + [
            pl.BlockSpec((orders, 1, hy_w, hid), lambda s: (0, s // (slabs // 2), 0, 0)),
            whole(dl),
        ],
        out_specs=pl.BlockSpec((orders, 1, hy_w, LANE), lambda s: (0, s, 0, 0)),
        compiler_params=_cparams("parallel"),
        name="hyena_filter_taps",
    )(*args, fw3t, dl)


def _lane_block_abs_norm(x, orders):
    rows_per = x.shape[0] // orders
    cw = x.shape[1] // LANE
    out = []
    for o in range(orders):
        xo = x[o * rows_per:(o + 1) * rows_per]
        a = jnp.sum(jnp.abs(xo), axis=0, keepdims=True)
        inv = [jnp.broadcast_to(1.0 / (jnp.sum(a[:, c * LANE:(c + 1) * LANE], axis=1, keepdims=True) + EPS),
                                (1, LANE)) for c in range(cw)]
        out.append(xo * jnp.concatenate(inv, axis=1))
    return jnp.concatenate(out, axis=0)


def _filtspec_kernel(k_ref, w1_ref, w2_ref, tc_ref, ts_ref, re_ref, im_ref, *, half, orders):
    m1 = half // orders
    x0 = _lane_block_abs_norm(k_ref[...], orders)
    s = _fft_forward(x0, w1_ref[...], w2_ref[...], tc_ref[...], ts_ref[...], half)
    cw = x0.shape[1] // LANE
    re_ref[...] = s[:, :LANE].reshape(cw, orders, m1, LANE)
    im_ref[...] = s[:, LANE:].reshape(cw, orders, m1, LANE)


def hyena_filter_spectrum(taps, bsz, cw):
    orders, m1, c, _ = taps.shape
    mats, tabs, half, w1f = _fft_constants(bsz, m1 // 2)
    assert half == orders * m1
    consts = [w1f, mats[1], tabs[0], tabs[1]]
    out = jax.ShapeDtypeStruct((c, orders, m1, LANE), jnp.float32)
    ospec = pl.BlockSpec((cw, orders, m1, LANE), lambda j: (j, 0, 0, 0))
    return pl.pallas_call(
        functools.partial(_filtspec_kernel, half=half, orders=orders),
        out_shape=(out, out),
        grid=(c // cw,),
        in_specs=[pl.BlockSpec((orders * m1, cw * LANE), lambda j: (0, j))] +
                 [pl.BlockSpec(a.shape, lambda j: (0, 0)) for a in consts],
        out_specs=(ospec, ospec),
        compiler_params=_cparams("parallel"),
        name="hyena_filter_spectrum",
    )(taps.reshape(orders * m1, c * LANE), *consts)


def _dense_dft_constants(seq):
    n = 2 * seq
    t = np.arange(seq)[:, None]
    f = np.arange(n)[None, :]
    ps = 2.0 * np.pi * t * f / n
    fwd = np.block([[np.cos(ps), -np.sin(ps)], [np.sin(ps), np.cos(ps)]])
    inv = fwd.T / n
    tk = np.arange(n)[:, None]
    pk = 2.0 * np.pi * tk * f / n
    fk = np.concatenate([np.cos(pk), -np.sin(pk)], axis=1)
    return [jnp.asarray(m, jnp.bfloat16) for m in (fwd, inv, fk)]


def _ctxspec_kernel(k_ref, fk_ref, re_ref, im_ref):
    orders, slabs = k_ref.shape[0], k_ref.shape[1]
    n = slabs * LANE
    for o in range(orders):
        x = jnp.concatenate([k_ref[o, s] for s in range(slabs)], axis=1)
        x = x / (jnp.sum(jnp.abs(x), axis=1, keepdims=True) + EPS)
        kf = _mxu(x, fk_ref[...])
        re_ref[o] = kf[:, :n]
        im_ref[o] = kf[:, n:]


def ctx_filter_spectrum(taps, cw):
    orders, slabs, c, _ = taps.shape
    n = slabs * LANE
    fk = _dense_dft_constants(n // 2)[2]
    out = jax.ShapeDtypeStruct((orders, c, n), jnp.float32)
    ospec = pl.BlockSpec((orders, cw, n), lambda j: (0, j, 0))
    return pl.pallas_call(
        _ctxspec_kernel,
        out_shape=(out, out),
        grid=(c // cw,),
        in_specs=[pl.BlockSpec((orders, slabs, cw, LANE), lambda j: (0, 0, j, 0)),
                  pl.BlockSpec(fk.shape, lambda j: (0, 0))],
        out_specs=(ospec, ospec),
        compiler_params=_cparams("parallel"),
        name="ctx_filter_spectrum",
    )(taps, fk)


def _ctxconv_kernel(z_ref, g_ref, kre_ref, kim_ref, b_ref, fwd_ref, inv_ref, o_ref):
    bsz, slabs, cw = z_ref.shape[0], z_ref.shape[1], z_ref.shape[2]
    seq = slabs * LANE
    zs = [jnp.concatenate([z_ref[b, s] for s in range(slabs)], axis=1) for b in range(bsz)]
    lhs = jnp.concatenate([jnp.concatenate([zs[2 * p], zs[2 * p + 1]], axis=1) for p in range(bsz // 2)], axis=0)
    s = _mxu(lhs, fwd_ref[...])
    n = 2 * seq
    sr, si = s[:, :n], s[:, n:]
    kr = jnp.concatenate([kre_ref[0]] * (bsz // 2), axis=0)
    ki = jnp.concatenate([kim_ref[0]] * (bsz // 2), axis=0)
    y = _mxu(jnp.concatenate([sr * kr - si * ki, sr * ki + si * kr], axis=1), inv_ref[...])
    bias = jnp.concatenate([b_ref[...]] * slabs, axis=1)
    for b in range(bsz):
        p, m = b // 2, b % 2
        yb = y[p * cw:(p + 1) * cw, m * seq:(m + 1) * seq]
        gate = jnp.concatenate([g_ref[b, s] for s in range(slabs)], axis=1)
        res = gate * (yb + zs[b] * bias)
        for sl in range(slabs):
            o_ref[b, sl] = res[:, sl * LANE:(sl + 1) * LANE]


def ctx_hyena_conv(ut, z_src, z_col, g_col, kre, kim, order, bias, cw):
    bsz, slabs, _, _ = ut.shape
    c = bias.shape[0]
    n = 2 * slabs * LANE
    fwd, inv, _ = _dense_dft_constants(slabs * LANE)
    zb, gb = z_col // cw, g_col // cw
    bb = jnp.broadcast_to(bias[:, None], (c, LANE))
    kspec = pl.BlockSpec((1, cw, n), lambda j: (order, j, 0))
    return pl.pallas_call(
        _ctxconv_kernel,
        out_shape=jax.ShapeDtypeStruct((bsz, slabs, c, LANE), jnp.float32),
        grid=(c // cw,),
        in_specs=[
            pl.BlockSpec((bsz, slabs, cw, LANE), lambda j: (0, 0, zb + j, 0)),
            pl.BlockSpec((bsz, slabs, cw, LANE), lambda j: (0, 0, gb + j, 0)),
            kspec, kspec,
            pl.BlockSpec((cw, LANE), lambda j: (j, 0)),
            pl.BlockSpec(fwd.shape, lambda j: (0, 0)),
            pl.BlockSpec(inv.shape, lambda j: (0, 0)),
        ],
        out_specs=pl.BlockSpec((bsz, slabs, cw, LANE), lambda j: (0, 0, j, 0)),
        compiler_params=_cparams("parallel"),
        name="ctx_hyena_conv",
    )(z_src, ut, kre, kim, bb, fwd, inv)


def _mlp_kernel(x_ref, nw_ref, sc_ref, sh_ref, g_ref, w1_ref, w2_ref, fw_ref, o_ref, h_ref, *, final_norm):
    f = pl.program_id(2)

    @pl.when(f == 0)
    def _():
        _normmod_store(h_ref, x_ref, nw_ref[...], sc_ref[0], sh_ref[0])
        o_ref[...] = jnp.zeros_like(o_ref)

    a = jnp.dot(h_ref[...], w1_ref[...], preferred_element_type=jnp.float32)
    a = jnp.square(jnp.maximum(a, 0.0)).astype(jnp.bfloat16)
    o_ref[0] += jnp.dot(a, w2_ref[...], preferred_element_type=jnp.float32)

    @pl.when(f == pl.num_programs(2) - 1)
    def _():
        y = x_ref[0] + g_ref[0] * o_ref[0]
        if final_norm:
            ms = jnp.mean(y * y, axis=-1, keepdims=True)
            y = y * lax.rsqrt(ms + EPS) * fw_ref[...]
        o_ref[0] = y


def mlp_residual(x, nw, sc, sh, g, w1, w2, fw, tm, tf, final_norm):
    b, l, d = x.shape
    dff = w1.shape[1]
    vec = pl.BlockSpec((1, 1, d), lambda i, j, k: (i, 0, 0))
    row = pl.BlockSpec((1, d), lambda i, j, k: (0, 0))
    return pl.pallas_call(
        functools.partial(_mlp_kernel, final_norm=final_norm),
        out_shape=jax.ShapeDtypeStruct((b, l, d), jnp.float32),
        grid=(b, l // tm, dff // tf),
        in_specs=[
            pl.BlockSpec((1, tm, d), lambda i, j, k: (i, j, 0)),
            row, vec, vec, vec,
            pl.BlockSpec((d, tf), lambda i, j, k: (0, k)),
            pl.BlockSpec((tf, d), lambda i, j, k: (k, 0)),
            row,
        ],
        out_specs=pl.BlockSpec((1, tm, d), lambda i, j, k: (i, j, 0)),
        scratch_shapes=[pltpu.VMEM((tm, d), jnp.bfloat16)],
        compiler_params=_cparams("parallel", "parallel", "arbitrary"),
        name="mlp_residual",
    )(x, nw.reshape(1, d), sc.reshape(b, 1, d), sh.reshape(b, 1, d), g.reshape(b, 1, d), w1, w2,
      fw.reshape(1, d))


def _split3(v):
    f32, bf16 = jnp.float32, jnp.bfloat16
    hi = v.astype(bf16)
    r1 = v - hi.astype(f32)
    mid = r1.astype(bf16)
    lo = (r1 - mid.astype(f32)).astype(bf16)
    return jnp.concatenate([hi, mid, lo], axis=1)


def _lane_repeat(v, rep):
    h = v.shape[1]
    row = lax.broadcasted_iota(jnp.int32, (3 * h, h * rep), 0) % h
    col = lax.broadcasted_iota(jnp.int32, (3 * h, h * rep), 1) // rep
    e = (row == col).astype(jnp.bfloat16)
    return jnp.dot(_split3(v), e, preferred_element_type=jnp.float32)


def _silu(v):
    return v * jax.nn.sigmoid(v)


def _softplus(v):
    return jnp.maximum(v, 0.0) + jnp.log1p(jnp.exp(-jnp.abs(v)))


def _token_conv3(u, w, bias, period):
    t, c = u.shape
    pos = lax.broadcasted_iota(jnp.int32, (t, c), 0) % period
    up = jnp.where(pos == 0, 0.0, pltpu.roll(u, 1, 0))
    dn = jnp.where(pos == period - 1, 0.0, pltpu.roll(u, t - 1, 0))
    return bias + w[0:1] * up + w[1:2] * u + w[2:3] * dn


def _ssd_chunk(xs, bm, cm, dtr, s_ref, a_row, dtb_row, reverse, heads, hpg):
    f32, bf16 = jnp.float32, jnp.bfloat16
    q = xs.shape[0]
    p = SSD_HEADDIM
    n = SSD_STATE
    gw = hpg * p
    dt = _softplus(dtr + dtb_row)
    a = dt * a_row
    ri = lax.broadcasted_iota(jnp.int32, (q, q), 0)
    ci = lax.broadcasted_iota(jnp.int32, (q, q), 1)
    keep = (ci >= ri) if reverse else (ci <= ri)
    a3 = jnp.dot(keep.astype(bf16), _split3(a), preferred_element_type=f32)
    cs = a3[:, :heads] + a3[:, heads:2 * heads] + a3[:, 2 * heads:]
    dt_rep = _lane_repeat(dt, p)
    cs_rep = _lane_repeat(cs, p)
    cs_wide = _lane_repeat(cs, q)
    end = 0 if reverse else q - 1
    cs_end = cs_rep[end:end + 1]
    xdt = xs * dt_rep
    xw = (xdt * jnp.exp(cs_end - cs_rep)).astype(bf16)
    ecs = jnp.exp(cs_rep)
    chunk_decay = jnp.exp(cs_end)
    xdt_b = xdt.astype(bf16)
    lane = lax.broadcasted_iota(jnp.int32, (q, 2 * p), 1)
    ys = []
    for g in range(SSD_GROUPS):
        bg = bm[:, g * n:(g + 1) * n].astype(bf16)
        cg = cm[:, g * n:(g + 1) * n].astype(bf16)
        cb = lax.dot_general(cg, bg, (((1,), (1,)), ((), ())), preferred_element_type=f32)
        s_old = s_ref[0, g]
        y_off = jnp.dot(cg, s_old.astype(bf16), preferred_element_type=f32) * ecs[:, g * gw:(g + 1) * gw]
        for pr in range(hpg // 2):
            xpair = xdt_b[:, g * gw + pr * 2 * p:g * gw + (pr + 1) * 2 * p]
            acc = None
            for k in range(2):
                h = g * hpg + pr * 2 + k
                csr = cs_wide[:, h * q:(h + 1) * q]
                seg = csr - csr.T
                gmat = (cb * jnp.where(keep, jnp.exp(seg), 0.0)).astype(bf16)
                xh = jnp.where((lane // p) == k, xpair, jnp.zeros_like(xpair))
                part = jnp.dot(gmat, xh, preferred_element_type=f32)
                acc = part if acc is None else acc + part
            lo = pr * 2 * p
            ys.append(acc + y_off[:, lo:lo + 2 * p])
        upd = jnp.dot(bg.T, xw[:, g * gw:(g + 1) * gw], preferred_element_type=f32)
        s_ref[0, g] = s_old * chunk_decay[:, g * gw:(g + 1) * gw] + upd
    return jnp.concatenate(ys, axis=1)


def _ssd_kernel(z_ref, xs_ref, bc_ref, dt_ref, cwx_ref, cbx_ref, cwb_ref, cbb_ref, dtb_ref, alog_ref, drep_ref,
                nw_ref, initf_ref, initb_ref, y_ref, sf_ref, sb_ref, yb_ref, *, nb, tb, period, heads, hpg):
    j = pl.program_id(1)
    q = SSD_CHUNK
    n = SSD_STATE
    gn = SSD_GROUPS * n
    nchunk = tb // q
    ssd_w = heads * SSD_HEADDIM

    @pl.when(j == 0)
    def _():
        sb_ref[...] = initb_ref[...]

    @pl.when(j == nb)
    def _():
        sf_ref[...] = initf_ref[...]

    xs_all = _silu(_token_conv3(xs_ref[0], cwx_ref[...], cbx_ref[...], period))
    bc_all = _silu(_token_conv3(bc_ref[0], cwb_ref[...], cbb_ref[...], period))
    a_all = -jnp.exp(alog_ref[...])

    def run(reverse):
        d = 1 if reverse else 0
        blk = (nb - 1 - j) if reverse else (j - nb)
        order = range(nchunk - 1, -1, -1) if reverse else range(nchunk)
        for ci in order:
            sl = slice(ci * q, (ci + 1) * q)
            y = _ssd_chunk(xs_all[sl], bc_all[sl, :gn], bc_all[sl, gn:], dt_ref[0, sl, d * heads:(d + 1) * heads],
                           sb_ref if reverse else sf_ref, a_all[d:d + 1], dtb_ref[d:d + 1], reverse, heads, hpg)
            row0 = pl.multiple_of(blk * tb + ci * q, q)
            if reverse:
                yb_ref[pl.ds(row0, q), :] = y
            else:
                y = y + yb_ref[pl.ds(row0, q), :] + drep_ref[...] * xs_all[sl]
                y = y * _silu(z_ref[0, sl, :])
                gw = ssd_w // SSD_GROUPS
                outs = []
                for g in range(SSD_GROUPS):
                    yg = y[:, g * gw:(g + 1) * gw]
                    ms = jnp.mean(yg * yg, axis=-1, keepdims=True)
                    outs.append(yg * lax.rsqrt(ms + EPS) * nw_ref[:, g * gw:(g + 1) * gw])
                y_ref[0, sl, :] = jnp.concatenate(outs, axis=1)

    @pl.when(j < nb)
    def _():
        run(True)

    @pl.when(j >= nb)
    def _():
        run(False)


def ssd_mixer(srcs, cols, conv_w_x, conv_b_x, conv_w_bc, conv_b_bc, dt_bias, a_log, ssd_d, norm_w, init_f, init_b,
              period, tb):
    b, l, _ = srcs[0].shape
    heads = dt_bias.shape[1]
    hpg = heads // SSD_GROUPS
    ssd_w = heads * SSD_HEADDIM
    gn = SSD_GROUPS * SSD_STATE
    nb = l // tb
    oz, ox, obc, odt = cols
    f32 = jnp.float32

    def tok(width, off):
        blk_idx = off // width
        return pl.BlockSpec((1, tb, width),
                            lambda i, j: (i, jnp.where(j < nb, nb - 1 - j, j - nb), blk_idx))

    def tok_fwd(width, off):
        blk_idx = off // width
        return pl.BlockSpec((1, tb, width), lambda i, j: (i, jnp.where(j < nb, 0, j - nb), blk_idx))

    def whole(shape):
        return pl.BlockSpec(shape, lambda i, j: (0,) * len(shape))

    st_shape = (b, SSD_GROUPS, SSD_STATE, hpg * SSD_HEADDIM)
    st_spec = pl.BlockSpec((1,) + st_shape[1:], lambda i, j: (i, 0, 0, 0))
    drep = jnp.repeat(ssd_d, SSD_HEADDIM).reshape(1, ssd_w)
    kern = functools.partial(_ssd_kernel, nb=nb, tb=tb, period=period, heads=heads, hpg=hpg)
    return pl.pallas_call(
        kern,
        out_shape=(jax.ShapeDtypeStruct((b, l, ssd_w), f32), jax.ShapeDtypeStruct(st_shape, f32),
                   jax.ShapeDtypeStruct(st_shape, f32)),
        grid=(b, 2 * nb),
        in_specs=[
            tok_fwd(ssd_w, oz), tok(ssd_w, ox), tok(2 * gn, obc), tok(LANE, odt),
            whole((3, ssd_w)), whole((1, ssd_w)), whole((3, 2 * gn)), whole((1, 2 * gn)),
            whole((2, heads)), whole((2, heads)), whole((1, ssd_w)), whole((1, ssd_w)),
            st_spec, st_spec,
        ],
        out_specs=(pl.BlockSpec((1, tb, ssd_w), lambda i, j: (i, jnp.where(j < nb, 0, j - nb), 0)),
                   st_spec, st_spec),
        scratch_shapes=[pltpu.VMEM((l, ssd_w), f32)],
        compiler_params=_cparams("parallel", "arbitrary"),
        name="ssd_mixer",
    )(*srcs, conv_w_x, conv_b_x.reshape(1, -1), conv_w_bc, conv_b_bc.reshape(1, -1), dt_bias, a_log, drep,
      norm_w.reshape(1, ssd_w), init_f, init_b)


def _pick(l, pref):
    return pref if l % pref == 0 else l


def kernel(x, c, ctx, c_ctx, w_ada, b_ada, norm1_w, w_in, hy_conv_w, hy_conv_b, filt_w1, filt_b1, filt_freq,
           filt_w2, filt_b2, filt_w3, hy_bias, hy_norm_w, ssd_conv_w, ssd_conv_b, dt_bias, a_log, ssd_d,
           ssd_norm_w, w_out, norm2_w, w_mlp1, w_mlp2, final_norm_w):
    depth = w_in.shape[0]
    bsz, seq, d = x.shape
    hy_w = hy_norm_w.shape[1]
    hy_proj = hy_conv_w.shape[2]
    ssd_w = ssd_norm_w.shape[1]
    ssd_xbc = ssd_conv_w.shape[2]
    heads = ssd_w // SSD_HEADDIM
    hpg = heads // SSD_GROUPS
    ssd_dt = 2 * heads
    o0, o1 = hy_proj, hy_proj + ssd_xbc
    o2 = o1 + ssd_dt
    dt_pad = (-ssd_dt) % LANE
    bf16 = jnp.bfloat16
    ssd_cols = (0, o0, o0 + ssd_w, ssd_w)
    tn = o1 // 3

    n1 = seq // LANE
    n1c = ctx.shape[1] // LANE
    zero_state = jnp.zeros((bsz, SSD_GROUPS, SSD_STATE, hpg * SSD_HEADDIM), jnp.float32)

    rows = 8
    c_all = jnp.zeros((rows, d), jnp.float32).at[:bsz].set(c).at[bsz].set(c_ctx)
    mod_all = ada_mod(c_all, w_ada, b_ada)

    w_a_all, w_b_all = cast_in_proj(w_in, o1, o2, w_in.shape[2], ssd_w + ssd_dt + dt_pad)

    h_ctx = ctx
    tm_l = _pick(seq, 512)
    tm_c = _pick(ctx.shape[1], 256)
    for i in range(depth):
        lp = dict(hy_conv_w=hy_conv_w[i], hy_conv_b=hy_conv_b[i], filt_w1=filt_w1[i], filt_b1=filt_b1[i],
                  filt_freq=filt_freq[i], filt_w2=filt_w2[i], filt_b2=filt_b2[i], filt_w3=filt_w3[i],
                  hy_bias=hy_bias[i], hy_norm_w=hy_norm_w[i], ssd_conv_w=ssd_conv_w[i],
                  ssd_conv_b=ssd_conv_b[i], dt_bias=dt_bias[i], a_log=a_log[i], ssd_d=ssd_d[i],
                  ssd_norm_w=ssd_norm_w[i])
        w_a, w_b = w_a_all[i], w_b_all[i]
        w_o_hy = w_out[i, :hy_w].astype(bf16)
        w_o_ssd = w_out[i, hy_w:].astype(bf16)
        w1 = w_mlp1[i].astype(bf16)
        w2 = w_mlp2[i].astype(bf16)

        mod = mod_all[i, :bsz].reshape(bsz, N_MOD, d)
        mod_c = jnp.broadcast_to(mod_all[i, bsz].reshape(1, N_MOD, d), (bsz, N_MOD, d))
        def ssd(pa, pb, init_f, init_b, period):
            cw, cb = lp['ssd_conv_w'], lp['ssd_conv_b']
            return ssd_mixer((pb, pa, pa, pb), ssd_cols, cw[:, :ssd_w], cb[:ssd_w], cw[:, ssd_w:], cb[ssd_w:],
                             lp['dt_bias'], lp['a_log'], lp['ssd_d'], lp['ssd_norm_w'], init_f, init_b, period,
                             _pick(pa.shape[1], 256))

        def filt_taps(length):
            return hyena_filter_taps(length, lp['filt_w1'], lp['filt_b1'], lp['filt_freq'], lp['filt_w2'],
                                     lp['filt_b2'], lp['filt_w3'], hy_w)

        pca, pcb = normmod_matmul(h_ctx, norm1_w[i], mod_c[:, 1], mod_c[:, 0], w_a, w_b, tm_c, tn)
        y_ssd, s_f, s_b = ssd(pca, pcb, zero_state, zero_state, pca.shape[1])
        if i < depth - 1:
            utc = hyena_pre(pca, 0, lp['hy_conv_w'], lp['hy_conv_b'], pca.shape[1], pca.shape[1], 512)
            kre, kim = ctx_filter_spectrum(filt_taps(pca.shape[1]), 128)
            zt = utc
            for o in range(HY_ORDER):
                zt = ctx_hyena_conv(utc, zt, 0, (o + 1) * hy_w, kre, kim, o, lp['hy_bias'][o], 128)
            h_ctx = proj_residual(zt, y_ssd, lp['hy_norm_w'], w_o_hy, w_o_ssd, h_ctx, mod_c[:, 2], tm_c,
                                  HY_GROUPS)
            h_ctx = mlp_residual(h_ctx, norm2_w[i], mod_c[:, 4], mod_c[:, 3], mod_c[:, 5], w1, w2,
                                 final_norm_w, tm_c, 512, False)
        pla, plb = normmod_matmul(x, norm1_w[i], mod[:, 1], mod[:, 0], w_a, w_b, tm_l, tn)
        ut = hyena_pre_rows(pla, 0, lp['hy_conv_w'], lp['hy_conv_b'], GRID_W, LANE)
        kre, kim = hyena_filter_spectrum(filt_taps(seq), bsz, 16)
        zt = ut
        for o in range(HY_ORDER):
            zt = hyena_conv(ut, zt, 0, (o + 1) * hy_w, kre, kim, o, lp['hy_bias'][o], bsz, 16, o == HY_ORDER - 1)
        zh_t = zt.reshape(bsz, n1, hy_w, LANE)
        y_ssd, _, _ = ssd(pla, plb, s_f, s_b, GRID_W)
        x = proj_residual(zh_t, y_ssd, lp['hy_norm_w'], w_o_hy, w_o_ssd, x, mod[:, 2], tm_l, HY_GROUPS)
        x = mlp_residual(x, norm2_w[i], mod[:, 4], mod[:, 3], mod[:, 5], w1, w2, final_norm_w, tm_l, 512,
                         i == depth - 1)
    return x
```

```python
import functools
import math

import numpy as np
import jax
import jax.numpy as jnp
from jax import lax
from jax.experimental import pallas as pl
from jax.experimental.pallas import tpu as pltpu

EPS = 1e-6
N_MOD = 6
GRID_W = 64
HY_GROUPS = 8
HY_ORDER = 2
HY_POS_EMB = 33
HY_TARGET = 1e-2
HY_FAST = 0.3
HY_SLOW = 1.5
SSD_HEADDIM = 64
SSD_GROUPS = 2
SSD_STATE = 128
SSD_CHUNK = 128

LANE = 128
VMEM_LIMIT = 56 * 1024 * 1024


def _cparams(*sem):
    return pltpu.CompilerParams(dimension_semantics=sem, vmem_limit_bytes=VMEM_LIMIT)


def _ada_kernel(c_ref, w_ref, b_ref, o_ref):
    c = c_ref[...]
    a = (c * jax.nn.sigmoid(c)).astype(jnp.bfloat16)
    w = w_ref[0].astype(jnp.bfloat16)
    o_ref[0] = jnp.dot(a, w, preferred_element_type=jnp.float32) + b_ref[0]


def ada_mod(c_all, w_ada, b_ada, tn=1024):
    depth, d, n = w_ada.shape
    r = c_all.shape[0]
    return pl.pallas_call(
        _ada_kernel,
        out_shape=jax.ShapeDtypeStruct((depth, r, n), jnp.float32),
        grid=(depth, n // tn),
        in_specs=[
            pl.BlockSpec((r, d), lambda i, j: (0, 0)),
            pl.BlockSpec((1, d, tn), lambda i, j: (i, 0, j)),
            pl.BlockSpec((1, 1, tn), lambda i, j: (i, 0, j)),
        ],
        out_specs=pl.BlockSpec((1, r, tn), lambda i, j: (i, 0, j)),
        compiler_params=_cparams("parallel", "parallel"),
        name="ada_mod",
    )(c_all, w_ada, b_ada.reshape(depth, 1, n))


def _wcast_kernel(w_ref, a_ref, b_ref, *, na, c0, c1):
    bf16 = jnp.bfloat16
    w = w_ref[0]
    a_ref[0] = w[:, :na].astype(bf16)
    nb = b_ref.shape[2]
    pad = nb - (c1 - c0) - (c0 - na)
    b_ref[0] = jnp.concatenate([w[:, c0:c1], w[:, na:c0], jnp.zeros((w.shape[0], pad), w.dtype)], axis=1).astype(bf16)


def cast_in_proj(w_in, na, c0, c1, nb, tr=256):
    depth, d, n = w_in.shape
    return pl.pallas_call(
        functools.partial(_wcast_kernel, na=na, c0=c0, c1=c1),
        out_shape=(jax.ShapeDtypeStruct((depth, d, na), jnp.bfloat16),
                   jax.ShapeDtypeStruct((depth, d, nb), jnp.bfloat16)),
        grid=(depth, d // tr),
        in_specs=[pl.BlockSpec((1, tr, n), lambda i, j: (i, j, 0))],
        out_specs=(pl.BlockSpec((1, tr, na), lambda i, j: (i, j, 0)),
                   pl.BlockSpec((1, tr, nb), lambda i, j: (i, j, 0))),
        compiler_params=_cparams("parallel", "parallel"),
        name="cast_in_proj",
    )(w_in)


NORM_ROWS = 16


def _normmod_store(h_ref, x_ref, nw, sc, sh):
    w = nw * (1.0 + sc)

    def body(i, carry):
        r0 = pl.multiple_of(i * NORM_ROWS, NORM_ROWS)
        xb = x_ref[0, pl.ds(r0, NORM_ROWS), :]
        ms = jnp.mean(xb * xb, axis=-1, keepdims=True)
        h_ref[pl.ds(r0, NORM_ROWS), :] = ((xb * lax.rsqrt(ms + EPS)) * w + sh).astype(h_ref.dtype)
        return carry

    lax.fori_loop(0, h_ref.shape[0] // NORM_ROWS, body, 0, unroll=8)


def _nm_matmul_kernel(x_ref, nw_ref, sc_ref, sh_ref, wa_ref, wb_ref, oa_ref, ob_ref, h_ref, *, na):
    k = pl.program_id(2)

    @pl.when(k == 0)
    def _():
        _normmod_store(h_ref, x_ref, nw_ref[...], sc_ref[0], sh_ref[0])

    @pl.when(k < na)
    def _():
        oa_ref[0] = jnp.dot(h_ref[...], wa_ref[...], preferred_element_type=jnp.float32)

    @pl.when(k >= na)
    def _():
        ob_ref[0] = jnp.dot(h_ref[...], wb_ref[...], preferred_element_type=jnp.float32)


def normmod_matmul(x, nw, sc, sh, wa, wb, layer, tm, tn):
    b, l, d = x.shape
    n_a, n_b = wa.shape[2], wb.shape[2]
    na = n_a // tn
    last = na - 1
    return pl.pallas_call(
        functools.partial(_nm_matmul_kernel, na=na),
        out_shape=(jax.ShapeDtypeStruct((b, l, n_a), jnp.float32), jax.ShapeDtypeStruct((b, l, n_b), jnp.float32)),
        grid=(b, l // tm, na + 1),
        in_specs=[
            pl.BlockSpec((1, tm, d), lambda i, j, k: (i, j, 0)),
            pl.BlockSpec((1, d), lambda i, j, k: (0, 0)),
            pl.BlockSpec((1, 1, d), lambda i, j, k: (i, 0, 0)),
            pl.BlockSpec((1, 1, d), lambda i, j, k: (i, 0, 0)),
            pl.BlockSpec((None, d, tn), lambda i, j, k: (layer, 0, jnp.minimum(k, last))),
            pl.BlockSpec((None, d, n_b), lambda i, j, k: (layer, 0, 0)),
        ],
        out_specs=(pl.BlockSpec((1, tm, tn), lambda i, j, k: (i, j, jnp.minimum(k, last))),
                   pl.BlockSpec((1, tm, n_b), lambda i, j, k: (i, j, 0))),
        scratch_shapes=[pltpu.VMEM((tm, d), jnp.bfloat16)],
        compiler_params=_cparams("parallel", "parallel", "arbitrary"),
        name="normmod_matmul",
    )(x, nw.reshape(1, d), sc.reshape(b, 1, d), sh.reshape(b, 1, d), wa, wb)


def _proj_res_kernel(zh_ref, ys_ref, nwh_ref, wh_ref, ws_ref, x_ref, g_ref, o_ref, *, groups):
    bf16 = jnp.bfloat16
    slabs, hy_w = zh_ref.shape[1], zh_ref.shape[2]
    gs = hy_w // groups
    rows = []
    for s in range(slabs):
        cols = []
        for gi in range(groups):
            zg = zh_ref[0, s, gi * gs:(gi + 1) * gs, :]
            ms = jnp.mean(zg * zg, axis=0, keepdims=True)
            cols.append((zg * lax.rsqrt(ms + EPS)).T)
        rows.append(jnp.concatenate(cols, axis=1))
    yh = (jnp.concatenate(rows, axis=0) * nwh_ref[...]).astype(bf16)
    acc = jnp.dot(yh, wh_ref[...], preferred_element_type=jnp.float32)
    acc += jnp.dot(ys_ref[0].astype(bf16), ws_ref[...], preferred_element_type=jnp.float32)
    o_ref[0] = x_ref[0] + g_ref[0] * acc


def proj_residual(zh_t, y_ssd, hy_norm_w, w_out, layer, x, g, tm, groups):
    b, n1, hy_w, _ = zh_t.shape
    l, ssd_w = y_ssd.shape[1], y_ssd.shape[2]
    n = w_out.shape[2]
    assert hy_w % ssd_w == 0
    return pl.pallas_call(
        functools.partial(_proj_res_kernel, groups=groups),
        out_shape=jax.ShapeDtypeStruct((b, l, n), jnp.float32),
        grid=(b, l // tm),
        in_specs=[
            pl.BlockSpec((1, tm // LANE, hy_w, LANE), lambda i, j: (i, j, 0, 0)),
            pl.BlockSpec((1, tm, ssd_w), lambda i, j: (i, j, 0)),
            pl.BlockSpec((1, hy_w), lambda i, j: (0, 0)),
            pl.BlockSpec((None, hy_w, n), lambda i, j: (layer, 0, 0)),
            pl.BlockSpec((None, ssd_w, n), lambda i, j: (layer, hy_w // ssd_w, 0)),
            pl.BlockSpec((1, tm, n), lambda i, j: (i, j, 0)),
            pl.BlockSpec((1, 1, n), lambda i, j: (i, 0, 0)),
        ],
        out_specs=pl.BlockSpec((1, tm, n), lambda i, j: (i, j, 0)),
        compiler_params=_cparams("parallel", "parallel"),
        name="proj_residual",
    )(zh_t, y_ssd, hy_norm_w.reshape(1, hy_w), w_out, w_out, x, g.reshape(b, 1, n))


def _hypre_kernel(u_ref, w_ref, b_ref, o_ref, *, period):
    u = _token_conv3(u_ref[0], w_ref[...], b_ref[...], period)
    tb, cw = u.shape
    for s in range(tb // LANE):
        for j in range(cw // LANE):
            o_ref[0, s, j * LANE:(j + 1) * LANE, :] = u[s * LANE:(s + 1) * LANE, j * LANE:(j + 1) * LANE].T


def hyena_pre(p, col0, conv_w, conv_b, period, tb, cw):
    b, l, _ = p.shape
    c = conv_w.shape[1]
    blk0 = col0 // cw
    return pl.pallas_call(
        functools.partial(_hypre_kernel, period=period),
        out_shape=jax.ShapeDtypeStruct((b, l // LANE, c, LANE), jnp.float32),
        grid=(b, l // tb, c // cw),
        in_specs=[
            pl.BlockSpec((1, tb, cw), lambda i, j, k: (i, j, blk0 + k)),
            pl.BlockSpec((3, cw), lambda i, j, k: (0, k)),
            pl.BlockSpec((1, cw), lambda i, j, k: (0, k)),
        ],
        out_specs=pl.BlockSpec((1, tb // LANE, cw, LANE), lambda i, j, k: (i, j, k, 0)),
        compiler_params=_cparams("parallel", "parallel", "parallel"),
        name="hyena_pre",
    )(p, conv_w, conv_b.reshape(1, c))


def _hypre_rows_kernel(u_ref, w_ref, b_ref, o_ref, s_ref, *, period):
    u = _token_conv3(u_ref[0], w_ref[...], b_ref[...], period)
    l, cw = u.shape
    n1 = l // LANE
    for s in range(n1):
        for j in range(cw // LANE):
            s_ref[s * cw + j * LANE:s * cw + (j + 1) * LANE, :] = u[s * LANE:(s + 1) * LANE,
                                                                    j * LANE:(j + 1) * LANE].T
    for c in range(cw):
        o_ref[:, c * LANE:(c + 1) * LANE] = s_ref[pl.ds(c, n1, stride=cw), :]


def hyena_pre_rows(p, col0, conv_w, conv_b, period, cw):
    b, l, _ = p.shape
    c = conv_w.shape[1]
    n1 = l // LANE
    blk0 = col0 // cw
    return pl.pallas_call(
        functools.partial(_hypre_rows_kernel, period=period),
        out_shape=jax.ShapeDtypeStruct((b * n1, c * LANE), jnp.float32),
        grid=(b, c // cw),
        in_specs=[
            pl.BlockSpec((1, l, cw), lambda i, k: (i, 0, blk0 + k)),
            pl.BlockSpec((3, cw), lambda i, k: (0, k)),
            pl.BlockSpec((1, cw), lambda i, k: (0, k)),
        ],
        out_specs=pl.BlockSpec((n1, cw * LANE), lambda i, k: (i, k)),
        scratch_shapes=[pltpu.VMEM((n1 * cw, LANE), jnp.float32)],
        compiler_params=_cparams("parallel", "parallel"),
        name="hyena_pre_rows",
    )(p, conv_w, conv_b.reshape(1, c))


def _fft_constants(bsz, n1):
    m1 = 2 * n1
    n = m1 * LANE
    pairs = bsz // 2
    half = pairs * m1
    r = bsz * n1
    t1 = np.arange(n1)[:, None]
    f1 = np.arange(m1)[None, :]
    th = 2.0 * np.pi * t1 * f1 / m1
    w1 = np.zeros((r, 2 * half))
    for pr in range(pairs):
        re = slice(pr * m1, (pr + 1) * m1)
        im = slice(half + pr * m1, half + (pr + 1) * m1)
        ra = slice((2 * pr) * n1, (2 * pr + 1) * n1)
        rb = slice((2 * pr + 1) * n1, (2 * pr + 2) * n1)
        w1[ra, re], w1[ra, im] = np.cos(th), -np.sin(th)
        w1[rb, re], w1[rb, im] = np.sin(th), np.cos(th)
    w4 = w1.T / n
    t2 = np.arange(LANE)[:, None]
    ps = 2.0 * np.pi * t2 * np.arange(LANE)[None, :] / LANE
    w2 = np.block([[np.cos(ps), -np.sin(ps)], [np.sin(ps), np.cos(ps)]])
    ph = 2.0 * np.pi * t2 * f1 / n
    tc = np.tile(np.cos(ph), (1, pairs))
    ts = np.tile(np.sin(ph), (1, pairs))
    orders = half // m1
    tf = 2.0 * np.pi * np.arange(m1)[:, None] * f1 / m1
    w1f = np.zeros((orders * m1, 2 * half))
    for o in range(orders):
        w1f[o * m1:(o + 1) * m1, o * m1:(o + 1) * m1] = np.cos(tf)
        w1f[o * m1:(o + 1) * m1, half + o * m1:half + (o + 1) * m1] = -np.sin(tf)
    mats = [jnp.asarray(m, jnp.bfloat16) for m in (w1, w2, w2.T, w4)]
    tabs = [jnp.asarray(m, jnp.float32) for m in (tc, ts, tc.T, ts.T)]
    return mats, tabs, half, jnp.asarray(w1f, jnp.bfloat16)


def _mxu(a, w):
    return jnp.dot(a.astype(jnp.bfloat16), w, preferred_element_type=jnp.float32)


def _fft_forward(x0, w1, w2, tc, ts, half):
    cw = x0.shape[1] // LANE
    a1 = _mxu(x0.T, w1)
    lhs2 = []
    for c in range(cw):
        ar = a1[c * LANE:(c + 1) * LANE, :half]
        ai = a1[c * LANE:(c + 1) * LANE, half:]
        lhs2.append(jnp.concatenate([(ar * tc + ai * ts).T, (ai * tc - ar * ts).T], axis=1))
    return _mxu(jnp.concatenate(lhs2, axis=0), w2)


def _hyconv_kernel(z_ref, g_ref, kre_ref, kim_ref, brep_ref, w1_ref, w2_ref, w2i_ref, w4_ref, tc_ref, ts_ref,
                   tct_ref, tst_ref, o_ref, *, half):
    x0 = z_ref[...]
    cw = x0.shape[1] // LANE
    tct, tst = tct_ref[...], tst_ref[...]
    s = _fft_forward(x0, w1_ref[...], w2_ref[...], tc_ref[...], ts_ref[...], half)
    sr, si = s[:, :LANE], s[:, LANE:]
    m1 = kre_ref.shape[2]
    pairs = half // m1
    kr = jnp.concatenate([kre_ref[:, 0]] * pairs, axis=1).reshape(cw * half, LANE)
    ki = jnp.concatenate([kim_ref[:, 0]] * pairs, axis=1).reshape(cw * half, LANE)
    y2 = jnp.concatenate([sr * kr - si * ki, sr * ki + si * kr], axis=1)
    bq = _mxu(y2, w2i_ref[...])
    lhs4 = []
    for c in range(cw):
        br = bq[c * half:(c + 1) * half, :LANE]
        bi = bq[c * half:(c + 1) * half, LANE:]
        lhs4.append(jnp.concatenate([(br * tct - bi * tst).T, (bi * tct + br * tst).T], axis=1))
    lhs4 = jnp.concatenate(lhs4, axis=0)
    yt = _mxu(lhs4, w4_ref[...])
    if len(o_ref.shape) == 2:
        o_ref[...] = g_ref[...] * (yt.T + x0 * brep_ref[...])
    else:
        for c in range(cw):
            sl = slice(c * LANE, (c + 1) * LANE)
            o_ref[:, c, :] = g_ref[:, sl] * (yt[sl].T + x0[:, sl] * brep_ref[:, sl])


def hyena_conv(ut, z_src, z_col, g_col, kre, kim, order, bias, bsz, cw, channel_tiles):
    r = ut.shape[0]
    n1 = r // bsz
    c = bias.shape[0]
    mats, tabs, half, _ = _fft_constants(bsz, n1)
    blk = cw * LANE
    zb, gb = z_col // cw, g_col // cw
    brep = jnp.repeat(bias, LANE).reshape(1, c * LANE)
    kspec = pl.BlockSpec((cw, 1, 2 * n1, LANE), lambda j: (j, order, 0, 0))

    def const(a):
        return pl.BlockSpec(a.shape, lambda j: (0, 0))

    return pl.pallas_call(
        functools.partial(_hyconv_kernel, half=half),
        out_shape=jax.ShapeDtypeStruct((r, c, LANE) if channel_tiles else (r, c * LANE), jnp.float32),
        grid=(c // cw,),
        in_specs=[
            pl.BlockSpec((r, blk), lambda j: (0, zb + j)),
            pl.BlockSpec((r, blk), lambda j: (0, gb + j)),
            kspec, kspec,
            pl.BlockSpec((1, blk), lambda j: (0, j)),
        ] + [const(a) for a in mats + tabs],
        out_specs=(pl.BlockSpec((r, cw, LANE), lambda j: (0, j, 0)) if channel_tiles
                   else pl.BlockSpec((r, blk), lambda j: (0, j))),
        compiler_params=_cparams("parallel"),
        name="hyena_conv",
    )(z_src, ut, kre, kim, brep, *mats, *tabs)


_HI = lax.Precision.HIGHEST


def _filtgen_kernel(a0_ref, wc_ref, ws_ref, fb1_ref, freq_ref, fw2t_ref, fb2_ref, fw3t_ref, dl_ref, o_ref, *,
                    seq, bands):
    f32 = jnp.float32
    s = pl.program_id(0)
    t = s * LANE + lax.broadcasted_iota(jnp.int32, (1, LANE), 1)
    pos = jnp.where(t < seq, t, 2 * seq - t).astype(f32)
    tt = pos / (seq - 1.0)
    ang = (2.0 * math.pi / seq) * pos
    j = lax.broadcasted_iota(jnp.int32, (bands, 1), 0).astype(f32)
    fj = 1e-4 + j * ((bands - 1.0 - 1e-4) / (bands - 1.0))
    fa = fj * ang
    freq = freq_ref[...]
    pre = a0_ref[...] * tt + jnp.dot(wc_ref[...], jnp.cos(fa), precision=_HI, preferred_element_type=f32) \
        - jnp.dot(ws_ref[...], jnp.sin(fa), precision=_HI, preferred_element_type=f32) + fb1_ref[...]
    h1 = jnp.sin(freq * pre)
    h2 = jnp.sin(freq * (jnp.dot(fw2t_ref[...], h1, precision=_HI, preferred_element_type=f32) + fb2_ref[...]))
    orders, _, c, hid = fw3t_ref.shape
    w3 = fw3t_ref[:, 0].reshape(orders * c, hid)
    h = jnp.dot(w3, h2, precision=_HI, preferred_element_type=f32)
    h = h * jnp.exp(-tt * dl_ref[...])
    h = jnp.where(t == seq, 0.0, h)
    o_ref[:, 0] = h.reshape(orders, c, LANE)


def hyena_filter_taps(seq, fw1, fb1, freq, fw2, fb2, fw3, hy_w):
    hid = fw2.shape[0]
    bands = (HY_POS_EMB - 1) // 2
    orders = fw3.shape[1] // (2 * hy_w)
    slabs = 2 * seq // LANE
    col = lambda v: v.reshape(-1, 1)
    fw3t = fw3.T.reshape(orders, 2, hy_w, hid)
    deltas = jnp.abs(jnp.linspace(math.log(HY_TARGET) / HY_SLOW, math.log(HY_TARGET) / HY_FAST, hy_w,
                                  dtype=jnp.float32))
    dl = jnp.tile(deltas, orders).reshape(-1, 1)

    def whole(a):
        return pl.BlockSpec(a.shape, lambda s: (0,) * a.ndim)

    args = [col(fw1[0]), fw1[1:1 + bands].T, fw1[1 + bands:].T, col(fb1), col(freq), fw2.T, col(fb2)]
    return pl.pallas_call(
        functools.partial(_filtgen_kernel, seq=seq, bands=bands),
        out_shape=jax.ShapeDtypeStruct((orders, slabs, hy_w, LANE), jnp.float32),
        grid=(slabs,),
        in_specs=[whole(a) for a in args] + [
            pl.BlockSpec((orders, 1, hy_w, hid), lambda s: (0, s // (slabs // 2), 0, 0)),
            whole(dl),
        ],
        out_specs=pl.BlockSpec((orders, 1, hy_w, LANE), lambda s: (0, s, 0, 0)),
        compiler_params=_cparams("parallel"),
        name="hyena_filter_taps",
    )(*args, fw3t, dl)


def _lane_block_abs_norm(x, orders):
    rows_per = x.shape[0] // orders
    cw = x.shape[1] // LANE
    out = []
    for o in range(orders):
        xo = x[o * rows_per:(o + 1) * rows_per]
        a = jnp.sum(jnp.abs(xo), axis=0, keepdims=True)
        inv = [jnp.broadcast_to(1.0 / (jnp.sum(a[:, c * LANE:(c + 1) * LANE], axis=1, keepdims=True) + EPS),
                                (1, LANE)) for c in range(cw)]
        out.append(xo * jnp.concatenate(inv, axis=1))
    return jnp.concatenate(out, axis=0)


def _filtspec_kernel(k_ref, w1_ref, w2_ref, tc_ref, ts_ref, re_ref, im_ref, *, half, orders):
    m1 = half // orders
    x0 = _lane_block_abs_norm(k_ref[...], orders)
    s = _fft_forward(x0, w1_ref[...], w2_ref[...], tc_ref[...], ts_ref[...], half)
    cw = x0.shape[1] // LANE
    re_ref[...] = s[:, :LANE].reshape(cw, orders, m1, LANE)
    im_ref[...] = s[:, LANE:].reshape(cw, orders, m1, LANE)


def hyena_filter_spectrum(taps, bsz, cw):
    orders, m1, c, _ = taps.shape
    mats, tabs, half, w1f = _fft_constants(bsz, m1 // 2)
    assert half == orders * m1
    consts = [w1f, mats[1], tabs[0], tabs[1]]
    out = jax.ShapeDtypeStruct((c, orders, m1, LANE), jnp.float32)
    ospec = pl.BlockSpec((cw, orders, m1, LANE), lambda j: (j, 0, 0, 0))
    return pl.pallas_call(
        functools.partial(_filtspec_kernel, half=half, orders=orders),
        out_shape=(out, out),
        grid=(c // cw,),
        in_specs=[pl.BlockSpec((orders * m1, cw * LANE), lambda j: (0, j))] +
                 [pl.BlockSpec(a.shape, lambda j: (0, 0)) for a in consts],
        out_specs=(ospec, ospec),
        compiler_params=_cparams("parallel"),
        name="hyena_filter_spectrum",
    )(taps.reshape(orders * m1, c * LANE), *consts)


def _dense_dft_constants(seq):
    n = 2 * seq
    t = np.arange(seq)[:, None]
    f = np.arange(n)[None, :]
    ps = 2.0 * np.pi * t * f / n
    fwd = np.block([[np.cos(ps), -np.sin(ps)], [np.sin(ps), np.cos(ps)]])
    inv = fwd.T / n
    tk = np.arange(n)[:, None]
    pk = 2.0 * np.pi * tk * f / n
    fk = np.concatenate([np.cos(pk), -np.sin(pk)], axis=1)
    return [jnp.asarray(m, jnp.bfloat16) for m in (fwd, inv, fk)]


def _ctxspec_kernel(k_ref, fk_ref, re_ref, im_ref):
    orders, slabs = k_ref.shape[0], k_ref.shape[1]
    n = slabs * LANE
    for o in range(orders):
        x = jnp.concatenate([k_ref[o, s] for s in range(slabs)], axis=1)
        x = x / (jnp.sum(jnp.abs(x), axis=1, keepdims=True) + EPS)
        kf = _mxu(x, fk_ref[...])
        re_ref[o] = kf[:, :n]
        im_ref[o] = kf[:, n:]


def ctx_filter_spectrum(taps, cw):
    orders, slabs, c, _ = taps.shape
    n = slabs * LANE
    fk = _dense_dft_constants(n // 2)[2]
    out = jax.ShapeDtypeStruct((orders, c, n), jnp.float32)
    ospec = pl.BlockSpec((orders, cw, n), lambda j: (0, j, 0))
    return pl.pallas_call(
        _ctxspec_kernel,
        out_shape=(out, out),
        grid=(c // cw,),
        in_specs=[pl.BlockSpec((orders, slabs, cw, LANE), lambda j: (0, 0, j, 0)),
                  pl.BlockSpec(fk.shape, lambda j: (0, 0))],
        out_specs=(ospec, ospec),
        compiler_params=_cparams("parallel"),
        name="ctx_filter_spectrum",
    )(taps, fk)


def _ctxconv_kernel(z_ref, g_ref, kre_ref, kim_ref, b_ref, fwd_ref, inv_ref, o_ref):
    bsz, slabs, cw = z_ref.shape[0], z_ref.shape[1], z_ref.shape[2]
    seq = slabs * LANE
    zs = [jnp.concatenate([z_ref[b, s] for s in range(slabs)], axis=1) for b in range(bsz)]
    lhs = jnp.concatenate([jnp.concatenate([zs[2 * p], zs[2 * p + 1]], axis=1) for p in range(bsz // 2)], axis=0)
    s = _mxu(lhs, fwd_ref[...])
    n = 2 * seq
    sr, si = s[:, :n], s[:, n:]
    kr = jnp.concatenate([kre_ref[0]] * (bsz // 2), axis=0)
    ki = jnp.concatenate([kim_ref[0]] * (bsz // 2), axis=0)
    y = _mxu(jnp.concatenate([sr * kr - si * ki, sr * ki + si * kr], axis=1), inv_ref[...])
    bias = jnp.concatenate([b_ref[...]] * slabs, axis=1)
    for b in range(bsz):
        p, m = b // 2, b % 2
        yb = y[p * cw:(p + 1) * cw, m * seq:(m + 1) * seq]
        gate = jnp.concatenate([g_ref[b, s] for s in range(slabs)], axis=1)
        res = gate * (yb + zs[b] * bias)
        for sl in range(slabs):
            o_ref[b, sl] = res[:, sl * LANE:(sl + 1) * LANE]


def ctx_hyena_conv(ut, z_src, z_col, g_col, kre, kim, order, bias, cw):
    bsz, slabs, _, _ = ut.shape
    c = bias.shape[0]
    n = 2 * slabs * LANE
    fwd, inv, _ = _dense_dft_constants(slabs * LANE)
    zb, gb = z_col // cw, g_col // cw
    bb = jnp.broadcast_to(bias[:, None], (c, LANE))
    kspec = pl.BlockSpec((1, cw, n), lambda j: (order, j, 0))
    return pl.pallas_call(
        _ctxconv_kernel,
        out_shape=jax.ShapeDtypeStruct((bsz, slabs, c, LANE), jnp.float32),
        grid=(c // cw,),
        in_specs=[
            pl.BlockSpec((bsz, slabs, cw, LANE), lambda j: (0, 0, zb + j, 0)),
            pl.BlockSpec((bsz, slabs, cw, LANE), lambda j: (0, 0, gb + j, 0)),
            kspec, kspec,
            pl.BlockSpec((cw, LANE), lambda j: (j, 0)),
            pl.BlockSpec(fwd.shape, lambda j: (0, 0)),
            pl.BlockSpec(inv.shape, lambda j: (0, 0)),
        ],
        out_specs=pl.BlockSpec((bsz, slabs, cw, LANE), lambda j: (0, 0, j, 0)),
        compiler_params=_cparams("parallel"),
        name="ctx_hyena_conv",
    )(z_src, ut, kre, kim, bb, fwd, inv)


def _mlp_kernel(x_ref, nw_ref, sc_ref, sh_ref, g_ref, w1_ref, w2_ref, fw_ref, o_ref, h_ref, *, final_norm):
    f = pl.program_id(2)

    @pl.when(f == 0)
    def _():
        _normmod_store(h_ref, x_ref, nw_ref[...], sc_ref[0], sh_ref[0])
        o_ref[...] = jnp.zeros_like(o_ref)

    a = jnp.dot(h_ref[...], w1_ref[...], preferred_element_type=jnp.float32)
    a = jnp.square(jnp.maximum(a, 0.0)).astype(jnp.bfloat16)
    o_ref[0] += jnp.dot(a, w2_ref[...], preferred_element_type=jnp.float32)

    @pl.when(f == pl.num_programs(2) - 1)
    def _():
        y = x_ref[0] + g_ref[0] * o_ref[0]
        if final_norm:
            ms = jnp.mean(y * y, axis=-1, keepdims=True)
            y = y * lax.rsqrt(ms + EPS) * fw_ref[...]
        o_ref[0] = y


def mlp_residual(x, nw, sc, sh, g, w1, w2, layer, fw, tm, tf, final_norm):
    b, l, d = x.shape
    dff = w1.shape[2]
    vec = pl.BlockSpec((1, 1, d), lambda i, j, k: (i, 0, 0))
    row = pl.BlockSpec((1, d), lambda i, j, k: (0, 0))
    return pl.pallas_call(
        functools.partial(_mlp_kernel, final_norm=final_norm),
        out_shape=jax.ShapeDtypeStruct((b, l, d), jnp.float32),
        grid=(b, l // tm, dff // tf),
        in_specs=[
            pl.BlockSpec((1, tm, d), lambda i, j, k: (i, j, 0)),
            row, vec, vec, vec,
            pl.BlockSpec((None, d, tf), lambda i, j, k: (layer, 0, k)),
            pl.BlockSpec((None, tf, d), lambda i, j, k: (layer, k, 0)),
            row,
        ],
        out_specs=pl.BlockSpec((1, tm, d), lambda i, j, k: (i, j, 0)),
        scratch_shapes=[pltpu.VMEM((tm, d), jnp.bfloat16)],
        compiler_params=_cparams("parallel", "parallel", "arbitrary"),
        name="mlp_residual",
    )(x, nw.reshape(1, d), sc.reshape(b, 1, d), sh.reshape(b, 1, d), g.reshape(b, 1, d), w1, w2,
      fw.reshape(1, d))


def _split3(v):
    f32, bf16 = jnp.float32, jnp.bfloat16
    hi = v.astype(bf16)
    r1 = v - hi.astype(f32)
    mid = r1.astype(bf16)
    lo = (r1 - mid.astype(f32)).astype(bf16)
    return jnp.concatenate([hi, mid, lo], axis=1)


def _lane_repeat(v, rep):
    h = v.shape[1]
    row = lax.broadcasted_iota(jnp.int32, (3 * h, h * rep), 0) % h
    col = lax.broadcasted_iota(jnp.int32, (3 * h, h * rep), 1) // rep
    e = (row == col).astype(jnp.bfloat16)
    return jnp.dot(_split3(v), e, preferred_element_type=jnp.float32)


def _silu(v):
    h = 0.5 * v
    return h + h * jnp.tanh(h)


def _softplus(v):
    return jnp.maximum(v, 0.0) + jnp.log1p(jnp.exp(-jnp.abs(v)))


def _token_conv3(u, w, bias, period):
    t, c = u.shape
    pos = lax.broadcasted_iota(jnp.int32, (t, c), 0) % period
    up = jnp.where(pos == 0, 0.0, pltpu.roll(u, 1, 0))
    dn = jnp.where(pos == period - 1, 0.0, pltpu.roll(u, t - 1, 0))
    return bias + w[0:1] * up + w[1:2] * u + w[2:3] * dn


def _ssd_chunk(xs, bm, cm, dtr, s_ref, a_row, dtb_row, reverse, heads, hpg):
    f32, bf16 = jnp.float32, jnp.bfloat16
    q = xs.shape[0]
    p = SSD_HEADDIM
    n = SSD_STATE
    gw = hpg * p
    dt = _softplus(dtr + dtb_row)
    a = dt * a_row
    ri = lax.broadcasted_iota(jnp.int32, (q, q), 0)
    ci = lax.broadcasted_iota(jnp.int32, (q, q), 1)
    keep = (ci >= ri) if reverse else (ci <= ri)
    a3 = jnp.dot(keep.astype(bf16), _split3(a), preferred_element_type=f32)
    cs = a3[:, :heads] + a3[:, heads:2 * heads] + a3[:, 2 * heads:]
    dt_rep = _lane_repeat(dt, p)
    cs_rep = _lane_repeat(cs, p)
    cs_wide = _lane_repeat(cs, q)
    end = 0 if reverse else q - 1
    cs_end = cs_rep[end:end + 1]
    xdt = xs * dt_rep
    xw = (xdt * jnp.exp(cs_end - cs_rep)).astype(bf16)
    ecs = jnp.exp(cs_rep)
    chunk_decay = jnp.exp(cs_end)
    xdt_b = xdt.astype(bf16)
    lane = lax.broadcasted_iota(jnp.int32, (q, 2 * p), 1)
    ys = []
    for g in range(SSD_GROUPS):
        bg = bm[:, g * n:(g + 1) * n].astype(bf16)
        cg = cm[:, g * n:(g + 1) * n].astype(bf16)
        cb = lax.dot_general(cg, bg, (((1,), (1,)), ((), ())), preferred_element_type=f32)
        s_old = s_ref[0, g]
        y_off = jnp.dot(cg, s_old.astype(bf16), preferred_element_type=f32) * ecs[:, g * gw:(g + 1) * gw]
        for pr in range(hpg // 2):
            xpair = xdt_b[:, g * gw + pr * 2 * p:g * gw + (pr + 1) * 2 * p]
            acc = None
            for k in range(2):
                h = g * hpg + pr * 2 + k
                csr = cs_wide[:, h * q:(h + 1) * q]
                seg = csr - csr.T
                gmat = (cb * jnp.where(keep, jnp.exp(seg), 0.0)).astype(bf16)
                xh = jnp.where((lane // p) == k, xpair, jnp.zeros_like(xpair))
                part = jnp.dot(gmat, xh, preferred_element_type=f32)
                acc = part if acc is None else acc + part
            lo = pr * 2 * p
            ys.append(acc + y_off[:, lo:lo + 2 * p])
        upd = jnp.dot(bg.T, xw[:, g * gw:(g + 1) * gw], preferred_element_type=f32)
        s_ref[0, g] = s_old * chunk_decay[:, g * gw:(g + 1) * gw] + upd
    return jnp.concatenate(ys, axis=1)


def _ssd_kernel(z_ref, xs_ref, bc_ref, dt_ref, cwx_ref, cbx_ref, cwb_ref, cbb_ref, dtb_ref, alog_ref, drep_ref,
                nw_ref, initf_ref, initb_ref, y_ref, sf_ref, sb_ref, yb_ref, *, nb, tb, period, heads, hpg):
    j = pl.program_id(1)
    q = SSD_CHUNK
    n = SSD_STATE
    gn = SSD_GROUPS * n
    nchunk = tb // q
    ssd_w = heads * SSD_HEADDIM

    @pl.when(j == 0)
    def _():
        sb_ref[...] = initb_ref[...]

    @pl.when(j == nb)
    def _():
        sf_ref[...] = initf_ref[...]

    xs_all = _silu(_token_conv3(xs_ref[0], cwx_ref[...], cbx_ref[...], period))
    bc_all = _silu(_token_conv3(bc_ref[0], cwb_ref[...], cbb_ref[...], period))
    a_all = -jnp.exp(alog_ref[...])

    def run(reverse):
        d = 1 if reverse else 0
        blk = (nb - 1 - j) if reverse else (j - nb)
        order = range(nchunk - 1, -1, -1) if reverse else range(nchunk)
        for ci in order:
            sl = slice(ci * q, (ci + 1) * q)
            y = _ssd_chunk(xs_all[sl], bc_all[sl, :gn], bc_all[sl, gn:], dt_ref[0, sl, d * heads:(d + 1) * heads],
                           sb_ref if reverse else sf_ref, a_all[d:d + 1], dtb_ref[d:d + 1], reverse, heads, hpg)
            row0 = pl.multiple_of(blk * tb + ci * q, q)
            if reverse:
                yb_ref[pl.ds(row0, q), :] = y
            else:
                y = y + yb_ref[pl.ds(row0, q), :] + drep_ref[...] * xs_all[sl]
                y = y * _silu(z_ref[0, sl, :])
                gw = ssd_w // SSD_GROUPS
                outs = []
                for g in range(SSD_GROUPS):
                    yg = y[:, g * gw:(g + 1) * gw]
                    ms = jnp.mean(yg * yg, axis=-1, keepdims=True)
                    outs.append(yg * lax.rsqrt(ms + EPS) * nw_ref[:, g * gw:(g + 1) * gw])
                y_ref[0, sl, :] = jnp.concatenate(outs, axis=1)

    @pl.when(j < nb)
    def _():
        run(True)

    @pl.when(j >= nb)
    def _():
        run(False)


def ssd_mixer(srcs, cols, conv_w_x, conv_b_x, conv_w_bc, conv_b_bc, dt_bias, a_log, ssd_d, norm_w, init_f, init_b,
              period, tb):
    b, l, _ = srcs[0].shape
    heads = dt_bias.shape[1]
    hpg = heads // SSD_GROUPS
    ssd_w = heads * SSD_HEADDIM
    gn = SSD_GROUPS * SSD_STATE
    nb = l // tb
    oz, ox, obc, odt = cols
    f32 = jnp.float32

    def tok(width, off):
        blk_idx = off // width
        return pl.BlockSpec((1, tb, width),
                            lambda i, j: (i, jnp.where(j < nb, nb - 1 - j, j - nb), blk_idx))

    def tok_fwd(width, off):
        blk_idx = off // width
        return pl.BlockSpec((1, tb, width), lambda i, j: (i, jnp.where(j < nb, 0, j - nb), blk_idx))

    def whole(shape):
        return pl.BlockSpec(shape, lambda i, j: (0,) * len(shape))

    st_shape = (b, SSD_GROUPS, SSD_STATE, hpg * SSD_HEADDIM)
    st_spec = pl.BlockSpec((1,) + st_shape[1:], lambda i, j: (i, 0, 0, 0))
    drep = jnp.repeat(ssd_d, SSD_HEADDIM).reshape(1, ssd_w)
    kern = functools.partial(_ssd_kernel, nb=nb, tb=tb, period=period, heads=heads, hpg=hpg)
    return pl.pallas_call(
        kern,
        out_shape=(jax.ShapeDtypeStruct((b, l, ssd_w), f32), jax.ShapeDtypeStruct(st_shape, f32),
                   jax.ShapeDtypeStruct(st_shape, f32)),
        grid=(b, 2 * nb),
        in_specs=[
            tok_fwd(ssd_w, oz), tok(ssd_w, ox), tok(2 * gn, obc), tok(LANE, odt),
            whole((3, ssd_w)), whole((1, ssd_w)), whole((3, 2 * gn)), whole((1, 2 * gn)),
            whole((2, heads)), whole((2, heads)), whole((1, ssd_w)), whole((1, ssd_w)),
            st_spec, st_spec,
        ],
        out_specs=(pl.BlockSpec((1, tb, ssd_w), lambda i, j: (i, jnp.where(j < nb, 0, j - nb), 0)),
                   st_spec, st_spec),
        scratch_shapes=[pltpu.VMEM((l, ssd_w), f32)],
        compiler_params=_cparams("parallel", "arbitrary"),
        name="ssd_mixer",
    )(*srcs, conv_w_x, conv_b_x.reshape(1, -1), conv_w_bc, conv_b_bc.reshape(1, -1), dt_bias, a_log, drep,
      norm_w.reshape(1, ssd_w), init_f, init_b)


def _pick(l, pref):
    return pref if l % pref == 0 else l


def kernel(x, c, ctx, c_ctx, w_ada, b_ada, norm1_w, w_in, hy_conv_w, hy_conv_b, filt_w1, filt_b1, filt_freq,
           filt_w2, filt_b2, filt_w3, hy_bias, hy_norm_w, ssd_conv_w, ssd_conv_b, dt_bias, a_log, ssd_d,
           ssd_norm_w, w_out, norm2_w, w_mlp1, w_mlp2, final_norm_w):
    depth = w_in.shape[0]
    bsz, seq, d = x.shape
    hy_w = hy_norm_w.shape[1]
    hy_proj = hy_conv_w.shape[2]
    ssd_w = ssd_norm_w.shape[1]
    ssd_xbc = ssd_conv_w.shape[2]
    heads = ssd_w // SSD_HEADDIM
    hpg = heads // SSD_GROUPS
    ssd_dt = 2 * heads
    o0, o1 = hy_proj, hy_proj + ssd_xbc
    o2 = o1 + ssd_dt
    dt_pad = (-ssd_dt) % LANE
    bf16 = jnp.bfloat16
    ssd_cols = (0, o0, o0 + ssd_w, ssd_w)
    tn = o1 // 3

    n1 = seq // LANE
    n1c = ctx.shape[1] // LANE
    zero_state = jnp.zeros((bsz, SSD_GROUPS, SSD_STATE, hpg * SSD_HEADDIM), jnp.float32)

    rows = 8
    c_all = jnp.zeros((rows, d), jnp.float32).at[:bsz].set(c).at[bsz].set(c_ctx)
    mod_all = ada_mod(c_all, w_ada, b_ada)

    w_a_all, w_b_all = cast_in_proj(w_in, o1, o2, w_in.shape[2], ssd_w + ssd_dt + dt_pad)
    w_out_b, w1_b, w2_b = w_out.astype(bf16), w_mlp1.astype(bf16), w_mlp2.astype(bf16)

    h_ctx = ctx
    tm_l = _pick(seq, 512)
    tm_c = _pick(ctx.shape[1], 256)
    for i in range(depth):
        lp = dict(hy_conv_w=hy_conv_w[i], hy_conv_b=hy_conv_b[i], filt_w1=filt_w1[i], filt_b1=filt_b1[i],
                  filt_freq=filt_freq[i], filt_w2=filt_w2[i], filt_b2=filt_b2[i], filt_w3=filt_w3[i],
                  hy_bias=hy_bias[i], hy_norm_w=hy_norm_w[i], ssd_conv_w=ssd_conv_w[i],
                  ssd_conv_b=ssd_conv_b[i], dt_bias=dt_bias[i], a_log=a_log[i], ssd_d=ssd_d[i],
                  ssd_norm_w=ssd_norm_w[i])
        mod = mod_all[i, :bsz].reshape(bsz, N_MOD, d)
        mod_c = jnp.broadcast_to(mod_all[i, bsz].reshape(1, N_MOD, d), (bsz, N_MOD, d))
        def ssd(pa, pb, init_f, init_b, period):
            cw, cb = lp['ssd_conv_w'], lp['ssd_conv_b']
            return ssd_mixer((pb, pa, pa, pb), ssd_cols, cw[:, :ssd_w], cb[:ssd_w], cw[:, ssd_w:], cb[ssd_w:],
                             lp['dt_bias'], lp['a_log'], lp['ssd_d'], lp['ssd_norm_w'], init_f, init_b, period,
                             _pick(pa.shape[1], 256))

        def filt_taps(length):
            return hyena_filter_taps(length, lp['filt_w1'], lp['filt_b1'], lp['filt_freq'], lp['filt_w2'],
                                     lp['filt_b2'], lp['filt_w3'], hy_w)

        pca, pcb = normmod_matmul(h_ctx, norm1_w[i], mod_c[:, 1], mod_c[:, 0], w_a_all, w_b_all, i, tm_c, tn)
        y_ssd, s_f, s_b = ssd(pca, pcb, zero_state, zero_state, pca.shape[1])
        if i < depth - 1:
            utc = hyena_pre(pca, 0, lp['hy_conv_w'], lp['hy_conv_b'], pca.shape[1], pca.shape[1], 512)
            kre, kim = ctx_filter_spectrum(filt_taps(pca.shape[1]), 128)
            zt = utc
            for o in range(HY_ORDER):
                zt = ctx_hyena_conv(utc, zt, 0, (o + 1) * hy_w, kre, kim, o, lp['hy_bias'][o], 128)
            h_ctx = proj_residual(zt, y_ssd, lp['hy_norm_w'], w_out_b, i, h_ctx, mod_c[:, 2], tm_c, HY_GROUPS)
            h_ctx = mlp_residual(h_ctx, norm2_w[i], mod_c[:, 4], mod_c[:, 3], mod_c[:, 5], w1_b, w2_b, i,
                                 final_norm_w, tm_c, 1024, False)
        pla, plb = normmod_matmul(x, norm1_w[i], mod[:, 1], mod[:, 0], w_a_all, w_b_all, i, tm_l, tn)
        ut = hyena_pre_rows(pla, 0, lp['hy_conv_w'], lp['hy_conv_b'], GRID_W, LANE)
        kre, kim = hyena_filter_spectrum(filt_taps(seq), bsz, 16)
        zt = ut
        for o in range(HY_ORDER):
            zt = hyena_conv(ut, zt, 0, (o + 1) * hy_w, kre, kim, o, lp['hy_bias'][o], bsz, 16, o == HY_ORDER - 1)
        zh_t = zt.reshape(bsz, n1, hy_w, LANE)
        y_ssd, _, _ = ssd(pla, plb, s_f, s_b, GRID_W)
        x = proj_residual(zh_t, y_ssd, lp['hy_norm_w'], w_out_b, i, x, mod[:, 2], tm_l, HY_GROUPS)
        x = mlp_residual(x, norm2_w[i], mod[:, 4], mod[:, 3], mod[:, 5], w1_b, w2_b, i, final_norm_w, tm_l, 1024,
                         i == depth - 1)
    return x
```

```python
import functools
import math

import numpy as np
import jax
import jax.numpy as jnp
from jax import lax
from jax.experimental import pallas as pl
from jax.experimental.pallas import tpu as pltpu

EPS = 1e-6
N_MOD = 6
GRID_W = 64
HY_GROUPS = 8
HY_ORDER = 2
HY_POS_EMB = 33
HY_TARGET = 1e-2
HY_FAST = 0.3
HY_SLOW = 1.5
SSD_HEADDIM = 64
SSD_GROUPS = 2
SSD_STATE = 128
SSD_CHUNK = 128

LANE = 128
VMEM_LIMIT = 56 * 1024 * 1024


def _cparams(*sem):
    return pltpu.CompilerParams(dimension_semantics=sem, vmem_limit_bytes=VMEM_LIMIT)


def _ada_kernel(c_ref, w_ref, b_ref, o_ref):
    c = c_ref[...]
    a = (c * jax.nn.sigmoid(c)).astype(jnp.bfloat16)
    w = w_ref[0].astype(jnp.bfloat16)
    o_ref[0] = jnp.dot(a, w, preferred_element_type=jnp.float32) + b_ref[0]


def ada_mod(c_all, w_ada, b_ada, tn=1024):
    depth, d, n = w_ada.shape
    r = c_all.shape[0]
    return pl.pallas_call(
        _ada_kernel,
        out_shape=jax.ShapeDtypeStruct((depth, r, n), jnp.float32),
        grid=(depth, n // tn),
        in_specs=[
            pl.BlockSpec((r, d), lambda i, j: (0, 0)),
            pl.BlockSpec((1, d, tn), lambda i, j: (i, 0, j)),
            pl.BlockSpec((1, 1, tn), lambda i, j: (i, 0, j)),
        ],
        out_specs=pl.BlockSpec((1, r, tn), lambda i, j: (i, 0, j)),
        compiler_params=_cparams("parallel", "parallel"),
        name="ada_mod",
    )(c_all, w_ada, b_ada.reshape(depth, 1, n))


def _wcast_kernel(w_ref, a_ref, b_ref, *, na, c0, c1):
    bf16 = jnp.bfloat16
    w = w_ref[0]
    a_ref[0] = w[:, :na].astype(bf16)
    nb = b_ref.shape[2]
    pad = nb - (c1 - c0) - (c0 - na)
    b_ref[0] = jnp.concatenate([w[:, c0:c1], w[:, na:c0], jnp.zeros((w.shape[0], pad), w.dtype)], axis=1).astype(bf16)


def cast_in_proj(w_in, na, c0, c1, nb, tr=256):
    depth, d, n = w_in.shape
    return pl.pallas_call(
        functools.partial(_wcast_kernel, na=na, c0=c0, c1=c1),
        out_shape=(jax.ShapeDtypeStruct((depth, d, na), jnp.bfloat16),
                   jax.ShapeDtypeStruct((depth, d, nb), jnp.bfloat16)),
        grid=(depth, d // tr),
        in_specs=[pl.BlockSpec((1, tr, n), lambda i, j: (i, j, 0))],
        out_specs=(pl.BlockSpec((1, tr, na), lambda i, j: (i, j, 0)),
                   pl.BlockSpec((1, tr, nb), lambda i, j: (i, j, 0))),
        compiler_params=_cparams("parallel", "parallel"),
        name="cast_in_proj",
    )(w_in)


NORM_ROWS = 16


def _normmod_store(h_ref, x_ref, nw, sc, sh):
    w = nw * (1.0 + sc)

    def body(i, carry):
        r0 = pl.multiple_of(i * NORM_ROWS, NORM_ROWS)
        xb = x_ref[0, pl.ds(r0, NORM_ROWS), :]
        ms = jnp.mean(xb * xb, axis=-1, keepdims=True)
        h_ref[pl.ds(r0, NORM_ROWS), :] = ((xb * lax.rsqrt(ms + EPS)) * w + sh).astype(h_ref.dtype)
        return carry

    lax.fori_loop(0, h_ref.shape[0] // NORM_ROWS, body, 0, unroll=8)


def _nm_matmul_kernel(x_ref, nw_ref, sc_ref, sh_ref, wa_ref, wb_ref, oa_ref, ob_ref, h_ref, *, na):
    k = pl.program_id(2)

    @pl.when(k == 0)
    def _():
        _normmod_store(h_ref, x_ref, nw_ref[...], sc_ref[0], sh_ref[0])

    @pl.when(k < na)
    def _():
        oa_ref[0] = jnp.dot(h_ref[...], wa_ref[...], preferred_element_type=jnp.float32)

    @pl.when(k >= na)
    def _():
        ob_ref[0] = jnp.dot(h_ref[...], wb_ref[...], preferred_element_type=jnp.float32)


def normmod_matmul(x, nw, sc, sh, wa, wb, layer, tm, tn):
    b, l, d = x.shape
    n_a, n_b = wa.shape[2], wb.shape[2]
    na = n_a // tn
    last = na - 1
    return pl.pallas_call(
        functools.partial(_nm_matmul_kernel, na=na),
        out_shape=(jax.ShapeDtypeStruct((b, l, n_a), jnp.float32), jax.ShapeDtypeStruct((b, l, n_b), jnp.float32)),
        grid=(b, l // tm, na + 1),
        in_specs=[
            pl.BlockSpec((1, tm, d), lambda i, j, k: (i, j, 0)),
            pl.BlockSpec((1, d), lambda i, j, k: (0, 0)),
            pl.BlockSpec((1, 1, d), lambda i, j, k: (i, 0, 0)),
            pl.BlockSpec((1, 1, d), lambda i, j, k: (i, 0, 0)),
            pl.BlockSpec((None, d, tn), lambda i, j, k: (layer, 0, jnp.minimum(k, last))),
            pl.BlockSpec((None, d, n_b), lambda i, j, k: (layer, 0, 0)),
        ],
        out_specs=(pl.BlockSpec((1, tm, tn), lambda i, j, k: (i, j, jnp.minimum(k, last))),
                   pl.BlockSpec((1, tm, n_b), lambda i, j, k: (i, j, 0))),
        scratch_shapes=[pltpu.VMEM((tm, d), jnp.bfloat16)],
        compiler_params=_cparams("parallel", "parallel", "arbitrary"),
        name="normmod_matmul",
    )(x, nw.reshape(1, d), sc.reshape(b, 1, d), sh.reshape(b, 1, d), wa, wb)


def _proj_res_kernel(zh_ref, ys_ref, nwh_ref, wh_ref, ws_ref, x_ref, g_ref, o_ref, *, groups):
    bf16 = jnp.bfloat16
    slabs, hy_w = zh_ref.shape[1], zh_ref.shape[2]
    gs = hy_w // groups
    rows = []
    for s in range(slabs):
        cols = []
        for gi in range(groups):
            zg = zh_ref[0, s, gi * gs:(gi + 1) * gs, :]
            ms = jnp.mean(zg * zg, axis=0, keepdims=True)
            cols.append((zg * lax.rsqrt(ms + EPS)).T)
        rows.append(jnp.concatenate(cols, axis=1))
    yh = (jnp.concatenate(rows, axis=0) * nwh_ref[...]).astype(bf16)
    acc = jnp.dot(yh, wh_ref[...], preferred_element_type=jnp.float32)
    acc += jnp.dot(ys_ref[0].astype(bf16), ws_ref[...], preferred_element_type=jnp.float32)
    o_ref[0] = x_ref[0] + g_ref[0] * acc


def proj_residual(zh_t, y_ssd, hy_norm_w, w_out, layer, x, g, tm, groups):
    b, n1, hy_w, _ = zh_t.shape
    l, ssd_w = y_ssd.shape[1], y_ssd.shape[2]
    n = w_out.shape[2]
    assert hy_w % ssd_w == 0
    return pl.pallas_call(
        functools.partial(_proj_res_kernel, groups=groups),
        out_shape=jax.ShapeDtypeStruct((b, l, n), jnp.float32),
        grid=(b, l // tm),
        in_specs=[
            pl.BlockSpec((1, tm // LANE, hy_w, LANE), lambda i, j: (i, j, 0, 0)),
            pl.BlockSpec((1, tm, ssd_w), lambda i, j: (i, j, 0)),
            pl.BlockSpec((1, hy_w), lambda i, j: (0, 0)),
            pl.BlockSpec((None, hy_w, n), lambda i, j: (layer, 0, 0)),
            pl.BlockSpec((None, ssd_w, n), lambda i, j: (layer, hy_w // ssd_w, 0)),
            pl.BlockSpec((1, tm, n), lambda i, j: (i, j, 0)),
            pl.BlockSpec((1, 1, n), lambda i, j: (i, 0, 0)),
        ],
        out_specs=pl.BlockSpec((1, tm, n), lambda i, j: (i, j, 0)),
        compiler_params=_cparams("parallel", "parallel"),
        name="proj_residual",
    )(zh_t, y_ssd, hy_norm_w.reshape(1, hy_w), w_out, w_out, x, g.reshape(b, 1, n))


def _hypre_kernel(u_ref, w_ref, b_ref, o_ref, *, period):
    u = _token_conv3(u_ref[0], w_ref[...], b_ref[...], period)
    tb, cw = u.shape
    for s in range(tb // LANE):
        for j in range(cw // LANE):
            o_ref[0, s, j * LANE:(j + 1) * LANE, :] = u[s * LANE:(s + 1) * LANE, j * LANE:(j + 1) * LANE].T


def hyena_pre(p, col0, conv_w, conv_b, period, tb, cw):
    b, l, _ = p.shape
    c = conv_w.shape[1]
    blk0 = col0 // cw
    return pl.pallas_call(
        functools.partial(_hypre_kernel, period=period),
        out_shape=jax.ShapeDtypeStruct((b, l // LANE, c, LANE), jnp.float32),
        grid=(b, l // tb, c // cw),
        in_specs=[
            pl.BlockSpec((1, tb, cw), lambda i, j, k: (i, j, blk0 + k)),
            pl.BlockSpec((3, cw), lambda i, j, k: (0, k)),
            pl.BlockSpec((1, cw), lambda i, j, k: (0, k)),
        ],
        out_specs=pl.BlockSpec((1, tb // LANE, cw, LANE), lambda i, j, k: (i, j, k, 0)),
        compiler_params=_cparams("parallel", "parallel", "parallel"),
        name="hyena_pre",
    )(p, conv_w, conv_b.reshape(1, c))


def _hypre_rows_kernel(u_ref, w_ref, b_ref, o_ref, s_ref, *, period):
    u = _token_conv3(u_ref[0], w_ref[...], b_ref[...], period)
    l, cw = u.shape
    n1 = l // LANE
    for s in range(n1):
        for j in range(cw // LANE):
            s_ref[s * cw + j * LANE:s * cw + (j + 1) * LANE, :] = u[s * LANE:(s + 1) * LANE,
                                                                    j * LANE:(j + 1) * LANE].T
    for c in range(cw):
        o_ref[:, c * LANE:(c + 1) * LANE] = s_ref[pl.ds(c, n1, stride=cw), :]


def hyena_pre_rows(p, col0, conv_w, conv_b, period, cw):
    b, l, _ = p.shape
    c = conv_w.shape[1]
    n1 = l // LANE
    blk0 = col0 // cw
    return pl.pallas_call(
        functools.partial(_hypre_rows_kernel, period=period),
        out_shape=jax.ShapeDtypeStruct((b * n1, c * LANE), jnp.float32),
        grid=(b, c // cw),
        in_specs=[
            pl.BlockSpec((1, l, cw), lambda i, k: (i, 0, blk0 + k)),
            pl.BlockSpec((3, cw), lambda i, k: (0, k)),
            pl.BlockSpec((1, cw), lambda i, k: (0, k)),
        ],
        out_specs=pl.BlockSpec((n1, cw * LANE), lambda i, k: (i, k)),
        scratch_shapes=[pltpu.VMEM((n1 * cw, LANE), jnp.float32)],
        compiler_params=_cparams("parallel", "parallel"),
        name="hyena_pre_rows",
    )(p, conv_w, conv_b.reshape(1, c))


def _fft_constants(bsz, n1):
    m1 = 2 * n1
    n = m1 * LANE
    pairs = bsz // 2
    half = pairs * m1
    r = bsz * n1
    t1 = np.arange(n1)[:, None]
    f1 = np.arange(m1)[None, :]
    th = 2.0 * np.pi * t1 * f1 / m1
    w1 = np.zeros((r, 2 * half))
    for pr in range(pairs):
        re = slice(pr * m1, (pr + 1) * m1)
        im = slice(half + pr * m1, half + (pr + 1) * m1)
        ra = slice((2 * pr) * n1, (2 * pr + 1) * n1)
        rb = slice((2 * pr + 1) * n1, (2 * pr + 2) * n1)
        w1[ra, re], w1[ra, im] = np.cos(th), -np.sin(th)
        w1[rb, re], w1[rb, im] = np.sin(th), np.cos(th)
    w4 = w1.T / n
    t2 = np.arange(LANE)[:, None]
    ps = 2.0 * np.pi * t2 * np.arange(LANE)[None, :] / LANE
    w2 = np.block([[np.cos(ps), -np.sin(ps)], [np.sin(ps), np.cos(ps)]])
    ph = 2.0 * np.pi * t2 * f1 / n
    tc = np.tile(np.cos(ph), (1, pairs))
    ts = np.tile(np.sin(ph), (1, pairs))
    orders = half // m1
    tf = 2.0 * np.pi * np.arange(m1)[:, None] * f1 / m1
    w1f = np.zeros((orders * m1, 2 * half))
    for o in range(orders):
        w1f[o * m1:(o + 1) * m1, o * m1:(o + 1) * m1] = np.cos(tf)
        w1f[o * m1:(o + 1) * m1, half + o * m1:half + (o + 1) * m1] = -np.sin(tf)
    mats = [jnp.asarray(m, jnp.bfloat16) for m in (w1, w2, w2.T, w4)]
    tabs = [jnp.asarray(m, jnp.float32) for m in (tc, ts, tc.T, ts.T)]
    return mats, tabs, half, jnp.asarray(w1f, jnp.bfloat16)


def _mxu(a, w):
    return jnp.dot(a.astype(jnp.bfloat16), w, preferred_element_type=jnp.float32)


def _fft_forward(x0, w1, w2, tc, ts, half):
    cw = x0.shape[1] // LANE
    a1 = _mxu(x0.T, w1)
    lhs2 = []
    for c in range(cw):
        ar = a1[c * LANE:(c + 1) * LANE, :half]
        ai = a1[c * LANE:(c + 1) * LANE, half:]
        lhs2.append(jnp.concatenate([(ar * tc + ai * ts).T, (ai * tc - ar * ts).T], axis=1))
    return _mxu(jnp.concatenate(lhs2, axis=0), w2)


def _hyconv_kernel(z_ref, g_ref, kre_ref, kim_ref, brep_ref, w1_ref, w2_ref, w2i_ref, w4_ref, tc_ref, ts_ref,
                   tct_ref, tst_ref, o_ref, *, half):
    x0 = z_ref[...]
    cw = x0.shape[1] // LANE
    tct, tst = tct_ref[...], tst_ref[...]
    s = _fft_forward(x0, w1_ref[...], w2_ref[...], tc_ref[...], ts_ref[...], half)
    sr, si = s[:, :LANE], s[:, LANE:]
    m1 = kre_ref.shape[2]
    pairs = half // m1
    kr = jnp.concatenate([kre_ref[:, 0]] * pairs, axis=1).reshape(cw * half, LANE)
    ki = jnp.concatenate([kim_ref[:, 0]] * pairs, axis=1).reshape(cw * half, LANE)
    y2 = jnp.concatenate([sr * kr - si * ki, sr * ki + si * kr], axis=1)
    bq = _mxu(y2, w2i_ref[...])
    lhs4 = []
    for c in range(cw):
        br = bq[c * half:(c + 1) * half, :LANE]
        bi = bq[c * half:(c + 1) * half, LANE:]
        lhs4.append(jnp.concatenate([(br * tct - bi * tst).T, (bi * tct + br * tst).T], axis=1))
    lhs4 = jnp.concatenate(lhs4, axis=0)
    yt = _mxu(lhs4, w4_ref[...])
    if len(o_ref.shape) == 2:
        o_ref[...] = g_ref[...] * (yt.T + x0 * brep_ref[...])
    else:
        for c in range(cw):
            sl = slice(c * LANE, (c + 1) * LANE)
            o_ref[:, c, :] = g_ref[:, sl] * (yt[sl].T + x0[:, sl] * brep_ref[:, sl])


def hyena_conv(ut, z_src, z_col, g_col, kre, kim, order, bias, bsz, cw, channel_tiles):
    r = ut.shape[0]
    n1 = r // bsz
    c = bias.shape[0]
    mats, tabs, half, _ = _fft_constants(bsz, n1)
    blk = cw * LANE
    zb, gb = z_col // cw, g_col // cw
    brep = jnp.repeat(bias, LANE).reshape(1, c * LANE)
    kspec = pl.BlockSpec((cw, 1, 2 * n1, LANE), lambda j: (j, order, 0, 0))

    def const(a):
        return pl.BlockSpec(a.shape, lambda j: (0, 0))

    return pl.pallas_call(
        functools.partial(_hyconv_kernel, half=half),
        out_shape=jax.ShapeDtypeStruct((r, c, LANE) if channel_tiles else (r, c * LANE), jnp.float32),
        grid=(c // cw,),
        in_specs=[
            pl.BlockSpec((r, blk), lambda j: (0, zb + j)),
            pl.BlockSpec((r, blk), lambda j: (0, gb + j)),
            kspec, kspec,
            pl.BlockSpec((1, blk), lambda j: (0, j)),
        ] + [const(a) for a in mats + tabs],
        out_specs=(pl.BlockSpec((r, cw, LANE), lambda j: (0, j, 0)) if channel_tiles
                   else pl.BlockSpec((r, blk), lambda j: (0, j))),
        compiler_params=_cparams("parallel"),
        name="hyena_conv",
    )(z_src, ut, kre, kim, brep, *mats, *tabs)


_HI = lax.Precision.HIGHEST


def _filtgen_kernel(a0_ref, wc_ref, ws_ref, fb1_ref, freq_ref, fw2t_ref, fb2_ref, fw3t_ref, dl_ref, o_ref, *,
                    seq, bands):
    f32 = jnp.float32
    s = pl.program_id(0)
    t = s * LANE + lax.broadcasted_iota(jnp.int32, (1, LANE), 1)
    pos = jnp.where(t < seq, t, 2 * seq - t).astype(f32)
    tt = pos / (seq - 1.0)
    ang = (2.0 * math.pi / seq) * pos
    j = lax.broadcasted_iota(jnp.int32, (bands, 1), 0).astype(f32)
    fj = 1e-4 + j * ((bands - 1.0 - 1e-4) / (bands - 1.0))
    fa = fj * ang
    freq = freq_ref[...]
    pre = a0_ref[...] * tt + jnp.dot(wc_ref[...], jnp.cos(fa), precision=_HI, preferred_element_type=f32) \
        - jnp.dot(ws_ref[...], jnp.sin(fa), precision=_HI, preferred_element_type=f32) + fb1_ref[...]
    h1 = jnp.sin(freq * pre)
    h2 = jnp.sin(freq * (jnp.dot(fw2t_ref[...], h1, precision=_HI, preferred_element_type=f32) + fb2_ref[...]))
    orders, _, c, hid = fw3t_ref.shape
    w3 = fw3t_ref[:, 0].reshape(orders * c, hid)
    h = jnp.dot(w3, h2, precision=_HI, preferred_element_type=f32)
    h = h * jnp.exp(-tt * dl_ref[...])
    h = jnp.where(t == seq, 0.0, h)
    o_ref[:, 0] = h.reshape(orders, c, LANE)


def hyena_filter_taps(seq, fw1, fb1, freq, fw2, fb2, fw3, hy_w):
    hid = fw2.shape[0]
    bands = (HY_POS_EMB - 1) // 2
    orders = fw3.shape[1] // (2 * hy_w)
    slabs = 2 * seq // LANE
    col = lambda v: v.reshape(-1, 1)
    fw3t = fw3.T.reshape(orders, 2, hy_w, hid)
    deltas = jnp.abs(jnp.linspace(math.log(HY_TARGET) / HY_SLOW, math.log(HY_TARGET) / HY_FAST, hy_w,
                                  dtype=jnp.float32))
    dl = jnp.tile(deltas, orders).reshape(-1, 1)

    def whole(a):
        return pl.BlockSpec(a.shape, lambda s: (0,) * a.ndim)

    args = [col(fw1[0]), fw1[1:1 + bands].T, fw1[1 + bands:].T, col(fb1), col(freq), fw2.T, col(fb2)]
    return pl.pallas_call(
        functools.partial(_filtgen_kernel, seq=seq, bands=bands),
        out_shape=jax.ShapeDtypeStruct((orders, slabs, hy_w, LANE), jnp.float32),
        grid=(slabs,),
        in_specs=[whole(a) for a in args] + [
            pl.BlockSpec((orders, 1, hy_w, hid), lambda s: (0, s // (slabs // 2), 0, 0)),
            whole(dl),
        ],
        out_specs=pl.BlockSpec((orders, 1, hy_w, LANE), lambda s: (0, s, 0, 0)),
        compiler_params=_cparams("parallel"),
        name="hyena_filter_taps",
    )(*args, fw3t, dl)


def _lane_block_abs_norm(x, orders):
    rows_per = x.shape[0] // orders
    cw = x.shape[1] // LANE
    out = []
    for o in range(orders):
        xo = x[o * rows_per:(o + 1) * rows_per]
        a = jnp.sum(jnp.abs(xo), axis=0, keepdims=True)
        inv = [jnp.broadcast_to(1.0 / (jnp.sum(a[:, c * LANE:(c + 1) * LANE], axis=1, keepdims=True) + EPS),
                                (1, LANE)) for c in range(cw)]
        out.append(xo * jnp.concatenate(inv, axis=1))
    return jnp.concatenate(out, axis=0)


def _filtspec_kernel(k_ref, w1_ref, w2_ref, tc_ref, ts_ref, re_ref, im_ref, *, half, orders):
    m1 = half // orders
    x0 = _lane_block_abs_norm(k_ref[...], orders)
    s = _fft_forward(x0, w1_ref[...], w2_ref[...], tc_ref[...], ts_ref[...], half)
    cw = x0.shape[1] // LANE
    re_ref[...] = s[:, :LANE].reshape(cw, orders, m1, LANE)
    im_ref[...] = s[:, LANE:].reshape(cw, orders, m1, LANE)


def hyena_filter_spectrum(taps, bsz, cw):
    orders, m1, c, _ = taps.shape
    mats, tabs, half, w1f = _fft_constants(bsz, m1 // 2)
    assert half == orders * m1
    consts = [w1f, mats[1], tabs[0], tabs[1]]
    out = jax.ShapeDtypeStruct((c, orders, m1, LANE), jnp.float32)
    ospec = pl.BlockSpec((cw, orders, m1, LANE), lambda j: (j, 0, 0, 0))
    return pl.pallas_call(
        functools.partial(_filtspec_kernel, half=half, orders=orders),
        out_shape=(out, out),
        grid=(c // cw,),
        in_specs=[pl.BlockSpec((orders * m1, cw * LANE), lambda j: (0, j))] +
                 [pl.BlockSpec(a.shape, lambda j: (0, 0)) for a in consts],
        out_specs=(ospec, ospec),
        compiler_params=_cparams("parallel"),
        name="hyena_filter_spectrum",
    )(taps.reshape(orders * m1, c * LANE), *consts)


def _dense_dft_constants(seq):
    n = 2 * seq
    t = np.arange(seq)[:, None]
    f = np.arange(n)[None, :]
    ps = 2.0 * np.pi * t * f / n
    fwd = np.block([[np.cos(ps), -np.sin(ps)], [np.sin(ps), np.cos(ps)]])
    inv = fwd.T / n
    tk = np.arange(n)[:, None]
    pk = 2.0 * np.pi * tk * f / n
    fk = np.concatenate([np.cos(pk), -np.sin(pk)], axis=1)
    return [jnp.asarray(m, jnp.bfloat16) for m in (fwd, inv, fk)]


def _ctxspec_kernel(k_ref, fk_ref, re_ref, im_ref):
    orders, slabs = k_ref.shape[0], k_ref.shape[1]
    n = slabs * LANE
    for o in range(orders):
        x = jnp.concatenate([k_ref[o, s] for s in range(slabs)], axis=1)
        x = x / (jnp.sum(jnp.abs(x), axis=1, keepdims=True) + EPS)
        kf = _mxu(x, fk_ref[...])
        re_ref[o] = kf[:, :n]
        im_ref[o] = kf[:, n:]


def ctx_filter_spectrum(taps, cw):
    orders, slabs, c, _ = taps.shape
    n = slabs * LANE
    fk = _dense_dft_constants(n // 2)[2]
    out = jax.ShapeDtypeStruct((orders, c, n), jnp.float32)
    ospec = pl.BlockSpec((orders, cw, n), lambda j: (0, j, 0))
    return pl.pallas_call(
        _ctxspec_kernel,
        out_shape=(out, out),
        grid=(c // cw,),
        in_specs=[pl.BlockSpec((orders, slabs, cw, LANE), lambda j: (0, 0, j, 0)),
                  pl.BlockSpec(fk.shape, lambda j: (0, 0))],
        out_specs=(ospec, ospec),
        compiler_params=_cparams("parallel"),
        name="ctx_filter_spectrum",
    )(taps, fk)


def _ctxconv_kernel(z_ref, g_ref, kre_ref, kim_ref, b_ref, fwd_ref, inv_ref, o_ref):
    bsz, slabs, cw = z_ref.shape[0], z_ref.shape[1], z_ref.shape[2]
    seq = slabs * LANE
    zs = [jnp.concatenate([z_ref[b, s] for s in range(slabs)], axis=1) for b in range(bsz)]
    lhs = jnp.concatenate([jnp.concatenate([zs[2 * p], zs[2 * p + 1]], axis=1) for p in range(bsz // 2)], axis=0)
    s = _mxu(lhs, fwd_ref[...])
    n = 2 * seq
    sr, si = s[:, :n], s[:, n:]
    kr = jnp.concatenate([kre_ref[0]] * (bsz // 2), axis=0)
    ki = jnp.concatenate([kim_ref[0]] * (bsz // 2), axis=0)
    y = _mxu(jnp.concatenate([sr * kr - si * ki, sr * ki + si * kr], axis=1), inv_ref[...])
    bias = jnp.concatenate([b_ref[...]] * slabs, axis=1)
    for b in range(bsz):
        p, m = b // 2, b % 2
        yb = y[p * cw:(p + 1) * cw, m * seq:(m + 1) * seq]
        gate = jnp.concatenate([g_ref[b, s] for s in range(slabs)], axis=1)
        res = gate * (yb + zs[b] * bias)
        for sl in range(slabs):
            o_ref[b, sl] = res[:, sl * LANE:(sl + 1) * LANE]


def ctx_hyena_conv(ut, z_src, z_col, g_col, kre, kim, order, bias, cw):
    bsz, slabs, _, _ = ut.shape
    c = bias.shape[0]
    n = 2 * slabs * LANE
    fwd, inv, _ = _dense_dft_constants(slabs * LANE)
    zb, gb = z_col // cw, g_col // cw
    bb = jnp.broadcast_to(bias[:, None], (c, LANE))
    kspec = pl.BlockSpec((1, cw, n), lambda j: (order, j, 0))
    return pl.pallas_call(
        _ctxconv_kernel,
        out_shape=jax.ShapeDtypeStruct((bsz, slabs, c, LANE), jnp.float32),
        grid=(c // cw,),
        in_specs=[
            pl.BlockSpec((bsz, slabs, cw, LANE), lambda j: (0, 0, zb + j, 0)),
            pl.BlockSpec((bsz, slabs, cw, LANE), lambda j: (0, 0, gb + j, 0)),
            kspec, kspec,
            pl.BlockSpec((cw, LANE), lambda j: (j, 0)),
            pl.BlockSpec(fwd.shape, lambda j: (0, 0)),
            pl.BlockSpec(inv.shape, lambda j: (0, 0)),
        ],
        out_specs=pl.BlockSpec((bsz, slabs, cw, LANE), lambda j: (0, 0, j, 0)),
        compiler_params=_cparams("parallel"),
        name="ctx_hyena_conv",
    )(z_src, ut, kre, kim, bb, fwd, inv)


def _mlp_kernel(x_ref, nw_ref, sc_ref, sh_ref, g_ref, w1_ref, w2_ref, fw_ref, o_ref, h_ref, *, final_norm):
    f = pl.program_id(2)

    @pl.when(f == 0)
    def _():
        _normmod_store(h_ref, x_ref, nw_ref[...], sc_ref[0], sh_ref[0])
        o_ref[...] = jnp.zeros_like(o_ref)

    a = jnp.dot(h_ref[...], w1_ref[...], preferred_element_type=jnp.float32)
    a = jnp.square(jnp.maximum(a, 0.0)).astype(jnp.bfloat16)
    o_ref[0] += jnp.dot(a, w2_ref[...], preferred_element_type=jnp.float32)

    @pl.when(f == pl.num_programs(2) - 1)
    def _():
        y = x_ref[0] + g_ref[0] * o_ref[0]
        if final_norm:
            ms = jnp.mean(y * y, axis=-1, keepdims=True)
            y = y * lax.rsqrt(ms + EPS) * fw_ref[...]
        o_ref[0] = y


def mlp_residual(x, nw, sc, sh, g, w1, w2, layer, fw, tm, tf, final_norm):
    b, l, d = x.shape
    dff = w1.shape[2]
    vec = pl.BlockSpec((1, 1, d), lambda i, j, k: (i, 0, 0))
    row = pl.BlockSpec((1, d), lambda i, j, k: (0, 0))
    return pl.pallas_call(
        functools.partial(_mlp_kernel, final_norm=final_norm),
        out_shape=jax.ShapeDtypeStruct((b, l, d), jnp.float32),
        grid=(b, l // tm, dff // tf),
        in_specs=[
            pl.BlockSpec((1, tm, d), lambda i, j, k: (i, j, 0)),
            row, vec, vec, vec,
            pl.BlockSpec((None, d, tf), lambda i, j, k: (layer, 0, k)),
            pl.BlockSpec((None, tf, d), lambda i, j, k: (layer, k, 0)),
            row,
        ],
        out_specs=pl.BlockSpec((1, tm, d), lambda i, j, k: (i, j, 0)),
        scratch_shapes=[pltpu.VMEM((tm, d), jnp.bfloat16)],
        compiler_params=_cparams("parallel", "parallel", "arbitrary"),
        name="mlp_residual",
    )(x, nw.reshape(1, d), sc.reshape(b, 1, d), sh.reshape(b, 1, d), g.reshape(b, 1, d), w1, w2,
      fw.reshape(1, d))


def _split3(v):
    f32, bf16 = jnp.float32, jnp.bfloat16
    hi = v.astype(bf16)
    r1 = v - hi.astype(f32)
    mid = r1.astype(bf16)
    lo = (r1 - mid.astype(f32)).astype(bf16)
    return jnp.concatenate([hi, mid, lo], axis=1)


def _lane_repeat(v, rep):
    h = v.shape[1]
    row = lax.broadcasted_iota(jnp.int32, (3 * h, h * rep), 0) % h
    col = lax.broadcasted_iota(jnp.int32, (3 * h, h * rep), 1) // rep
    e = (row == col).astype(jnp.bfloat16)
    return jnp.dot(_split3(v), e, preferred_element_type=jnp.float32)


def _silu(v):
    h = 0.5 * v
    return h + h * jnp.tanh(h)


def _softplus(v):
    return jnp.maximum(v, 0.0) + jnp.log1p(jnp.exp(-jnp.abs(v)))


def _token_conv3(u, w, bias, period):
    t, c = u.shape
    pos = lax.broadcasted_iota(jnp.int32, (t, c), 0) % period
    up = jnp.where(pos == 0, 0.0, pltpu.roll(u, 1, 0))
    dn = jnp.where(pos == period - 1, 0.0, pltpu.roll(u, t - 1, 0))
    return bias + w[0:1] * up + w[1:2] * u + w[2:3] * dn


def _ssd_chunk(xs, bm, cm, dtr, s_ref, a_row, dtb_row, reverse, heads, hpg):
    f32, bf16 = jnp.float32, jnp.bfloat16
    q = xs.shape[0]
    p = SSD_HEADDIM
    n = SSD_STATE
    gw = hpg * p
    dt = _softplus(dtr + dtb_row)
    a = dt * a_row
    ri = lax.broadcasted_iota(jnp.int32, (q, q), 0)
    ci = lax.broadcasted_iota(jnp.int32, (q, q), 1)
    keep = (ci >= ri) if reverse else (ci <= ri)
    a3 = jnp.dot(keep.astype(bf16), _split3(a), preferred_element_type=f32)
    cs = a3[:, :heads] + a3[:, heads:2 * heads] + a3[:, 2 * heads:]
    dt_rep = _lane_repeat(dt, p)
    cs_rep = _lane_repeat(cs, p)
    cs_wide = _lane_repeat(cs, q)
    end = 0 if reverse else q - 1
    cs_end = cs_rep[end:end + 1]
    xdt = xs * dt_rep
    xw = (xdt * jnp.exp(cs_end - cs_rep)).astype(bf16)
    ecs = jnp.exp(cs_rep)
    chunk_decay = jnp.exp(cs_end)
    xdt_b = xdt.astype(bf16)
    lane = lax.broadcasted_iota(jnp.int32, (q, 2 * p), 1)
    ys = []
    for g in range(SSD_GROUPS):
        bg = bm[:, g * n:(g + 1) * n].astype(bf16)
        cg = cm[:, g * n:(g + 1) * n].astype(bf16)
        cb = lax.dot_general(cg, bg, (((1,), (1,)), ((), ())), preferred_element_type=f32)
        s_old = s_ref[0, g]
        y_off = jnp.dot(cg, s_old.astype(bf16), preferred_element_type=f32) * ecs[:, g * gw:(g + 1) * gw]
        for pr in range(hpg // 2):
            xpair = xdt_b[:, g * gw + pr * 2 * p:g * gw + (pr + 1) * 2 * p]
            acc = None
            for k in range(2):
                h = g * hpg + pr * 2 + k
                csr = cs_wide[:, h * q:(h + 1) * q]
                seg = csr - csr.T
                gmat = (cb * jnp.where(keep, jnp.exp(seg), 0.0)).astype(bf16)
                xh = jnp.where((lane // p) == k, xpair, jnp.zeros_like(xpair))
                part = jnp.dot(gmat, xh, preferred_element_type=f32)
                acc = part if acc is None else acc + part
            lo = pr * 2 * p
            ys.append(acc + y_off[:, lo:lo + 2 * p])
        upd = jnp.dot(bg.T, xw[:, g * gw:(g + 1) * gw], preferred_element_type=f32)
        s_ref[0, g] = s_old * chunk_decay[:, g * gw:(g + 1) * gw] + upd
    return jnp.concatenate(ys, axis=1)


def _ssd_kernel(z_ref, xs_ref, bc_ref, dt_ref, cwx_ref, cbx_ref, cwb_ref, cbb_ref, dtb_ref, alog_ref, drep_ref,
                nw_ref, initf_ref, initb_ref, y_ref, sf_ref, sb_ref, yb_ref, xc_ref, bcc_ref, *, nb, tb, period,
                heads, hpg):
    j = pl.program_id(1)
    q = SSD_CHUNK
    n = SSD_STATE
    gn = SSD_GROUPS * n
    nchunk = tb // q
    ssd_w = heads * SSD_HEADDIM

    @pl.when(j == 0)
    def _():
        sb_ref[...] = initb_ref[...]

    @pl.when(j == nb)
    def _():
        sf_ref[...] = initf_ref[...]

    a_all = -jnp.exp(alog_ref[...])

    def run(reverse):
        d = 1 if reverse else 0
        blk = (nb - 1 - j) if reverse else (j - nb)
        order = range(nchunk - 1, -1, -1) if reverse else range(nchunk)
        rows = pl.ds(pl.multiple_of(blk * tb, tb), tb)
        if reverse:
            xs_all = _silu(_token_conv3(xs_ref[0], cwx_ref[...], cbx_ref[...], period))
            bc_all = _silu(_token_conv3(bc_ref[0], cwb_ref[...], cbb_ref[...], period)).astype(bcc_ref.dtype)
            xc_ref[rows, :] = xs_all
            bcc_ref[rows, :] = bc_all
        else:
            xs_all = xc_ref[rows, :]
            bc_all = bcc_ref[rows, :]
        for ci in order:
            sl = slice(ci * q, (ci + 1) * q)
            y = _ssd_chunk(xs_all[sl], bc_all[sl, :gn], bc_all[sl, gn:], dt_ref[0, sl, d * heads:(d + 1) * heads],
                           sb_ref if reverse else sf_ref, a_all[d:d + 1], dtb_ref[d:d + 1], reverse, heads, hpg)
            row0 = pl.multiple_of(blk * tb + ci * q, q)
            if reverse:
                yb_ref[pl.ds(row0, q), :] = y
            else:
                y = y + yb_ref[pl.ds(row0, q), :] + drep_ref[...] * xs_all[sl]
                y = y * _silu(z_ref[0, sl, :])
                gw = ssd_w // SSD_GROUPS
                outs = []
                for g in range(SSD_GROUPS):
                    yg = y[:, g * gw:(g + 1) * gw]
                    ms = jnp.mean(yg * yg, axis=-1, keepdims=True)
                    outs.append(yg * lax.rsqrt(ms + EPS) * nw_ref[:, g * gw:(g + 1) * gw])
                y_ref[0, sl, :] = jnp.concatenate(outs, axis=1)

    @pl.when(j < nb)
    def _():
        run(True)

    @pl.when(j >= nb)
    def _():
        run(False)


def ssd_mixer(srcs, cols, conv_w_x, conv_b_x, conv_w_bc, conv_b_bc, dt_bias, a_log, ssd_d, norm_w, init_f, init_b,
              period, tb):
    b, l, _ = srcs[0].shape
    heads = dt_bias.shape[1]
    hpg = heads // SSD_GROUPS
    ssd_w = heads * SSD_HEADDIM
    gn = SSD_GROUPS * SSD_STATE
    nb = l // tb
    oz, ox, obc, odt = cols
    f32 = jnp.float32

    def tok(width, off):
        blk_idx = off // width
        return pl.BlockSpec((1, tb, width),
                            lambda i, j: (i, jnp.where(j < nb, nb - 1 - j, j - nb), blk_idx))

    def tok_fwd(width, off):
        blk_idx = off // width
        return pl.BlockSpec((1, tb, width), lambda i, j: (i, jnp.where(j < nb, 0, j - nb), blk_idx))

    def tok_rev(width, off):
        blk_idx = off // width
        return pl.BlockSpec((1, tb, width), lambda i, j: (i, jnp.maximum(nb - 1 - j, 0), blk_idx))

    def whole(shape):
        return pl.BlockSpec(shape, lambda i, j: (0,) * len(shape))

    st_shape = (b, SSD_GROUPS, SSD_STATE, hpg * SSD_HEADDIM)
    st_spec = pl.BlockSpec((1,) + st_shape[1:], lambda i, j: (i, 0, 0, 0))
    drep = jnp.repeat(ssd_d, SSD_HEADDIM).reshape(1, ssd_w)
    kern = functools.partial(_ssd_kernel, nb=nb, tb=tb, period=period, heads=heads, hpg=hpg)
    return pl.pallas_call(
        kern,
        out_shape=(jax.ShapeDtypeStruct((b, l, ssd_w), f32), jax.ShapeDtypeStruct(st_shape, f32),
                   jax.ShapeDtypeStruct(st_shape, f32)),
        grid=(b, 2 * nb),
        in_specs=[
            tok_fwd(ssd_w, oz), tok_rev(ssd_w, ox), tok_rev(2 * gn, obc), tok(LANE, odt),
            whole((3, ssd_w)), whole((1, ssd_w)), whole((3, 2 * gn)), whole((1, 2 * gn)),
            whole((2, heads)), whole((2, heads)), whole((1, ssd_w)), whole((1, ssd_w)),
            st_spec, st_spec,
        ],
        out_specs=(pl.BlockSpec((1, tb, ssd_w), lambda i, j: (i, jnp.where(j < nb, 0, j - nb), 0)),
                   st_spec, st_spec),
        scratch_shapes=[pltpu.VMEM((l, ssd_w), f32), pltpu.VMEM((l, ssd_w), f32),
                        pltpu.VMEM((l, 2 * gn), jnp.bfloat16)],
        compiler_params=_cparams("parallel", "arbitrary"),
        name="ssd_mixer",
    )(*srcs, conv_w_x, conv_b_x.reshape(1, -1), conv_w_bc, conv_b_bc.reshape(1, -1), dt_bias, a_log, drep,
      norm_w.reshape(1, ssd_w), init_f, init_b)


def _pick(l, pref):
    return pref if l % pref == 0 else l


def kernel(x, c, ctx, c_ctx, w_ada, b_ada, norm1_w, w_in, hy_conv_w, hy_conv_b, filt_w1, filt_b1, filt_freq,
           filt_w2, filt_b2, filt_w3, hy_bias, hy_norm_w, ssd_conv_w, ssd_conv_b, dt_bias, a_log, ssd_d,
           ssd_norm_w, w_out, norm2_w, w_mlp1, w_mlp2, final_norm_w):
    depth = w_in.shape[0]
    bsz, seq, d = x.shape
    hy_w = hy_norm_w.shape[1]
    hy_proj = hy_conv_w.shape[2]
    ssd_w = ssd_norm_w.shape[1]
    ssd_xbc = ssd_conv_w.shape[2]
    heads = ssd_w // SSD_HEADDIM
    hpg = heads // SSD_GROUPS
    ssd_dt = 2 * heads
    o0, o1 = hy_proj, hy_proj + ssd_xbc
    o2 = o1 + ssd_dt
    dt_pad = (-ssd_dt) % LANE
    bf16 = jnp.bfloat16
    ssd_cols = (0, o0, o0 + ssd_w, ssd_w)
    tn = o1 // 3

    n1 = seq // LANE
    n1c = ctx.shape[1] // LANE
    zero_state = jnp.zeros((bsz, SSD_GROUPS, SSD_STATE, hpg * SSD_HEADDIM), jnp.float32)

    rows = 8
    c_all = jnp.zeros((rows, d), jnp.float32).at[:bsz].set(c).at[bsz].set(c_ctx)
    mod_all = ada_mod(c_all, w_ada, b_ada)

    w_a_all, w_b_all = cast_in_proj(w_in, o1, o2, w_in.shape[2], ssd_w + ssd_dt + dt_pad)
    w_out_b, w1_b, w2_b = w_out.astype(bf16), w_mlp1.astype(bf16), w_mlp2.astype(bf16)

    h_ctx = ctx
    tm_l = _pick(seq, 512)
    tm_c = _pick(ctx.shape[1], 256)
    for i in range(depth):
        lp = dict(hy_conv_w=hy_conv_w[i], hy_conv_b=hy_conv_b[i], filt_w1=filt_w1[i], filt_b1=filt_b1[i],
                  filt_freq=filt_freq[i], filt_w2=filt_w2[i], filt_b2=filt_b2[i], filt_w3=filt_w3[i],
                  hy_bias=hy_bias[i], hy_norm_w=hy_norm_w[i], ssd_conv_w=ssd_conv_w[i],
                  ssd_conv_b=ssd_conv_b[i], dt_bias=dt_bias[i], a_log=a_log[i], ssd_d=ssd_d[i],
                  ssd_norm_w=ssd_norm_w[i])
        mod = mod_all[i, :bsz].reshape(bsz, N_MOD, d)
        mod_c = jnp.broadcast_to(mod_all[i, bsz].reshape(1, N_MOD, d), (bsz, N_MOD, d))
        def ssd(pa, pb, init_f, init_b, period):
            cw, cb = lp['ssd_conv_w'], lp['ssd_conv_b']
            return ssd_mixer((pb, pa, pa, pb), ssd_cols, cw[:, :ssd_w], cb[:ssd_w], cw[:, ssd_w:], cb[ssd_w:],
                             lp['dt_bias'], lp['a_log'], lp['ssd_d'], lp['ssd_norm_w'], init_f, init_b, period,
                             _pick(pa.shape[1], 256))

        def filt_taps(length):
            return hyena_filter_taps(length, lp['filt_w1'], lp['filt_b1'], lp['filt_freq'], lp['filt_w2'],
                                     lp['filt_b2'], lp['filt_w3'], hy_w)

        pca, pcb = normmod_matmul(h_ctx, norm1_w[i], mod_c[:, 1], mod_c[:, 0], w_a_all, w_b_all, i, tm_c, tn)
        y_ssd, s_f, s_b = ssd(pca, pcb, zero_state, zero_state, pca.shape[1])
        if i < depth - 1:
            utc = hyena_pre(pca, 0, lp['hy_conv_w'], lp['hy_conv_b'], pca.shape[1], pca.shape[1], 512)
            kre, kim = ctx_filter_spectrum(filt_taps(pca.shape[1]), 128)
            zt = utc
            for o in range(HY_ORDER):
                zt = ctx_hyena_conv(utc, zt, 0, (o + 1) * hy_w, kre, kim, o, lp['hy_bias'][o], 128)
            h_ctx = proj_residual(zt, y_ssd, lp['hy_norm_w'], w_out_b, i, h_ctx, mod_c[:, 2], tm_c, HY_GROUPS)
            h_ctx = mlp_residual(h_ctx, norm2_w[i], mod_c[:, 4], mod_c[:, 3], mod_c[:, 5], w1_b, w2_b, i,
                                 final_norm_w, tm_c, 1024, False)
        pla, plb = normmod_matmul(x, norm1_w[i], mod[:, 1], mod[:, 0], w_a_all, w_b_all, i, tm_l, tn)
        ut = hyena_pre_rows(pla, 0, lp['hy_conv_w'], lp['hy_conv_b'], GRID_W, LANE)
        kre, kim = hyena_filter_spectrum(filt_taps(seq), bsz, 16)
        zt = ut
        for o in range(HY_ORDER):
            zt = hyena_conv(ut, zt, 0, (o + 1) * hy_w, kre, kim, o, lp['hy_bias'][o], bsz, 16, o == HY_ORDER - 1)
        zh_t = zt.reshape(bsz, n1, hy_w, LANE)
        y_ssd, _, _ = ssd(pla, plb, s_f, s_b, GRID_W)
        x = proj_residual(zh_t, y_ssd, lp['hy_norm_w'], w_out_b, i, x, mod[:, 2], tm_l, HY_GROUPS)
        x = mlp_residual(x, norm2_w[i], mod[:, 4], mod[:, 3], mod[:, 5], w1_b, w2_b, i, final_norm_w, tm_l, 1024,
                         i == depth - 1)
    return x
```

```python
import functools
import math

import numpy as np
import jax
import jax.numpy as jnp
from jax import lax
from jax.experimental import pallas as pl
from jax.experimental.pallas import tpu as pltpu

EPS = 1e-6
N_MOD = 6
GRID_W = 64
HY_GROUPS = 8
HY_ORDER = 2
HY_POS_EMB = 33
HY_TARGET = 1e-2
HY_FAST = 0.3
HY_SLOW = 1.5
SSD_HEADDIM = 64
SSD_GROUPS = 2
SSD_STATE = 128
SSD_CHUNK = 128

LANE = 128
LOG2E = 1.4426950408889634
VMEM_LIMIT = 56 * 1024 * 1024


def _cparams(*sem):
    return pltpu.CompilerParams(dimension_semantics=sem, vmem_limit_bytes=VMEM_LIMIT)


def _ada_kernel(c_ref, w_ref, b_ref, o_ref):
    c = c_ref[...]
    a = (c * jax.nn.sigmoid(c)).astype(jnp.bfloat16)
    w = w_ref[0].astype(jnp.bfloat16)
    o_ref[0] = jnp.dot(a, w, preferred_element_type=jnp.float32) + b_ref[0]


def ada_mod(c_all, w_ada, b_ada, tn=1024):
    depth, d, n = w_ada.shape
    r = c_all.shape[0]
    return pl.pallas_call(
        _ada_kernel,
        out_shape=jax.ShapeDtypeStruct((depth, r, n), jnp.float32),
        grid=(depth, n // tn),
        in_specs=[
            pl.BlockSpec((r, d), lambda i, j: (0, 0)),
            pl.BlockSpec((1, d, tn), lambda i, j: (i, 0, j)),
            pl.BlockSpec((1, 1, tn), lambda i, j: (i, 0, j)),
        ],
        out_specs=pl.BlockSpec((1, r, tn), lambda i, j: (i, 0, j)),
        compiler_params=_cparams("parallel", "parallel"),
        name="ada_mod",
    )(c_all, w_ada, b_ada.reshape(depth, 1, n))


def _wcast_kernel(w_ref, a_ref, b_ref, *, na, c0, c1):
    bf16 = jnp.bfloat16
    w = w_ref[0]
    a_ref[0] = w[:, :na].astype(bf16)
    nb = b_ref.shape[2]
    pad = nb - (c1 - c0) - (c0 - na)
    b_ref[0] = jnp.concatenate([w[:, c0:c1], w[:, na:c0], jnp.zeros((w.shape[0], pad), w.dtype)], axis=1).astype(bf16)


def cast_in_proj(w_in, na, c0, c1, nb, tr=256):
    depth, d, n = w_in.shape
    return pl.pallas_call(
        functools.partial(_wcast_kernel, na=na, c0=c0, c1=c1),
        out_shape=(jax.ShapeDtypeStruct((depth, d, na), jnp.bfloat16),
                   jax.ShapeDtypeStruct((depth, d, nb), jnp.bfloat16)),
        grid=(depth, d // tr),
        in_specs=[pl.BlockSpec((1, tr, n), lambda i, j: (i, j, 0))],
        out_specs=(pl.BlockSpec((1, tr, na), lambda i, j: (i, j, 0)),
                   pl.BlockSpec((1, tr, nb), lambda i, j: (i, j, 0))),
        compiler_params=_cparams("parallel", "parallel"),
        name="cast_in_proj",
    )(w_in)


NORM_ROWS = 16


def _normmod_store(h_ref, x_ref, nw, sc, sh):
    w = nw * (1.0 + sc)

    def body(i, carry):
        r0 = pl.multiple_of(i * NORM_ROWS, NORM_ROWS)
        xb = x_ref[0, pl.ds(r0, NORM_ROWS), :]
        ms = jnp.mean(xb * xb, axis=-1, keepdims=True)
        h_ref[pl.ds(r0, NORM_ROWS), :] = ((xb * lax.rsqrt(ms + EPS)) * w + sh).astype(h_ref.dtype)
        return carry

    lax.fori_loop(0, h_ref.shape[0] // NORM_ROWS, body, 0, unroll=8)


def _nm_matmul_kernel(x_ref, nw_ref, sc_ref, sh_ref, wa_ref, wb_ref, oa_ref, ob_ref, h_ref, *, na):
    k = pl.program_id(2)

    @pl.when(k == 0)
    def _():
        _normmod_store(h_ref, x_ref, nw_ref[...], sc_ref[0], sh_ref[0])

    @pl.when(k < na)
    def _():
        oa_ref[0] = jnp.dot(h_ref[...], wa_ref[...], preferred_element_type=jnp.float32)

    @pl.when(k >= na)
    def _():
        ob_ref[0] = jnp.dot(h_ref[...], wb_ref[...], preferred_element_type=jnp.float32)


def normmod_matmul(x, nw, sc, sh, wa, wb, layer, tm, tn):
    b, l, d = x.shape
    n_a, n_b = wa.shape[2], wb.shape[2]
    na = n_a // tn
    last = na - 1
    return pl.pallas_call(
        functools.partial(_nm_matmul_kernel, na=na),
        out_shape=(jax.ShapeDtypeStruct((b, l, n_a), jnp.float32), jax.ShapeDtypeStruct((b, l, n_b), jnp.float32)),
        grid=(b, l // tm, na + 1),
        in_specs=[
            pl.BlockSpec((1, tm, d), lambda i, j, k: (i, j, 0)),
            pl.BlockSpec((1, d), lambda i, j, k: (0, 0)),
            pl.BlockSpec((1, 1, d), lambda i, j, k: (i, 0, 0)),
            pl.BlockSpec((1, 1, d), lambda i, j, k: (i, 0, 0)),
            pl.BlockSpec((None, d, tn), lambda i, j, k: (layer, 0, jnp.minimum(k, last))),
            pl.BlockSpec((None, d, n_b), lambda i, j, k: (layer, 0, 0)),
        ],
        out_specs=(pl.BlockSpec((1, tm, tn), lambda i, j, k: (i, j, jnp.minimum(k, last))),
                   pl.BlockSpec((1, tm, n_b), lambda i, j, k: (i, j, 0))),
        scratch_shapes=[pltpu.VMEM((tm, d), jnp.bfloat16)],
        compiler_params=_cparams("parallel", "parallel", "arbitrary"),
        name="normmod_matmul",
    )(x, nw.reshape(1, d), sc.reshape(b, 1, d), sh.reshape(b, 1, d), wa, wb)


def _proj_res_kernel(zh_ref, ys_ref, nwh_ref, wh_ref, ws_ref, x_ref, g_ref, o_ref, *, groups):
    bf16 = jnp.bfloat16
    slabs, hy_w = zh_ref.shape[1], zh_ref.shape[2]
    gs = hy_w // groups
    rows = []
    for s in range(slabs):
        cols = []
        for gi in range(groups):
            zg = zh_ref[0, s, gi * gs:(gi + 1) * gs, :]
            ms = jnp.mean(zg * zg, axis=0, keepdims=True)
            cols.append((zg * lax.rsqrt(ms + EPS)).T)
        rows.append(jnp.concatenate(cols, axis=1))
    yh = (jnp.concatenate(rows, axis=0) * nwh_ref[...]).astype(bf16)
    acc = jnp.dot(yh, wh_ref[...], preferred_element_type=jnp.float32)
    acc += jnp.dot(ys_ref[0].astype(bf16), ws_ref[...], preferred_element_type=jnp.float32)
    o_ref[0] = x_ref[0] + g_ref[0] * acc


def proj_residual(zh_t, y_ssd, hy_norm_w, w_out, layer, x, g, tm, groups):
    b, n1, hy_w, _ = zh_t.shape
    l, ssd_w = y_ssd.shape[1], y_ssd.shape[2]
    n = w_out.shape[2]
    assert hy_w % ssd_w == 0
    return pl.pallas_call(
        functools.partial(_proj_res_kernel, groups=groups),
        out_shape=jax.ShapeDtypeStruct((b, l, n), jnp.float32),
        grid=(b, l // tm),
        in_specs=[
            pl.BlockSpec((1, tm // LANE, hy_w, LANE), lambda i, j: (i, j, 0, 0)),
            pl.BlockSpec((1, tm, ssd_w), lambda i, j: (i, j, 0)),
            pl.BlockSpec((1, hy_w), lambda i, j: (0, 0)),
            pl.BlockSpec((None, hy_w, n), lambda i, j: (layer, 0, 0)),
            pl.BlockSpec((None, ssd_w, n), lambda i, j: (layer, hy_w // ssd_w, 0)),
            pl.BlockSpec((1, tm, n), lambda i, j: (i, j, 0)),
            pl.BlockSpec((1, 1, n), lambda i, j: (i, 0, 0)),
        ],
        out_specs=pl.BlockSpec((1, tm, n), lambda i, j: (i, j, 0)),
        compiler_params=_cparams("parallel", "parallel"),
        name="proj_residual",
    )(zh_t, y_ssd, hy_norm_w.reshape(1, hy_w), w_out, w_out, x, g.reshape(b, 1, n))


def _hypre_kernel(u_ref, w_ref, b_ref, o_ref, *, period):
    u = _token_conv3(u_ref[0], w_ref[...], b_ref[...], period)
    tb, cw = u.shape
    for s in range(tb // LANE):
        for j in range(cw // LANE):
            o_ref[0, s, j * LANE:(j + 1) * LANE, :] = u[s * LANE:(s + 1) * LANE, j * LANE:(j + 1) * LANE].T


def hyena_pre(p, col0, conv_w, conv_b, period, tb, cw):
    b, l, _ = p.shape
    c = conv_w.shape[1]
    blk0 = col0 // cw
    return pl.pallas_call(
        functools.partial(_hypre_kernel, period=period),
        out_shape=jax.ShapeDtypeStruct((b, l // LANE, c, LANE), jnp.float32),
        grid=(b, l // tb, c // cw),
        in_specs=[
            pl.BlockSpec((1, tb, cw), lambda i, j, k: (i, j, blk0 + k)),
            pl.BlockSpec((3, cw), lambda i, j, k: (0, k)),
            pl.BlockSpec((1, cw), lambda i, j, k: (0, k)),
        ],
        out_specs=pl.BlockSpec((1, tb // LANE, cw, LANE), lambda i, j, k: (i, j, k, 0)),
        compiler_params=_cparams("parallel", "parallel", "parallel"),
        name="hyena_pre",
    )(p, conv_w, conv_b.reshape(1, c))


def _hypre_rows_kernel(u_ref, w_ref, b_ref, o_ref, s_ref, *, period):
    l, cw = u_ref.shape[1], u_ref.shape[2]
    n1 = l // LANE
    assert LANE % period == 0
    w, bias = w_ref[...], b_ref[...]

    def slab(s, carry):
        u = _token_conv3(u_ref[0, pl.ds(pl.multiple_of(s * LANE, LANE), LANE), :], w, bias, period)
        for j in range(cw // LANE):
            s_ref[pl.ds(pl.multiple_of(s * cw + j * LANE, LANE), LANE), :] = u[:, j * LANE:(j + 1) * LANE].T
        return carry

    lax.fori_loop(0, n1, slab, 0, unroll=2)
    for c in range(cw):
        o_ref[:, c * LANE:(c + 1) * LANE] = s_ref[pl.ds(c, n1, stride=cw), :]


def hyena_pre_rows(p, col0, conv_w, conv_b, period, cw):
    b, l, _ = p.shape
    c = conv_w.shape[1]
    n1 = l // LANE
    blk0 = col0 // cw
    return pl.pallas_call(
        functools.partial(_hypre_rows_kernel, period=period),
        out_shape=jax.ShapeDtypeStruct((b * n1, c * LANE), jnp.float32),
        grid=(b, c // cw),
        in_specs=[
            pl.BlockSpec((1, l, cw), lambda i, k: (i, 0, blk0 + k)),
            pl.BlockSpec((3, cw), lambda i, k: (0, k)),
            pl.BlockSpec((1, cw), lambda i, k: (0, k)),
        ],
        out_specs=pl.BlockSpec((n1, cw * LANE), lambda i, k: (i, k)),
        scratch_shapes=[pltpu.VMEM((n1 * cw, LANE), jnp.float32)],
        compiler_params=_cparams("parallel", "parallel"),
        name="hyena_pre_rows",
    )(p, conv_w, conv_b.reshape(1, c))


def _fft_constants(bsz, n1):
    m1 = 2 * n1
    n = m1 * LANE
    pairs = bsz // 2
    half = pairs * m1
    r = bsz * n1
    t1 = np.arange(n1)[:, None]
    f1 = np.arange(m1)[None, :]
    th = 2.0 * np.pi * t1 * f1 / m1
    w1 = np.zeros((r, 2 * half))
    for pr in range(pairs):
        re = slice(pr * m1, (pr + 1) * m1)
        im = slice(half + pr * m1, half + (pr + 1) * m1)
        ra = slice((2 * pr) * n1, (2 * pr + 1) * n1)
        rb = slice((2 * pr + 1) * n1, (2 * pr + 2) * n1)
        w1[ra, re], w1[ra, im] = np.cos(th), -np.sin(th)
        w1[rb, re], w1[rb, im] = np.sin(th), np.cos(th)
    w4 = w1.T / n
    t2 = np.arange(LANE)[:, None]
    ps = 2.0 * np.pi * t2 * np.arange(LANE)[None, :] / LANE
    w2 = np.block([[np.cos(ps), -np.sin(ps)], [np.sin(ps), np.cos(ps)]])
    ph = 2.0 * np.pi * t2 * f1 / n
    tc = np.tile(np.cos(ph), (1, pairs))
    ts = np.tile(np.sin(ph), (1, pairs))
    orders = half // m1
    tf = 2.0 * np.pi * np.arange(m1)[:, None] * f1 / m1
    w1f = np.zeros((orders * m1, 2 * half))
    for o in range(orders):
        w1f[o * m1:(o + 1) * m1, o * m1:(o + 1) * m1] = np.cos(tf)
        w1f[o * m1:(o + 1) * m1, half + o * m1:half + (o + 1) * m1] = -np.sin(tf)
    mats = [jnp.asarray(m, jnp.bfloat16) for m in (w1, w2, w2.T, w4)]
    tabs = [jnp.asarray(m, jnp.float32) for m in (tc, ts, tc.T, ts.T)]
    return mats, tabs, half, jnp.asarray(w1f, jnp.bfloat16)


def _mxu(a, w):
    return jnp.dot(a.astype(jnp.bfloat16), w, preferred_element_type=jnp.float32)


def _fft_forward(x0, w1, w2, tc, ts, half):
    cw = x0.shape[1] // LANE
    a1 = _mxu(x0.T, w1)
    lhs2 = []
    for c in range(cw):
        ar = a1[c * LANE:(c + 1) * LANE, :half]
        ai = a1[c * LANE:(c + 1) * LANE, half:]
        lhs2.append(jnp.concatenate([(ar * tc + ai * ts).T, (ai * tc - ar * ts).T], axis=1))
    return _mxu(jnp.concatenate(lhs2, axis=0), w2)


def _hyconv_kernel(z_ref, g_ref, kre_ref, kim_ref, brep_ref, w1_ref, w2_ref, w2i_ref, w4_ref, tc_ref, ts_ref,
                   tct_ref, tst_ref, o_ref, *, half):
    x0 = z_ref[...]
    cw = x0.shape[1] // LANE
    tct, tst = tct_ref[...], tst_ref[...]
    s = _fft_forward(x0, w1_ref[...], w2_ref[...], tc_ref[...], ts_ref[...], half)
    sr, si = s[:, :LANE], s[:, LANE:]
    m1 = kre_ref.shape[2]
    pairs = half // m1
    shape4 = (cw, pairs, m1, LANE)
    sr, si = sr.reshape(shape4), si.reshape(shape4)
    kr, ki = kre_ref[:, 0][:, None], kim_ref[:, 0][:, None]
    y2 = jnp.concatenate([(sr * kr - si * ki).reshape(cw * half, LANE),
                          (sr * ki + si * kr).reshape(cw * half, LANE)], axis=1)
    bq = _mxu(y2, w2i_ref[...])
    lhs4 = []
    for c in range(cw):
        br = bq[c * half:(c + 1) * half, :LANE]
        bi = bq[c * half:(c + 1) * half, LANE:]
        lhs4.append(jnp.concatenate([(br * tct - bi * tst).T, (bi * tct + br * tst).T], axis=1))
    lhs4 = jnp.concatenate(lhs4, axis=0)
    yt = _mxu(lhs4, w4_ref[...])
    if len(o_ref.shape) == 2:
        o_ref[...] = g_ref[...] * (yt.T + x0 * brep_ref[...])
    else:
        for c in range(cw):
            sl = slice(c * LANE, (c + 1) * LANE)
            o_ref[:, c, :] = g_ref[:, sl] * (yt[sl].T + x0[:, sl] * brep_ref[:, sl])


def hyena_conv(ut, z_src, z_col, g_col, kre, kim, order, bias, bsz, cw, channel_tiles):
    r = ut.shape[0]
    n1 = r // bsz
    c = bias.shape[0]
    mats, tabs, half, _ = _fft_constants(bsz, n1)
    blk = cw * LANE
    zb, gb = z_col // cw, g_col // cw
    brep = jnp.repeat(bias, LANE).reshape(1, c * LANE)
    kspec = pl.BlockSpec((cw, 1, 2 * n1, LANE), lambda j: (j, order, 0, 0))

    def const(a):
        return pl.BlockSpec(a.shape, lambda j: (0, 0))

    return pl.pallas_call(
        functools.partial(_hyconv_kernel, half=half),
        out_shape=jax.ShapeDtypeStruct((r, c, LANE) if channel_tiles else (r, c * LANE), jnp.float32),
        grid=(c // cw,),
        in_specs=[
            pl.BlockSpec((r, blk), lambda j: (0, zb + j)),
            pl.BlockSpec((r, blk), lambda j: (0, gb + j)),
            kspec, kspec,
            pl.BlockSpec((1, blk), lambda j: (0, j)),
        ] + [const(a) for a in mats + tabs],
        out_specs=(pl.BlockSpec((r, cw, LANE), lambda j: (0, j, 0)) if channel_tiles
                   else pl.BlockSpec((r, blk), lambda j: (0, j))),
        compiler_params=_cparams("parallel"),
        name="hyena_conv",
    )(z_src, ut, kre, kim, brep, *mats, *tabs)


_HI = lax.Precision.HIGHEST


def _filtgen_kernel(a0_ref, wc_ref, ws_ref, fb1_ref, freq_ref, fw2t_ref, fb2_ref, fw3t_ref, dl_ref, o_ref, *,
                    seq, bands):
    f32 = jnp.float32
    s = pl.program_id(0)
    t = s * LANE + lax.broadcasted_iota(jnp.int32, (1, LANE), 1)
    pos = jnp.where(t < seq, t, 2 * seq - t).astype(f32)
    tt = pos / (seq - 1.0)
    ang = (2.0 * math.pi / seq) * pos
    j = lax.broadcasted_iota(jnp.int32, (bands, 1), 0).astype(f32)
    fj = 1e-4 + j * ((bands - 1.0 - 1e-4) / (bands - 1.0))
    fa = fj * ang
    freq = freq_ref[...]
    pre = a0_ref[...] * tt + jnp.dot(wc_ref[...], jnp.cos(fa), precision=_HI, preferred_element_type=f32) \
        - jnp.dot(ws_ref[...], jnp.sin(fa), precision=_HI, preferred_element_type=f32) + fb1_ref[...]
    h1 = jnp.sin(freq * pre)
    h2 = jnp.sin(freq * (jnp.dot(fw2t_ref[...], h1, precision=_HI, preferred_element_type=f32) + fb2_ref[...]))
    orders, _, c, hid = fw3t_ref.shape
    w3 = fw3t_ref[:, 0].reshape(orders * c, hid)
    h = jnp.dot(w3, h2, precision=_HI, preferred_element_type=f32)
    h = h * jnp.exp(-tt * dl_ref[...])
    h = jnp.where(t == seq, 0.0, h)
    o_ref[:, 0] = h.reshape(orders, c, LANE)


def hyena_filter_taps(seq, fw1, fb1, freq, fw2, fb2, fw3, hy_w):
    hid = fw2.shape[0]
    bands = (HY_POS_EMB - 1) // 2
    orders = fw3.shape[1] // (2 * hy_w)
    slabs = 2 * seq // LANE
    col = lambda v: v.reshape(-1, 1)
    fw3t = fw3.T.reshape(orders, 2, hy_w, hid)
    deltas = jnp.abs(jnp.linspace(math.log(HY_TARGET) / HY_SLOW, math.log(HY_TARGET) / HY_FAST, hy_w,
                                  dtype=jnp.float32))
    dl = jnp.tile(deltas, orders).reshape(-1, 1)

    def whole(a):
        return pl.BlockSpec(a.shape, lambda s: (0,) * a.ndim)

    args = [col(fw1[0]), fw1[1:1 + bands].T, fw1[1 + bands:].T, col(fb1), col(freq), fw2.T, col(fb2)]
    return pl.pallas_call(
        functools.partial(_filtgen_kernel, seq=seq, bands=bands),
        out_shape=jax.ShapeDtypeStruct((orders, slabs, hy_w, LANE), jnp.float32),
        grid=(slabs,),
        in_specs=[whole(a) for a in args] + [
            pl.BlockSpec((orders, 1, hy_w, hid), lambda s: (0, s // (slabs // 2), 0, 0)),
            whole(dl),
        ],
        out_specs=pl.BlockSpec((orders, 1, hy_w, LANE), lambda s: (0, s, 0, 0)),
        compiler_params=_cparams("parallel"),
        name="hyena_filter_taps",
    )(*args, fw3t, dl)


def _lane_block_abs_norm(x, orders):
    rows_per = x.shape[0] // orders
    cw = x.shape[1] // LANE
    out = []
    for o in range(orders):
        xo = x[o * rows_per:(o + 1) * rows_per]
        a = jnp.sum(jnp.abs(xo), axis=0, keepdims=True)
        inv = [jnp.broadcast_to(1.0 / (jnp.sum(a[:, c * LANE:(c + 1) * LANE], axis=1, keepdims=True) + EPS),
                                (1, LANE)) for c in range(cw)]
        out.append(xo * jnp.concatenate(inv, axis=1))
    return jnp.concatenate(out, axis=0)


def _filtspec_kernel(k_ref, w1_ref, w2_ref, tc_ref, ts_ref, re_ref, im_ref, *, half, orders):
    m1 = half // orders
    x0 = _lane_block_abs_norm(k_ref[...], orders)
    s = _fft_forward(x0, w1_ref[...], w2_ref[...], tc_ref[...], ts_ref[...], half)
    cw = x0.shape[1] // LANE
    re_ref[...] = s[:, :LANE].reshape(cw, orders, m1, LANE)
    im_ref[...] = s[:, LANE:].reshape(cw, orders, m1, LANE)


def hyena_filter_spectrum(taps, bsz, cw):
    orders, m1, c, _ = taps.shape
    mats, tabs, half, w1f = _fft_constants(bsz, m1 // 2)
    assert half == orders * m1
    consts = [w1f, mats[1], tabs[0], tabs[1]]
    out = jax.ShapeDtypeStruct((c, orders, m1, LANE), jnp.float32)
    ospec = pl.BlockSpec((cw, orders, m1, LANE), lambda j: (j, 0, 0, 0))
    return pl.pallas_call(
        functools.partial(_filtspec_kernel, half=half, orders=orders),
        out_shape=(out, out),
        grid=(c // cw,),
        in_specs=[pl.BlockSpec((orders * m1, cw * LANE), lambda j: (0, j))] +
                 [pl.BlockSpec(a.shape, lambda j: (0, 0)) for a in consts],
        out_specs=(ospec, ospec),
        compiler_params=_cparams("parallel"),
        name="hyena_filter_spectrum",
    )(taps.reshape(orders * m1, c * LANE), *consts)


def _dense_dft_constants(seq):
    n = 2 * seq
    t = np.arange(seq)[:, None]
    f = np.arange(n)[None, :]
    ps = 2.0 * np.pi * t * f / n
    fwd = np.block([[np.cos(ps), -np.sin(ps)], [np.sin(ps), np.cos(ps)]])
    inv = fwd.T / n
    tk = np.arange(n)[:, None]
    pk = 2.0 * np.pi * tk * f / n
    fk = np.concatenate([np.cos(pk), -np.sin(pk)], axis=1)
    return [jnp.asarray(m, jnp.bfloat16) for m in (fwd, inv, fk)]


def _ctxspec_kernel(k_ref, fk_ref, re_ref, im_ref):
    orders, slabs = k_ref.shape[0], k_ref.shape[1]
    n = slabs * LANE
    for o in range(orders):
        x = jnp.concatenate([k_ref[o, s] for s in range(slabs)], axis=1)
        x = x / (jnp.sum(jnp.abs(x), axis=1, keepdims=True) + EPS)
        kf = _mxu(x, fk_ref[...])
        re_ref[o] = kf[:, :n]
        im_ref[o] = kf[:, n:]


def ctx_filter_spectrum(taps, cw):
    orders, slabs, c, _ = taps.shape
    n = slabs * LANE
    fk = _dense_dft_constants(n // 2)[2]
    out = jax.ShapeDtypeStruct((orders, c, n), jnp.float32)
    ospec = pl.BlockSpec((orders, cw, n), lambda j: (0, j, 0))
    return pl.pallas_call(
        _ctxspec_kernel,
        out_shape=(out, out),
        grid=(c // cw,),
        in_specs=[pl.BlockSpec((orders, slabs, cw, LANE), lambda j: (0, 0, j, 0)),
                  pl.BlockSpec(fk.shape, lambda j: (0, 0))],
        out_specs=(ospec, ospec),
        compiler_params=_cparams("parallel"),
        name="ctx_filter_spectrum",
    )(taps, fk)


def _ctxconv_kernel(z_ref, g_ref, kre_ref, kim_ref, b_ref, fwd_ref, inv_ref, o_ref):
    bsz, slabs, cw = z_ref.shape[0], z_ref.shape[1], z_ref.shape[2]
    seq = slabs * LANE
    zs = [jnp.concatenate([z_ref[b, s] for s in range(slabs)], axis=1) for b in range(bsz)]
    lhs = jnp.concatenate([jnp.concatenate([zs[2 * p], zs[2 * p + 1]], axis=1) for p in range(bsz // 2)], axis=0)
    s = _mxu(lhs, fwd_ref[...])
    n = 2 * seq
    sr, si = s[:, :n], s[:, n:]
    kr = jnp.concatenate([kre_ref[0]] * (bsz // 2), axis=0)
    ki = jnp.concatenate([kim_ref[0]] * (bsz // 2), axis=0)
    y = _mxu(jnp.concatenate([sr * kr - si * ki, sr * ki + si * kr], axis=1), inv_ref[...])
    bias = jnp.concatenate([b_ref[...]] * slabs, axis=1)
    for b in range(bsz):
        p, m = b // 2, b % 2
        yb = y[p * cw:(p + 1) * cw, m * seq:(m + 1) * seq]
        gate = jnp.concatenate([g_ref[b, s] for s in range(slabs)], axis=1)
        res = gate * (yb + zs[b] * bias)
        for sl in range(slabs):
            o_ref[b, sl] = res[:, sl * LANE:(sl + 1) * LANE]


def ctx_hyena_conv(ut, z_src, z_col, g_col, kre, kim, order, bias, cw):
    bsz, slabs, _, _ = ut.shape
    c = bias.shape[0]
    n = 2 * slabs * LANE
    fwd, inv, _ = _dense_dft_constants(slabs * LANE)
    zb, gb = z_col // cw, g_col // cw
    bb = jnp.broadcast_to(bias[:, None], (c, LANE))
    kspec = pl.BlockSpec((1, cw, n), lambda j: (order, j, 0))
    return pl.pallas_call(
        _ctxconv_kernel,
        out_shape=jax.ShapeDtypeStruct((bsz, slabs, c, LANE), jnp.float32),
        grid=(c // cw,),
        in_specs=[
            pl.BlockSpec((bsz, slabs, cw, LANE), lambda j: (0, 0, zb + j, 0)),
            pl.BlockSpec((bsz, slabs, cw, LANE), lambda j: (0, 0, gb + j, 0)),
            kspec, kspec,
            pl.BlockSpec((cw, LANE), lambda j: (j, 0)),
            pl.BlockSpec(fwd.shape, lambda j: (0, 0)),
            pl.BlockSpec(inv.shape, lambda j: (0, 0)),
        ],
        out_specs=pl.BlockSpec((bsz, slabs, cw, LANE), lambda j: (0, 0, j, 0)),
        compiler_params=_cparams("parallel"),
        name="ctx_hyena_conv",
    )(z_src, ut, kre, kim, bb, fwd, inv)


def _mlp_kernel(x_ref, nw_ref, sc_ref, sh_ref, g_ref, w1_ref, w2_ref, fw_ref, o_ref, h_ref, *, final_norm):
    f = pl.program_id(2)

    @pl.when(f == 0)
    def _():
        _normmod_store(h_ref, x_ref, nw_ref[...], sc_ref[0], sh_ref[0])
        o_ref[...] = jnp.zeros_like(o_ref)

    a = jnp.dot(h_ref[...], w1_ref[...], preferred_element_type=jnp.float32)
    a = jnp.square(jnp.maximum(a, 0.0)).astype(jnp.bfloat16)
    o_ref[0] += jnp.dot(a, w2_ref[...], preferred_element_type=jnp.float32)

    @pl.when(f == pl.num_programs(2) - 1)
    def _():
        y = x_ref[0] + g_ref[0] * o_ref[0]
        if final_norm:
            ms = jnp.mean(y * y, axis=-1, keepdims=True)
            y = y * lax.rsqrt(ms + EPS) * fw_ref[...]
        o_ref[0] = y


def mlp_residual(x, nw, sc, sh, g, w1, w2, layer, fw, tm, tf, final_norm):
    b, l, d = x.shape
    dff = w1.shape[2]
    vec = pl.BlockSpec((1, 1, d), lambda i, j, k: (i, 0, 0))
    row = pl.BlockSpec((1, d), lambda i, j, k: (0, 0))
    return pl.pallas_call(
        functools.partial(_mlp_kernel, final_norm=final_norm),
        out_shape=jax.ShapeDtypeStruct((b, l, d), jnp.float32),
        grid=(b, l // tm, dff // tf),
        in_specs=[
            pl.BlockSpec((1, tm, d), lambda i, j, k: (i, j, 0)),
            row, vec, vec, vec,
            pl.BlockSpec((None, d, tf), lambda i, j, k: (layer, 0, k)),
            pl.BlockSpec((None, tf, d), lambda i, j, k: (layer, k, 0)),
            row,
        ],
        out_specs=pl.BlockSpec((1, tm, d), lambda i, j, k: (i, j, 0)),
        scratch_shapes=[pltpu.VMEM((tm, d), jnp.bfloat16)],
        compiler_params=_cparams("parallel", "parallel", "arbitrary"),
        name="mlp_residual",
    )(x, nw.reshape(1, d), sc.reshape(b, 1, d), sh.reshape(b, 1, d), g.reshape(b, 1, d), w1, w2,
      fw.reshape(1, d))


def _split3(v):
    f32, bf16 = jnp.float32, jnp.bfloat16
    hi = v.astype(bf16)
    r1 = v - hi.astype(f32)
    mid = r1.astype(bf16)
    lo = (r1 - mid.astype(f32)).astype(bf16)
    return jnp.concatenate([hi, mid, lo], axis=1)


def _lane_repeat(v, rep):
    h = v.shape[1]
    row = lax.broadcasted_iota(jnp.int32, (3 * h, h * rep), 0) % h
    col = lax.broadcasted_iota(jnp.int32, (3 * h, h * rep), 1) // rep
    e = (row == col).astype(jnp.bfloat16)
    return jnp.dot(_split3(v), e, preferred_element_type=jnp.float32)


def _silu(v):
    h = 0.5 * v
    return h + h * jnp.tanh(h)


def _softplus(v):
    return jnp.maximum(v, 0.0) + jnp.log1p(jnp.exp(-jnp.abs(v)))


def _token_conv3(u, w, bias, period):
    t, c = u.shape
    pos = lax.broadcasted_iota(jnp.int32, (t, c), 0) % period
    up = jnp.where(pos == 0, 0.0, pltpu.roll(u, 1, 0))
    dn = jnp.where(pos == period - 1, 0.0, pltpu.roll(u, t - 1, 0))
    return bias + w[0:1] * up + w[1:2] * u + w[2:3] * dn


def _ssd_chunk(xs, bm, cm, dtr, s_ref, a_row, dtb_row, reverse, heads, hpg):
    f32, bf16 = jnp.float32, jnp.bfloat16
    q = xs.shape[0]
    p = SSD_HEADDIM
    n = SSD_STATE
    gw = hpg * p
    dt = _softplus(dtr + dtb_row)
    a = dt * (a_row * LOG2E)
    ri = lax.broadcasted_iota(jnp.int32, (q, q), 0)
    ci = lax.broadcasted_iota(jnp.int32, (q, q), 1)
    keep = (ci >= ri) if reverse else (ci <= ri)
    a3 = jnp.dot(keep.astype(bf16), _split3(a), preferred_element_type=f32)
    cs = a3[:, :heads] + a3[:, heads:2 * heads] + a3[:, 2 * heads:]
    dt_rep = _lane_repeat(dt, p)
    cs_rep = _lane_repeat(cs, p)
    cs_wide = _lane_repeat(cs, q)
    end = 0 if reverse else q - 1
    cs_end = cs_rep[end:end + 1]
    xdt = xs * dt_rep
    xw = (xdt * jnp.exp2(cs_end - cs_rep)).astype(bf16)
    ecs = jnp.exp2(cs_rep)
    chunk_decay = jnp.exp2(cs_end)
    xdt_b = xdt.astype(bf16)
    lane = lax.broadcasted_iota(jnp.int32, (q, 2 * p), 1)
    ys = []
    for g in range(SSD_GROUPS):
        bg = bm[:, g * n:(g + 1) * n].astype(bf16)
        cg = cm[:, g * n:(g + 1) * n].astype(bf16)
        cb = lax.dot_general(cg, bg, (((1,), (1,)), ((), ())), preferred_element_type=f32)
        s_old = s_ref[0, g]
        y_off = jnp.dot(cg, s_old.astype(bf16), preferred_element_type=f32) * ecs[:, g * gw:(g + 1) * gw]
        for pr in range(hpg // 2):
            xpair = xdt_b[:, g * gw + pr * 2 * p:g * gw + (pr + 1) * 2 * p]
            acc = None
            for k in range(2):
                h = g * hpg + pr * 2 + k
                csr = cs_wide[:, h * q:(h + 1) * q]
                seg = csr - csr.T
                gmat = (cb * jnp.where(keep, jnp.exp2(seg), 0.0)).astype(bf16)
                xh = jnp.where((lane // p) == k, xpair, jnp.zeros_like(xpair))
                part = jnp.dot(gmat, xh, preferred_element_type=f32)
                acc = part if acc is None else acc + part
            lo = pr * 2 * p
            ys.append(acc + y_off[:, lo:lo + 2 * p])
        upd = jnp.dot(bg.T, xw[:, g * gw:(g + 1) * gw], preferred_element_type=f32)
        s_ref[0, g] = s_old * chunk_decay[:, g * gw:(g + 1) * gw] + upd
    return jnp.concatenate(ys, axis=1)


def _ssd_kernel(z_ref, xs_ref, bc_ref, dt_ref, cwx_ref, cbx_ref, cwb_ref, cbb_ref, dtb_ref, alog_ref, drep_ref,
                nw_ref, initf_ref, initb_ref, y_ref, sf_ref, sb_ref, yb_ref, xc_ref, bcc_ref, *, nb, tb, period,
                heads, hpg):
    j = pl.program_id(1)
    q = SSD_CHUNK
    n = SSD_STATE
    gn = SSD_GROUPS * n
    nchunk = tb // q
    ssd_w = heads * SSD_HEADDIM

    @pl.when(j == 0)
    def _():
        sb_ref[...] = initb_ref[...]

    @pl.when(j == nb)
    def _():
        sf_ref[...] = initf_ref[...]

    a_all = -jnp.exp(alog_ref[...])

    def run(reverse):
        d = 1 if reverse else 0
        blk = (nb - 1 - j) if reverse else (j - nb)
        order = range(nchunk - 1, -1, -1) if reverse else range(nchunk)
        rows = pl.ds(pl.multiple_of(blk * tb, tb), tb)
        if reverse:
            xs_all = _silu(_token_conv3(xs_ref[0], cwx_ref[...], cbx_ref[...], period))
            bc_all = _silu(_token_conv3(bc_ref[0], cwb_ref[...], cbb_ref[...], period)).astype(bcc_ref.dtype)
            xc_ref[rows, :] = xs_all
            bcc_ref[rows, :] = bc_all
        else:
            xs_all = xc_ref[rows, :]
            bc_all = bcc_ref[rows, :]
        for ci in order:
            sl = slice(ci * q, (ci + 1) * q)
            y = _ssd_chunk(xs_all[sl], bc_all[sl, :gn], bc_all[sl, gn:], dt_ref[0, sl, d * heads:(d + 1) * heads],
                           sb_ref if reverse else sf_ref, a_all[d:d + 1], dtb_ref[d:d + 1], reverse, heads, hpg)
            row0 = pl.multiple_of(blk * tb + ci * q, q)
            if reverse:
                yb_ref[pl.ds(row0, q), :] = y
            else:
                y = y + yb_ref[pl.ds(row0, q), :] + drep_ref[...] * xs_all[sl]
                y = y * _silu(z_ref[0, sl, :])
                gw = ssd_w // SSD_GROUPS
                outs = []
                for g in range(SSD_GROUPS):
                    yg = y[:, g * gw:(g + 1) * gw]
                    ms = jnp.mean(yg * yg, axis=-1, keepdims=True)
                    outs.append(yg * lax.rsqrt(ms + EPS) * nw_ref[:, g * gw:(g + 1) * gw])
                y_ref[0, sl, :] = jnp.concatenate(outs, axis=1)

    @pl.when(j < nb)
    def _():
        run(True)

    @pl.when(j >= nb)
    def _():
        run(False)


def ssd_mixer(srcs, cols, conv_w_x, conv_b_x, conv_w_bc, conv_b_bc, dt_bias, a_log, ssd_d, norm_w, init_f, init_b,
              period, tb):
    b, l, _ = srcs[0].shape
    heads = dt_bias.shape[1]
    hpg = heads // SSD_GROUPS
    ssd_w = heads * SSD_HEADDIM
    gn = SSD_GROUPS * SSD_STATE
    nb = l // tb
    oz, ox, obc, odt = cols
    f32 = jnp.float32

    def tok(width, off):
        blk_idx = off // width
        return pl.BlockSpec((1, tb, width),
                            lambda i, j: (i, jnp.where(j < nb, nb - 1 - j, j - nb), blk_idx))

    def tok_fwd(width, off):
        blk_idx = off // width
        return pl.BlockSpec((1, tb, width), lambda i, j: (i, jnp.where(j < nb, 0, j - nb), blk_idx))

    def tok_rev(width, off):
        blk_idx = off // width
        return pl.BlockSpec((1, tb, width), lambda i, j: (i, jnp.maximum(nb - 1 - j, 0), blk_idx))

    def whole(shape):
        return pl.BlockSpec(shape, lambda i, j: (0,) * len(shape))

    st_shape = (b, SSD_GROUPS, SSD_STATE, hpg * SSD_HEADDIM)
    st_spec = pl.BlockSpec((1,) + st_shape[1:], lambda i, j: (i, 0, 0, 0))
    drep = jnp.repeat(ssd_d, SSD_HEADDIM).reshape(1, ssd_w)
    kern = functools.partial(_ssd_kernel, nb=nb, tb=tb, period=period, heads=heads, hpg=hpg)
    return pl.pallas_call(
        kern,
        out_shape=(jax.ShapeDtypeStruct((b, l, ssd_w), f32), jax.ShapeDtypeStruct(st_shape, f32),
                   jax.ShapeDtypeStruct(st_shape, f32)),
        grid=(b, 2 * nb),
        in_specs=[
            tok_fwd(ssd_w, oz), tok_rev(ssd_w, ox), tok_rev(2 * gn, obc), tok(LANE, odt),
            whole((3, ssd_w)), whole((1, ssd_w)), whole((3, 2 * gn)), whole((1, 2 * gn)),
            whole((2, heads)), whole((2, heads)), whole((1, ssd_w)), whole((1, ssd_w)),
            st_spec, st_spec,
        ],
        out_specs=(pl.BlockSpec((1, tb, ssd_w), lambda i, j: (i, jnp.where(j < nb, 0, j - nb), 0)),
                   st_spec, st_spec),
        scratch_shapes=[pltpu.VMEM((l, ssd_w), f32), pltpu.VMEM((l, ssd_w), f32),
                        pltpu.VMEM((l, 2 * gn), jnp.bfloat16)],
        compiler_params=_cparams("parallel", "arbitrary"),
        name="ssd_mixer",
    )(*srcs, conv_w_x, conv_b_x.reshape(1, -1), conv_w_bc, conv_b_bc.reshape(1, -1), dt_bias, a_log, drep,
      norm_w.reshape(1, ssd_w), init_f, init_b)


def _pick(l, pref):
    return pref if l % pref == 0 else l


def kernel(x, c, ctx, c_ctx, w_ada, b_ada, norm1_w, w_in, hy_conv_w, hy_conv_b, filt_w1, filt_b1, filt_freq,
           filt_w2, filt_b2, filt_w3, hy_bias, hy_norm_w, ssd_conv_w, ssd_conv_b, dt_bias, a_log, ssd_d,
           ssd_norm_w, w_out, norm2_w, w_mlp1, w_mlp2, final_norm_w):
    depth = w_in.shape[0]
    bsz, seq, d = x.shape
    hy_w = hy_norm_w.shape[1]
    hy_proj = hy_conv_w.shape[2]
    ssd_w = ssd_norm_w.shape[1]
    ssd_xbc = ssd_conv_w.shape[2]
    heads = ssd_w // SSD_HEADDIM
    hpg = heads // SSD_GROUPS
    ssd_dt = 2 * heads
    o0, o1 = hy_proj, hy_proj + ssd_xbc
    o2 = o1 + ssd_dt
    dt_pad = (-ssd_dt) % LANE
    bf16 = jnp.bfloat16
    ssd_cols = (0, o0, o0 + ssd_w, ssd_w)
    tn = o1 // 3

    n1 = seq // LANE
    n1c = ctx.shape[1] // LANE
    zero_state = jnp.zeros((bsz, SSD_GROUPS, SSD_STATE, hpg * SSD_HEADDIM), jnp.float32)

    rows = 8
    c_all = jnp.zeros((rows, d), jnp.float32).at[:bsz].set(c).at[bsz].set(c_ctx)
    mod_all = ada_mod(c_all, w_ada, b_ada)

    w_a_all, w_b_all = cast_in_proj(w_in, o1, o2, w_in.shape[2], ssd_w + ssd_dt + dt_pad)
    w_out_b, w1_b, w2_b = w_out.astype(bf16), w_mlp1.astype(bf16), w_mlp2.astype(bf16)

    h_ctx = ctx
    tm_l = _pick(seq, 512)
    tm_c = _pick(ctx.shape[1], 256)
    for i in range(depth):
        lp = dict(hy_conv_w=hy_conv_w[i], hy_conv_b=hy_conv_b[i], filt_w1=filt_w1[i], filt_b1=filt_b1[i],
                  filt_freq=filt_freq[i], filt_w2=filt_w2[i], filt_b2=filt_b2[i], filt_w3=filt_w3[i],
                  hy_bias=hy_bias[i], hy_norm_w=hy_norm_w[i], ssd_conv_w=ssd_conv_w[i],
                  ssd_conv_b=ssd_conv_b[i], dt_bias=dt_bias[i], a_log=a_log[i], ssd_d=ssd_d[i],
                  ssd_norm_w=ssd_norm_w[i])
        mod = mod_all[i, :bsz].reshape(bsz, N_MOD, d)
        mod_c = jnp.broadcast_to(mod_all[i, bsz].reshape(1, N_MOD, d), (bsz, N_MOD, d))
        def ssd(pa, pb, init_f, init_b, period):
            cw, cb = lp['ssd_conv_w'], lp['ssd_conv_b']
            return ssd_mixer((pb, pa, pa, pb), ssd_cols, cw[:, :ssd_w], cb[:ssd_w], cw[:, ssd_w:], cb[ssd_w:],
                             lp['dt_bias'], lp['a_log'], lp['ssd_d'], lp['ssd_norm_w'], init_f, init_b, period,
                             _pick(pa.shape[1], 256))

        def filt_taps(length):
            return hyena_filter_taps(length, lp['filt_w1'], lp['filt_b1'], lp['filt_freq'], lp['filt_w2'],
                                     lp['filt_b2'], lp['filt_w3'], hy_w)

        pca, pcb = normmod_matmul(h_ctx, norm1_w[i], mod_c[:, 1], mod_c[:, 0], w_a_all, w_b_all, i, tm_c, tn)
        y_ssd, s_f, s_b = ssd(pca, pcb, zero_state, zero_state, pca.shape[1])
        if i < depth - 1:
            utc = hyena_pre(pca, 0, lp['hy_conv_w'], lp['hy_conv_b'], pca.shape[1], pca.shape[1], 512)
            kre, kim = ctx_filter_spectrum(filt_taps(pca.shape[1]), 128)
            zt = utc
            for o in range(HY_ORDER):
                zt = ctx_hyena_conv(utc, zt, 0, (o + 1) * hy_w, kre, kim, o, lp['hy_bias'][o], 128)
            h_ctx = proj_residual(zt, y_ssd, lp['hy_norm_w'], w_out_b, i, h_ctx, mod_c[:, 2], tm_c, HY_GROUPS)
            h_ctx = mlp_residual(h_ctx, norm2_w[i], mod_c[:, 4], mod_c[:, 3], mod_c[:, 5], w1_b, w2_b, i,
                                 final_norm_w, tm_c, 1024, False)
        pla, plb = normmod_matmul(x, norm1_w[i], mod[:, 1], mod[:, 0], w_a_all, w_b_all, i, tm_l, tn)
        ut = hyena_pre_rows(pla, 0, lp['hy_conv_w'], lp['hy_conv_b'], GRID_W, LANE)
        kre, kim = hyena_filter_spectrum(filt_taps(seq), bsz, 16)
        zt = ut
        for o in range(HY_ORDER):
            zt = hyena_conv(ut, zt, 0, (o + 1) * hy_w, kre, kim, o, lp['hy_bias'][o], bsz, 16, o == HY_ORDER - 1)
        zh_t = zt.reshape(bsz, n1, hy_w, LANE)
        y_ssd, _, _ = ssd(pla, plb, s_f, s_b, GRID_W)
        x = proj_residual(zh_t, y_ssd, lp['hy_norm_w'], w_out_b, i, x, mod[:, 2], tm_l, HY_GROUPS)
        x = mlp_residual(x, norm2_w[i], mod[:, 4], mod[:, 3], mod[:, 5], w1_b, w2_b, i, final_norm_w, tm_l, 1024,
                         i == depth - 1)
    return x
```

```python
import functools
import math

import numpy as np
import jax
import jax.numpy as jnp
from jax import lax
from jax.experimental import pallas as pl
from jax.experimental.pallas import tpu as pltpu

EPS = 1e-6
N_MOD = 6
GRID_W = 64
HY_GROUPS = 8
HY_ORDER = 2
HY_POS_EMB = 33
HY_TARGET = 1e-2
HY_FAST = 0.3
HY_SLOW = 1.5
SSD_HEADDIM = 64
SSD_GROUPS = 2
SSD_STATE = 128
SSD_CHUNK = 128

LANE = 128
LOG2E = 1.4426950408889634
VMEM_LIMIT = 56 * 1024 * 1024


def _cparams(*sem):
    return pltpu.CompilerParams(dimension_semantics=sem, vmem_limit_bytes=VMEM_LIMIT)


def _ada_kernel(c_ref, w_ref, b_ref, o_ref):
    c = c_ref[...]
    a = (c * jax.nn.sigmoid(c)).astype(jnp.bfloat16)
    w = w_ref[0].astype(jnp.bfloat16)
    o_ref[0] = jnp.dot(a, w, preferred_element_type=jnp.float32) + b_ref[0]


def ada_mod(c_all, w_ada, b_ada, tn=1024):
    depth, d, n = w_ada.shape
    r = c_all.shape[0]
    return pl.pallas_call(
        _ada_kernel,
        out_shape=jax.ShapeDtypeStruct((depth, r, n), jnp.float32),
        grid=(depth, n // tn),
        in_specs=[
            pl.BlockSpec((r, d), lambda i, j: (0, 0)),
            pl.BlockSpec((1, d, tn), lambda i, j: (i, 0, j)),
            pl.BlockSpec((1, 1, tn), lambda i, j: (i, 0, j)),
        ],
        out_specs=pl.BlockSpec((1, r, tn), lambda i, j: (i, 0, j)),
        compiler_params=_cparams("parallel", "parallel"),
        name="ada_mod",
    )(c_all, w_ada, b_ada.reshape(depth, 1, n))


def _wcast_kernel(w_ref, a_ref, b_ref, *, na, c0, c1):
    bf16 = jnp.bfloat16
    w = w_ref[0]
    a_ref[0] = w[:, :na].astype(bf16)
    nb = b_ref.shape[2]
    pad = nb - (c1 - c0) - (c0 - na)
    b_ref[0] = jnp.concatenate([w[:, c0:c1], w[:, na:c0], jnp.zeros((w.shape[0], pad), w.dtype)], axis=1).astype(bf16)


def cast_in_proj(w_in, na, c0, c1, nb, tr=256):
    depth, d, n = w_in.shape
    return pl.pallas_call(
        functools.partial(_wcast_kernel, na=na, c0=c0, c1=c1),
        out_shape=(jax.ShapeDtypeStruct((depth, d, na), jnp.bfloat16),
                   jax.ShapeDtypeStruct((depth, d, nb), jnp.bfloat16)),
        grid=(depth, d // tr),
        in_specs=[pl.BlockSpec((1, tr, n), lambda i, j: (i, j, 0))],
        out_specs=(pl.BlockSpec((1, tr, na), lambda i, j: (i, j, 0)),
                   pl.BlockSpec((1, tr, nb), lambda i, j: (i, j, 0))),
        compiler_params=_cparams("parallel", "parallel"),
        name="cast_in_proj",
    )(w_in)


NORM_ROWS = 16


def _normmod_store(h_ref, x_ref, nw, sc, sh):
    w = nw * (1.0 + sc)

    def body(i, carry):
        r0 = pl.multiple_of(i * NORM_ROWS, NORM_ROWS)
        xb = x_ref[0, pl.ds(r0, NORM_ROWS), :]
        ms = jnp.mean(xb * xb, axis=-1, keepdims=True)
        h_ref[pl.ds(r0, NORM_ROWS), :] = ((xb * lax.rsqrt(ms + EPS)) * w + sh).astype(h_ref.dtype)
        return carry

    lax.fori_loop(0, h_ref.shape[0] // NORM_ROWS, body, 0, unroll=8)


def _nm_matmul_kernel(x_ref, nw_ref, sc_ref, sh_ref, wa_ref, wb_ref, oa_ref, ob_ref, h_ref, *, na):
    k = pl.program_id(2)

    @pl.when(k == 0)
    def _():
        _normmod_store(h_ref, x_ref, nw_ref[...], sc_ref[0], sh_ref[0])

    @pl.when(k < na)
    def _():
        oa_ref[0] = jnp.dot(h_ref[...], wa_ref[...], preferred_element_type=jnp.float32)

    @pl.when(k >= na)
    def _():
        ob_ref[0] = jnp.dot(h_ref[...], wb_ref[...], preferred_element_type=jnp.float32)


def normmod_matmul(x, nw, sc, sh, wa, wb, layer, tm, tn):
    b, l, d = x.shape
    n_a, n_b = wa.shape[2], wb.shape[2]
    na = n_a // tn
    last = na - 1
    return pl.pallas_call(
        functools.partial(_nm_matmul_kernel, na=na),
        out_shape=(jax.ShapeDtypeStruct((b, l, n_a), jnp.float32), jax.ShapeDtypeStruct((b, l, n_b), jnp.float32)),
        grid=(b, l // tm, na + 1),
        in_specs=[
            pl.BlockSpec((1, tm, d), lambda i, j, k: (i, j, 0)),
            pl.BlockSpec((1, d), lambda i, j, k: (0, 0)),
            pl.BlockSpec((1, 1, d), lambda i, j, k: (i, 0, 0)),
            pl.BlockSpec((1, 1, d), lambda i, j, k: (i, 0, 0)),
            pl.BlockSpec((None, d, tn), lambda i, j, k: (layer, 0, jnp.minimum(k, last))),
            pl.BlockSpec((None, d, n_b), lambda i, j, k: (layer, 0, 0)),
        ],
        out_specs=(pl.BlockSpec((1, tm, tn), lambda i, j, k: (i, j, jnp.minimum(k, last))),
                   pl.BlockSpec((1, tm, n_b), lambda i, j, k: (i, j, 0))),
        scratch_shapes=[pltpu.VMEM((tm, d), jnp.bfloat16)],
        compiler_params=_cparams("parallel", "parallel", "arbitrary"),
        name="normmod_matmul",
    )(x, nw.reshape(1, d), sc.reshape(b, 1, d), sh.reshape(b, 1, d), wa, wb)


def _proj_res_kernel(zh_ref, ys_ref, nwh_ref, wh_ref, ws_ref, x_ref, g_ref, o_ref, *, groups):
    bf16 = jnp.bfloat16
    slabs, hy_w = zh_ref.shape[1], zh_ref.shape[2]
    gs = hy_w // groups
    rows = []
    for s in range(slabs):
        cols = []
        for gi in range(groups):
            zg = zh_ref[0, s, gi * gs:(gi + 1) * gs, :]
            ms = jnp.mean(zg * zg, axis=0, keepdims=True)
            cols.append((zg * lax.rsqrt(ms + EPS)).T)
        rows.append(jnp.concatenate(cols, axis=1))
    yh = (jnp.concatenate(rows, axis=0) * nwh_ref[...]).astype(bf16)
    acc = jnp.dot(yh, wh_ref[...], preferred_element_type=jnp.float32)
    acc += jnp.dot(ys_ref[0].astype(bf16), ws_ref[...], preferred_element_type=jnp.float32)
    o_ref[0] = x_ref[0] + g_ref[0] * acc


def proj_residual(zh_t, y_ssd, hy_norm_w, w_out, layer, x, g, tm, groups):
    b, n1, hy_w, _ = zh_t.shape
    l, ssd_w = y_ssd.shape[1], y_ssd.shape[2]
    n = w_out.shape[2]
    assert hy_w % ssd_w == 0
    return pl.pallas_call(
        functools.partial(_proj_res_kernel, groups=groups),
        out_shape=jax.ShapeDtypeStruct((b, l, n), jnp.float32),
        grid=(b, l // tm),
        in_specs=[
            pl.BlockSpec((1, tm // LANE, hy_w, LANE), lambda i, j: (i, j, 0, 0)),
            pl.BlockSpec((1, tm, ssd_w), lambda i, j: (i, j, 0)),
            pl.BlockSpec((1, hy_w), lambda i, j: (0, 0)),
            pl.BlockSpec((None, hy_w, n), lambda i, j: (layer, 0, 0)),
            pl.BlockSpec((None, ssd_w, n), lambda i, j: (layer, hy_w // ssd_w, 0)),
            pl.BlockSpec((1, tm, n), lambda i, j: (i, j, 0)),
            pl.BlockSpec((1, 1, n), lambda i, j: (i, 0, 0)),
        ],
        out_specs=pl.BlockSpec((1, tm, n), lambda i, j: (i, j, 0)),
        compiler_params=_cparams("parallel", "parallel"),
        name="proj_residual",
    )(zh_t, y_ssd, hy_norm_w.reshape(1, hy_w), w_out, w_out, x, g.reshape(b, 1, n))


def _hypre_kernel(u_ref, w_ref, b_ref, o_ref, *, period):
    u = _token_conv3(u_ref[0], w_ref[...], b_ref[...], period)
    tb, cw = u.shape
    for s in range(tb // LANE):
        for j in range(cw // LANE):
            o_ref[0, s, j * LANE:(j + 1) * LANE, :] = u[s * LANE:(s + 1) * LANE, j * LANE:(j + 1) * LANE].T


def hyena_pre(p, col0, conv_w, conv_b, period, tb, cw):
    b, l, _ = p.shape
    c = conv_w.shape[1]
    blk0 = col0 // cw
    return pl.pallas_call(
        functools.partial(_hypre_kernel, period=period),
        out_shape=jax.ShapeDtypeStruct((b, l // LANE, c, LANE), jnp.float32),
        grid=(b, l // tb, c // cw),
        in_specs=[
            pl.BlockSpec((1, tb, cw), lambda i, j, k: (i, j, blk0 + k)),
            pl.BlockSpec((3, cw), lambda i, j, k: (0, k)),
            pl.BlockSpec((1, cw), lambda i, j, k: (0, k)),
        ],
        out_specs=pl.BlockSpec((1, tb // LANE, cw, LANE), lambda i, j, k: (i, j, k, 0)),
        compiler_params=_cparams("parallel", "parallel", "parallel"),
        name="hyena_pre",
    )(p, conv_w, conv_b.reshape(1, c))


def _hypre_rows_kernel(u_ref, w_ref, b_ref, o_ref, s_ref, *, period):
    l, cw = u_ref.shape[1], u_ref.shape[2]
    n1 = l // LANE
    assert LANE % period == 0
    w, bias = w_ref[...], b_ref[...]

    for s in range(n1):
        u = _token_conv3(u_ref[0, s * LANE:(s + 1) * LANE, :], w, bias, period)
        for j in range(cw // LANE):
            s_ref[s * cw + j * LANE:s * cw + (j + 1) * LANE, :] = u[:, j * LANE:(j + 1) * LANE].T
    for c in range(cw):
        o_ref[:, c * LANE:(c + 1) * LANE] = s_ref[pl.ds(c, n1, stride=cw), :]


def hyena_pre_rows(p, col0, conv_w, conv_b, period, cw):
    b, l, _ = p.shape
    c = conv_w.shape[1]
    n1 = l // LANE
    blk0 = col0 // cw
    return pl.pallas_call(
        functools.partial(_hypre_rows_kernel, period=period),
        out_shape=jax.ShapeDtypeStruct((b * n1, c * LANE), jnp.float32),
        grid=(b, c // cw),
        in_specs=[
            pl.BlockSpec((1, l, cw), lambda i, k: (i, 0, blk0 + k)),
            pl.BlockSpec((3, cw), lambda i, k: (0, k)),
            pl.BlockSpec((1, cw), lambda i, k: (0, k)),
        ],
        out_specs=pl.BlockSpec((n1, cw * LANE), lambda i, k: (i, k)),
        scratch_shapes=[pltpu.VMEM((n1 * cw, LANE), jnp.float32)],
        compiler_params=_cparams("parallel", "parallel"),
        name="hyena_pre_rows",
    )(p, conv_w, conv_b.reshape(1, c))


def _fft_constants(bsz, n1):
    m1 = 2 * n1
    n = m1 * LANE
    pairs = bsz // 2
    half = pairs * m1
    r = bsz * n1
    t1 = np.arange(n1)[:, None]
    f1 = np.arange(m1)[None, :]
    th = 2.0 * np.pi * t1 * f1 / m1
    w1 = np.zeros((r, 2 * half))
    for pr in range(pairs):
        re = slice(pr * m1, (pr + 1) * m1)
        im = slice(half + pr * m1, half + (pr + 1) * m1)
        ra = slice((2 * pr) * n1, (2 * pr + 1) * n1)
        rb = slice((2 * pr + 1) * n1, (2 * pr + 2) * n1)
        w1[ra, re], w1[ra, im] = np.cos(th), -np.sin(th)
        w1[rb, re], w1[rb, im] = np.sin(th), np.cos(th)
    w4 = w1.T / n
    t2 = np.arange(LANE)[:, None]
    ps = 2.0 * np.pi * t2 * np.arange(LANE)[None, :] / LANE
    w2 = np.block([[np.cos(ps), -np.sin(ps)], [np.sin(ps), np.cos(ps)]])
    ph = 2.0 * np.pi * t2 * f1 / n
    tc = np.tile(np.cos(ph), (1, pairs))
    ts = np.tile(np.sin(ph), (1, pairs))
    orders = half // m1
    tf = 2.0 * np.pi * np.arange(m1)[:, None] * f1 / m1
    w1f = np.zeros((orders * m1, 2 * half))
    for o in range(orders):
        w1f[o * m1:(o + 1) * m1, o * m1:(o + 1) * m1] = np.cos(tf)
        w1f[o * m1:(o + 1) * m1, half + o * m1:half + (o + 1) * m1] = -np.sin(tf)
    mats = [jnp.asarray(m, jnp.bfloat16) for m in (w1, w2, w2.T, w4)]
    tabs = [jnp.asarray(m, jnp.float32) for m in (tc, ts, tc.T, ts.T)]
    return mats, tabs, half, jnp.asarray(w1f, jnp.bfloat16)


def _mxu(a, w):
    return jnp.dot(a.astype(jnp.bfloat16), w, preferred_element_type=jnp.float32)


def _fft_forward(x0, w1, w2, tc, ts, half):
    cw = x0.shape[1] // LANE
    a1 = _mxu(x0.T, w1)
    lhs2 = []
    for c in range(cw):
        ar = a1[c * LANE:(c + 1) * LANE, :half]
        ai = a1[c * LANE:(c + 1) * LANE, half:]
        lhs2.append(jnp.concatenate([(ar * tc + ai * ts).T, (ai * tc - ar * ts).T], axis=1))
    return _mxu(jnp.concatenate(lhs2, axis=0), w2)


def _hyconv_kernel(z_ref, g_ref, kre_ref, kim_ref, brep_ref, w1_ref, w2_ref, w2i_ref, w4_ref, tc_ref, ts_ref,
                   tct_ref, tst_ref, o_ref, *, half):
    x0 = z_ref[...]
    cw = x0.shape[1] // LANE
    tct, tst = tct_ref[...], tst_ref[...]
    s = _fft_forward(x0, w1_ref[...], w2_ref[...], tc_ref[...], ts_ref[...], half)
    sr, si = s[:, :LANE], s[:, LANE:]
    m1 = kre_ref.shape[2]
    pairs = half // m1
    shape4 = (cw, pairs, m1, LANE)
    sr, si = sr.reshape(shape4), si.reshape(shape4)
    kr, ki = kre_ref[:, 0][:, None], kim_ref[:, 0][:, None]
    y2 = jnp.concatenate([(sr * kr - si * ki).reshape(cw * half, LANE),
                          (sr * ki + si * kr).reshape(cw * half, LANE)], axis=1)
    bq = _mxu(y2, w2i_ref[...])
    lhs4 = []
    for c in range(cw):
        br = bq[c * half:(c + 1) * half, :LANE]
        bi = bq[c * half:(c + 1) * half, LANE:]
        lhs4.append(jnp.concatenate([(br * tct - bi * tst).T, (bi * tct + br * tst).T], axis=1))
    lhs4 = jnp.concatenate(lhs4, axis=0)
    yt = _mxu(lhs4, w4_ref[...])
    if len(o_ref.shape) == 2:
        o_ref[...] = g_ref[...] * (yt.T + x0 * brep_ref[...])
    else:
        for c in range(cw):
            sl = slice(c * LANE, (c + 1) * LANE)
            o_ref[:, c, :] = g_ref[:, sl] * (yt[sl].T + x0[:, sl] * brep_ref[:, sl])


def hyena_conv(ut, z_src, z_col, g_col, kre, kim, order, bias, bsz, cw, channel_tiles):
    r = ut.shape[0]
    n1 = r // bsz
    c = bias.shape[0]
    mats, tabs, half, _ = _fft_constants(bsz, n1)
    blk = cw * LANE
    zb, gb = z_col // cw, g_col // cw
    brep = jnp.repeat(bias, LANE).reshape(1, c * LANE)
    kspec = pl.BlockSpec((cw, 1, 2 * n1, LANE), lambda j: (j, order, 0, 0))

    def const(a):
        return pl.BlockSpec(a.shape, lambda j: (0, 0))

    return pl.pallas_call(
        functools.partial(_hyconv_kernel, half=half),
        out_shape=jax.ShapeDtypeStruct((r, c, LANE) if channel_tiles else (r, c * LANE), jnp.float32),
        grid=(c // cw,),
        in_specs=[
            pl.BlockSpec((r, blk), lambda j: (0, zb + j)),
            pl.BlockSpec((r, blk), lambda j: (0, gb + j)),
            kspec, kspec,
            pl.BlockSpec((1, blk), lambda j: (0, j)),
        ] + [const(a) for a in mats + tabs],
        out_specs=(pl.BlockSpec((r, cw, LANE), lambda j: (0, j, 0)) if channel_tiles
                   else pl.BlockSpec((r, blk), lambda j: (0, j))),
        compiler_params=_cparams("parallel"),
        name="hyena_conv",
    )(z_src, ut, kre, kim, brep, *mats, *tabs)


_HI = lax.Precision.HIGHEST


def _filtgen_kernel(a0_ref, wc_ref, ws_ref, fb1_ref, freq_ref, fw2t_ref, fb2_ref, fw3t_ref, dl_ref, o_ref, *,
                    seq, bands):
    f32 = jnp.float32
    s = pl.program_id(0)
    t = s * LANE + lax.broadcasted_iota(jnp.int32, (1, LANE), 1)
    pos = jnp.where(t < seq, t, 2 * seq - t).astype(f32)
    tt = pos / (seq - 1.0)
    ang = (2.0 * math.pi / seq) * pos
    j = lax.broadcasted_iota(jnp.int32, (bands, 1), 0).astype(f32)
    fj = 1e-4 + j * ((bands - 1.0 - 1e-4) / (bands - 1.0))
    fa = fj * ang
    freq = freq_ref[...]
    pre = a0_ref[...] * tt + jnp.dot(wc_ref[...], jnp.cos(fa), precision=_HI, preferred_element_type=f32) \
        - jnp.dot(ws_ref[...], jnp.sin(fa), precision=_HI, preferred_element_type=f32) + fb1_ref[...]
    h1 = jnp.sin(freq * pre)
    h2 = jnp.sin(freq * (jnp.dot(fw2t_ref[...], h1, precision=_HI, preferred_element_type=f32) + fb2_ref[...]))
    orders, _, c, hid3 = fw3t_ref.shape
    h2_hi = h2.astype(jnp.bfloat16)
    h2_lo = (h2 - h2_hi.astype(f32)).astype(jnp.bfloat16)
    h = jnp.dot(fw3t_ref[:, 0].reshape(orders * c, hid3), jnp.concatenate([h2_hi, h2_lo, h2_hi], axis=0),
                preferred_element_type=f32)
    h = h * jnp.exp(-tt * dl_ref[...])
    h = jnp.where(t == seq, 0.0, h)
    o_ref[:, 0] = h.reshape(orders, c, LANE)


def hyena_filter_taps(seq, fw1, fb1, freq, fw2, fb2, fw3, hy_w):
    hid = fw2.shape[0]
    bands = (HY_POS_EMB - 1) // 2
    orders = fw3.shape[1] // (2 * hy_w)
    slabs = 2 * seq // LANE
    col = lambda v: v.reshape(-1, 1)
    fw3t = fw3.T.reshape(orders, 2, hy_w, hid)
    w_hi = fw3t.astype(jnp.bfloat16)
    w_lo = (fw3t - w_hi.astype(jnp.float32)).astype(jnp.bfloat16)
    fw3t = jnp.concatenate([w_hi, w_hi, w_lo], axis=-1)
    deltas = jnp.abs(jnp.linspace(math.log(HY_TARGET) / HY_SLOW, math.log(HY_TARGET) / HY_FAST, hy_w,
                                  dtype=jnp.float32))
    dl = jnp.tile(deltas, orders).reshape(-1, 1)

    def whole(a):
        return pl.BlockSpec(a.shape, lambda s: (0,) * a.ndim)

    args = [col(fw1[0]), fw1[1:1 + bands].T, fw1[1 + bands:].T, col(fb1), col(freq), fw2.T, col(fb2)]
    return pl.pallas_call(
        functools.partial(_filtgen_kernel, seq=seq, bands=bands),
        out_shape=jax.ShapeDtypeStruct((orders, slabs, hy_w, LANE), jnp.float32),
        grid=(slabs,),
        in_specs=[whole(a) for a in args] + [
            pl.BlockSpec((orders, 1, hy_w, 3 * hid), lambda s: (0, s // (slabs // 2), 0, 0)),
            whole(dl),
        ],
        out_specs=pl.BlockSpec((orders, 1, hy_w, LANE), lambda s: (0, s, 0, 0)),
        compiler_params=_cparams("parallel"),
        name="hyena_filter_taps",
    )(*args, fw3t, dl)


def _lane_block_abs_norm(x, orders):
    rows_per = x.shape[0] // orders
    cw = x.shape[1] // LANE
    out = []
    for o in range(orders):
        xo = x[o * rows_per:(o + 1) * rows_per]
        a = jnp.sum(jnp.abs(xo), axis=0, keepdims=True)
        inv = [jnp.broadcast_to(1.0 / (jnp.sum(a[:, c * LANE:(c + 1) * LANE], axis=1, keepdims=True) + EPS),
                                (1, LANE)) for c in range(cw)]
        out.append(xo * jnp.concatenate(inv, axis=1))
    return jnp.concatenate(out, axis=0)


def _filtspec_kernel(k_ref, w1_ref, w2_ref, tc_ref, ts_ref, re_ref, im_ref, *, half, orders):
    m1 = half // orders
    x0 = _lane_block_abs_norm(k_ref[...], orders)
    s = _fft_forward(x0, w1_ref[...], w2_ref[...], tc_ref[...], ts_ref[...], half)
    cw = x0.shape[1] // LANE
    re_ref[...] = s[:, :LANE].reshape(cw, orders, m1, LANE)
    im_ref[...] = s[:, LANE:].reshape(cw, orders, m1, LANE)


def hyena_filter_spectrum(taps, bsz, cw):
    orders, m1, c, _ = taps.shape
    mats, tabs, half, w1f = _fft_constants(bsz, m1 // 2)
    assert half == orders * m1
    consts = [w1f, mats[1], tabs[0], tabs[1]]
    out = jax.ShapeDtypeStruct((c, orders, m1, LANE), jnp.float32)
    ospec = pl.BlockSpec((cw, orders, m1, LANE), lambda j: (j, 0, 0, 0))
    return pl.pallas_call(
        functools.partial(_filtspec_kernel, half=half, orders=orders),
        out_shape=(out, out),
        grid=(c // cw,),
        in_specs=[pl.BlockSpec((orders * m1, cw * LANE), lambda j: (0, j))] +
                 [pl.BlockSpec(a.shape, lambda j: (0, 0)) for a in consts],
        out_specs=(ospec, ospec),
        compiler_params=_cparams("parallel"),
        name="hyena_filter_spectrum",
    )(taps.reshape(orders * m1, c * LANE), *consts)


def _dense_dft_constants(seq):
    n = 2 * seq
    t = np.arange(seq)[:, None]
    f = np.arange(n)[None, :]
    ps = 2.0 * np.pi * t * f / n
    fwd = np.block([[np.cos(ps), -np.sin(ps)], [np.sin(ps), np.cos(ps)]])
    inv = fwd.T / n
    tk = np.arange(n)[:, None]
    pk = 2.0 * np.pi * tk * f / n
    fk = np.concatenate([np.cos(pk), -np.sin(pk)], axis=1)
    return [jnp.asarray(m, jnp.bfloat16) for m in (fwd, inv, fk)]


def _ctxspec_kernel(k_ref, fk_ref, re_ref, im_ref):
    orders, slabs = k_ref.shape[0], k_ref.shape[1]
    n = slabs * LANE
    for o in range(orders):
        x = jnp.concatenate([k_ref[o, s] for s in range(slabs)], axis=1)
        x = x / (jnp.sum(jnp.abs(x), axis=1, keepdims=True) + EPS)
        kf = _mxu(x, fk_ref[...])
        re_ref[o] = kf[:, :n]
        im_ref[o] = kf[:, n:]


def ctx_filter_spectrum(taps, cw):
    orders, slabs, c, _ = taps.shape
    n = slabs * LANE
    fk = _dense_dft_constants(n // 2)[2]
    out = jax.ShapeDtypeStruct((orders, c, n), jnp.float32)
    ospec = pl.BlockSpec((orders, cw, n), lambda j: (0, j, 0))
    return pl.pallas_call(
        _ctxspec_kernel,
        out_shape=(out, out),
        grid=(c // cw,),
        in_specs=[pl.BlockSpec((orders, slabs, cw, LANE), lambda j: (0, 0, j, 0)),
                  pl.BlockSpec(fk.shape, lambda j: (0, 0))],
        out_specs=(ospec, ospec),
        compiler_params=_cparams("parallel"),
        name="ctx_filter_spectrum",
    )(taps, fk)


def _ctxconv_kernel(z_ref, g_ref, kre_ref, kim_ref, b_ref, fwd_ref, inv_ref, o_ref):
    bsz, slabs, cw = z_ref.shape[0], z_ref.shape[1], z_ref.shape[2]
    seq = slabs * LANE
    zs = [jnp.concatenate([z_ref[b, s] for s in range(slabs)], axis=1) for b in range(bsz)]
    lhs = jnp.concatenate([jnp.concatenate([zs[2 * p], zs[2 * p + 1]], axis=1) for p in range(bsz // 2)], axis=0)
    s = _mxu(lhs, fwd_ref[...])
    n = 2 * seq
    sr, si = s[:, :n], s[:, n:]
    kr = jnp.concatenate([kre_ref[0]] * (bsz // 2), axis=0)
    ki = jnp.concatenate([kim_ref[0]] * (bsz // 2), axis=0)
    y = _mxu(jnp.concatenate([sr * kr - si * ki, sr * ki + si * kr], axis=1), inv_ref[...])
    bias = jnp.concatenate([b_ref[...]] * slabs, axis=1)
    for b in range(bsz):
        p, m = b // 2, b % 2
        yb = y[p * cw:(p + 1) * cw, m * seq:(m + 1) * seq]
        gate = jnp.concatenate([g_ref[b, s] for s in range(slabs)], axis=1)
        res = gate * (yb + zs[b] * bias)
        for sl in range(slabs):
            o_ref[b, sl] = res[:, sl * LANE:(sl + 1) * LANE]


def ctx_hyena_conv(ut, z_src, z_col, g_col, kre, kim, order, bias, cw):
    bsz, slabs, _, _ = ut.shape
    c = bias.shape[0]
    n = 2 * slabs * LANE
    fwd, inv, _ = _dense_dft_constants(slabs * LANE)
    zb, gb = z_col // cw, g_col // cw
    bb = jnp.broadcast_to(bias[:, None], (c, LANE))
    kspec = pl.BlockSpec((1, cw, n), lambda j: (order, j, 0))
    return pl.pallas_call(
        _ctxconv_kernel,
        out_shape=jax.ShapeDtypeStruct((bsz, slabs, c, LANE), jnp.float32),
        grid=(c // cw,),
        in_specs=[
            pl.BlockSpec((bsz, slabs, cw, LANE), lambda j: (0, 0, zb + j, 0)),
            pl.BlockSpec((bsz, slabs, cw, LANE), lambda j: (0, 0, gb + j, 0)),
            kspec, kspec,
            pl.BlockSpec((cw, LANE), lambda j: (j, 0)),
            pl.BlockSpec(fwd.shape, lambda j: (0, 0)),
            pl.BlockSpec(inv.shape, lambda j: (0, 0)),
        ],
        out_specs=pl.BlockSpec((bsz, slabs, cw, LANE), lambda j: (0, 0, j, 0)),
        compiler_params=_cparams("parallel"),
        name="ctx_hyena_conv",
    )(z_src, ut, kre, kim, bb, fwd, inv)


def _mlp_kernel(x_ref, nw_ref, sc_ref, sh_ref, g_ref, w1_ref, w2_ref, fw_ref, o_ref, h_ref, *, final_norm):
    f = pl.program_id(2)

    @pl.when(f == 0)
    def _():
        _normmod_store(h_ref, x_ref, nw_ref[...], sc_ref[0], sh_ref[0])
        o_ref[...] = jnp.zeros_like(o_ref)

    a = jnp.dot(h_ref[...], w1_ref[...], preferred_element_type=jnp.float32)
    a = jnp.square(jnp.maximum(a, 0.0)).astype(jnp.bfloat16)
    o_ref[0] += jnp.dot(a, w2_ref[...], preferred_element_type=jnp.float32)

    @pl.when(f == pl.num_programs(2) - 1)
    def _():
        y = x_ref[0] + g_ref[0] * o_ref[0]
        if final_norm:
            ms = jnp.mean(y * y, axis=-1, keepdims=True)
            y = y * lax.rsqrt(ms + EPS) * fw_ref[...]
        o_ref[0] = y


def mlp_residual(x, nw, sc, sh, g, w1, w2, layer, fw, tm, tf, final_norm):
    b, l, d = x.shape
    dff = w1.shape[2]
    vec = pl.BlockSpec((1, 1, d), lambda i, j, k: (i, 0, 0))
    row = pl.BlockSpec((1, d), lambda i, j, k: (0, 0))
    return pl.pallas_call(
        functools.partial(_mlp_kernel, final_norm=final_norm),
        out_shape=jax.ShapeDtypeStruct((b, l, d), jnp.float32),
        grid=(b, l // tm, dff // tf),
        in_specs=[
            pl.BlockSpec((1, tm, d), lambda i, j, k: (i, j, 0)),
            row, vec, vec, vec,
            pl.BlockSpec((None, d, tf), lambda i, j, k: (layer, 0, k)),
            pl.BlockSpec((None, tf, d), lambda i, j, k: (layer, k, 0)),
            row,
        ],
        out_specs=pl.BlockSpec((1, tm, d), lambda i, j, k: (i, j, 0)),
        scratch_shapes=[pltpu.VMEM((tm, d), jnp.bfloat16)],
        compiler_params=_cparams("parallel", "parallel", "arbitrary"),
        name="mlp_residual",
    )(x, nw.reshape(1, d), sc.reshape(b, 1, d), sh.reshape(b, 1, d), g.reshape(b, 1, d), w1, w2,
      fw.reshape(1, d))


def _split3(v):
    f32, bf16 = jnp.float32, jnp.bfloat16
    hi = v.astype(bf16)
    r1 = v - hi.astype(f32)
    mid = r1.astype(bf16)
    lo = (r1 - mid.astype(f32)).astype(bf16)
    return jnp.concatenate([hi, mid, lo], axis=1)


def _lane_repeat(v, rep):
    h = v.shape[1]
    row = lax.broadcasted_iota(jnp.int32, (3 * h, h * rep), 0) % h
    col = lax.broadcasted_iota(jnp.int32, (3 * h, h * rep), 1) // rep
    e = (row == col).astype(jnp.bfloat16)
    return jnp.dot(_split3(v), e, preferred_element_type=jnp.float32)


def _silu(v):
    h = 0.5 * v
    return h + h * jnp.tanh(h)


def _softplus(v):
    return jnp.maximum(v, 0.0) + jnp.log1p(jnp.exp(-jnp.abs(v)))


def _token_conv3(u, w, bias, period):
    t, c = u.shape
    pos = lax.broadcasted_iota(jnp.int32, (t, c), 0) % period
    up = jnp.where(pos == 0, 0.0, pltpu.roll(u, 1, 0))
    dn = jnp.where(pos == period - 1, 0.0, pltpu.roll(u, t - 1, 0))
    return bias + w[0:1] * up + w[1:2] * u + w[2:3] * dn


def _ssd_chunk(xs, bm, cm, dtr, s_ref, a_row, dtb_row, reverse, heads, hpg):
    f32, bf16 = jnp.float32, jnp.bfloat16
    q = xs.shape[0]
    p = SSD_HEADDIM
    n = SSD_STATE
    gw = hpg * p
    dt = _softplus(dtr + dtb_row)
    a = dt * (a_row * LOG2E)
    ri = lax.broadcasted_iota(jnp.int32, (q, q), 0)
    ci = lax.broadcasted_iota(jnp.int32, (q, q), 1)
    keep = (ci >= ri) if reverse else (ci <= ri)
    a3 = jnp.dot(keep.astype(bf16), _split3(a), preferred_element_type=f32)
    cs = a3[:, :heads] + a3[:, heads:2 * heads] + a3[:, 2 * heads:]
    dt_rep = _lane_repeat(dt, p)
    cs_rep = _lane_repeat(cs, p)
    cs_wide = _lane_repeat(cs, q)
    end = 0 if reverse else q - 1
    cs_end = cs_rep[end:end + 1]
    xdt = xs * dt_rep
    xw = (xdt * jnp.exp2(cs_end - cs_rep)).astype(bf16)
    ecs = jnp.exp2(cs_rep)
    chunk_decay = jnp.exp2(cs_end)
    xdt_b = xdt.astype(bf16)
    lane = lax.broadcasted_iota(jnp.int32, (q, 2 * p), 1)
    ys = []
    for g in range(SSD_GROUPS):
        bg = bm[:, g * n:(g + 1) * n].astype(bf16)
        cg = cm[:, g * n:(g + 1) * n].astype(bf16)
        cb = lax.dot_general(cg, bg, (((1,), (1,)), ((), ())), preferred_element_type=f32)
        s_old = s_ref[0, g]
        y_off = jnp.dot(cg, s_old.astype(bf16), preferred_element_type=f32) * ecs[:, g * gw:(g + 1) * gw]
        for pr in range(hpg // 2):
            xpair = xdt_b[:, g * gw + pr * 2 * p:g * gw + (pr + 1) * 2 * p]
            acc = None
            for k in range(2):
                h = g * hpg + pr * 2 + k
                csr = cs_wide[:, h * q:(h + 1) * q]
                seg = csr - csr.T
                gmat = (cb * jnp.where(keep, jnp.exp2(seg), 0.0)).astype(bf16)
                xh = jnp.where((lane // p) == k, xpair, jnp.zeros_like(xpair))
                part = jnp.dot(gmat, xh, preferred_element_type=f32)
                acc = part if acc is None else acc + part
            lo = pr * 2 * p
            ys.append(acc + y_off[:, lo:lo + 2 * p])
        upd = jnp.dot(bg.T, xw[:, g * gw:(g + 1) * gw], preferred_element_type=f32)
        s_ref[0, g] = s_old * chunk_decay[:, g * gw:(g + 1) * gw] + upd
    return jnp.concatenate(ys, axis=1)


def _ssd_kernel(z_ref, xs_ref, bc_ref, dt_ref, cwx_ref, cbx_ref, cwb_ref, cbb_ref, dtb_ref, alog_ref, drep_ref,
                nw_ref, initf_ref, initb_ref, y_ref, sf_ref, sb_ref, yb_ref, xc_ref, bcc_ref, *, nb, tb, period,
                heads, hpg):
    j = pl.program_id(1)
    q = SSD_CHUNK
    n = SSD_STATE
    gn = SSD_GROUPS * n
    nchunk = tb // q
    ssd_w = heads * SSD_HEADDIM

    @pl.when(j == 0)
    def _():
        sb_ref[...] = initb_ref[...]

    @pl.when(j == nb)
    def _():
        sf_ref[...] = initf_ref[...]

    a_all = -jnp.exp(alog_ref[...])

    def run(reverse):
        d = 1 if reverse else 0
        blk = (nb - 1 - j) if reverse else (j - nb)
        order = range(nchunk - 1, -1, -1) if reverse else range(nchunk)
        rows = pl.ds(pl.multiple_of(blk * tb, tb), tb)
        if reverse:
            xs_all = _silu(_token_conv3(xs_ref[0], cwx_ref[...], cbx_ref[...], period))
            bc_all = _silu(_token_conv3(bc_ref[0], cwb_ref[...], cbb_ref[...], period)).astype(bcc_ref.dtype)
            xc_ref[rows, :] = xs_all
            bcc_ref[rows, :] = bc_all
        else:
            xs_all = xc_ref[rows, :]
            bc_all = bcc_ref[rows, :]
        for ci in order:
            sl = slice(ci * q, (ci + 1) * q)
            y = _ssd_chunk(xs_all[sl], bc_all[sl, :gn], bc_all[sl, gn:], dt_ref[0, sl, d * heads:(d + 1) * heads],
                           sb_ref if reverse else sf_ref, a_all[d:d + 1], dtb_ref[d:d + 1], reverse, heads, hpg)
            row0 = pl.multiple_of(blk * tb + ci * q, q)
            if reverse:
                yb_ref[pl.ds(row0, q), :] = y
            else:
                y = y + yb_ref[pl.ds(row0, q), :] + drep_ref[...] * xs_all[sl]
                y = y * _silu(z_ref[0, sl, :])
                gw = ssd_w // SSD_GROUPS
                outs = []
                for g in range(SSD_GROUPS):
                    yg = y[:, g * gw:(g + 1) * gw]
                    ms = jnp.mean(yg * yg, axis=-1, keepdims=True)
                    outs.append(yg * lax.rsqrt(ms + EPS) * nw_ref[:, g * gw:(g + 1) * gw])
                y_ref[0, sl, :] = jnp.concatenate(outs, axis=1)

    @pl.when(j < nb)
    def _():
        run(True)

    @pl.when(j >= nb)
    def _():
        run(False)


def ssd_mixer(srcs, cols, conv_w_x, conv_b_x, conv_w_bc, conv_b_bc, dt_bias, a_log, ssd_d, norm_w, init_f, init_b,
              period, tb):
    b, l, _ = srcs[0].shape
    heads = dt_bias.shape[1]
    hpg = heads // SSD_GROUPS
    ssd_w = heads * SSD_HEADDIM
    gn = SSD_GROUPS * SSD_STATE
    nb = l // tb
    oz, ox, obc, odt = cols
    f32 = jnp.float32

    def tok(width, off):
        blk_idx = off // width
        return pl.BlockSpec((1, tb, width),
                            lambda i, j: (i, jnp.where(j < nb, nb - 1 - j, j - nb), blk_idx))

    def tok_fwd(width, off):
        blk_idx = off // width
        return pl.BlockSpec((1, tb, width), lambda i, j: (i, jnp.where(j < nb, 0, j - nb), blk_idx))

    def tok_rev(width, off):
        blk_idx = off // width
        return pl.BlockSpec((1, tb, width), lambda i, j: (i, jnp.maximum(nb - 1 - j, 0), blk_idx))

    def whole(shape):
        return pl.BlockSpec(shape, lambda i, j: (0,) * len(shape))

    st_shape = (b, SSD_GROUPS, SSD_STATE, hpg * SSD_HEADDIM)
    st_spec = pl.BlockSpec((1,) + st_shape[1:], lambda i, j: (i, 0, 0, 0))
    drep = jnp.repeat(ssd_d, SSD_HEADDIM).reshape(1, ssd_w)
    kern = functools.partial(_ssd_kernel, nb=nb, tb=tb, period=period, heads=heads, hpg=hpg)
    return pl.pallas_call(
        kern,
        out_shape=(jax.ShapeDtypeStruct((b, l, ssd_w), f32), jax.ShapeDtypeStruct(st_shape, f32),
                   jax.ShapeDtypeStruct(st_shape, f32)),
        grid=(b, 2 * nb),
        in_specs=[
            tok_fwd(ssd_w, oz), tok_rev(ssd_w, ox), tok_rev(2 * gn, obc), tok(LANE, odt),
            whole((3, ssd_w)), whole((1, ssd_w)), whole((3, 2 * gn)), whole((1, 2 * gn)),
            whole((2, heads)), whole((2, heads)), whole((1, ssd_w)), whole((1, ssd_w)),
            st_spec, st_spec,
        ],
        out_specs=(pl.BlockSpec((1, tb, ssd_w), lambda i, j: (i, jnp.where(j < nb, 0, j - nb), 0)),
                   st_spec, st_spec),
        scratch_shapes=[pltpu.VMEM((l, ssd_w), f32), pltpu.VMEM((l, ssd_w), f32),
                        pltpu.VMEM((l, 2 * gn), jnp.bfloat16)],
        compiler_params=_cparams("parallel", "arbitrary"),
        name="ssd_mixer",
    )(*srcs, conv_w_x, conv_b_x.reshape(1, -1), conv_w_bc, conv_b_bc.reshape(1, -1), dt_bias, a_log, drep,
      norm_w.reshape(1, ssd_w), init_f, init_b)


def _pick(l, pref):
    return pref if l % pref == 0 else l


def kernel(x, c, ctx, c_ctx, w_ada, b_ada, norm1_w, w_in, hy_conv_w, hy_conv_b, filt_w1, filt_b1, filt_freq,
           filt_w2, filt_b2, filt_w3, hy_bias, hy_norm_w, ssd_conv_w, ssd_conv_b, dt_bias, a_log, ssd_d,
           ssd_norm_w, w_out, norm2_w, w_mlp1, w_mlp2, final_norm_w):
    depth = w_in.shape[0]
    bsz, seq, d = x.shape
    hy_w = hy_norm_w.shape[1]
    hy_proj = hy_conv_w.shape[2]
    ssd_w = ssd_norm_w.shape[1]
    ssd_xbc = ssd_conv_w.shape[2]
    heads = ssd_w // SSD_HEADDIM
    hpg = heads // SSD_GROUPS
    ssd_dt = 2 * heads
    o0, o1 = hy_proj, hy_proj + ssd_xbc
    o2 = o1 + ssd_dt
    dt_pad = (-ssd_dt) % LANE
    bf16 = jnp.bfloat16
    ssd_cols = (0, o0, o0 + ssd_w, ssd_w)
    tn = o1 // 3

    n1 = seq // LANE
    n1c = ctx.shape[1] // LANE
    zero_state = jnp.zeros((bsz, SSD_GROUPS, SSD_STATE, hpg * SSD_HEADDIM), jnp.float32)

    rows = 8
    c_all = jnp.zeros((rows, d), jnp.float32).at[:bsz].set(c).at[bsz].set(c_ctx)
    mod_all = ada_mod(c_all, w_ada, b_ada)

    w_a_all, w_b_all = cast_in_proj(w_in, o1, o2, w_in.shape[2], ssd_w + ssd_dt + dt_pad)
    w_out_b, w1_b, w2_b = w_out.astype(bf16), w_mlp1.astype(bf16), w_mlp2.astype(bf16)

    h_ctx = ctx
    tm_l = _pick(seq, 512)
    tm_c = _pick(ctx.shape[1], 256)
    for i in range(depth):
        lp = dict(hy_conv_w=hy_conv_w[i], hy_conv_b=hy_conv_b[i], filt_w1=filt_w1[i], filt_b1=filt_b1[i],
                  filt_freq=filt_freq[i], filt_w2=filt_w2[i], filt_b2=filt_b2[i], filt_w3=filt_w3[i],
                  hy_bias=hy_bias[i], hy_norm_w=hy_norm_w[i], ssd_conv_w=ssd_conv_w[i],
                  ssd_conv_b=ssd_conv_b[i], dt_bias=dt_bias[i], a_log=a_log[i], ssd_d=ssd_d[i],
                  ssd_norm_w=ssd_norm_w[i])
        mod = mod_all[i, :bsz].reshape(bsz, N_MOD, d)
        mod_c = jnp.broadcast_to(mod_all[i, bsz].reshape(1, N_MOD, d), (bsz, N_MOD, d))
        def ssd(pa, pb, init_f, init_b, period):
            cw, cb = lp['ssd_conv_w'], lp['ssd_conv_b']
            return ssd_mixer((pb, pa, pa, pb), ssd_cols, cw[:, :ssd_w], cb[:ssd_w], cw[:, ssd_w:], cb[ssd_w:],
                             lp['dt_bias'], lp['a_log'], lp['ssd_d'], lp['ssd_norm_w'], init_f, init_b, period,
                             _pick(pa.shape[1], 256))

        def filt_taps(length):
            return hyena_filter_taps(length, lp['filt_w1'], lp['filt_b1'], lp['filt_freq'], lp['filt_w2'],
                                     lp['filt_b2'], lp['filt_w3'], hy_w)

        pca, pcb = normmod_matmul(h_ctx, norm1_w[i], mod_c[:, 1], mod_c[:, 0], w_a_all, w_b_all, i, tm_c, tn)
        y_ssd, s_f, s_b = ssd(pca, pcb, zero_state, zero_state, pca.shape[1])
        if i < depth - 1:
            utc = hyena_pre(pca, 0, lp['hy_conv_w'], lp['hy_conv_b'], pca.shape[1], pca.shape[1], 512)
            kre, kim = ctx_filter_spectrum(filt_taps(pca.shape[1]), 128)
            zt = utc
            for o in range(HY_ORDER):
                zt = ctx_hyena_conv(utc, zt, 0, (o + 1) * hy_w, kre, kim, o, lp['hy_bias'][o], 128)
            h_ctx = proj_residual(zt, y_ssd, lp['hy_norm_w'], w_out_b, i, h_ctx, mod_c[:, 2], tm_c, HY_GROUPS)
            h_ctx = mlp_residual(h_ctx, norm2_w[i], mod_c[:, 4], mod_c[:, 3], mod_c[:, 5], w1_b, w2_b, i,
                                 final_norm_w, tm_c, 1024, False)
        pla, plb = normmod_matmul(x, norm1_w[i], mod[:, 1], mod[:, 0], w_a_all, w_b_all, i, tm_l, tn)
        ut = hyena_pre_rows(pla, 0, lp['hy_conv_w'], lp['hy_conv_b'], GRID_W, LANE)
        kre, kim = hyena_filter_spectrum(filt_taps(seq), bsz, 16)
        zt = ut
        for o in range(HY_ORDER):
            zt = hyena_conv(ut, zt, 0, (o + 1) * hy_w, kre, kim, o, lp['hy_bias'][o], bsz, 16, o == HY_ORDER - 1)
        zh_t = zt.reshape(bsz, n1, hy_w, LANE)
        y_ssd, _, _ = ssd(pla, plb, s_f, s_b, GRID_W)
        x = proj_residual(zh_t, y_ssd, lp['hy_norm_w'], w_out_b, i, x, mod[:, 2], tm_l, HY_GROUPS)
        x = mlp_residual(x, norm2_w[i], mod[:, 4], mod[:, 3], mod[:, 5], w1_b, w2_b, i, final_norm_w, tm_l, 1024,
                         i == depth - 1)
    return x
```

```python
import functools
import math

import numpy as np
import jax
import jax.numpy as jnp
from jax import lax
from jax.experimental import pallas as pl
from jax.experimental.pallas import tpu as pltpu

EPS = 1e-6
N_MOD = 6
GRID_W = 64
HY_GROUPS = 8
HY_ORDER = 2
HY_POS_EMB = 33
HY_TARGET = 1e-2
HY_FAST = 0.3
HY_SLOW = 1.5
SSD_HEADDIM = 64
SSD_GROUPS = 2
SSD_STATE = 128
SSD_CHUNK = 128

LANE = 128
LOG2E = 1.4426950408889634
VMEM_LIMIT = 56 * 1024 * 1024


def _cparams(*sem):
    return pltpu.CompilerParams(dimension_semantics=sem, vmem_limit_bytes=VMEM_LIMIT)


def _ada_kernel(c_ref, w_ref, b_ref, o_ref):
    c = c_ref[...]
    a = (c * jax.nn.sigmoid(c)).astype(jnp.bfloat16)
    w = w_ref[0].astype(jnp.bfloat16)
    o_ref[0] = jnp.dot(a, w, preferred_element_type=jnp.float32) + b_ref[0]


def ada_mod(c_all, w_ada, b_ada, tn=1024):
    depth, d, n = w_ada.shape
    r = c_all.shape[0]
    return pl.pallas_call(
        _ada_kernel,
        out_shape=jax.ShapeDtypeStruct((depth, r, n), jnp.float32),
        grid=(depth, n // tn),
        in_specs=[
            pl.BlockSpec((r, d), lambda i, j: (0, 0)),
            pl.BlockSpec((1, d, tn), lambda i, j: (i, 0, j)),
            pl.BlockSpec((1, 1, tn), lambda i, j: (i, 0, j)),
        ],
        out_specs=pl.BlockSpec((1, r, tn), lambda i, j: (i, 0, j)),
        compiler_params=_cparams("parallel", "parallel"),
        name="ada_mod",
    )(c_all, w_ada, b_ada.reshape(depth, 1, n))


def _wcast_kernel(w_ref, a_ref, b_ref, *, na, c0, c1):
    bf16 = jnp.bfloat16
    w = w_ref[0]
    a_ref[0] = w[:, :na].astype(bf16)
    nb = b_ref.shape[2]
    pad = nb - (c1 - c0) - (c0 - na)
    b_ref[0] = jnp.concatenate([w[:, c0:c1], w[:, na:c0], jnp.zeros((w.shape[0], pad), w.dtype)], axis=1).astype(bf16)


def cast_in_proj(w_in, na, c0, c1, nb, tr=256):
    depth, d, n = w_in.shape
    return pl.pallas_call(
        functools.partial(_wcast_kernel, na=na, c0=c0, c1=c1),
        out_shape=(jax.ShapeDtypeStruct((depth, d, na), jnp.bfloat16),
                   jax.ShapeDtypeStruct((depth, d, nb), jnp.bfloat16)),
        grid=(depth, d // tr),
        in_specs=[pl.BlockSpec((1, tr, n), lambda i, j: (i, j, 0))],
        out_specs=(pl.BlockSpec((1, tr, na), lambda i, j: (i, j, 0)),
                   pl.BlockSpec((1, tr, nb), lambda i, j: (i, j, 0))),
        compiler_params=_cparams("parallel", "parallel"),
        name="cast_in_proj",
    )(w_in)


NORM_ROWS = 16


def _normmod_store(h_ref, x_ref, nw, sc, sh):
    w = nw * (1.0 + sc)

    def body(i, carry):
        r0 = pl.multiple_of(i * NORM_ROWS, NORM_ROWS)
        xb = x_ref[0, pl.ds(r0, NORM_ROWS), :]
        ms = jnp.mean(xb * xb, axis=-1, keepdims=True)
        h_ref[pl.ds(r0, NORM_ROWS), :] = ((xb * lax.rsqrt(ms + EPS)) * w + sh).astype(h_ref.dtype)
        return carry

    lax.fori_loop(0, h_ref.shape[0] // NORM_ROWS, body, 0, unroll=8)


def _normmod_kernel(x_ref, nw_ref, sc_ref, sh_ref, h_ref):
    _normmod_store(h_ref.at[0], x_ref, nw_ref[...], sc_ref[0], sh_ref[0])


def _proj_kernel(h_ref, w_ref, o_ref):
    o_ref[0] = jnp.dot(h_ref[0], w_ref[...], preferred_element_type=jnp.float32)


def _proj_matmul(h, w, layer, tm, tn):
    b, l, d = h.shape
    n = w.shape[2]
    return pl.pallas_call(
        _proj_kernel,
        out_shape=jax.ShapeDtypeStruct((b, l, n), jnp.float32),
        grid=(n // tn, b, l // tm),
        in_specs=[
            pl.BlockSpec((1, tm, d), lambda k, i, j: (i, j, 0)),
            pl.BlockSpec((None, d, tn), lambda k, i, j: (layer, 0, k)),
        ],
        out_specs=pl.BlockSpec((1, tm, tn), lambda k, i, j: (i, j, k)),
        compiler_params=_cparams("parallel", "parallel", "parallel"),
        name="in_proj",
    )(h, w)


def normmod_matmul(x, nw, sc, sh, wa, wb, layer, tm, tn):
    b, l, d = x.shape
    tn_norm = min(tm, 512)
    h = pl.pallas_call(
        _normmod_kernel,
        out_shape=jax.ShapeDtypeStruct((b, l, d), jnp.bfloat16),
        grid=(b, l // tn_norm),
        in_specs=[
            pl.BlockSpec((1, tn_norm, d), lambda i, j: (i, j, 0)),
            pl.BlockSpec((1, d), lambda i, j: (0, 0)),
            pl.BlockSpec((1, 1, d), lambda i, j: (i, 0, 0)),
            pl.BlockSpec((1, 1, d), lambda i, j: (i, 0, 0)),
        ],
        out_specs=pl.BlockSpec((1, tn_norm, d), lambda i, j: (i, j, 0)),
        compiler_params=_cparams("parallel", "parallel"),
        name="normmod",
    )(x, nw.reshape(1, d), sc.reshape(b, 1, d), sh.reshape(b, 1, d))
    return _proj_matmul(h, wa, layer, tm, tn), _proj_matmul(h, wb, layer, tm, wb.shape[2])


def _proj_res_kernel(zh_ref, ys_ref, nwh_ref, wh_ref, ws_ref, x_ref, g_ref, o_ref, *, groups):
    bf16 = jnp.bfloat16
    slabs, hy_w = zh_ref.shape[1], zh_ref.shape[2]
    gs = hy_w // groups
    rows = []
    for s in range(slabs):
        cols = []
        for gi in range(groups):
            zg = zh_ref[0, s, gi * gs:(gi + 1) * gs, :]
            ms = jnp.mean(zg * zg, axis=0, keepdims=True)
            cols.append((zg * lax.rsqrt(ms + EPS)).T)
        rows.append(jnp.concatenate(cols, axis=1))
    yh = (jnp.concatenate(rows, axis=0) * nwh_ref[...]).astype(bf16)
    acc = jnp.dot(yh, wh_ref[...], preferred_element_type=jnp.float32)
    acc += jnp.dot(ys_ref[0].astype(bf16), ws_ref[...], preferred_element_type=jnp.float32)
    o_ref[0] = x_ref[0] + g_ref[0] * acc


def proj_residual(zh_t, y_ssd, hy_norm_w, w_out, layer, x, g, tm, groups):
    b, n1, hy_w, _ = zh_t.shape
    l, ssd_w = y_ssd.shape[1], y_ssd.shape[2]
    n = w_out.shape[2]
    assert hy_w % ssd_w == 0
    return pl.pallas_call(
        functools.partial(_proj_res_kernel, groups=groups),
        out_shape=jax.ShapeDtypeStruct((b, l, n), jnp.float32),
        grid=(b, l // tm),
        in_specs=[
            pl.BlockSpec((1, tm // LANE, hy_w, LANE), lambda i, j: (i, j, 0, 0)),
            pl.BlockSpec((1, tm, ssd_w), lambda i, j: (i, j, 0)),
            pl.BlockSpec((1, hy_w), lambda i, j: (0, 0)),
            pl.BlockSpec((None, hy_w, n), lambda i, j: (layer, 0, 0)),
            pl.BlockSpec((None, ssd_w, n), lambda i, j: (layer, hy_w // ssd_w, 0)),
            pl.BlockSpec((1, tm, n), lambda i, j: (i, j, 0)),
            pl.BlockSpec((1, 1, n), lambda i, j: (i, 0, 0)),
        ],
        out_specs=pl.BlockSpec((1, tm, n), lambda i, j: (i, j, 0)),
        compiler_params=_cparams("parallel", "parallel"),
        name="proj_residual",
    )(zh_t, y_ssd, hy_norm_w.reshape(1, hy_w), w_out, w_out, x, g.reshape(b, 1, n))


def _hypre_kernel(u_ref, w_ref, b_ref, o_ref, *, period):
    u = _token_conv3(u_ref[0], w_ref[...], b_ref[...], period)
    tb, cw = u.shape
    for s in range(tb // LANE):
        for j in range(cw // LANE):
            o_ref[0, s, j * LANE:(j + 1) * LANE, :] = u[s * LANE:(s + 1) * LANE, j * LANE:(j + 1) * LANE].T


def hyena_pre(p, col0, conv_w, conv_b, period, tb, cw):
    b, l, _ = p.shape
    c = conv_w.shape[1]
    blk0 = col0 // cw
    return pl.pallas_call(
        functools.partial(_hypre_kernel, period=period),
        out_shape=jax.ShapeDtypeStruct((b, l // LANE, c, LANE), jnp.float32),
        grid=(b, l // tb, c // cw),
        in_specs=[
            pl.BlockSpec((1, tb, cw), lambda i, j, k: (i, j, blk0 + k)),
            pl.BlockSpec((3, cw), lambda i, j, k: (0, k)),
            pl.BlockSpec((1, cw), lambda i, j, k: (0, k)),
        ],
        out_specs=pl.BlockSpec((1, tb // LANE, cw, LANE), lambda i, j, k: (i, j, k, 0)),
        compiler_params=_cparams("parallel", "parallel", "parallel"),
        name="hyena_pre",
    )(p, conv_w, conv_b.reshape(1, c))


def _hypre_rows_kernel(u_ref, w_ref, b_ref, o_ref, s_ref, *, period):
    l, cw = u_ref.shape[1], u_ref.shape[2]
    n1 = l // LANE
    assert LANE % period == 0
    w, bias = w_ref[...], b_ref[...]

    for s in range(n1):
        u = _token_conv3(u_ref[0, s * LANE:(s + 1) * LANE, :], w, bias, period)
        for j in range(cw // LANE):
            s_ref[s * cw + j * LANE:s * cw + (j + 1) * LANE, :] = u[:, j * LANE:(j + 1) * LANE].T
    for c in range(cw):
        o_ref[:, c * LANE:(c + 1) * LANE] = s_ref[pl.ds(c, n1, stride=cw), :]


def hyena_pre_rows(p, col0, conv_w, conv_b, period, cw):
    b, l, _ = p.shape
    c = conv_w.shape[1]
    n1 = l // LANE
    blk0 = col0 // cw
    return pl.pallas_call(
        functools.partial(_hypre_rows_kernel, period=period),
        out_shape=jax.ShapeDtypeStruct((b * n1, c * LANE), jnp.float32),
        grid=(b, c // cw),
        in_specs=[
            pl.BlockSpec((1, l, cw), lambda i, k: (i, 0, blk0 + k)),
            pl.BlockSpec((3, cw), lambda i, k: (0, k)),
            pl.BlockSpec((1, cw), lambda i, k: (0, k)),
        ],
        out_specs=pl.BlockSpec((n1, cw * LANE), lambda i, k: (i, k)),
        scratch_shapes=[pltpu.VMEM((n1 * cw, LANE), jnp.float32)],
        compiler_params=_cparams("parallel", "parallel"),
        name="hyena_pre_rows",
    )(p, conv_w, conv_b.reshape(1, c))


def _fft_constants(bsz, n1):
    m1 = 2 * n1
    n = m1 * LANE
    pairs = bsz // 2
    half = pairs * m1
    r = bsz * n1
    t1 = np.arange(n1)[:, None]
    f1 = np.arange(m1)[None, :]
    th = 2.0 * np.pi * t1 * f1 / m1
    w1 = np.zeros((r, 2 * half))
    for pr in range(pairs):
        re = slice(pr * m1, (pr + 1) * m1)
        im = slice(half + pr * m1, half + (pr + 1) * m1)
        ra = slice((2 * pr) * n1, (2 * pr + 1) * n1)
        rb = slice((2 * pr + 1) * n1, (2 * pr + 2) * n1)
        w1[ra, re], w1[ra, im] = np.cos(th), -np.sin(th)
        w1[rb, re], w1[rb, im] = np.sin(th), np.cos(th)
    w4 = w1.T / n
    t2 = np.arange(LANE)[:, None]
    ps = 2.0 * np.pi * t2 * np.arange(LANE)[None, :] / LANE
    w2 = np.block([[np.cos(ps), -np.sin(ps)], [np.sin(ps), np.cos(ps)]])
    ph = 2.0 * np.pi * t2 * f1 / n
    tc = np.tile(np.cos(ph), (1, pairs))
    ts = np.tile(np.sin(ph), (1, pairs))
    orders = half // m1
    tf = 2.0 * np.pi * np.arange(m1)[:, None] * f1 / m1
    w1f = np.zeros((orders * m1, 2 * half))
    for o in range(orders):
        w1f[o * m1:(o + 1) * m1, o * m1:(o + 1) * m1] = np.cos(tf)
        w1f[o * m1:(o + 1) * m1, half + o * m1:half + (o + 1) * m1] = -np.sin(tf)
    mats = [jnp.asarray(m, jnp.bfloat16) for m in (w1, w2, w2.T, w4)]
    tabs = [jnp.asarray(m, jnp.float32) for m in (tc, ts, tc.T, ts.T)]
    return mats, tabs, half, jnp.asarray(w1f, jnp.bfloat16)


def _mxu(a, w):
    return jnp.dot(a.astype(jnp.bfloat16), w, preferred_element_type=jnp.float32)


def _fft_forward(x0, w1, w2, tc, ts, half):
    cw = x0.shape[1] // LANE
    a1 = _mxu(x0.T, w1)
    lhs2 = []
    for c in range(cw):
        ar = a1[c * LANE:(c + 1) * LANE, :half]
        ai = a1[c * LANE:(c + 1) * LANE, half:]
        lhs2.append(jnp.concatenate([(ar * tc + ai * ts).T, (ai * tc - ar * ts).T], axis=1))
    return _mxu(jnp.concatenate(lhs2, axis=0), w2)


def _hyconv_kernel(z_ref, g_ref, kre_ref, kim_ref, brep_ref, w1_ref, w2_ref, w2i_ref, w4_ref, tc_ref, ts_ref,
                   tct_ref, tst_ref, o_ref, *, half):
    x0 = z_ref[...]
    cw = x0.shape[1] // LANE
    tct, tst = tct_ref[...], tst_ref[...]
    s = _fft_forward(x0, w1_ref[...], w2_ref[...], tc_ref[...], ts_ref[...], half)
    sr, si = s[:, :LANE], s[:, LANE:]
    m1 = kre_ref.shape[2]
    pairs = half // m1
    shape4 = (cw, pairs, m1, LANE)
    sr, si = sr.reshape(shape4), si.reshape(shape4)
    kr, ki = kre_ref[:, 0][:, None], kim_ref[:, 0][:, None]
    y2 = jnp.concatenate([(sr * kr - si * ki).reshape(cw * half, LANE),
                          (sr * ki + si * kr).reshape(cw * half, LANE)], axis=1)
    bq = _mxu(y2, w2i_ref[...])
    lhs4 = []
    for c in range(cw):
        br = bq[c * half:(c + 1) * half, :LANE]
        bi = bq[c * half:(c + 1) * half, LANE:]
        lhs4.append(jnp.concatenate([(br * tct - bi * tst).T, (bi * tct + br * tst).T], axis=1))
    lhs4 = jnp.concatenate(lhs4, axis=0)
    yt = _mxu(lhs4, w4_ref[...])
    if len(o_ref.shape) == 2:
        o_ref[...] = g_ref[...] * (yt.T + x0 * brep_ref[...])
    else:
        for c in range(cw):
            sl = slice(c * LANE, (c + 1) * LANE)
            o_ref[:, c, :] = g_ref[:, sl] * (yt[sl].T + x0[:, sl] * brep_ref[:, sl])


def hyena_conv(ut, z_src, z_col, g_col, kre, kim, order, bias, bsz, cw, channel_tiles):
    r = ut.shape[0]
    n1 = r // bsz
    c = bias.shape[0]
    mats, tabs, half, _ = _fft_constants(bsz, n1)
    blk = cw * LANE
    zb, gb = z_col // cw, g_col // cw
    brep = jnp.repeat(bias, LANE).reshape(1, c * LANE)
    kspec = pl.BlockSpec((cw, 1, 2 * n1, LANE), lambda j: (j, order, 0, 0))

    def const(a):
        return pl.BlockSpec(a.shape, lambda j: (0, 0))

    return pl.pallas_call(
        functools.partial(_hyconv_kernel, half=half),
        out_shape=jax.ShapeDtypeStruct((r, c, LANE) if channel_tiles else (r, c * LANE), jnp.float32),
        grid=(c // cw,),
        in_specs=[
            pl.BlockSpec((r, blk), lambda j: (0, zb + j)),
            pl.BlockSpec((r, blk), lambda j: (0, gb + j)),
            kspec, kspec,
            pl.BlockSpec((1, blk), lambda j: (0, j)),
        ] + [const(a) for a in mats + tabs],
        out_specs=(pl.BlockSpec((r, cw, LANE), lambda j: (0, j, 0)) if channel_tiles
                   else pl.BlockSpec((r, blk), lambda j: (0, j))),
        compiler_params=_cparams("parallel"),
        name="hyena_conv",
    )(z_src, ut, kre, kim, brep, *mats, *tabs)


_HI = lax.Precision.HIGHEST


def _filtgen_kernel(a0_ref, wc_ref, ws_ref, fb1_ref, freq_ref, fw2t_ref, fb2_ref, fw3t_ref, dl_ref, o_ref, *,
                    seq, bands):
    f32 = jnp.float32
    s = pl.program_id(0)
    t = s * LANE + lax.broadcasted_iota(jnp.int32, (1, LANE), 1)
    pos = jnp.where(t < seq, t, 2 * seq - t).astype(f32)
    tt = pos / (seq - 1.0)
    ang = (2.0 * math.pi / seq) * pos
    j = lax.broadcasted_iota(jnp.int32, (bands, 1), 0).astype(f32)
    fj = 1e-4 + j * ((bands - 1.0 - 1e-4) / (bands - 1.0))
    fa = fj * ang
    freq = freq_ref[...]
    pre = a0_ref[...] * tt + jnp.dot(wc_ref[...], jnp.cos(fa), precision=_HI, preferred_element_type=f32) \
        - jnp.dot(ws_ref[...], jnp.sin(fa), precision=_HI, preferred_element_type=f32) + fb1_ref[...]
    h1 = jnp.sin(freq * pre)
    h2 = jnp.sin(freq * (jnp.dot(fw2t_ref[...], h1, precision=_HI, preferred_element_type=f32) + fb2_ref[...]))
    orders, _, c, hid3 = fw3t_ref.shape
    h2_hi = h2.astype(jnp.bfloat16)
    h2_lo = (h2 - h2_hi.astype(f32)).astype(jnp.bfloat16)
    h = jnp.dot(fw3t_ref[:, 0].reshape(orders * c, hid3), jnp.concatenate([h2_hi, h2_lo, h2_hi], axis=0),
                preferred_element_type=f32)
    h = h * jnp.exp(-tt * dl_ref[...])
    h = jnp.where(t == seq, 0.0, h)
    o_ref[:, 0] = h.reshape(orders, c, LANE)


def hyena_filter_taps(seq, fw1, fb1, freq, fw2, fb2, fw3, hy_w):
    hid = fw2.shape[0]
    bands = (HY_POS_EMB - 1) // 2
    orders = fw3.shape[1] // (2 * hy_w)
    slabs = 2 * seq // LANE
    col = lambda v: v.reshape(-1, 1)
    fw3t = fw3.T.reshape(orders, 2, hy_w, hid)
    w_hi = fw3t.astype(jnp.bfloat16)
    w_lo = (fw3t - w_hi.astype(jnp.float32)).astype(jnp.bfloat16)
    fw3t = jnp.concatenate([w_hi, w_hi, w_lo], axis=-1)
    deltas = jnp.abs(jnp.linspace(math.log(HY_TARGET) / HY_SLOW, math.log(HY_TARGET) / HY_FAST, hy_w,
                                  dtype=jnp.float32))
    dl = jnp.tile(deltas, orders).reshape(-1, 1)

    def whole(a):
        return pl.BlockSpec(a.shape, lambda s: (0,) * a.ndim)

    args = [col(fw1[0]), fw1[1:1 + bands].T, fw1[1 + bands:].T, col(fb1), col(freq), fw2.T, col(fb2)]
    return pl.pallas_call(
        functools.partial(_filtgen_kernel, seq=seq, bands=bands),
        out_shape=jax.ShapeDtypeStruct((orders, slabs, hy_w, LANE), jnp.float32),
        grid=(slabs,),
        in_specs=[whole(a) for a in args] + [
            pl.BlockSpec((orders, 1, hy_w, 3 * hid), lambda s: (0, s // (slabs // 2), 0, 0)),
            whole(dl),
        ],
        out_specs=pl.BlockSpec((orders, 1, hy_w, LANE), lambda s: (0, s, 0, 0)),
        compiler_params=_cparams("parallel"),
        name="hyena_filter_taps",
    )(*args, fw3t, dl)


def _lane_block_abs_norm(x, orders):
    rows_per = x.shape[0] // orders
    cw = x.shape[1] // LANE
    out = []
    for o in range(orders):
        xo = x[o * rows_per:(o + 1) * rows_per]
        a = jnp.sum(jnp.abs(xo), axis=0, keepdims=True)
        inv = [jnp.broadcast_to(1.0 / (jnp.sum(a[:, c * LANE:(c + 1) * LANE], axis=1, keepdims=True) + EPS),
                                (1, LANE)) for c in range(cw)]
        out.append(xo * jnp.concatenate(inv, axis=1))
    return jnp.concatenate(out, axis=0)


def _filtspec_kernel(k_ref, w1_ref, w2_ref, tc_ref, ts_ref, re_ref, im_ref, *, half, orders):
    m1 = half // orders
    x0 = _lane_block_abs_norm(k_ref[...], orders)
    s = _fft_forward(x0, w1_ref[...], w2_ref[...], tc_ref[...], ts_ref[...], half)
    cw = x0.shape[1] // LANE
    re_ref[...] = s[:, :LANE].reshape(cw, orders, m1, LANE)
    im_ref[...] = s[:, LANE:].reshape(cw, orders, m1, LANE)


def hyena_filter_spectrum(taps, bsz, cw):
    orders, m1, c, _ = taps.shape
    mats, tabs, half, w1f = _fft_constants(bsz, m1 // 2)
    assert half == orders * m1
    consts = [w1f, mats[1], tabs[0], tabs[1]]
    out = jax.ShapeDtypeStruct((c, orders, m1, LANE), jnp.float32)
    ospec = pl.BlockSpec((cw, orders, m1, LANE), lambda j: (j, 0, 0, 0))
    return pl.pallas_call(
        functools.partial(_filtspec_kernel, half=half, orders=orders),
        out_shape=(out, out),
        grid=(c // cw,),
        in_specs=[pl.BlockSpec((orders * m1, cw * LANE), lambda j: (0, j))] +
                 [pl.BlockSpec(a.shape, lambda j: (0, 0)) for a in consts],
        out_specs=(ospec, ospec),
        compiler_params=_cparams("parallel"),
        name="hyena_filter_spectrum",
    )(taps.reshape(orders * m1, c * LANE), *consts)


def _dense_dft_constants(seq):
    n = 2 * seq
    t = np.arange(seq)[:, None]
    f = np.arange(n)[None, :]
    ps = 2.0 * np.pi * t * f / n
    fwd = np.block([[np.cos(ps), -np.sin(ps)], [np.sin(ps), np.cos(ps)]])
    inv = fwd.T / n
    tk = np.arange(n)[:, None]
    pk = 2.0 * np.pi * tk * f / n
    fk = np.concatenate([np.cos(pk), -np.sin(pk)], axis=1)
    return [jnp.asarray(m, jnp.bfloat16) for m in (fwd, inv, fk)]


def _ctxspec_kernel(k_ref, fk_ref, re_ref, im_ref):
    orders, slabs = k_ref.shape[0], k_ref.shape[1]
    n = slabs * LANE
    for o in range(orders):
        x = jnp.concatenate([k_ref[o, s] for s in range(slabs)], axis=1)
        x = x / (jnp.sum(jnp.abs(x), axis=1, keepdims=True) + EPS)
        kf = _mxu(x, fk_ref[...])
        re_ref[o] = kf[:, :n]
        im_ref[o] = kf[:, n:]


def ctx_filter_spectrum(taps, cw):
    orders, slabs, c, _ = taps.shape
    n = slabs * LANE
    fk = _dense_dft_constants(n // 2)[2]
    out = jax.ShapeDtypeStruct((orders, c, n), jnp.float32)
    ospec = pl.BlockSpec((orders, cw, n), lambda j: (0, j, 0))
    return pl.pallas_call(
        _ctxspec_kernel,
        out_shape=(out, out),
        grid=(c // cw,),
        in_specs=[pl.BlockSpec((orders, slabs, cw, LANE), lambda j: (0, 0, j, 0)),
                  pl.BlockSpec(fk.shape, lambda j: (0, 0))],
        out_specs=(ospec, ospec),
        compiler_params=_cparams("parallel"),
        name="ctx_filter_spectrum",
    )(taps, fk)


def _ctxconv_kernel(z_ref, g_ref, kre_ref, kim_ref, b_ref, fwd_ref, inv_ref, o_ref):
    bsz, slabs, cw = z_ref.shape[0], z_ref.shape[1], z_ref.shape[2]
    seq = slabs * LANE
    zs = [jnp.concatenate([z_ref[b, s] for s in range(slabs)], axis=1) for b in range(bsz)]
    lhs = jnp.concatenate([jnp.concatenate([zs[2 * p], zs[2 * p + 1]], axis=1) for p in range(bsz // 2)], axis=0)
    s = _mxu(lhs, fwd_ref[...])
    n = 2 * seq
    sr, si = s[:, :n], s[:, n:]
    kr = jnp.concatenate([kre_ref[0]] * (bsz // 2), axis=0)
    ki = jnp.concatenate([kim_ref[0]] * (bsz // 2), axis=0)
    y = _mxu(jnp.concatenate([sr * kr - si * ki, sr * ki + si * kr], axis=1), inv_ref[...])
    bias = jnp.concatenate([b_ref[...]] * slabs, axis=1)
    for b in range(bsz):
        p, m = b // 2, b % 2
        yb = y[p * cw:(p + 1) * cw, m * seq:(m + 1) * seq]
        gate = jnp.concatenate([g_ref[b, s] for s in range(slabs)], axis=1)
        res = gate * (yb + zs[b] * bias)
        for sl in range(slabs):
            o_ref[b, sl] = res[:, sl * LANE:(sl + 1) * LANE]


def ctx_hyena_conv(ut, z_src, z_col, g_col, kre, kim, order, bias, cw):
    bsz, slabs, _, _ = ut.shape
    c = bias.shape[0]
    n = 2 * slabs * LANE
    fwd, inv, _ = _dense_dft_constants(slabs * LANE)
    zb, gb = z_col // cw, g_col // cw
    bb = jnp.broadcast_to(bias[:, None], (c, LANE))
    kspec = pl.BlockSpec((1, cw, n), lambda j: (order, j, 0))
    return pl.pallas_call(
        _ctxconv_kernel,
        out_shape=jax.ShapeDtypeStruct((bsz, slabs, c, LANE), jnp.float32),
        grid=(c // cw,),
        in_specs=[
            pl.BlockSpec((bsz, slabs, cw, LANE), lambda j: (0, 0, zb + j, 0)),
            pl.BlockSpec((bsz, slabs, cw, LANE), lambda j: (0, 0, gb + j, 0)),
            kspec, kspec,
            pl.BlockSpec((cw, LANE), lambda j: (j, 0)),
            pl.BlockSpec(fwd.shape, lambda j: (0, 0)),
            pl.BlockSpec(inv.shape, lambda j: (0, 0)),
        ],
        out_specs=pl.BlockSpec((bsz, slabs, cw, LANE), lambda j: (0, 0, j, 0)),
        compiler_params=_cparams("parallel"),
        name="ctx_hyena_conv",
    )(z_src, ut, kre, kim, bb, fwd, inv)


def _mlp_kernel(x_ref, nw_ref, sc_ref, sh_ref, g_ref, w1_ref, w2_ref, fw_ref, o_ref, h_ref, *, final_norm):
    f = pl.program_id(2)

    @pl.when(f == 0)
    def _():
        _normmod_store(h_ref, x_ref, nw_ref[...], sc_ref[0], sh_ref[0])
        o_ref[...] = jnp.zeros_like(o_ref)

    a = jnp.dot(h_ref[...], w1_ref[...], preferred_element_type=jnp.float32)
    a = jnp.square(jnp.maximum(a, 0.0)).astype(jnp.bfloat16)
    o_ref[0] += jnp.dot(a, w2_ref[...], preferred_element_type=jnp.float32)

    @pl.when(f == pl.num_programs(2) - 1)
    def _():
        y = x_ref[0] + g_ref[0] * o_ref[0]
        if final_norm:
            ms = jnp.mean(y * y, axis=-1, keepdims=True)
            y = y * lax.rsqrt(ms + EPS) * fw_ref[...]
        o_ref[0] = y


def mlp_residual(x, nw, sc, sh, g, w1, w2, layer, fw, tm, tf, final_norm):
    b, l, d = x.shape
    dff = w1.shape[2]
    vec = pl.BlockSpec((1, 1, d), lambda i, j, k: (i, 0, 0))
    row = pl.BlockSpec((1, d), lambda i, j, k: (0, 0))
    return pl.pallas_call(
        functools.partial(_mlp_kernel, final_norm=final_norm),
        out_shape=jax.ShapeDtypeStruct((b, l, d), jnp.float32),
        grid=(b, l // tm, dff // tf),
        in_specs=[
            pl.BlockSpec((1, tm, d), lambda i, j, k: (i, j, 0)),
            row, vec, vec, vec,
            pl.BlockSpec((None, d, tf), lambda i, j, k: (layer, 0, k)),
            pl.BlockSpec((None, tf, d), lambda i, j, k: (layer, k, 0)),
            row,
        ],
        out_specs=pl.BlockSpec((1, tm, d), lambda i, j, k: (i, j, 0)),
        scratch_shapes=[pltpu.VMEM((tm, d), jnp.bfloat16)],
        compiler_params=_cparams("parallel", "parallel", "arbitrary"),
        name="mlp_residual",
    )(x, nw.reshape(1, d), sc.reshape(b, 1, d), sh.reshape(b, 1, d), g.reshape(b, 1, d), w1, w2,
      fw.reshape(1, d))


def _split3(v):
    f32, bf16 = jnp.float32, jnp.bfloat16
    hi = v.astype(bf16)
    r1 = v - hi.astype(f32)
    mid = r1.astype(bf16)
    lo = (r1 - mid.astype(f32)).astype(bf16)
    return jnp.concatenate([hi, mid, lo], axis=1)


def _lane_repeat(v, rep):
    h = v.shape[1]
    row = lax.broadcasted_iota(jnp.int32, (3 * h, h * rep), 0) % h
    col = lax.broadcasted_iota(jnp.int32, (3 * h, h * rep), 1) // rep
    e = (row == col).astype(jnp.bfloat16)
    return jnp.dot(_split3(v), e, preferred_element_type=jnp.float32)


def _silu(v):
    h = 0.5 * v
    return h + h * jnp.tanh(h)


def _softplus(v):
    return jnp.maximum(v, 0.0) + jnp.log1p(jnp.exp(-jnp.abs(v)))


def _token_conv3(u, w, bias, period):
    t, c = u.shape
    pos = lax.broadcasted_iota(jnp.int32, (t, c), 0) % period
    up = jnp.where(pos == 0, 0.0, pltpu.roll(u, 1, 0))
    dn = jnp.where(pos == period - 1, 0.0, pltpu.roll(u, t - 1, 0))
    return bias + w[0:1] * up + w[1:2] * u + w[2:3] * dn


def _ssd_chunk(xs, bm, cm, dtr, s_ref, a_row, dtb_row, reverse, heads, hpg):
    f32, bf16 = jnp.float32, jnp.bfloat16
    q = xs.shape[0]
    p = SSD_HEADDIM
    n = SSD_STATE
    gw = hpg * p
    dt = _softplus(dtr + dtb_row)
    a = dt * (a_row * LOG2E)
    ri = lax.broadcasted_iota(jnp.int32, (q, q), 0)
    ci = lax.broadcasted_iota(jnp.int32, (q, q), 1)
    keep = (ci >= ri) if reverse else (ci <= ri)
    a3 = jnp.dot(keep.astype(bf16), _split3(a), preferred_element_type=f32)
    cs = a3[:, :heads] + a3[:, heads:2 * heads] + a3[:, 2 * heads:]
    dt_rep = _lane_repeat(dt, p)
    cs_rep = _lane_repeat(cs, p)
    cs_wide = _lane_repeat(cs, q)
    end = 0 if reverse else q - 1
    cs_end = cs_rep[end:end + 1]
    xdt = xs * dt_rep
    xw = (xdt * jnp.exp2(cs_end - cs_rep)).astype(bf16)
    ecs = jnp.exp2(cs_rep)
    chunk_decay = jnp.exp2(cs_end)
    xdt_b = xdt.astype(bf16)
    lane = lax.broadcasted_iota(jnp.int32, (q, 2 * p), 1)
    ys = []
    for g in range(SSD_GROUPS):
        bg = bm[:, g * n:(g + 1) * n].astype(bf16)
        cg = cm[:, g * n:(g + 1) * n].astype(bf16)
        cb = lax.dot_general(cg, bg, (((1,), (1,)), ((), ())), preferred_element_type=f32)
        s_old = s_ref[0, g]
        y_off = jnp.dot(cg, s_old.astype(bf16), preferred_element_type=f32) * ecs[:, g * gw:(g + 1) * gw]
        for pr in range(hpg // 2):
            xpair = xdt_b[:, g * gw + pr * 2 * p:g * gw + (pr + 1) * 2 * p]
            acc = None
            for k in range(2):
                h = g * hpg + pr * 2 + k
                csr = cs_wide[:, h * q:(h + 1) * q]
                seg = csr - csr.T
                gmat = (cb * jnp.where(keep, jnp.exp2(seg), 0.0)).astype(bf16)
                xh = jnp.where((lane // p) == k, xpair, jnp.zeros_like(xpair))
                part = jnp.dot(gmat, xh, preferred_element_type=f32)
                acc = part if acc is None else acc + part
            lo = pr * 2 * p
            ys.append(acc + y_off[:, lo:lo + 2 * p])
        upd = jnp.dot(bg.T, xw[:, g * gw:(g + 1) * gw], preferred_element_type=f32)
        s_ref[0, g] = s_old * chunk_decay[:, g * gw:(g + 1) * gw] + upd
    return jnp.concatenate(ys, axis=1)


def _ssd_kernel(z_ref, xs_ref, bc_ref, dt_ref, cwx_ref, cbx_ref, cwb_ref, cbb_ref, dtb_ref, alog_ref, drep_ref,
                nw_ref, initf_ref, initb_ref, y_ref, sf_ref, sb_ref, yb_ref, xc_ref, bcc_ref, *, nb, tb, period,
                heads, hpg):
    j = pl.program_id(1)
    q = SSD_CHUNK
    n = SSD_STATE
    gn = SSD_GROUPS * n
    nchunk = tb // q
    ssd_w = heads * SSD_HEADDIM

    @pl.when(j == 0)
    def _():
        sb_ref[...] = initb_ref[...]

    @pl.when(j == nb)
    def _():
        sf_ref[...] = initf_ref[...]

    a_all = -jnp.exp(alog_ref[...])

    def run(reverse):
        d = 1 if reverse else 0
        blk = (nb - 1 - j) if reverse else (j - nb)
        order = range(nchunk - 1, -1, -1) if reverse else range(nchunk)
        rows = pl.ds(pl.multiple_of(blk * tb, tb), tb)
        if reverse:
            xs_all = _silu(_token_conv3(xs_ref[0], cwx_ref[...], cbx_ref[...], period))
            bc_all = _silu(_token_conv3(bc_ref[0], cwb_ref[...], cbb_ref[...], period)).astype(bcc_ref.dtype)
            xc_ref[rows, :] = xs_all
            bcc_ref[rows, :] = bc_all
        else:
            xs_all = xc_ref[rows, :]
            bc_all = bcc_ref[rows, :]
        for ci in order:
            sl = slice(ci * q, (ci + 1) * q)
            y = _ssd_chunk(xs_all[sl], bc_all[sl, :gn], bc_all[sl, gn:], dt_ref[0, sl, d * heads:(d + 1) * heads],
                           sb_ref if reverse else sf_ref, a_all[d:d + 1], dtb_ref[d:d + 1], reverse, heads, hpg)
            row0 = pl.multiple_of(blk * tb + ci * q, q)
            if reverse:
                yb_ref[pl.ds(row0, q), :] = y
            else:
                y = y + yb_ref[pl.ds(row0, q), :] + drep_ref[...] * xs_all[sl]
                y = y * _silu(z_ref[0, sl, :])
                gw = ssd_w // SSD_GROUPS
                outs = []
                for g in range(SSD_GROUPS):
                    yg = y[:, g * gw:(g + 1) * gw]
                    ms = jnp.mean(yg * yg, axis=-1, keepdims=True)
                    outs.append(yg * lax.rsqrt(ms + EPS) * nw_ref[:, g * gw:(g + 1) * gw])
                y_ref[0, sl, :] = jnp.concatenate(outs, axis=1)

    @pl.when(j < nb)
    def _():
        run(True)

    @pl.when(j >= nb)
    def _():
        run(False)


def ssd_mixer(srcs, cols, conv_w_x, conv_b_x, conv_w_bc, conv_b_bc, dt_bias, a_log, ssd_d, norm_w, init_f, init_b,
              period, tb):
    b, l, _ = srcs[0].shape
    heads = dt_bias.shape[1]
    hpg = heads // SSD_GROUPS
    ssd_w = heads * SSD_HEADDIM
    gn = SSD_GROUPS * SSD_STATE
    nb = l // tb
    oz, ox, obc, odt = cols
    f32 = jnp.float32

    def tok(width, off):
        blk_idx = off // width
        return pl.BlockSpec((1, tb, width),
                            lambda i, j: (i, jnp.where(j < nb, nb - 1 - j, j - nb), blk_idx))

    def tok_fwd(width, off):
        blk_idx = off // width
        return pl.BlockSpec((1, tb, width), lambda i, j: (i, jnp.where(j < nb, 0, j - nb), blk_idx))

    def tok_rev(width, off):
        blk_idx = off // width
        return pl.BlockSpec((1, tb, width), lambda i, j: (i, jnp.maximum(nb - 1 - j, 0), blk_idx))

    def whole(shape):
        return pl.BlockSpec(shape, lambda i, j: (0,) * len(shape))

    st_shape = (b, SSD_GROUPS, SSD_STATE, hpg * SSD_HEADDIM)
    st_spec = pl.BlockSpec((1,) + st_shape[1:], lambda i, j: (i, 0, 0, 0))
    drep = jnp.repeat(ssd_d, SSD_HEADDIM).reshape(1, ssd_w)
    kern = functools.partial(_ssd_kernel, nb=nb, tb=tb, period=period, heads=heads, hpg=hpg)
    return pl.pallas_call(
        kern,
        out_shape=(jax.ShapeDtypeStruct((b, l, ssd_w), f32), jax.ShapeDtypeStruct(st_shape, f32),
                   jax.ShapeDtypeStruct(st_shape, f32)),
        grid=(b, 2 * nb),
        in_specs=[
            tok_fwd(ssd_w, oz), tok_rev(ssd_w, ox), tok_rev(2 * gn, obc), tok(LANE, odt),
            whole((3, ssd_w)), whole((1, ssd_w)), whole((3, 2 * gn)), whole((1, 2 * gn)),
            whole((2, heads)), whole((2, heads)), whole((1, ssd_w)), whole((1, ssd_w)),
            st_spec, st_spec,
        ],
        out_specs=(pl.BlockSpec((1, tb, ssd_w), lambda i, j: (i, jnp.where(j < nb, 0, j - nb), 0)),
                   st_spec, st_spec),
        scratch_shapes=[pltpu.VMEM((l, ssd_w), f32), pltpu.VMEM((l, ssd_w), f32),
                        pltpu.VMEM((l, 2 * gn), jnp.bfloat16)],
        compiler_params=_cparams("parallel", "arbitrary"),
        name="ssd_mixer",
    )(*srcs, conv_w_x, conv_b_x.reshape(1, -1), conv_w_bc, conv_b_bc.reshape(1, -1), dt_bias, a_log, drep,
      norm_w.reshape(1, ssd_w), init_f, init_b)


def _pick(l, pref):
    return pref if l % pref == 0 else l


def kernel(x, c, ctx, c_ctx, w_ada, b_ada, norm1_w, w_in, hy_conv_w, hy_conv_b, filt_w1, filt_b1, filt_freq,
           filt_w2, filt_b2, filt_w3, hy_bias, hy_norm_w, ssd_conv_w, ssd_conv_b, dt_bias, a_log, ssd_d,
           ssd_norm_w, w_out, norm2_w, w_mlp1, w_mlp2, final_norm_w):
    depth = w_in.shape[0]
    bsz, seq, d = x.shape
    hy_w = hy_norm_w.shape[1]
    hy_proj = hy_conv_w.shape[2]
    ssd_w = ssd_norm_w.shape[1]
    ssd_xbc = ssd_conv_w.shape[2]
    heads = ssd_w // SSD_HEADDIM
    hpg = heads // SSD_GROUPS
    ssd_dt = 2 * heads
    o0, o1 = hy_proj, hy_proj + ssd_xbc
    o2 = o1 + ssd_dt
    dt_pad = (-ssd_dt) % LANE
    bf16 = jnp.bfloat16
    ssd_cols = (0, o0, o0 + ssd_w, ssd_w)
    tn = o1 // 3

    n1 = seq // LANE
    n1c = ctx.shape[1] // LANE
    zero_state = jnp.zeros((bsz, SSD_GROUPS, SSD_STATE, hpg * SSD_HEADDIM), jnp.float32)

    rows = 8
    c_all = jnp.zeros((rows, d), jnp.float32).at[:bsz].set(c).at[bsz].set(c_ctx)
    mod_all = ada_mod(c_all, w_ada, b_ada)

    w_a_all, w_b_all = cast_in_proj(w_in, o1, o2, w_in.shape[2], ssd_w + ssd_dt + dt_pad)
    w_out_b, w1_b, w2_b = w_out.astype(bf16), w_mlp1.astype(bf16), w_mlp2.astype(bf16)

    h_ctx = ctx
    tm_l = _pick(seq, 512)
    tm_c = _pick(ctx.shape[1], 256)
    for i in range(depth):
        lp = dict(hy_conv_w=hy_conv_w[i], hy_conv_b=hy_conv_b[i], filt_w1=filt_w1[i], filt_b1=filt_b1[i],
                  filt_freq=filt_freq[i], filt_w2=filt_w2[i], filt_b2=filt_b2[i], filt_w3=filt_w3[i],
                  hy_bias=hy_bias[i], hy_norm_w=hy_norm_w[i], ssd_conv_w=ssd_conv_w[i],
                  ssd_conv_b=ssd_conv_b[i], dt_bias=dt_bias[i], a_log=a_log[i], ssd_d=ssd_d[i],
                  ssd_norm_w=ssd_norm_w[i])
        mod = mod_all[i, :bsz].reshape(bsz, N_MOD, d)
        mod_c = jnp.broadcast_to(mod_all[i, bsz].reshape(1, N_MOD, d), (bsz, N_MOD, d))
        def ssd(pa, pb, init_f, init_b, period):
            cw, cb = lp['ssd_conv_w'], lp['ssd_conv_b']
            return ssd_mixer((pb, pa, pa, pb), ssd_cols, cw[:, :ssd_w], cb[:ssd_w], cw[:, ssd_w:], cb[ssd_w:],
                             lp['dt_bias'], lp['a_log'], lp['ssd_d'], lp['ssd_norm_w'], init_f, init_b, period,
                             _pick(pa.shape[1], 256))

        def filt_taps(length):
            return hyena_filter_taps(length, lp['filt_w1'], lp['filt_b1'], lp['filt_freq'], lp['filt_w2'],
                                     lp['filt_b2'], lp['filt_w3'], hy_w)

        pca, pcb = normmod_matmul(h_ctx, norm1_w[i], mod_c[:, 1], mod_c[:, 0], w_a_all, w_b_all, i, tm_c, tn)
        y_ssd, s_f, s_b = ssd(pca, pcb, zero_state, zero_state, pca.shape[1])
        if i < depth - 1:
            utc = hyena_pre(pca, 0, lp['hy_conv_w'], lp['hy_conv_b'], pca.shape[1], pca.shape[1], 512)
            kre, kim = ctx_filter_spectrum(filt_taps(pca.shape[1]), 128)
            zt = utc
            for o in range(HY_ORDER):
                zt = ctx_hyena_conv(utc, zt, 0, (o + 1) * hy_w, kre, kim, o, lp['hy_bias'][o], 128)
            h_ctx = proj_residual(zt, y_ssd, lp['hy_norm_w'], w_out_b, i, h_ctx, mod_c[:, 2], tm_c, HY_GROUPS)
            h_ctx = mlp_residual(h_ctx, norm2_w[i], mod_c[:, 4], mod_c[:, 3], mod_c[:, 5], w1_b, w2_b, i,
                                 final_norm_w, tm_c, 1024, False)
        pla, plb = normmod_matmul(x, norm1_w[i], mod[:, 1], mod[:, 0], w_a_all, w_b_all, i, _pick(seq, 1024), tn)
        ut = hyena_pre_rows(pla, 0, lp['hy_conv_w'], lp['hy_conv_b'], GRID_W, LANE)
        kre, kim = hyena_filter_spectrum(filt_taps(seq), bsz, 16)
        zt = ut
        for o in range(HY_ORDER):
            zt = hyena_conv(ut, zt, 0, (o + 1) * hy_w, kre, kim, o, lp['hy_bias'][o], bsz, 16, o == HY_ORDER - 1)
        zh_t = zt.reshape(bsz, n1, hy_w, LANE)
        y_ssd, _, _ = ssd(pla, plb, s_f, s_b, GRID_W)
        x = proj_residual(zh_t, y_ssd, lp['hy_norm_w'], w_out_b, i, x, mod[:, 2], tm_l, HY_GROUPS)
        x = mlp_residual(x, norm2_w[i], mod[:, 4], mod[:, 3], mod[:, 5], w1_b, w2_b, i, final_norm_w, tm_l, 1024,
                         i == depth - 1)
    return x
```

```python
import functools
import math

import numpy as np
import jax
import jax.numpy as jnp
from jax import lax
from jax.experimental import pallas as pl
from jax.experimental.pallas import tpu as pltpu

EPS = 1e-6
N_MOD = 6
GRID_W = 64
HY_GROUPS = 8
HY_ORDER = 2
HY_POS_EMB = 33
HY_TARGET = 1e-2
HY_FAST = 0.3
HY_SLOW = 1.5
SSD_HEADDIM = 64
SSD_GROUPS = 2
SSD_STATE = 128
SSD_CHUNK = 128

LANE = 128
LOG2E = 1.4426950408889634
VMEM_LIMIT = 56 * 1024 * 1024


def _cparams(*sem):
    return pltpu.CompilerParams(dimension_semantics=sem, vmem_limit_bytes=VMEM_LIMIT)


def _ada_kernel(c_ref, w_ref, b_ref, o_ref):
    c = c_ref[...]
    a = (c * jax.nn.sigmoid(c)).astype(jnp.bfloat16)
    w = w_ref[0].astype(jnp.bfloat16)
    o_ref[0] = jnp.dot(a, w, preferred_element_type=jnp.float32) + b_ref[0]


def ada_mod(c_all, w_ada, b_ada, tn=1024):
    depth, d, n = w_ada.shape
    r = c_all.shape[0]
    return pl.pallas_call(
        _ada_kernel,
        out_shape=jax.ShapeDtypeStruct((depth, r, n), jnp.float32),
        grid=(depth, n // tn),
        in_specs=[
            pl.BlockSpec((r, d), lambda i, j: (0, 0)),
            pl.BlockSpec((1, d, tn), lambda i, j: (i, 0, j)),
            pl.BlockSpec((1, 1, tn), lambda i, j: (i, 0, j)),
        ],
        out_specs=pl.BlockSpec((1, r, tn), lambda i, j: (i, 0, j)),
        compiler_params=_cparams("parallel", "parallel"),
        name="ada_mod",
    )(c_all, w_ada, b_ada.reshape(depth, 1, n))


def _wcast_kernel(w_ref, a_ref, b_ref, *, na, c0, c1):
    bf16 = jnp.bfloat16
    w = w_ref[0]
    a_ref[0] = w[:, :na].astype(bf16)
    nb = b_ref.shape[2]
    pad = nb - (c1 - c0) - (c0 - na)
    b_ref[0] = jnp.concatenate([w[:, c0:c1], w[:, na:c0], jnp.zeros((w.shape[0], pad), w.dtype)], axis=1).astype(bf16)


def cast_in_proj(w_in, na, c0, c1, nb, tr=256):
    depth, d, n = w_in.shape
    return pl.pallas_call(
        functools.partial(_wcast_kernel, na=na, c0=c0, c1=c1),
        out_shape=(jax.ShapeDtypeStruct((depth, d, na), jnp.bfloat16),
                   jax.ShapeDtypeStruct((depth, d, nb), jnp.bfloat16)),
        grid=(depth, d // tr),
        in_specs=[pl.BlockSpec((1, tr, n), lambda i, j: (i, j, 0))],
        out_specs=(pl.BlockSpec((1, tr, na), lambda i, j: (i, j, 0)),
                   pl.BlockSpec((1, tr, nb), lambda i, j: (i, j, 0))),
        compiler_params=_cparams("parallel", "parallel"),
        name="cast_in_proj",
    )(w_in)


NORM_ROWS = 16


def _normmod_store(h_ref, x_ref, nw, sc, sh):
    w = nw * (1.0 + sc)

    def body(i, carry):
        r0 = pl.multiple_of(i * NORM_ROWS, NORM_ROWS)
        xb = x_ref[0, pl.ds(r0, NORM_ROWS), :]
        ms = jnp.mean(xb * xb, axis=-1, keepdims=True)
        h_ref[pl.ds(r0, NORM_ROWS), :] = ((xb * lax.rsqrt(ms + EPS)) * w + sh).astype(h_ref.dtype)
        return carry

    lax.fori_loop(0, h_ref.shape[0] // NORM_ROWS, body, 0, unroll=8)


def _normmod_kernel(x_ref, nw_ref, sc_ref, sh_ref, h_ref):
    _normmod_store(h_ref.at[0], x_ref, nw_ref[...], sc_ref[0], sh_ref[0])


def _proj_kernel(h_ref, w_ref, o_ref):
    o_ref[0] = jnp.dot(h_ref[0], w_ref[...], preferred_element_type=jnp.float32)


def _proj_matmul(h, w, layer, tm, tn):
    b, l, d = h.shape
    n = w.shape[2]
    return pl.pallas_call(
        _proj_kernel,
        out_shape=jax.ShapeDtypeStruct((b, l, n), jnp.float32),
        grid=(n // tn, b, l // tm),
        in_specs=[
            pl.BlockSpec((1, tm, d), lambda k, i, j: (i, j, 0)),
            pl.BlockSpec((None, d, tn), lambda k, i, j: (layer, 0, k)),
        ],
        out_specs=pl.BlockSpec((1, tm, tn), lambda k, i, j: (i, j, k)),
        compiler_params=_cparams("parallel", "parallel", "parallel"),
        name="in_proj",
    )(h, w)


def normmod_matmul(x, nw, sc, sh, wa, wb, layer, tm, tn):
    b, l, d = x.shape
    tn_norm = min(tm, 512)
    h = pl.pallas_call(
        _normmod_kernel,
        out_shape=jax.ShapeDtypeStruct((b, l, d), jnp.bfloat16),
        grid=(b, l // tn_norm),
        in_specs=[
            pl.BlockSpec((1, tn_norm, d), lambda i, j: (i, j, 0)),
            pl.BlockSpec((1, d), lambda i, j: (0, 0)),
            pl.BlockSpec((1, 1, d), lambda i, j: (i, 0, 0)),
            pl.BlockSpec((1, 1, d), lambda i, j: (i, 0, 0)),
        ],
        out_specs=pl.BlockSpec((1, tn_norm, d), lambda i, j: (i, j, 0)),
        compiler_params=_cparams("parallel", "parallel"),
        name="normmod",
    )(x, nw.reshape(1, d), sc.reshape(b, 1, d), sh.reshape(b, 1, d))
    return _proj_matmul(h, wa, layer, tm, tn), _proj_matmul(h, wb, layer, tm, wb.shape[2])


def _proj_res_kernel(zh_ref, ys_ref, nwh_ref, wh_ref, ws_ref, x_ref, g_ref, o_ref, *, groups):
    bf16 = jnp.bfloat16
    slabs, hy_w = zh_ref.shape[1], zh_ref.shape[2]
    gs = hy_w // groups
    rows = []
    for s in range(slabs):
        cols = []
        for gi in range(groups):
            zg = zh_ref[0, s, gi * gs:(gi + 1) * gs, :]
            ms = jnp.mean(zg * zg, axis=0, keepdims=True)
            cols.append((zg * lax.rsqrt(ms + EPS)).T)
        rows.append(jnp.concatenate(cols, axis=1))
    yh = (jnp.concatenate(rows, axis=0) * nwh_ref[...]).astype(bf16)
    acc = jnp.dot(yh, wh_ref[...], preferred_element_type=jnp.float32)
    acc += jnp.dot(ys_ref[0].astype(bf16), ws_ref[...], preferred_element_type=jnp.float32)
    o_ref[0] = x_ref[0] + g_ref[0] * acc


def proj_residual(zh_t, y_ssd, hy_norm_w, w_out, layer, x, g, tm, groups):
    b, n1, hy_w, _ = zh_t.shape
    l, ssd_w = y_ssd.shape[1], y_ssd.shape[2]
    n = w_out.shape[2]
    assert hy_w % ssd_w == 0
    return pl.pallas_call(
        functools.partial(_proj_res_kernel, groups=groups),
        out_shape=jax.ShapeDtypeStruct((b, l, n), jnp.float32),
        grid=(b, l // tm),
        in_specs=[
            pl.BlockSpec((1, tm // LANE, hy_w, LANE), lambda i, j: (i, j, 0, 0)),
            pl.BlockSpec((1, tm, ssd_w), lambda i, j: (i, j, 0)),
            pl.BlockSpec((1, hy_w), lambda i, j: (0, 0)),
            pl.BlockSpec((None, hy_w, n), lambda i, j: (layer, 0, 0)),
            pl.BlockSpec((None, ssd_w, n), lambda i, j: (layer, hy_w // ssd_w, 0)),
            pl.BlockSpec((1, tm, n), lambda i, j: (i, j, 0)),
            pl.BlockSpec((1, 1, n), lambda i, j: (i, 0, 0)),
        ],
        out_specs=pl.BlockSpec((1, tm, n), lambda i, j: (i, j, 0)),
        compiler_params=_cparams("parallel", "parallel"),
        name="proj_residual",
    )(zh_t, y_ssd, hy_norm_w.reshape(1, hy_w), w_out, w_out, x, g.reshape(b, 1, n))


def _hypre_kernel(u_ref, w_ref, b_ref, o_ref, *, period):
    u = _token_conv3(u_ref[0], w_ref[...], b_ref[...], period)
    tb, cw = u.shape
    for s in range(tb // LANE):
        for j in range(cw // LANE):
            o_ref[0, s, j * LANE:(j + 1) * LANE, :] = u[s * LANE:(s + 1) * LANE, j * LANE:(j + 1) * LANE].T


def hyena_pre(p, col0, conv_w, conv_b, period, tb, cw):
    b, l, _ = p.shape
    c = conv_w.shape[1]
    blk0 = col0 // cw
    return pl.pallas_call(
        functools.partial(_hypre_kernel, period=period),
        out_shape=jax.ShapeDtypeStruct((b, l // LANE, c, LANE), jnp.float32),
        grid=(b, l // tb, c // cw),
        in_specs=[
            pl.BlockSpec((1, tb, cw), lambda i, j, k: (i, j, blk0 + k)),
            pl.BlockSpec((3, cw), lambda i, j, k: (0, k)),
            pl.BlockSpec((1, cw), lambda i, j, k: (0, k)),
        ],
        out_specs=pl.BlockSpec((1, tb // LANE, cw, LANE), lambda i, j, k: (i, j, k, 0)),
        compiler_params=_cparams("parallel", "parallel", "parallel"),
        name="hyena_pre",
    )(p, conv_w, conv_b.reshape(1, c))


def _hypre_rows_kernel(u_ref, w_ref, b_ref, o_ref, s_ref, *, period):
    l, cw = u_ref.shape[1], u_ref.shape[2]
    n1 = l // LANE
    assert LANE % period == 0
    w, bias = w_ref[...], b_ref[...]

    for s in range(n1):
        u = _token_conv3(u_ref[0, s * LANE:(s + 1) * LANE, :], w, bias, period)
        for j in range(cw // LANE):
            s_ref[s * cw + j * LANE:s * cw + (j + 1) * LANE, :] = u[:, j * LANE:(j + 1) * LANE].T
    for c in range(cw):
        o_ref[:, c * LANE:(c + 1) * LANE] = s_ref[pl.ds(c, n1, stride=cw), :]


def hyena_pre_rows(p, col0, conv_w, conv_b, period, cw):
    b, l, _ = p.shape
    c = conv_w.shape[1]
    n1 = l // LANE
    blk0 = col0 // cw
    return pl.pallas_call(
        functools.partial(_hypre_rows_kernel, period=period),
        out_shape=jax.ShapeDtypeStruct((b * n1, c * LANE), jnp.float32),
        grid=(b, c // cw),
        in_specs=[
            pl.BlockSpec((1, l, cw), lambda i, k: (i, 0, blk0 + k)),
            pl.BlockSpec((3, cw), lambda i, k: (0, k)),
            pl.BlockSpec((1, cw), lambda i, k: (0, k)),
        ],
        out_specs=pl.BlockSpec((n1, cw * LANE), lambda i, k: (i, k)),
        scratch_shapes=[pltpu.VMEM((n1 * cw, LANE), jnp.float32)],
        compiler_params=_cparams("parallel", "parallel"),
        name="hyena_pre_rows",
    )(p, conv_w, conv_b.reshape(1, c))


def _fft_constants(bsz, n1):
    m1 = 2 * n1
    n = m1 * LANE
    pairs = bsz // 2
    half = pairs * m1
    r = bsz * n1
    t1 = np.arange(n1)[:, None]
    f1 = np.arange(m1)[None, :]
    th = 2.0 * np.pi * t1 * f1 / m1
    w1 = np.zeros((r, 2 * half))
    for pr in range(pairs):
        re = slice(pr * m1, (pr + 1) * m1)
        im = slice(half + pr * m1, half + (pr + 1) * m1)
        ra = slice((2 * pr) * n1, (2 * pr + 1) * n1)
        rb = slice((2 * pr + 1) * n1, (2 * pr + 2) * n1)
        w1[ra, re], w1[ra, im] = np.cos(th), -np.sin(th)
        w1[rb, re], w1[rb, im] = np.sin(th), np.cos(th)
    w4 = w1.T / n
    t2 = np.arange(LANE)[:, None]
    ps = 2.0 * np.pi * t2 * np.arange(LANE)[None, :] / LANE
    w2 = np.block([[np.cos(ps), -np.sin(ps)], [np.sin(ps), np.cos(ps)]])
    ph = 2.0 * np.pi * t2 * f1 / n
    tc = np.tile(np.cos(ph), (1, pairs))
    ts = np.tile(np.sin(ph), (1, pairs))
    orders = half // m1
    tf = 2.0 * np.pi * np.arange(m1)[:, None] * f1 / m1
    w1f = np.zeros((orders * m1, 2 * half))
    for o in range(orders):
        w1f[o * m1:(o + 1) * m1, o * m1:(o + 1) * m1] = np.cos(tf)
        w1f[o * m1:(o + 1) * m1, half + o * m1:half + (o + 1) * m1] = -np.sin(tf)
    mats = [jnp.asarray(m, jnp.bfloat16) for m in (w1, w2, w2.T, w4)]
    tabs = [jnp.asarray(m, jnp.float32) for m in (tc, ts, tc.T, ts.T)]
    return mats, tabs, half, jnp.asarray(w1f, jnp.bfloat16)


def _mxu(a, w):
    return jnp.dot(a.astype(jnp.bfloat16), w, preferred_element_type=jnp.float32)


def _fft_forward(x0, w1, w2, tc, ts, half):
    cw = x0.shape[1] // LANE
    a1 = _mxu(x0.T, w1)
    lhs2 = []
    for c in range(cw):
        ar = a1[c * LANE:(c + 1) * LANE, :half]
        ai = a1[c * LANE:(c + 1) * LANE, half:]
        lhs2.append(jnp.concatenate([(ar * tc + ai * ts).T, (ai * tc - ar * ts).T], axis=1))
    return _mxu(jnp.concatenate(lhs2, axis=0), w2)


def _hyconv_kernel(z_ref, g_ref, kre_ref, kim_ref, brep_ref, w1_ref, w2_ref, w2i_ref, w4_ref, tc_ref, ts_ref,
                   tct_ref, tst_ref, o_ref, *, half):
    x0 = z_ref[...]
    cw = x0.shape[1] // LANE
    tct, tst = tct_ref[...], tst_ref[...]
    s = _fft_forward(x0, w1_ref[...], w2_ref[...], tc_ref[...], ts_ref[...], half)
    sr, si = s[:, :LANE], s[:, LANE:]
    m1 = kre_ref.shape[2]
    pairs = half // m1
    shape4 = (cw, pairs, m1, LANE)
    sr, si = sr.reshape(shape4), si.reshape(shape4)
    kr, ki = kre_ref[:, 0][:, None], kim_ref[:, 0][:, None]
    y2 = jnp.concatenate([(sr * kr - si * ki).reshape(cw * half, LANE),
                          (sr * ki + si * kr).reshape(cw * half, LANE)], axis=1)
    bq = _mxu(y2, w2i_ref[...])
    lhs4 = []
    for c in range(cw):
        br = bq[c * half:(c + 1) * half, :LANE]
        bi = bq[c * half:(c + 1) * half, LANE:]
        lhs4.append(jnp.concatenate([(br * tct - bi * tst).T, (bi * tct + br * tst).T], axis=1))
    lhs4 = jnp.concatenate(lhs4, axis=0)
    yt = _mxu(lhs4, w4_ref[...])
    if len(o_ref.shape) == 2:
        o_ref[...] = g_ref[...] * (yt.T + x0 * brep_ref[...])
    else:
        for c in range(cw):
            sl = slice(c * LANE, (c + 1) * LANE)
            o_ref[:, c, :] = g_ref[:, sl] * (yt[sl].T + x0[:, sl] * brep_ref[:, sl])


def hyena_conv(ut, z_src, z_col, g_col, kre, kim, order, bias, bsz, cw, channel_tiles):
    r = ut.shape[0]
    n1 = r // bsz
    c = bias.shape[0]
    mats, tabs, half, _ = _fft_constants(bsz, n1)
    blk = cw * LANE
    zb, gb = z_col // cw, g_col // cw
    brep = jnp.repeat(bias, LANE).reshape(1, c * LANE)
    kspec = pl.BlockSpec((cw, 1, 2 * n1, LANE), lambda j: (j, order, 0, 0))

    def const(a):
        return pl.BlockSpec(a.shape, lambda j: (0, 0))

    return pl.pallas_call(
        functools.partial(_hyconv_kernel, half=half),
        out_shape=jax.ShapeDtypeStruct((r, c, LANE) if channel_tiles else (r, c * LANE), jnp.float32),
        grid=(c // cw,),
        in_specs=[
            pl.BlockSpec((r, blk), lambda j: (0, zb + j)),
            pl.BlockSpec((r, blk), lambda j: (0, gb + j)),
            kspec, kspec,
            pl.BlockSpec((1, blk), lambda j: (0, j)),
        ] + [const(a) for a in mats + tabs],
        out_specs=(pl.BlockSpec((r, cw, LANE), lambda j: (0, j, 0)) if channel_tiles
                   else pl.BlockSpec((r, blk), lambda j: (0, j))),
        compiler_params=_cparams("parallel"),
        name="hyena_conv",
    )(z_src, ut, kre, kim, brep, *mats, *tabs)


_HI = lax.Precision.HIGHEST


def _filtgen_kernel(a0_ref, wc_ref, ws_ref, fb1_ref, freq_ref, fw2t_ref, fb2_ref, fw3t_ref, dl_ref, o_ref, *,
                    seq, bands):
    f32 = jnp.float32
    s = pl.program_id(0)
    t = s * LANE + lax.broadcasted_iota(jnp.int32, (1, LANE), 1)
    pos = jnp.where(t < seq, t, 2 * seq - t).astype(f32)
    tt = pos / (seq - 1.0)
    ang = (2.0 * math.pi / seq) * pos
    j = lax.broadcasted_iota(jnp.int32, (bands, 1), 0).astype(f32)
    fj = 1e-4 + j * ((bands - 1.0 - 1e-4) / (bands - 1.0))
    fa = fj * ang
    freq = freq_ref[...]
    pre = a0_ref[...] * tt + jnp.dot(wc_ref[...], jnp.cos(fa), precision=_HI, preferred_element_type=f32) \
        - jnp.dot(ws_ref[...], jnp.sin(fa), precision=_HI, preferred_element_type=f32) + fb1_ref[...]
    h1 = jnp.sin(freq * pre)
    h2 = jnp.sin(freq * (jnp.dot(fw2t_ref[...], h1, precision=_HI, preferred_element_type=f32) + fb2_ref[...]))
    orders, _, c, hid3 = fw3t_ref.shape
    h2_hi = h2.astype(jnp.bfloat16)
    h2_lo = (h2 - h2_hi.astype(f32)).astype(jnp.bfloat16)
    h = jnp.dot(fw3t_ref[:, 0].reshape(orders * c, hid3), jnp.concatenate([h2_hi, h2_lo, h2_hi], axis=0),
                preferred_element_type=f32)
    h = h * jnp.exp(-tt * dl_ref[...])
    h = jnp.where(t == seq, 0.0, h)
    o_ref[:, 0] = h.reshape(orders, c, LANE)


def hyena_filter_taps(seq, fw1, fb1, freq, fw2, fb2, fw3, hy_w):
    hid = fw2.shape[0]
    bands = (HY_POS_EMB - 1) // 2
    orders = fw3.shape[1] // (2 * hy_w)
    slabs = 2 * seq // LANE
    col = lambda v: v.reshape(-1, 1)
    fw3t = fw3.T.reshape(orders, 2, hy_w, hid)
    w_hi = fw3t.astype(jnp.bfloat16)
    w_lo = (fw3t - w_hi.astype(jnp.float32)).astype(jnp.bfloat16)
    fw3t = jnp.concatenate([w_hi, w_hi, w_lo], axis=-1)
    deltas = jnp.abs(jnp.linspace(math.log(HY_TARGET) / HY_SLOW, math.log(HY_TARGET) / HY_FAST, hy_w,
                                  dtype=jnp.float32))
    dl = jnp.tile(deltas, orders).reshape(-1, 1)

    def whole(a):
        return pl.BlockSpec(a.shape, lambda s: (0,) * a.ndim)

    args = [col(fw1[0]), fw1[1:1 + bands].T, fw1[1 + bands:].T, col(fb1), col(freq), fw2.T, col(fb2)]
    return pl.pallas_call(
        functools.partial(_filtgen_kernel, seq=seq, bands=bands),
        out_shape=jax.ShapeDtypeStruct((orders, slabs, hy_w, LANE), jnp.float32),
        grid=(slabs,),
        in_specs=[whole(a) for a in args] + [
            pl.BlockSpec((orders, 1, hy_w, 3 * hid), lambda s: (0, s // (slabs // 2), 0, 0)),
            whole(dl),
        ],
        out_specs=pl.BlockSpec((orders, 1, hy_w, LANE), lambda s: (0, s, 0, 0)),
        compiler_params=_cparams("parallel"),
        name="hyena_filter_taps",
    )(*args, fw3t, dl)


def _lane_block_abs_norm(x, orders):
    rows_per = x.shape[0] // orders
    cw = x.shape[1] // LANE
    out = []
    for o in range(orders):
        xo = x[o * rows_per:(o + 1) * rows_per]
        a = jnp.sum(jnp.abs(xo), axis=0, keepdims=True)
        inv = [jnp.broadcast_to(1.0 / (jnp.sum(a[:, c * LANE:(c + 1) * LANE], axis=1, keepdims=True) + EPS),
                                (1, LANE)) for c in range(cw)]
        out.append(xo * jnp.concatenate(inv, axis=1))
    return jnp.concatenate(out, axis=0)


def _filtspec_kernel(k_ref, w1_ref, w2_ref, tc_ref, ts_ref, re_ref, im_ref, *, half, orders):
    m1 = half // orders
    x0 = _lane_block_abs_norm(k_ref[...], orders)
    s = _fft_forward(x0, w1_ref[...], w2_ref[...], tc_ref[...], ts_ref[...], half)
    cw = x0.shape[1] // LANE
    re_ref[...] = s[:, :LANE].reshape(cw, orders, m1, LANE)
    im_ref[...] = s[:, LANE:].reshape(cw, orders, m1, LANE)


def hyena_filter_spectrum(taps, bsz, cw):
    orders, m1, c, _ = taps.shape
    mats, tabs, half, w1f = _fft_constants(bsz, m1 // 2)
    assert half == orders * m1
    consts = [w1f, mats[1], tabs[0], tabs[1]]
    out = jax.ShapeDtypeStruct((c, orders, m1, LANE), jnp.float32)
    ospec = pl.BlockSpec((cw, orders, m1, LANE), lambda j: (j, 0, 0, 0))
    return pl.pallas_call(
        functools.partial(_filtspec_kernel, half=half, orders=orders),
        out_shape=(out, out),
        grid=(c // cw,),
        in_specs=[pl.BlockSpec((orders * m1, cw * LANE), lambda j: (0, j))] +
                 [pl.BlockSpec(a.shape, lambda j: (0, 0)) for a in consts],
        out_specs=(ospec, ospec),
        compiler_params=_cparams("parallel"),
        name="hyena_filter_spectrum",
    )(taps.reshape(orders * m1, c * LANE), *consts)


def _dense_dft_constants(seq):
    n = 2 * seq
    t = np.arange(seq)[:, None]
    f = np.arange(n)[None, :]
    ps = 2.0 * np.pi * t * f / n
    fwd = np.block([[np.cos(ps), -np.sin(ps)], [np.sin(ps), np.cos(ps)]])
    inv = fwd.T / n
    tk = np.arange(n)[:, None]
    pk = 2.0 * np.pi * tk * f / n
    fk = np.concatenate([np.cos(pk), -np.sin(pk)], axis=1)
    return [jnp.asarray(m, jnp.bfloat16) for m in (fwd, inv, fk)]


def _ctxspec_kernel(k_ref, fk_ref, re_ref, im_ref):
    orders, slabs = k_ref.shape[0], k_ref.shape[1]
    n = slabs * LANE
    for o in range(orders):
        x = jnp.concatenate([k_ref[o, s] for s in range(slabs)], axis=1)
        x = x / (jnp.sum(jnp.abs(x), axis=1, keepdims=True) + EPS)
        kf = _mxu(x, fk_ref[...])
        re_ref[o] = kf[:, :n]
        im_ref[o] = kf[:, n:]


def ctx_filter_spectrum(taps, cw):
    orders, slabs, c, _ = taps.shape
    n = slabs * LANE
    fk = _dense_dft_constants(n // 2)[2]
    out = jax.ShapeDtypeStruct((orders, c, n), jnp.float32)
    ospec = pl.BlockSpec((orders, cw, n), lambda j: (0, j, 0))
    return pl.pallas_call(
        _ctxspec_kernel,
        out_shape=(out, out),
        grid=(c // cw,),
        in_specs=[pl.BlockSpec((orders, slabs, cw, LANE), lambda j: (0, 0, j, 0)),
                  pl.BlockSpec(fk.shape, lambda j: (0, 0))],
        out_specs=(ospec, ospec),
        compiler_params=_cparams("parallel"),
        name="ctx_filter_spectrum",
    )(taps, fk)


def _ctxconv_kernel(z_ref, g_ref, kre_ref, kim_ref, b_ref, fwd_ref, inv_ref, o_ref):
    bsz, slabs, cw = z_ref.shape[0], z_ref.shape[1], z_ref.shape[2]
    seq = slabs * LANE
    zs = [jnp.concatenate([z_ref[b, s] for s in range(slabs)], axis=1) for b in range(bsz)]
    lhs = jnp.concatenate([jnp.concatenate([zs[2 * p], zs[2 * p + 1]], axis=1) for p in range(bsz // 2)], axis=0)
    s = _mxu(lhs, fwd_ref[...])
    n = 2 * seq
    sr, si = s[:, :n], s[:, n:]
    kr = jnp.concatenate([kre_ref[0]] * (bsz // 2), axis=0)
    ki = jnp.concatenate([kim_ref[0]] * (bsz // 2), axis=0)
    y = _mxu(jnp.concatenate([sr * kr - si * ki, sr * ki + si * kr], axis=1), inv_ref[...])
    bias = jnp.concatenate([b_ref[...]] * slabs, axis=1)
    for b in range(bsz):
        p, m = b // 2, b % 2
        yb = y[p * cw:(p + 1) * cw, m * seq:(m + 1) * seq]
        gate = jnp.concatenate([g_ref[b, s] for s in range(slabs)], axis=1)
        res = gate * (yb + zs[b] * bias)
        for sl in range(slabs):
            o_ref[b, sl] = res[:, sl * LANE:(sl + 1) * LANE]


def ctx_hyena_conv(ut, z_src, z_col, g_col, kre, kim, order, bias, cw):
    bsz, slabs, _, _ = ut.shape
    c = bias.shape[0]
    n = 2 * slabs * LANE
    fwd, inv, _ = _dense_dft_constants(slabs * LANE)
    zb, gb = z_col // cw, g_col // cw
    bb = jnp.broadcast_to(bias[:, None], (c, LANE))
    kspec = pl.BlockSpec((1, cw, n), lambda j: (order, j, 0))
    return pl.pallas_call(
        _ctxconv_kernel,
        out_shape=jax.ShapeDtypeStruct((bsz, slabs, c, LANE), jnp.float32),
        grid=(c // cw,),
        in_specs=[
            pl.BlockSpec((bsz, slabs, cw, LANE), lambda j: (0, 0, zb + j, 0)),
            pl.BlockSpec((bsz, slabs, cw, LANE), lambda j: (0, 0, gb + j, 0)),
            kspec, kspec,
            pl.BlockSpec((cw, LANE), lambda j: (j, 0)),
            pl.BlockSpec(fwd.shape, lambda j: (0, 0)),
            pl.BlockSpec(inv.shape, lambda j: (0, 0)),
        ],
        out_specs=pl.BlockSpec((bsz, slabs, cw, LANE), lambda j: (0, 0, j, 0)),
        compiler_params=_cparams("parallel"),
        name="ctx_hyena_conv",
    )(z_src, ut, kre, kim, bb, fwd, inv)


def _mlp_kernel(x_ref, nw_ref, sc_ref, sh_ref, g_ref, w1_ref, w2_ref, fw_ref, o_ref, h_ref, *, final_norm):
    f = pl.program_id(2)

    @pl.when(f == 0)
    def _():
        _normmod_store(h_ref, x_ref, nw_ref[...], sc_ref[0], sh_ref[0])
        o_ref[...] = jnp.zeros_like(o_ref)

    a = jnp.dot(h_ref[...], w1_ref[...], preferred_element_type=jnp.float32)
    a = jnp.square(jnp.maximum(a, 0.0)).astype(jnp.bfloat16)
    o_ref[0] += jnp.dot(a, w2_ref[...], preferred_element_type=jnp.float32)

    @pl.when(f == pl.num_programs(2) - 1)
    def _():
        y = x_ref[0] + g_ref[0] * o_ref[0]
        if final_norm:
            ms = jnp.mean(y * y, axis=-1, keepdims=True)
            y = y * lax.rsqrt(ms + EPS) * fw_ref[...]
        o_ref[0] = y


def mlp_residual(x, nw, sc, sh, g, w1, w2, layer, fw, tm, tf, final_norm):
    b, l, d = x.shape
    dff = w1.shape[2]
    vec = pl.BlockSpec((1, 1, d), lambda i, j, k: (i, 0, 0))
    row = pl.BlockSpec((1, d), lambda i, j, k: (0, 0))
    return pl.pallas_call(
        functools.partial(_mlp_kernel, final_norm=final_norm),
        out_shape=jax.ShapeDtypeStruct((b, l, d), jnp.float32),
        grid=(b, l // tm, dff // tf),
        in_specs=[
            pl.BlockSpec((1, tm, d), lambda i, j, k: (i, j, 0)),
            row, vec, vec, vec,
            pl.BlockSpec((None, d, tf), lambda i, j, k: (layer, 0, k)),
            pl.BlockSpec((None, tf, d), lambda i, j, k: (layer, k, 0)),
            row,
        ],
        out_specs=pl.BlockSpec((1, tm, d), lambda i, j, k: (i, j, 0)),
        scratch_shapes=[pltpu.VMEM((tm, d), jnp.bfloat16)],
        compiler_params=_cparams("parallel", "parallel", "arbitrary"),
        name="mlp_residual",
    )(x, nw.reshape(1, d), sc.reshape(b, 1, d), sh.reshape(b, 1, d), g.reshape(b, 1, d), w1, w2,
      fw.reshape(1, d))


def _split3(v):
    f32, bf16 = jnp.float32, jnp.bfloat16
    hi = v.astype(bf16)
    r1 = v - hi.astype(f32)
    mid = r1.astype(bf16)
    lo = (r1 - mid.astype(f32)).astype(bf16)
    return jnp.concatenate([hi, mid, lo], axis=1)


def _lane_repeat(v, rep):
    h = v.shape[1]
    row = lax.broadcasted_iota(jnp.int32, (3 * h, h * rep), 0) % h
    col = lax.broadcasted_iota(jnp.int32, (3 * h, h * rep), 1) // rep
    e = (row == col).astype(jnp.bfloat16)
    return jnp.dot(_split3(v), e, preferred_element_type=jnp.float32)


def _silu(v):
    h = 0.5 * v
    return h + h * jnp.tanh(h)


def _softplus(v):
    return jnp.maximum(v, 0.0) + jnp.log1p(jnp.exp(-jnp.abs(v)))


def _token_conv3(u, w, bias, period):
    t, c = u.shape
    pos = lax.broadcasted_iota(jnp.int32, (t, c), 0) % period
    up = jnp.where(pos == 0, 0.0, pltpu.roll(u, 1, 0))
    dn = jnp.where(pos == period - 1, 0.0, pltpu.roll(u, t - 1, 0))
    return bias + w[0:1] * up + w[1:2] * u + w[2:3] * dn


def _ssd_chunk(xs, bm, cm, dtr, s_ref, a_row, dtb_row, reverse, heads, hpg):
    f32, bf16 = jnp.float32, jnp.bfloat16
    q = xs.shape[0]
    p = SSD_HEADDIM
    n = SSD_STATE
    gw = hpg * p
    dt = _softplus(dtr + dtb_row)
    a = dt * (a_row * LOG2E)
    ri = lax.broadcasted_iota(jnp.int32, (q, q), 0)
    ci = lax.broadcasted_iota(jnp.int32, (q, q), 1)
    keep = (ci >= ri) if reverse else (ci <= ri)
    a3 = jnp.dot(keep.astype(bf16), _split3(a), preferred_element_type=f32)
    cs = a3[:, :heads] + a3[:, heads:2 * heads] + a3[:, 2 * heads:]
    dt_rep = _lane_repeat(dt, p)
    cs_rep = _lane_repeat(cs, p)
    cs_wide = _lane_repeat(cs, q)
    end = 0 if reverse else q - 1
    cs_end = cs_rep[end:end + 1]
    xdt = xs * dt_rep
    xw = (xdt * jnp.exp2(cs_end - cs_rep)).astype(bf16)
    ecs = jnp.exp2(cs_rep)
    chunk_decay = jnp.exp2(cs_end)
    xdt_b = xdt.astype(bf16)
    lane = lax.broadcasted_iota(jnp.int32, (q, 2 * p), 1)
    ys = []
    for g in range(SSD_GROUPS):
        bg = bm[:, g * n:(g + 1) * n].astype(bf16)
        cg = cm[:, g * n:(g + 1) * n].astype(bf16)
        cb = lax.dot_general(cg, bg, (((1,), (1,)), ((), ())), preferred_element_type=f32)
        s_old = s_ref[0, g]
        y_off = jnp.dot(cg, s_old.astype(bf16), preferred_element_type=f32) * ecs[:, g * gw:(g + 1) * gw]
        for pr in range(hpg // 2):
            xpair = xdt_b[:, g * gw + pr * 2 * p:g * gw + (pr + 1) * 2 * p]
            acc = None
            for k in range(2):
                h = g * hpg + pr * 2 + k
                csr = cs_wide[:, h * q:(h + 1) * q]
                seg = csr - csr.T
                gmat = (cb * jnp.where(keep, jnp.exp2(seg), 0.0)).astype(bf16)
                xh = jnp.where((lane // p) == k, xpair, jnp.zeros_like(xpair))
                part = jnp.dot(gmat, xh, preferred_element_type=f32)
                acc = part if acc is None else acc + part
            lo = pr * 2 * p
            ys.append(acc + y_off[:, lo:lo + 2 * p])
        upd = jnp.dot(bg.T, xw[:, g * gw:(g + 1) * gw], preferred_element_type=f32)
        s_ref[0, g] = s_old * chunk_decay[:, g * gw:(g + 1) * gw] + upd
    return jnp.concatenate(ys, axis=1)


def _ssd_kernel(z_ref, xs_ref, bc_ref, dt_ref, cwx_ref, cbx_ref, cwb_ref, cbb_ref, dtb_ref, alog_ref, drep_ref,
                nw_ref, initf_ref, initb_ref, y_ref, sf_ref, sb_ref, yb_ref, xc_ref, bcc_ref, *, nb, tb, period,
                heads, hpg):
    j = pl.program_id(1)
    q = SSD_CHUNK
    n = SSD_STATE
    gn = SSD_GROUPS * n
    nchunk = tb // q
    ssd_w = heads * SSD_HEADDIM

    @pl.when(j == 0)
    def _():
        sb_ref[...] = initb_ref[...]

    @pl.when(j == nb)
    def _():
        sf_ref[...] = initf_ref[...]

    a_all = -jnp.exp(alog_ref[...])

    def run(reverse):
        d = 1 if reverse else 0
        blk = (nb - 1 - j) if reverse else (j - nb)
        order = range(nchunk - 1, -1, -1) if reverse else range(nchunk)
        rows = pl.ds(pl.multiple_of(blk * tb, tb), tb)
        if reverse:
            xs_all = _silu(_token_conv3(xs_ref[0], cwx_ref[...], cbx_ref[...], period))
            bc_all = _silu(_token_conv3(bc_ref[0], cwb_ref[...], cbb_ref[...], period)).astype(bcc_ref.dtype)
            xc_ref[rows, :] = xs_all
            bcc_ref[rows, :] = bc_all
        else:
            xs_all = xc_ref[rows, :]
            bc_all = bcc_ref[rows, :]
        for ci in order:
            sl = slice(ci * q, (ci + 1) * q)
            y = _ssd_chunk(xs_all[sl], bc_all[sl, :gn], bc_all[sl, gn:], dt_ref[0, sl, d * heads:(d + 1) * heads],
                           sb_ref if reverse else sf_ref, a_all[d:d + 1], dtb_ref[d:d + 1], reverse, heads, hpg)
            row0 = pl.multiple_of(blk * tb + ci * q, q)
            if reverse:
                yb_ref[pl.ds(row0, q), :] = y
            else:
                y = y + yb_ref[pl.ds(row0, q), :] + drep_ref[...] * xs_all[sl]
                y = y * _silu(z_ref[0, sl, :])
                gw = ssd_w // SSD_GROUPS
                outs = []
                for g in range(SSD_GROUPS):
                    yg = y[:, g * gw:(g + 1) * gw]
                    ms = jnp.mean(yg * yg, axis=-1, keepdims=True)
                    outs.append(yg * lax.rsqrt(ms + EPS) * nw_ref[:, g * gw:(g + 1) * gw])
                y_ref[0, sl, :] = jnp.concatenate(outs, axis=1)

    @pl.when(j < nb)
    def _():
        run(True)

    @pl.when(j >= nb)
    def _():
        run(False)


def ssd_mixer(srcs, cols, conv_w_x, conv_b_x, conv_w_bc, conv_b_bc, dt_bias, a_log, ssd_d, norm_w, init_f, init_b,
              period, tb):
    b, l, _ = srcs[0].shape
    heads = dt_bias.shape[1]
    hpg = heads // SSD_GROUPS
    ssd_w = heads * SSD_HEADDIM
    gn = SSD_GROUPS * SSD_STATE
    nb = l // tb
    oz, ox, obc, odt = cols
    f32 = jnp.float32

    def tok(width, off):
        blk_idx = off // width
        return pl.BlockSpec((1, tb, width),
                            lambda i, j: (i, jnp.where(j < nb, nb - 1 - j, j - nb), blk_idx))

    def tok_fwd(width, off):
        blk_idx = off // width
        return pl.BlockSpec((1, tb, width), lambda i, j: (i, jnp.where(j < nb, 0, j - nb), blk_idx))

    def tok_rev(width, off):
        blk_idx = off // width
        return pl.BlockSpec((1, tb, width), lambda i, j: (i, jnp.maximum(nb - 1 - j, 0), blk_idx))

    def whole(shape):
        return pl.BlockSpec(shape, lambda i, j: (0,) * len(shape))

    st_shape = (b, SSD_GROUPS, SSD_STATE, hpg * SSD_HEADDIM)
    st_spec = pl.BlockSpec((1,) + st_shape[1:], lambda i, j: (i, 0, 0, 0))
    drep = jnp.repeat(ssd_d, SSD_HEADDIM).reshape(1, ssd_w)
    kern = functools.partial(_ssd_kernel, nb=nb, tb=tb, period=period, heads=heads, hpg=hpg)
    return pl.pallas_call(
        kern,
        out_shape=(jax.ShapeDtypeStruct((b, l, ssd_w), f32), jax.ShapeDtypeStruct(st_shape, f32),
                   jax.ShapeDtypeStruct(st_shape, f32)),
        grid=(b, 2 * nb),
        in_specs=[
            tok_fwd(ssd_w, oz), tok_rev(ssd_w, ox), tok_rev(2 * gn, obc), tok(LANE, odt),
            whole((3, ssd_w)), whole((1, ssd_w)), whole((3, 2 * gn)), whole((1, 2 * gn)),
            whole((2, heads)), whole((2, heads)), whole((1, ssd_w)), whole((1, ssd_w)),
            st_spec, st_spec,
        ],
        out_specs=(pl.BlockSpec((1, tb, ssd_w), lambda i, j: (i, jnp.where(j < nb, 0, j - nb), 0)),
                   st_spec, st_spec),
        scratch_shapes=[pltpu.VMEM((l, ssd_w), f32), pltpu.VMEM((l, ssd_w), f32),
                        pltpu.VMEM((l, 2 * gn), jnp.bfloat16)],
        compiler_params=_cparams("parallel", "arbitrary"),
        name="ssd_mixer",
    )(*srcs, conv_w_x, conv_b_x.reshape(1, -1), conv_w_bc, conv_b_bc.reshape(1, -1), dt_bias, a_log, drep,
      norm_w.reshape(1, ssd_w), init_f, init_b)


def _pick(l, pref):
    return pref if l % pref == 0 else l


def kernel(x, c, ctx, c_ctx, w_ada, b_ada, norm1_w, w_in, hy_conv_w, hy_conv_b, filt_w1, filt_b1, filt_freq,
           filt_w2, filt_b2, filt_w3, hy_bias, hy_norm_w, ssd_conv_w, ssd_conv_b, dt_bias, a_log, ssd_d,
           ssd_norm_w, w_out, norm2_w, w_mlp1, w_mlp2, final_norm_w):
    depth = w_in.shape[0]
    bsz, seq, d = x.shape
    hy_w = hy_norm_w.shape[1]
    hy_proj = hy_conv_w.shape[2]
    ssd_w = ssd_norm_w.shape[1]
    ssd_xbc = ssd_conv_w.shape[2]
    heads = ssd_w // SSD_HEADDIM
    hpg = heads // SSD_GROUPS
    ssd_dt = 2 * heads
    o0, o1 = hy_proj, hy_proj + ssd_xbc
    o2 = o1 + ssd_dt
    dt_pad = (-ssd_dt) % LANE
    bf16 = jnp.bfloat16
    ssd_cols = (0, o0, o0 + ssd_w, ssd_w)
    tn = o1 // 3

    n1 = seq // LANE
    n1c = ctx.shape[1] // LANE
    zero_state = jnp.zeros((bsz, SSD_GROUPS, SSD_STATE, hpg * SSD_HEADDIM), jnp.float32)

    rows = 8
    c_all = jnp.zeros((rows, d), jnp.float32).at[:bsz].set(c).at[bsz].set(c_ctx)
    mod_all = ada_mod(c_all, w_ada, b_ada)

    w_a_all, w_b_all = cast_in_proj(w_in, o1, o2, w_in.shape[2], ssd_w + ssd_dt + dt_pad)
    w_out_b, w1_b, w2_b = w_out.astype(bf16), w_mlp1.astype(bf16), w_mlp2.astype(bf16)

    h_ctx = ctx
    tm_l = _pick(seq, 512)
    tm_c = _pick(ctx.shape[1], 256)
    for i in range(depth):
        lp = dict(hy_conv_w=hy_conv_w[i], hy_conv_b=hy_conv_b[i], filt_w1=filt_w1[i], filt_b1=filt_b1[i],
                  filt_freq=filt_freq[i], filt_w2=filt_w2[i], filt_b2=filt_b2[i], filt_w3=filt_w3[i],
                  hy_bias=hy_bias[i], hy_norm_w=hy_norm_w[i], ssd_conv_w=ssd_conv_w[i],
                  ssd_conv_b=ssd_conv_b[i], dt_bias=dt_bias[i], a_log=a_log[i], ssd_d=ssd_d[i],
                  ssd_norm_w=ssd_norm_w[i])
        mod = mod_all[i, :bsz].reshape(bsz, N_MOD, d)
        mod_c = jnp.broadcast_to(mod_all[i, bsz].reshape(1, N_MOD, d), (bsz, N_MOD, d))
        def ssd(pa, pb, init_f, init_b, period):
            cw, cb = lp['ssd_conv_w'], lp['ssd_conv_b']
            return ssd_mixer((pb, pa, pa, pb), ssd_cols, cw[:, :ssd_w], cb[:ssd_w], cw[:, ssd_w:], cb[ssd_w:],
                             lp['dt_bias'], lp['a_log'], lp['ssd_d'], lp['ssd_norm_w'], init_f, init_b, period,
                             _pick(pa.shape[1], 256))

        def filt_taps(length):
            return hyena_filter_taps(length, lp['filt_w1'], lp['filt_b1'], lp['filt_freq'], lp['filt_w2'],
                                     lp['filt_b2'], lp['filt_w3'], hy_w)

        pca, pcb = normmod_matmul(h_ctx, norm1_w[i], mod_c[:, 1], mod_c[:, 0], w_a_all, w_b_all, i, tm_c, tn)
        y_ssd, s_f, s_b = ssd(pca, pcb, zero_state, zero_state, pca.shape[1])
        if i < depth - 1:
            utc = hyena_pre(pca, 0, lp['hy_conv_w'], lp['hy_conv_b'], pca.shape[1], pca.shape[1], 512)
            kre, kim = ctx_filter_spectrum(filt_taps(pca.shape[1]), 128)
            zt = utc
            for o in range(HY_ORDER):
                zt = ctx_hyena_conv(utc, zt, 0, (o + 1) * hy_w, kre, kim, o, lp['hy_bias'][o], 128)
            h_ctx = proj_residual(zt, y_ssd, lp['hy_norm_w'], w_out_b, i, h_ctx, mod_c[:, 2], tm_c, HY_GROUPS)
            h_ctx = mlp_residual(h_ctx, norm2_w[i], mod_c[:, 4], mod_c[:, 3], mod_c[:, 5], w1_b, w2_b, i,
                                 final_norm_w, tm_c, 1024, False)
        pla, plb = normmod_matmul(x, norm1_w[i], mod[:, 1], mod[:, 0], w_a_all, w_b_all, i, _pick(seq, 1024), tn)
        ut = hyena_pre_rows(pla, 0, lp['hy_conv_w'], lp['hy_conv_b'], GRID_W, LANE)
        kre, kim = hyena_filter_spectrum(filt_taps(seq), bsz, 32)
        zt = ut
        for o in range(HY_ORDER):
            zt = hyena_conv(ut, zt, 0, (o + 1) * hy_w, kre, kim, o, lp['hy_bias'][o], bsz, 32, o == HY_ORDER - 1)
        zh_t = zt.reshape(bsz, n1, hy_w, LANE)
        y_ssd, _, _ = ssd(pla, plb, s_f, s_b, GRID_W)
        x = proj_residual(zh_t, y_ssd, lp['hy_norm_w'], w_out_b, i, x, mod[:, 2], tm_l, HY_GROUPS)
        x = mlp_residual(x, norm2_w[i], mod[:, 4], mod[:, 3], mod[:, 5], w1_b, w2_b, i, final_norm_w, tm_l, 1024,
                         i == depth - 1)
    return x
```

```python
import functools
import math

import numpy as np
import jax
import jax.numpy as jnp
from jax import lax
from jax.experimental import pallas as pl
from jax.experimental.pallas import tpu as pltpu

EPS = 1e-6
N_MOD = 6
GRID_W = 64
HY_GROUPS = 8
HY_ORDER = 2
HY_POS_EMB = 33
HY_TARGET = 1e-2
HY_FAST = 0.3
HY_SLOW = 1.5
SSD_HEADDIM = 64
SSD_GROUPS = 2
SSD_STATE = 128
SSD_CHUNK = 128

LANE = 128
LOG2E = 1.4426950408889634
VMEM_LIMIT = 56 * 1024 * 1024


def _cparams(*sem):
    return pltpu.CompilerParams(dimension_semantics=sem, vmem_limit_bytes=VMEM_LIMIT)


def _ada_kernel(c_ref, w_ref, b_ref, o_ref):
    c = c_ref[...]
    a = (c * jax.nn.sigmoid(c)).astype(jnp.bfloat16)
    w = w_ref[0].astype(jnp.bfloat16)
    o_ref[0] = jnp.dot(a, w, preferred_element_type=jnp.float32) + b_ref[0]


def ada_mod(c_all, w_ada, b_ada, tn=1024):
    depth, d, n = w_ada.shape
    r = c_all.shape[0]
    return pl.pallas_call(
        _ada_kernel,
        out_shape=jax.ShapeDtypeStruct((depth, r, n), jnp.float32),
        grid=(depth, n // tn),
        in_specs=[
            pl.BlockSpec((r, d), lambda i, j: (0, 0)),
            pl.BlockSpec((1, d, tn), lambda i, j: (i, 0, j)),
            pl.BlockSpec((1, 1, tn), lambda i, j: (i, 0, j)),
        ],
        out_specs=pl.BlockSpec((1, r, tn), lambda i, j: (i, 0, j)),
        compiler_params=_cparams("parallel", "parallel"),
        name="ada_mod",
    )(c_all, w_ada, b_ada.reshape(depth, 1, n))


def _wcast_kernel(w_ref, a_ref, b_ref, *, na, c0, c1):
    bf16 = jnp.bfloat16
    w = w_ref[0]
    a_ref[0] = w[:, :na].astype(bf16)
    nb = b_ref.shape[2]
    pad = nb - (c1 - c0) - (c0 - na)
    b_ref[0] = jnp.concatenate([w[:, c0:c1], w[:, na:c0], jnp.zeros((w.shape[0], pad), w.dtype)], axis=1).astype(bf16)


def cast_in_proj(w_in, na, c0, c1, nb, tr=256):
    depth, d, n = w_in.shape
    return pl.pallas_call(
        functools.partial(_wcast_kernel, na=na, c0=c0, c1=c1),
        out_shape=(jax.ShapeDtypeStruct((depth, d, na), jnp.bfloat16),
                   jax.ShapeDtypeStruct((depth, d, nb), jnp.bfloat16)),
        grid=(depth, d // tr),
        in_specs=[pl.BlockSpec((1, tr, n), lambda i, j: (i, j, 0))],
        out_specs=(pl.BlockSpec((1, tr, na), lambda i, j: (i, j, 0)),
                   pl.BlockSpec((1, tr, nb), lambda i, j: (i, j, 0))),
        compiler_params=_cparams("parallel", "parallel"),
        name="cast_in_proj",
    )(w_in)


NORM_ROWS = 16


def _normmod_store(h_ref, x_ref, nw, sc, sh):
    w = nw * (1.0 + sc)

    def body(i, carry):
        r0 = pl.multiple_of(i * NORM_ROWS, NORM_ROWS)
        xb = x_ref[0, pl.ds(r0, NORM_ROWS), :]
        ms = jnp.mean(xb * xb, axis=-1, keepdims=True)
        h_ref[pl.ds(r0, NORM_ROWS), :] = ((xb * lax.rsqrt(ms + EPS)) * w + sh).astype(h_ref.dtype)
        return carry

    lax.fori_loop(0, h_ref.shape[0] // NORM_ROWS, body, 0, unroll=8)


def _normmod_kernel(x_ref, nw_ref, sc_ref, sh_ref, h_ref):
    _normmod_store(h_ref.at[0], x_ref, nw_ref[...], sc_ref[0], sh_ref[0])


def _proj_kernel(h_ref, w_ref, o_ref):
    o_ref[0] = jnp.dot(h_ref[0], w_ref[...], preferred_element_type=jnp.float32)


def _proj_matmul(h, w, layer, tm, tn):
    b, l, d = h.shape
    n = w.shape[2]
    return pl.pallas_call(
        _proj_kernel,
        out_shape=jax.ShapeDtypeStruct((b, l, n), jnp.float32),
        grid=(n // tn, b, l // tm),
        in_specs=[
            pl.BlockSpec((1, tm, d), lambda k, i, j: (i, j, 0)),
            pl.BlockSpec((None, d, tn), lambda k, i, j: (layer, 0, k)),
        ],
        out_specs=pl.BlockSpec((1, tm, tn), lambda k, i, j: (i, j, k)),
        compiler_params=_cparams("parallel", "parallel", "parallel"),
        name="in_proj",
    )(h, w)


def normmod_matmul(x, nw, sc, sh, wa, wb, layer, tm, tn):
    b, l, d = x.shape
    tn_norm = min(tm, 512)
    h = pl.pallas_call(
        _normmod_kernel,
        out_shape=jax.ShapeDtypeStruct((b, l, d), jnp.bfloat16),
        grid=(b, l // tn_norm),
        in_specs=[
            pl.BlockSpec((1, tn_norm, d), lambda i, j: (i, j, 0)),
            pl.BlockSpec((1, d), lambda i, j: (0, 0)),
            pl.BlockSpec((1, 1, d), lambda i, j: (i, 0, 0)),
            pl.BlockSpec((1, 1, d), lambda i, j: (i, 0, 0)),
        ],
        out_specs=pl.BlockSpec((1, tn_norm, d), lambda i, j: (i, j, 0)),
        compiler_params=_cparams("parallel", "parallel"),
        name="normmod",
    )(x, nw.reshape(1, d), sc.reshape(b, 1, d), sh.reshape(b, 1, d))
    return _proj_matmul(h, wa, layer, tm, tn), _proj_matmul(h, wb, layer, tm, wb.shape[2])


def _proj_res_kernel(zh_ref, ys_ref, nwh_ref, wh_ref, ws_ref, x_ref, g_ref, o_ref, *, groups):
    bf16 = jnp.bfloat16
    slabs, hy_w = zh_ref.shape[1], zh_ref.shape[2]
    gs = hy_w // groups
    rows = []
    for s in range(slabs):
        cols = []
        for gi in range(groups):
            zg = zh_ref[0, s, gi * gs:(gi + 1) * gs, :]
            ms = jnp.mean(zg * zg, axis=0, keepdims=True)
            cols.append((zg * lax.rsqrt(ms + EPS)).T)
        rows.append(jnp.concatenate(cols, axis=1))
    yh = (jnp.concatenate(rows, axis=0) * nwh_ref[...]).astype(bf16)
    acc = jnp.dot(yh, wh_ref[...], preferred_element_type=jnp.float32)
    acc += jnp.dot(ys_ref[0].astype(bf16), ws_ref[...], preferred_element_type=jnp.float32)
    o_ref[0] = x_ref[0] + g_ref[0] * acc


def proj_residual(zh_t, y_ssd, hy_norm_w, w_out, layer, x, g, tm, groups):
    b, n1, hy_w, _ = zh_t.shape
    l, ssd_w = y_ssd.shape[1], y_ssd.shape[2]
    n = w_out.shape[2]
    assert hy_w % ssd_w == 0
    return pl.pallas_call(
        functools.partial(_proj_res_kernel, groups=groups),
        out_shape=jax.ShapeDtypeStruct((b, l, n), jnp.float32),
        grid=(b, l // tm),
        in_specs=[
            pl.BlockSpec((1, tm // LANE, hy_w, LANE), lambda i, j: (i, j, 0, 0)),
            pl.BlockSpec((1, tm, ssd_w), lambda i, j: (i, j, 0)),
            pl.BlockSpec((1, hy_w), lambda i, j: (0, 0)),
            pl.BlockSpec((None, hy_w, n), lambda i, j: (layer, 0, 0)),
            pl.BlockSpec((None, ssd_w, n), lambda i, j: (layer, hy_w // ssd_w, 0)),
            pl.BlockSpec((1, tm, n), lambda i, j: (i, j, 0)),
            pl.BlockSpec((1, 1, n), lambda i, j: (i, 0, 0)),
        ],
        out_specs=pl.BlockSpec((1, tm, n), lambda i, j: (i, j, 0)),
        compiler_params=_cparams("parallel", "parallel"),
        name="proj_residual",
    )(zh_t, y_ssd, hy_norm_w.reshape(1, hy_w), w_out, w_out, x, g.reshape(b, 1, n))


def _hypre_kernel(u_ref, w_ref, b_ref, o_ref, *, period):
    u = _token_conv3(u_ref[0], w_ref[...], b_ref[...], period)
    tb, cw = u.shape
    for s in range(tb // LANE):
        for j in range(cw // LANE):
            o_ref[0, s, j * LANE:(j + 1) * LANE, :] = u[s * LANE:(s + 1) * LANE, j * LANE:(j + 1) * LANE].T


def hyena_pre(p, col0, conv_w, conv_b, period, tb, cw):
    b, l, _ = p.shape
    c = conv_w.shape[1]
    blk0 = col0 // cw
    return pl.pallas_call(
        functools.partial(_hypre_kernel, period=period),
        out_shape=jax.ShapeDtypeStruct((b, l // LANE, c, LANE), jnp.float32),
        grid=(b, l // tb, c // cw),
        in_specs=[
            pl.BlockSpec((1, tb, cw), lambda i, j, k: (i, j, blk0 + k)),
            pl.BlockSpec((3, cw), lambda i, j, k: (0, k)),
            pl.BlockSpec((1, cw), lambda i, j, k: (0, k)),
        ],
        out_specs=pl.BlockSpec((1, tb // LANE, cw, LANE), lambda i, j, k: (i, j, k, 0)),
        compiler_params=_cparams("parallel", "parallel", "parallel"),
        name="hyena_pre",
    )(p, conv_w, conv_b.reshape(1, c))


def _hypre_rows_kernel(u_ref, w_ref, b_ref, o_ref, s_ref, *, period):
    l, cw = u_ref.shape[1], u_ref.shape[2]
    n1 = l // LANE
    assert LANE % period == 0
    w, bias = w_ref[...], b_ref[...]

    for s in range(n1):
        u = _token_conv3(u_ref[0, s * LANE:(s + 1) * LANE, :], w, bias, period)
        for j in range(cw // LANE):
            s_ref[s * cw + j * LANE:s * cw + (j + 1) * LANE, :] = u[:, j * LANE:(j + 1) * LANE].T
    for c in range(cw):
        o_ref[:, c * LANE:(c + 1) * LANE] = s_ref[pl.ds(c, n1, stride=cw), :]


def hyena_pre_rows(p, col0, conv_w, conv_b, period, cw):
    b, l, _ = p.shape
    c = conv_w.shape[1]
    n1 = l // LANE
    blk0 = col0 // cw
    return pl.pallas_call(
        functools.partial(_hypre_rows_kernel, period=period),
        out_shape=jax.ShapeDtypeStruct((b * n1, c * LANE), jnp.float32),
        grid=(b, c // cw),
        in_specs=[
            pl.BlockSpec((1, l, cw), lambda i, k: (i, 0, blk0 + k)),
            pl.BlockSpec((3, cw), lambda i, k: (0, k)),
            pl.BlockSpec((1, cw), lambda i, k: (0, k)),
        ],
        out_specs=pl.BlockSpec((n1, cw * LANE), lambda i, k: (i, k)),
        scratch_shapes=[pltpu.VMEM((n1 * cw, LANE), jnp.float32)],
        compiler_params=_cparams("parallel", "parallel"),
        name="hyena_pre_rows",
    )(p, conv_w, conv_b.reshape(1, c))


def _fft_constants(bsz, n1):
    m1 = 2 * n1
    n = m1 * LANE
    pairs = bsz // 2
    half = pairs * m1
    r = bsz * n1
    t1 = np.arange(n1)[:, None]
    f1 = np.arange(m1)[None, :]
    th = 2.0 * np.pi * t1 * f1 / m1
    w1 = np.zeros((r, 2 * half))
    for pr in range(pairs):
        re = slice(pr * m1, (pr + 1) * m1)
        im = slice(half + pr * m1, half + (pr + 1) * m1)
        ra = slice((2 * pr) * n1, (2 * pr + 1) * n1)
        rb = slice((2 * pr + 1) * n1, (2 * pr + 2) * n1)
        w1[ra, re], w1[ra, im] = np.cos(th), -np.sin(th)
        w1[rb, re], w1[rb, im] = np.sin(th), np.cos(th)
    w4 = w1.T / n
    t2 = np.arange(LANE)[:, None]
    ps = 2.0 * np.pi * t2 * np.arange(LANE)[None, :] / LANE
    w2 = np.block([[np.cos(ps), -np.sin(ps)], [np.sin(ps), np.cos(ps)]])
    ph = 2.0 * np.pi * t2 * f1 / n
    tc = np.tile(np.cos(ph), (1, pairs))
    ts = np.tile(np.sin(ph), (1, pairs))
    orders = half // m1
    tf = 2.0 * np.pi * np.arange(m1)[:, None] * f1 / m1
    w1f = np.zeros((orders * m1, 2 * half))
    for o in range(orders):
        w1f[o * m1:(o + 1) * m1, o * m1:(o + 1) * m1] = np.cos(tf)
        w1f[o * m1:(o + 1) * m1, half + o * m1:half + (o + 1) * m1] = -np.sin(tf)
    mats = [jnp.asarray(m, jnp.bfloat16) for m in (w1, w2, w2.T, w4)]
    tabs = [jnp.asarray(m, jnp.float32) for m in (tc, ts, tc.T, ts.T)]
    return mats, tabs, half, jnp.asarray(w1f, jnp.bfloat16)


def _mxu(a, w):
    return jnp.dot(a.astype(jnp.bfloat16), w, preferred_element_type=jnp.float32)


def _fft_forward(x0, w1, w2, tc, ts, half):
    cw = x0.shape[1] // LANE
    a1 = _mxu(x0.T, w1)
    lhs2 = []
    for c in range(cw):
        ar = a1[c * LANE:(c + 1) * LANE, :half]
        ai = a1[c * LANE:(c + 1) * LANE, half:]
        lhs2.append(jnp.concatenate([(ar * tc + ai * ts).T, (ai * tc - ar * ts).T], axis=1))
    return _mxu(jnp.concatenate(lhs2, axis=0), w2)


def _hyconv_kernel(z_ref, g_ref, kre_ref, kim_ref, brep_ref, w1_ref, w2_ref, w2i_ref, w4_ref, tc_ref, ts_ref,
                   tct_ref, tst_ref, o_ref, *, half):
    x0 = z_ref[...]
    cw = x0.shape[1] // LANE
    tct, tst = tct_ref[...], tst_ref[...]
    s = _fft_forward(x0, w1_ref[...], w2_ref[...], tc_ref[...], ts_ref[...], half)
    sr, si = s[:, :LANE], s[:, LANE:]
    m1 = kre_ref.shape[2]
    pairs = half // m1
    shape4 = (cw, pairs, m1, LANE)
    sr, si = sr.reshape(shape4), si.reshape(shape4)
    kr, ki = kre_ref[:, 0][:, None], kim_ref[:, 0][:, None]
    y2 = jnp.concatenate([(sr * kr - si * ki).reshape(cw * half, LANE),
                          (sr * ki + si * kr).reshape(cw * half, LANE)], axis=1)
    bq = _mxu(y2, w2i_ref[...])
    lhs4 = []
    for c in range(cw):
        br = bq[c * half:(c + 1) * half, :LANE]
        bi = bq[c * half:(c + 1) * half, LANE:]
        lhs4.append(jnp.concatenate([(br * tct - bi * tst).T, (bi * tct + br * tst).T], axis=1))
    lhs4 = jnp.concatenate(lhs4, axis=0)
    yt = _mxu(lhs4, w4_ref[...])
    if len(o_ref.shape) == 2:
        o_ref[...] = g_ref[...] * (yt.T + x0 * brep_ref[...])
    else:
        for c in range(cw):
            sl = slice(c * LANE, (c + 1) * LANE)
            o_ref[:, c, :] = g_ref[:, sl] * (yt[sl].T + x0[:, sl] * brep_ref[:, sl])


def hyena_conv(ut, z_src, z_col, g_col, kre, kim, order, bias, bsz, cw, channel_tiles):
    r = ut.shape[0]
    n1 = r // bsz
    c = bias.shape[0]
    mats, tabs, half, _ = _fft_constants(bsz, n1)
    blk = cw * LANE
    zb, gb = z_col // cw, g_col // cw
    brep = jnp.repeat(bias, LANE).reshape(1, c * LANE)
    kspec = pl.BlockSpec((cw, 1, 2 * n1, LANE), lambda j: (j, order, 0, 0))

    def const(a):
        return pl.BlockSpec(a.shape, lambda j: (0, 0))

    return pl.pallas_call(
        functools.partial(_hyconv_kernel, half=half),
        out_shape=jax.ShapeDtypeStruct((r, c, LANE) if channel_tiles else (r, c * LANE), jnp.float32),
        grid=(c // cw,),
        in_specs=[
            pl.BlockSpec((r, blk), lambda j: (0, zb + j)),
            pl.BlockSpec((r, blk), lambda j: (0, gb + j)),
            kspec, kspec,
            pl.BlockSpec((1, blk), lambda j: (0, j)),
        ] + [const(a) for a in mats + tabs],
        out_specs=(pl.BlockSpec((r, cw, LANE), lambda j: (0, j, 0)) if channel_tiles
                   else pl.BlockSpec((r, blk), lambda j: (0, j))),
        compiler_params=_cparams("parallel"),
        name="hyena_conv",
    )(z_src, ut, kre, kim, brep, *mats, *tabs)


_HI = lax.Precision.HIGHEST


def _filtgen_kernel(a0_ref, wc_ref, ws_ref, fb1_ref, freq_ref, fw2t_ref, fb2_ref, fw3t_ref, dl_ref, o_ref, *,
                    seq, bands):
    f32 = jnp.float32
    s = pl.program_id(0)
    t = s * LANE + lax.broadcasted_iota(jnp.int32, (1, LANE), 1)
    pos = jnp.where(t < seq, t, 2 * seq - t).astype(f32)
    tt = pos / (seq - 1.0)
    ang = (2.0 * math.pi / seq) * pos
    j = lax.broadcasted_iota(jnp.int32, (bands, 1), 0).astype(f32)
    fj = 1e-4 + j * ((bands - 1.0 - 1e-4) / (bands - 1.0))
    fa = fj * ang
    freq = freq_ref[...]
    pre = a0_ref[...] * tt + jnp.dot(wc_ref[...], jnp.cos(fa), precision=_HI, preferred_element_type=f32) \
        - jnp.dot(ws_ref[...], jnp.sin(fa), precision=_HI, preferred_element_type=f32) + fb1_ref[...]
    h1 = jnp.sin(freq * pre)
    h2 = jnp.sin(freq * (jnp.dot(fw2t_ref[...], h1, precision=_HI, preferred_element_type=f32) + fb2_ref[...]))
    orders, _, c, hid3 = fw3t_ref.shape
    h2_hi = h2.astype(jnp.bfloat16)
    h2_lo = (h2 - h2_hi.astype(f32)).astype(jnp.bfloat16)
    h = jnp.dot(fw3t_ref[:, 0].reshape(orders * c, hid3), jnp.concatenate([h2_hi, h2_lo, h2_hi], axis=0),
                preferred_element_type=f32)
    h = h * jnp.exp(-tt * dl_ref[...])
    h = jnp.where(t == seq, 0.0, h)
    o_ref[:, 0] = h.reshape(orders, c, LANE)


def hyena_filter_taps(seq, fw1, fb1, freq, fw2, fb2, fw3, hy_w):
    hid = fw2.shape[0]
    bands = (HY_POS_EMB - 1) // 2
    orders = fw3.shape[1] // (2 * hy_w)
    slabs = 2 * seq // LANE
    col = lambda v: v.reshape(-1, 1)
    fw3t = fw3.T.reshape(orders, 2, hy_w, hid)
    w_hi = fw3t.astype(jnp.bfloat16)
    w_lo = (fw3t - w_hi.astype(jnp.float32)).astype(jnp.bfloat16)
    fw3t = jnp.concatenate([w_hi, w_hi, w_lo], axis=-1)
    deltas = jnp.abs(jnp.linspace(math.log(HY_TARGET) / HY_SLOW, math.log(HY_TARGET) / HY_FAST, hy_w,
                                  dtype=jnp.float32))
    dl = jnp.tile(deltas, orders).reshape(-1, 1)

    def whole(a):
        return pl.BlockSpec(a.shape, lambda s: (0,) * a.ndim)

    args = [col(fw1[0]), fw1[1:1 + bands].T, fw1[1 + bands:].T, col(fb1), col(freq), fw2.T, col(fb2)]
    return pl.pallas_call(
        functools.partial(_filtgen_kernel, seq=seq, bands=bands),
        out_shape=jax.ShapeDtypeStruct((orders, slabs, hy_w, LANE), jnp.float32),
        grid=(slabs,),
        in_specs=[whole(a) for a in args] + [
            pl.BlockSpec((orders, 1, hy_w, 3 * hid), lambda s: (0, s // (slabs // 2), 0, 0)),
            whole(dl),
        ],
        out_specs=pl.BlockSpec((orders, 1, hy_w, LANE), lambda s: (0, s, 0, 0)),
        compiler_params=_cparams("parallel"),
        name="hyena_filter_taps",
    )(*args, fw3t, dl)


def _lane_block_abs_norm(x, orders):
    rows_per = x.shape[0] // orders
    cw = x.shape[1] // LANE
    out = []
    for o in range(orders):
        xo = x[o * rows_per:(o + 1) * rows_per]
        a = jnp.sum(jnp.abs(xo), axis=0, keepdims=True)
        inv = [jnp.broadcast_to(1.0 / (jnp.sum(a[:, c * LANE:(c + 1) * LANE], axis=1, keepdims=True) + EPS),
                                (1, LANE)) for c in range(cw)]
        out.append(xo * jnp.concatenate(inv, axis=1))
    return jnp.concatenate(out, axis=0)


def _filtspec_kernel(k_ref, w1_ref, w2_ref, tc_ref, ts_ref, re_ref, im_ref, *, half, orders):
    m1 = half // orders
    x0 = _lane_block_abs_norm(k_ref[...], orders)
    s = _fft_forward(x0, w1_ref[...], w2_ref[...], tc_ref[...], ts_ref[...], half)
    cw = x0.shape[1] // LANE
    re_ref[...] = s[:, :LANE].reshape(cw, orders, m1, LANE)
    im_ref[...] = s[:, LANE:].reshape(cw, orders, m1, LANE)


def hyena_filter_spectrum(taps, bsz, cw):
    orders, m1, c, _ = taps.shape
    mats, tabs, half, w1f = _fft_constants(bsz, m1 // 2)
    assert half == orders * m1
    consts = [w1f, mats[1], tabs[0], tabs[1]]
    out = jax.ShapeDtypeStruct((c, orders, m1, LANE), jnp.float32)
    ospec = pl.BlockSpec((cw, orders, m1, LANE), lambda j: (j, 0, 0, 0))
    return pl.pallas_call(
        functools.partial(_filtspec_kernel, half=half, orders=orders),
        out_shape=(out, out),
        grid=(c // cw,),
        in_specs=[pl.BlockSpec((orders * m1, cw * LANE), lambda j: (0, j))] +
                 [pl.BlockSpec(a.shape, lambda j: (0, 0)) for a in consts],
        out_specs=(ospec, ospec),
        compiler_params=_cparams("parallel"),
        name="hyena_filter_spectrum",
    )(taps.reshape(orders * m1, c * LANE), *consts)


def _dense_dft_constants(seq):
    n = 2 * seq
    t = np.arange(seq)[:, None]
    f = np.arange(n)[None, :]
    ps = 2.0 * np.pi * t * f / n
    fwd = np.block([[np.cos(ps), -np.sin(ps)], [np.sin(ps), np.cos(ps)]])
    inv = fwd.T / n
    tk = np.arange(n)[:, None]
    pk = 2.0 * np.pi * tk * f / n
    fk = np.concatenate([np.cos(pk), -np.sin(pk)], axis=1)
    return [jnp.asarray(m, jnp.bfloat16) for m in (fwd, inv, fk)]


def _ctxspec_kernel(k_ref, fk_ref, re_ref, im_ref):
    orders, slabs = k_ref.shape[0], k_ref.shape[1]
    n = slabs * LANE
    for o in range(orders):
        x = jnp.concatenate([k_ref[o, s] for s in range(slabs)], axis=1)
        x = x / (jnp.sum(jnp.abs(x), axis=1, keepdims=True) + EPS)
        kf = _mxu(x, fk_ref[...])
        re_ref[o] = kf[:, :n]
        im_ref[o] = kf[:, n:]


def ctx_filter_spectrum(taps, cw):
    orders, slabs, c, _ = taps.shape
    n = slabs * LANE
    fk = _dense_dft_constants(n // 2)[2]
    out = jax.ShapeDtypeStruct((orders, c, n), jnp.float32)
    ospec = pl.BlockSpec((orders, cw, n), lambda j: (0, j, 0))
    return pl.pallas_call(
        _ctxspec_kernel,
        out_shape=(out, out),
        grid=(c // cw,),
        in_specs=[pl.BlockSpec((orders, slabs, cw, LANE), lambda j: (0, 0, j, 0)),
                  pl.BlockSpec(fk.shape, lambda j: (0, 0))],
        out_specs=(ospec, ospec),
        compiler_params=_cparams("parallel"),
        name="ctx_filter_spectrum",
    )(taps, fk)


def _ctxconv_kernel(z_ref, g_ref, kre_ref, kim_ref, b_ref, fwd_ref, inv_ref, o_ref):
    bsz, slabs, cw = z_ref.shape[0], z_ref.shape[1], z_ref.shape[2]
    seq = slabs * LANE
    zs = [jnp.concatenate([z_ref[b, s] for s in range(slabs)], axis=1) for b in range(bsz)]
    lhs = jnp.concatenate([jnp.concatenate([zs[2 * p], zs[2 * p + 1]], axis=1) for p in range(bsz // 2)], axis=0)
    s = _mxu(lhs, fwd_ref[...])
    n = 2 * seq
    sr, si = s[:, :n], s[:, n:]
    kr = jnp.concatenate([kre_ref[0]] * (bsz // 2), axis=0)
    ki = jnp.concatenate([kim_ref[0]] * (bsz // 2), axis=0)
    y = _mxu(jnp.concatenate([sr * kr - si * ki, sr * ki + si * kr], axis=1), inv_ref[...])
    bias = jnp.concatenate([b_ref[...]] * slabs, axis=1)
    for b in range(bsz):
        p, m = b // 2, b % 2
        yb = y[p * cw:(p + 1) * cw, m * seq:(m + 1) * seq]
        gate = jnp.concatenate([g_ref[b, s] for s in range(slabs)], axis=1)
        res = gate * (yb + zs[b] * bias)
        for sl in range(slabs):
            o_ref[b, sl] = res[:, sl * LANE:(sl + 1) * LANE]


def ctx_hyena_conv(ut, z_src, z_col, g_col, kre, kim, order, bias, cw):
    bsz, slabs, _, _ = ut.shape
    c = bias.shape[0]
    n = 2 * slabs * LANE
    fwd, inv, _ = _dense_dft_constants(slabs * LANE)
    zb, gb = z_col // cw, g_col // cw
    bb = jnp.broadcast_to(bias[:, None], (c, LANE))
    kspec = pl.BlockSpec((1, cw, n), lambda j: (order, j, 0))
    return pl.pallas_call(
        _ctxconv_kernel,
        out_shape=jax.ShapeDtypeStruct((bsz, slabs, c, LANE), jnp.float32),
        grid=(c // cw,),
        in_specs=[
            pl.BlockSpec((bsz, slabs, cw, LANE), lambda j: (0, 0, zb + j, 0)),
            pl.BlockSpec((bsz, slabs, cw, LANE), lambda j: (0, 0, gb + j, 0)),
            kspec, kspec,
            pl.BlockSpec((cw, LANE), lambda j: (j, 0)),
            pl.BlockSpec(fwd.shape, lambda j: (0, 0)),
            pl.BlockSpec(inv.shape, lambda j: (0, 0)),
        ],
        out_specs=pl.BlockSpec((bsz, slabs, cw, LANE), lambda j: (0, 0, j, 0)),
        compiler_params=_cparams("parallel"),
        name="ctx_hyena_conv",
    )(z_src, ut, kre, kim, bb, fwd, inv)


def _mlp_kernel(x_ref, nw_ref, sc_ref, sh_ref, g_ref, w1_ref, w2_ref, fw_ref, o_ref, h_ref, *, final_norm):
    f = pl.program_id(2)

    @pl.when(f == 0)
    def _():
        _normmod_store(h_ref, x_ref, nw_ref[...], sc_ref[0], sh_ref[0])
        o_ref[...] = jnp.zeros_like(o_ref)

    a = jnp.dot(h_ref[...], w1_ref[...], preferred_element_type=jnp.float32)
    a = jnp.square(jnp.maximum(a, 0.0)).astype(jnp.bfloat16)
    o_ref[0] += jnp.dot(a, w2_ref[...], preferred_element_type=jnp.float32)

    @pl.when(f == pl.num_programs(2) - 1)
    def _():
        y = x_ref[0] + g_ref[0] * o_ref[0]
        if final_norm:
            ms = jnp.mean(y * y, axis=-1, keepdims=True)
            y = y * lax.rsqrt(ms + EPS) * fw_ref[...]
        o_ref[0] = y


def mlp_residual(x, nw, sc, sh, g, w1, w2, layer, fw, tm, tf, final_norm):
    b, l, d = x.shape
    dff = w1.shape[2]
    vec = pl.BlockSpec((1, 1, d), lambda i, j, k: (i, 0, 0))
    row = pl.BlockSpec((1, d), lambda i, j, k: (0, 0))
    return pl.pallas_call(
        functools.partial(_mlp_kernel, final_norm=final_norm),
        out_shape=jax.ShapeDtypeStruct((b, l, d), jnp.float32),
        grid=(b, l // tm, dff // tf),
        in_specs=[
            pl.BlockSpec((1, tm, d), lambda i, j, k: (i, j, 0)),
            row, vec, vec, vec,
            pl.BlockSpec((None, d, tf), lambda i, j, k: (layer, 0, k)),
            pl.BlockSpec((None, tf, d), lambda i, j, k: (layer, k, 0)),
            row,
        ],
        out_specs=pl.BlockSpec((1, tm, d), lambda i, j, k: (i, j, 0)),
        scratch_shapes=[pltpu.VMEM((tm, d), jnp.bfloat16)],
        compiler_params=_cparams("parallel", "parallel", "arbitrary"),
        name="mlp_residual",
    )(x, nw.reshape(1, d), sc.reshape(b, 1, d), sh.reshape(b, 1, d), g.reshape(b, 1, d), w1, w2,
      fw.reshape(1, d))


def _split3(v):
    f32, bf16 = jnp.float32, jnp.bfloat16
    hi = v.astype(bf16)
    r1 = v - hi.astype(f32)
    mid = r1.astype(bf16)
    lo = (r1 - mid.astype(f32)).astype(bf16)
    return jnp.concatenate([hi, mid, lo], axis=1)


def _lane_repeat(v, rep):
    h = v.shape[1]
    row = lax.broadcasted_iota(jnp.int32, (3 * h, h * rep), 0) % h
    col = lax.broadcasted_iota(jnp.int32, (3 * h, h * rep), 1) // rep
    e = (row == col).astype(jnp.bfloat16)
    return jnp.dot(_split3(v), e, preferred_element_type=jnp.float32)


def _silu(v):
    h = 0.5 * v
    return h + h * jnp.tanh(h)


def _softplus(v):
    return jnp.maximum(v, 0.0) + jnp.log1p(jnp.exp(-jnp.abs(v)))


def _token_conv3(u, w, bias, period):
    t, c = u.shape
    pos = lax.broadcasted_iota(jnp.int32, (t, c), 0) % period
    up = jnp.where(pos == 0, 0.0, pltpu.roll(u, 1, 0))
    dn = jnp.where(pos == period - 1, 0.0, pltpu.roll(u, t - 1, 0))
    return bias + w[0:1] * up + w[1:2] * u + w[2:3] * dn


def _ssd_chunk(xs, bm, cm, dtr, s_ref, a_row, dtb_row, reverse, heads, hpg):
    f32, bf16 = jnp.float32, jnp.bfloat16
    q = xs.shape[0]
    p = SSD_HEADDIM
    n = SSD_STATE
    gw = hpg * p
    dt = _softplus(dtr + dtb_row)
    a = dt * (a_row * LOG2E)
    ri = lax.broadcasted_iota(jnp.int32, (q, q), 0)
    ci = lax.broadcasted_iota(jnp.int32, (q, q), 1)
    keep = (ci >= ri) if reverse else (ci <= ri)
    a3 = jnp.dot(keep.astype(bf16), _split3(a), preferred_element_type=f32)
    cs = a3[:, :heads] + a3[:, heads:2 * heads] + a3[:, 2 * heads:]
    dt_rep = _lane_repeat(dt, p)
    cs_rep = _lane_repeat(cs, p)
    cs_wide = _lane_repeat(cs, q)
    end = 0 if reverse else q - 1
    cs_end = cs_rep[end:end + 1]
    xdt = xs * dt_rep
    xw = (xdt * jnp.exp2(cs_end - cs_rep)).astype(bf16)
    ecs = jnp.exp2(cs_rep)
    chunk_decay = jnp.exp2(cs_end)
    xdt_b = xdt.astype(bf16)
    lane = lax.broadcasted_iota(jnp.int32, (q, 2 * p), 1)
    ys = []
    for g in range(SSD_GROUPS):
        bg = bm[:, g * n:(g + 1) * n].astype(bf16)
        cg = cm[:, g * n:(g + 1) * n].astype(bf16)
        cb = lax.dot_general(cg, bg, (((1,), (1,)), ((), ())), preferred_element_type=f32)
        s_old = s_ref[0, g]
        y_off = jnp.dot(cg, s_old.astype(bf16), preferred_element_type=f32) * ecs[:, g * gw:(g + 1) * gw]
        for pr in range(hpg // 2):
            xpair = xdt_b[:, g * gw + pr * 2 * p:g * gw + (pr + 1) * 2 * p]
            acc = None
            for k in range(2):
                h = g * hpg + pr * 2 + k
                csr = cs_wide[:, h * q:(h + 1) * q]
                seg = csr - csr.T
                gmat = (cb * jnp.where(keep, jnp.exp2(seg), 0.0)).astype(bf16)
                xh = jnp.where((lane // p) == k, xpair, jnp.zeros_like(xpair))
                part = jnp.dot(gmat, xh, preferred_element_type=f32)
                acc = part if acc is None else acc + part
            lo = pr * 2 * p
            ys.append(acc + y_off[:, lo:lo + 2 * p])
        upd = jnp.dot(bg.T, xw[:, g * gw:(g + 1) * gw], preferred_element_type=f32)
        s_ref[0, g] = s_old * chunk_decay[:, g * gw:(g + 1) * gw] + upd
    return jnp.concatenate(ys, axis=1)


def _ssd_kernel(z_ref, xs_ref, bc_ref, dt_ref, cwx_ref, cbx_ref, cwb_ref, cbb_ref, dtb_ref, alog_ref, drep_ref,
                nw_ref, initf_ref, initb_ref, y_ref, sf_ref, sb_ref, yb_ref, xc_ref, bcc_ref, *, nb, tb, period,
                heads, hpg):
    j = pl.program_id(1)
    q = SSD_CHUNK
    n = SSD_STATE
    gn = SSD_GROUPS * n
    nchunk = tb // q
    ssd_w = heads * SSD_HEADDIM

    @pl.when(j == 0)
    def _():
        sb_ref[...] = initb_ref[...]

    @pl.when(j == nb)
    def _():
        sf_ref[...] = initf_ref[...]

    a_all = -jnp.exp(alog_ref[...])

    def run(reverse):
        d = 1 if reverse else 0
        blk = (nb - 1 - j) if reverse else (j - nb)
        order = range(nchunk - 1, -1, -1) if reverse else range(nchunk)
        rows = pl.ds(pl.multiple_of(blk * tb, tb), tb)
        if reverse:
            xs_all = _silu(_token_conv3(xs_ref[0], cwx_ref[...], cbx_ref[...], period))
            bc_all = _silu(_token_conv3(bc_ref[0], cwb_ref[...], cbb_ref[...], period)).astype(bcc_ref.dtype)
            xc_ref[rows, :] = xs_all
            bcc_ref[rows, :] = bc_all
        else:
            xs_all = xc_ref[rows, :]
            bc_all = bcc_ref[rows, :]
        for ci in order:
            sl = slice(ci * q, (ci + 1) * q)
            y = _ssd_chunk(xs_all[sl], bc_all[sl, :gn], bc_all[sl, gn:], dt_ref[0, sl, d * heads:(d + 1) * heads],
                           sb_ref if reverse else sf_ref, a_all[d:d + 1], dtb_ref[d:d + 1], reverse, heads, hpg)
            row0 = pl.multiple_of(blk * tb + ci * q, q)
            if reverse:
                yb_ref[pl.ds(row0, q), :] = y
            else:
                y = y + yb_ref[pl.ds(row0, q), :] + drep_ref[...] * xs_all[sl]
                y = y * _silu(z_ref[0, sl, :])
                gw = ssd_w // SSD_GROUPS
                outs = []
                for g in range(SSD_GROUPS):
                    yg = y[:, g * gw:(g + 1) * gw]
                    ms = jnp.mean(yg * yg, axis=-1, keepdims=True)
                    outs.append(yg * lax.rsqrt(ms + EPS) * nw_ref[:, g * gw:(g + 1) * gw])
                y_ref[0, sl, :] = jnp.concatenate(outs, axis=1)

    @pl.when(j < nb)
    def _():
        run(True)

    @pl.when(j >= nb)
    def _():
        run(False)


def ssd_mixer(srcs, cols, conv_w_x, conv_b_x, conv_w_bc, conv_b_bc, dt_bias, a_log, ssd_d, norm_w, init_f, init_b,
              period, tb):
    b, l, _ = srcs[0].shape
    heads = dt_bias.shape[1]
    hpg = heads // SSD_GROUPS
    ssd_w = heads * SSD_HEADDIM
    gn = SSD_GROUPS * SSD_STATE
    nb = l // tb
    oz, ox, obc, odt = cols
    f32 = jnp.float32

    def tok(width, off):
        blk_idx = off // width
        return pl.BlockSpec((1, tb, width),
                            lambda i, j: (i, jnp.where(j < nb, nb - 1 - j, j - nb), blk_idx))

    def tok_fwd(width, off):
        blk_idx = off // width
        return pl.BlockSpec((1, tb, width), lambda i, j: (i, jnp.where(j < nb, 0, j - nb), blk_idx))

    def tok_rev(width, off):
        blk_idx = off // width
        return pl.BlockSpec((1, tb, width), lambda i, j: (i, jnp.maximum(nb - 1 - j, 0), blk_idx))

    def whole(shape):
        return pl.BlockSpec(shape, lambda i, j: (0,) * len(shape))

    st_shape = (b, SSD_GROUPS, SSD_STATE, hpg * SSD_HEADDIM)
    st_spec = pl.BlockSpec((1,) + st_shape[1:], lambda i, j: (i, 0, 0, 0))
    drep = jnp.repeat(ssd_d, SSD_HEADDIM).reshape(1, ssd_w)
    kern = functools.partial(_ssd_kernel, nb=nb, tb=tb, period=period, heads=heads, hpg=hpg)
    return pl.pallas_call(
        kern,
        out_shape=(jax.ShapeDtypeStruct((b, l, ssd_w), f32), jax.ShapeDtypeStruct(st_shape, f32),
                   jax.ShapeDtypeStruct(st_shape, f32)),
        grid=(b, 2 * nb),
        in_specs=[
            tok_fwd(ssd_w, oz), tok_rev(ssd_w, ox), tok_rev(2 * gn, obc), tok(LANE, odt),
            whole((3, ssd_w)), whole((1, ssd_w)), whole((3, 2 * gn)), whole((1, 2 * gn)),
            whole((2, heads)), whole((2, heads)), whole((1, ssd_w)), whole((1, ssd_w)),
            st_spec, st_spec,
        ],
        out_specs=(pl.BlockSpec((1, tb, ssd_w), lambda i, j: (i, jnp.where(j < nb, 0, j - nb), 0)),
                   st_spec, st_spec),
        scratch_shapes=[pltpu.VMEM((l, ssd_w), f32), pltpu.VMEM((l, ssd_w), f32),
                        pltpu.VMEM((l, 2 * gn), jnp.bfloat16)],
        compiler_params=_cparams("parallel", "arbitrary"),
        name="ssd_mixer",
    )(*srcs, conv_w_x, conv_b_x.reshape(1, -1), conv_w_bc, conv_b_bc.reshape(1, -1), dt_bias, a_log, drep,
      norm_w.reshape(1, ssd_w), init_f, init_b)


def _pick(l, pref):
    return pref if l % pref == 0 else l


def kernel(x, c, ctx, c_ctx, w_ada, b_ada, norm1_w, w_in, hy_conv_w, hy_conv_b, filt_w1, filt_b1, filt_freq,
           filt_w2, filt_b2, filt_w3, hy_bias, hy_norm_w, ssd_conv_w, ssd_conv_b, dt_bias, a_log, ssd_d,
           ssd_norm_w, w_out, norm2_w, w_mlp1, w_mlp2, final_norm_w):
    depth = w_in.shape[0]
    bsz, seq, d = x.shape
    hy_w = hy_norm_w.shape[1]
    hy_proj = hy_conv_w.shape[2]
    ssd_w = ssd_norm_w.shape[1]
    ssd_xbc = ssd_conv_w.shape[2]
    heads = ssd_w // SSD_HEADDIM
    hpg = heads // SSD_GROUPS
    ssd_dt = 2 * heads
    o0, o1 = hy_proj, hy_proj + ssd_xbc
    o2 = o1 + ssd_dt
    dt_pad = (-ssd_dt) % LANE
    bf16 = jnp.bfloat16
    ssd_cols = (0, o0, o0 + ssd_w, ssd_w)
    tn = o1 // 3

    n1 = seq // LANE
    n1c = ctx.shape[1] // LANE
    zero_state = jnp.zeros((bsz, SSD_GROUPS, SSD_STATE, hpg * SSD_HEADDIM), jnp.float32)

    rows = 8
    c_all = jnp.zeros((rows, d), jnp.float32).at[:bsz].set(c).at[bsz].set(c_ctx)
    mod_all = ada_mod(c_all, w_ada, b_ada)

    w_a_all, w_b_all = cast_in_proj(w_in, o1, o2, w_in.shape[2], ssd_w + ssd_dt + dt_pad)
    w_out_b, w1_b, w2_b = w_out.astype(bf16), w_mlp1.astype(bf16), w_mlp2.astype(bf16)

    h_ctx = ctx
    tm_l = _pick(seq, 512)
    tm_c = _pick(ctx.shape[1], 256)
    for i in range(depth):
        lp = dict(hy_conv_w=hy_conv_w[i], hy_conv_b=hy_conv_b[i], filt_w1=filt_w1[i], filt_b1=filt_b1[i],
                  filt_freq=filt_freq[i], filt_w2=filt_w2[i], filt_b2=filt_b2[i], filt_w3=filt_w3[i],
                  hy_bias=hy_bias[i], hy_norm_w=hy_norm_w[i], ssd_conv_w=ssd_conv_w[i],
                  ssd_conv_b=ssd_conv_b[i], dt_bias=dt_bias[i], a_log=a_log[i], ssd_d=ssd_d[i],
                  ssd_norm_w=ssd_norm_w[i])
        mod = mod_all[i, :bsz].reshape(bsz, N_MOD, d)
        mod_c = jnp.broadcast_to(mod_all[i, bsz].reshape(1, N_MOD, d), (bsz, N_MOD, d))
        def ssd(pa, pb, init_f, init_b, period):
            cw, cb = lp['ssd_conv_w'], lp['ssd_conv_b']
            return ssd_mixer((pb, pa, pa, pb), ssd_cols, cw[:, :ssd_w], cb[:ssd_w], cw[:, ssd_w:], cb[ssd_w:],
                             lp['dt_bias'], lp['a_log'], lp['ssd_d'], lp['ssd_norm_w'], init_f, init_b, period,
                             _pick(pa.shape[1], 256))

        def filt_taps(length):
            return hyena_filter_taps(length, lp['filt_w1'], lp['filt_b1'], lp['filt_freq'], lp['filt_w2'],
                                     lp['filt_b2'], lp['filt_w3'], hy_w)

        pca, pcb = normmod_matmul(h_ctx, norm1_w[i], mod_c[:, 1], mod_c[:, 0], w_a_all, w_b_all, i, tm_c, tn)
        y_ssd, s_f, s_b = ssd(pca, pcb, zero_state, zero_state, pca.shape[1])
        if i < depth - 1:
            utc = hyena_pre(pca, 0, lp['hy_conv_w'], lp['hy_conv_b'], pca.shape[1], pca.shape[1], 512)
            kre, kim = ctx_filter_spectrum(filt_taps(pca.shape[1]), 128)
            zt = utc
            for o in range(HY_ORDER):
                zt = ctx_hyena_conv(utc, zt, 0, (o + 1) * hy_w, kre, kim, o, lp['hy_bias'][o], 128)
            h_ctx = proj_residual(zt, y_ssd, lp['hy_norm_w'], w_out_b, i, h_ctx, mod_c[:, 2], tm_c, HY_GROUPS)
            h_ctx = mlp_residual(h_ctx, norm2_w[i], mod_c[:, 4], mod_c[:, 3], mod_c[:, 5], w1_b, w2_b, i,
                                 final_norm_w, tm_c, 1024, False)
        pla, plb = normmod_matmul(x, norm1_w[i], mod[:, 1], mod[:, 0], w_a_all, w_b_all, i, _pick(seq, 1024), tn)
        ut = hyena_pre_rows(pla, 0, lp['hy_conv_w'], lp['hy_conv_b'], GRID_W, LANE)
        kre, kim = hyena_filter_spectrum(filt_taps(seq), bsz, 32)
        zt = ut
        for o in range(HY_ORDER):
            zt = hyena_conv(ut, zt, 0, (o + 1) * hy_w, kre, kim, o, lp['hy_bias'][o], bsz, 32, o == HY_ORDER - 1)
        zh_t = zt.reshape(bsz, n1, hy_w, LANE)
        y_ssd, _, _ = ssd(pla, plb, s_f, s_b, GRID_W)
        x = proj_residual(zh_t, y_ssd, lp['hy_norm_w'], w_out_b, i, x, mod[:, 2], tm_l, HY_GROUPS)
        x = mlp_residual(x, norm2_w[i], mod[:, 4], mod[:, 3], mod[:, 5], w1_b, w2_b, i, final_norm_w,
                         _pick(seq, 1024), 512,
                         i == depth - 1)
    return x
```

```python
import functools
import math

import numpy as np
import jax
import jax.numpy as jnp
from jax import lax
from jax.experimental import pallas as pl
from jax.experimental.pallas import tpu as pltpu

EPS = 1e-6
N_MOD = 6
GRID_W = 64
HY_GROUPS = 8
HY_ORDER = 2
HY_POS_EMB = 33
HY_TARGET = 1e-2
HY_FAST = 0.3
HY_SLOW = 1.5
SSD_HEADDIM = 64
SSD_GROUPS = 2
SSD_STATE = 128
SSD_CHUNK = 128

LANE = 128
LOG2E = 1.4426950408889634
VMEM_LIMIT = 56 * 1024 * 1024


def _cparams(*sem):
    return pltpu.CompilerParams(dimension_semantics=sem, vmem_limit_bytes=VMEM_LIMIT)


def _ada_kernel(c_ref, w_ref, b_ref, o_ref):
    c = c_ref[...]
    a = (c * jax.nn.sigmoid(c)).astype(jnp.bfloat16)
    w = w_ref[0].astype(jnp.bfloat16)
    o_ref[0] = jnp.dot(a, w, preferred_element_type=jnp.float32) + b_ref[0]


def ada_mod(c_all, w_ada, b_ada, tn=1024):
    depth, d, n = w_ada.shape
    r = c_all.shape[0]
    return pl.pallas_call(
        _ada_kernel,
        out_shape=jax.ShapeDtypeStruct((depth, r, n), jnp.float32),
        grid=(depth, n // tn),
        in_specs=[
            pl.BlockSpec((r, d), lambda i, j: (0, 0)),
            pl.BlockSpec((1, d, tn), lambda i, j: (i, 0, j)),
            pl.BlockSpec((1, 1, tn), lambda i, j: (i, 0, j)),
        ],
        out_specs=pl.BlockSpec((1, r, tn), lambda i, j: (i, 0, j)),
        compiler_params=_cparams("parallel", "parallel"),
        name="ada_mod",
    )(c_all, w_ada, b_ada.reshape(depth, 1, n))


def _wcast_kernel(w_ref, a_ref, b_ref, *, na, c0, c1):
    bf16 = jnp.bfloat16
    w = w_ref[0]
    a_ref[0] = w[:, :na].astype(bf16)
    nb = b_ref.shape[2]
    pad = nb - (c1 - c0) - (c0 - na)
    b_ref[0] = jnp.concatenate([w[:, c0:c1], w[:, na:c0], jnp.zeros((w.shape[0], pad), w.dtype)], axis=1).astype(bf16)


def cast_in_proj(w_in, na, c0, c1, nb, tr=256):
    depth, d, n = w_in.shape
    return pl.pallas_call(
        functools.partial(_wcast_kernel, na=na, c0=c0, c1=c1),
        out_shape=(jax.ShapeDtypeStruct((depth, d, na), jnp.bfloat16),
                   jax.ShapeDtypeStruct((depth, d, nb), jnp.bfloat16)),
        grid=(depth, d // tr),
        in_specs=[pl.BlockSpec((1, tr, n), lambda i, j: (i, j, 0))],
        out_specs=(pl.BlockSpec((1, tr, na), lambda i, j: (i, j, 0)),
                   pl.BlockSpec((1, tr, nb), lambda i, j: (i, j, 0))),
        compiler_params=_cparams("parallel", "parallel"),
        name="cast_in_proj",
    )(w_in)


NORM_ROWS = 16


def _normmod_store(h_ref, x_ref, nw, sc, sh):
    w = nw * (1.0 + sc)

    def body(i, carry):
        r0 = pl.multiple_of(i * NORM_ROWS, NORM_ROWS)
        xb = x_ref[0, pl.ds(r0, NORM_ROWS), :]
        ms = jnp.mean(xb * xb, axis=-1, keepdims=True)
        h_ref[pl.ds(r0, NORM_ROWS), :] = ((xb * lax.rsqrt(ms + EPS)) * w + sh).astype(h_ref.dtype)
        return carry

    lax.fori_loop(0, h_ref.shape[0] // NORM_ROWS, body, 0, unroll=8)


def _normmod_kernel(x_ref, nw_ref, sc_ref, sh_ref, h_ref):
    _normmod_store(h_ref.at[0], x_ref, nw_ref[...], sc_ref[0], sh_ref[0])


def _proj_kernel(h_ref, w_ref, o_ref):
    o_ref[0] = jnp.dot(h_ref[0], w_ref[...], preferred_element_type=jnp.float32)


def _proj_matmul(h, w, layer, tm, tn):
    b, l, d = h.shape
    n = w.shape[2]
    return pl.pallas_call(
        _proj_kernel,
        out_shape=jax.ShapeDtypeStruct((b, l, n), jnp.float32),
        grid=(n // tn, b, l // tm),
        in_specs=[
            pl.BlockSpec((1, tm, d), lambda k, i, j: (i, j, 0)),
            pl.BlockSpec((None, d, tn), lambda k, i, j: (layer, 0, k)),
        ],
        out_specs=pl.BlockSpec((1, tm, tn), lambda k, i, j: (i, j, k)),
        compiler_params=_cparams("parallel", "parallel", "parallel"),
        name="in_proj",
    )(h, w)


def normmod_matmul(x, nw, sc, sh, wa, wb, layer, tm, tn):
    b, l, d = x.shape
    tn_norm = min(tm, 512)
    h = pl.pallas_call(
        _normmod_kernel,
        out_shape=jax.ShapeDtypeStruct((b, l, d), jnp.bfloat16),
        grid=(b, l // tn_norm),
        in_specs=[
            pl.BlockSpec((1, tn_norm, d), lambda i, j: (i, j, 0)),
            pl.BlockSpec((1, d), lambda i, j: (0, 0)),
            pl.BlockSpec((1, 1, d), lambda i, j: (i, 0, 0)),
            pl.BlockSpec((1, 1, d), lambda i, j: (i, 0, 0)),
        ],
        out_specs=pl.BlockSpec((1, tn_norm, d), lambda i, j: (i, j, 0)),
        compiler_params=_cparams("parallel", "parallel"),
        name="normmod",
    )(x, nw.reshape(1, d), sc.reshape(b, 1, d), sh.reshape(b, 1, d))
    return _proj_matmul(h, wa, layer, tm, tn), _proj_matmul(h, wb, layer, tm, wb.shape[2])


def _proj_res_kernel(zh_ref, ys_ref, nwh_ref, wh_ref, ws_ref, x_ref, g_ref, o_ref, *, groups):
    bf16 = jnp.bfloat16
    slabs, hy_w = zh_ref.shape[1], zh_ref.shape[2]
    gs = hy_w // groups
    rows = []
    for s in range(slabs):
        cols = []
        for gi in range(groups):
            zg = zh_ref[0, s, gi * gs:(gi + 1) * gs, :]
            ms = jnp.mean(zg * zg, axis=0, keepdims=True)
            cols.append((zg * lax.rsqrt(ms + EPS)).T)
        rows.append(jnp.concatenate(cols, axis=1))
    yh = (jnp.concatenate(rows, axis=0) * nwh_ref[...]).astype(bf16)
    acc = jnp.dot(yh, wh_ref[...], preferred_element_type=jnp.float32)
    acc += jnp.dot(ys_ref[0].astype(bf16), ws_ref[...], preferred_element_type=jnp.float32)
    o_ref[0] = x_ref[0] + g_ref[0] * acc


def proj_residual(zh_t, y_ssd, hy_norm_w, w_out, layer, x, g, tm, groups):
    b, n1, hy_w, _ = zh_t.shape
    l, ssd_w = y_ssd.shape[1], y_ssd.shape[2]
    n = w_out.shape[2]
    assert hy_w % ssd_w == 0
    return pl.pallas_call(
        functools.partial(_proj_res_kernel, groups=groups),
        out_shape=jax.ShapeDtypeStruct((b, l, n), jnp.float32),
        grid=(b, l // tm),
        in_specs=[
            pl.BlockSpec((1, tm // LANE, hy_w, LANE), lambda i, j: (i, j, 0, 0)),
            pl.BlockSpec((1, tm, ssd_w), lambda i, j: (i, j, 0)),
            pl.BlockSpec((1, hy_w), lambda i, j: (0, 0)),
            pl.BlockSpec((None, hy_w, n), lambda i, j: (layer, 0, 0)),
            pl.BlockSpec((None, ssd_w, n), lambda i, j: (layer, hy_w // ssd_w, 0)),
            pl.BlockSpec((1, tm, n), lambda i, j: (i, j, 0)),
            pl.BlockSpec((1, 1, n), lambda i, j: (i, 0, 0)),
        ],
        out_specs=pl.BlockSpec((1, tm, n), lambda i, j: (i, j, 0)),
        compiler_params=_cparams("parallel", "parallel"),
        name="proj_residual",
    )(zh_t, y_ssd, hy_norm_w.reshape(1, hy_w), w_out, w_out, x, g.reshape(b, 1, n))


def _hypre_kernel(u_ref, w_ref, b_ref, o_ref, *, period):
    u = _token_conv3(u_ref[0], w_ref[...], b_ref[...], period)
    tb, cw = u.shape
    for s in range(tb // LANE):
        for j in range(cw // LANE):
            o_ref[0, s, j * LANE:(j + 1) * LANE, :] = u[s * LANE:(s + 1) * LANE, j * LANE:(j + 1) * LANE].T


def hyena_pre(p, col0, conv_w, conv_b, period, tb, cw):
    b, l, _ = p.shape
    c = conv_w.shape[1]
    blk0 = col0 // cw
    return pl.pallas_call(
        functools.partial(_hypre_kernel, period=period),
        out_shape=jax.ShapeDtypeStruct((b, l // LANE, c, LANE), jnp.float32),
        grid=(b, l // tb, c // cw),
        in_specs=[
            pl.BlockSpec((1, tb, cw), lambda i, j, k: (i, j, blk0 + k)),
            pl.BlockSpec((3, cw), lambda i, j, k: (0, k)),
            pl.BlockSpec((1, cw), lambda i, j, k: (0, k)),
        ],
        out_specs=pl.BlockSpec((1, tb // LANE, cw, LANE), lambda i, j, k: (i, j, k, 0)),
        compiler_params=_cparams("parallel", "parallel", "parallel"),
        name="hyena_pre",
    )(p, conv_w, conv_b.reshape(1, c))


def _hypre_rows_kernel(u_ref, w_ref, b_ref, o_ref, s_ref, *, period):
    l, cw = u_ref.shape[1], u_ref.shape[2]
    n1 = l // LANE
    assert LANE % period == 0
    w, bias = w_ref[...], b_ref[...]

    for s in range(n1):
        u = _token_conv3(u_ref[0, s * LANE:(s + 1) * LANE, :], w, bias, period)
        for j in range(cw // LANE):
            s_ref[s * cw + j * LANE:s * cw + (j + 1) * LANE, :] = u[:, j * LANE:(j + 1) * LANE].T
    for c in range(cw):
        o_ref[:, c * LANE:(c + 1) * LANE] = s_ref[pl.ds(c, n1, stride=cw), :]


def hyena_pre_rows(p, col0, conv_w, conv_b, period, cw):
    b, l, _ = p.shape
    c = conv_w.shape[1]
    n1 = l // LANE
    blk0 = col0 // cw
    return pl.pallas_call(
        functools.partial(_hypre_rows_kernel, period=period),
        out_shape=jax.ShapeDtypeStruct((b * n1, c * LANE), jnp.float32),
        grid=(b, c // cw),
        in_specs=[
            pl.BlockSpec((1, l, cw), lambda i, k: (i, 0, blk0 + k)),
            pl.BlockSpec((3, cw), lambda i, k: (0, k)),
            pl.BlockSpec((1, cw), lambda i, k: (0, k)),
        ],
        out_specs=pl.BlockSpec((n1, cw * LANE), lambda i, k: (i, k)),
        scratch_shapes=[pltpu.VMEM((n1 * cw, LANE), jnp.float32)],
        compiler_params=_cparams("parallel", "parallel"),
        name="hyena_pre_rows",
    )(p, conv_w, conv_b.reshape(1, c))


def _fft_constants(bsz, n1):
    m1 = 2 * n1
    n = m1 * LANE
    pairs = bsz // 2
    half = pairs * m1
    r = bsz * n1
    t1 = np.arange(n1)[:, None]
    f1 = np.arange(m1)[None, :]
    th = 2.0 * np.pi * t1 * f1 / m1
    w1 = np.zeros((r, 2 * half))
    for pr in range(pairs):
        re = slice(pr * m1, (pr + 1) * m1)
        im = slice(half + pr * m1, half + (pr + 1) * m1)
        ra = slice((2 * pr) * n1, (2 * pr + 1) * n1)
        rb = slice((2 * pr + 1) * n1, (2 * pr + 2) * n1)
        w1[ra, re], w1[ra, im] = np.cos(th), -np.sin(th)
        w1[rb, re], w1[rb, im] = np.sin(th), np.cos(th)
    w4 = w1.T / n
    t2 = np.arange(LANE)[:, None]
    ps = 2.0 * np.pi * t2 * np.arange(LANE)[None, :] / LANE
    w2 = np.block([[np.cos(ps), -np.sin(ps)], [np.sin(ps), np.cos(ps)]])
    ph = 2.0 * np.pi * t2 * f1 / n
    tc = np.tile(np.cos(ph), (1, pairs))
    ts = np.tile(np.sin(ph), (1, pairs))
    orders = half // m1
    tf = 2.0 * np.pi * np.arange(m1)[:, None] * f1 / m1
    w1f = np.zeros((orders * m1, 2 * half))
    for o in range(orders):
        w1f[o * m1:(o + 1) * m1, o * m1:(o + 1) * m1] = np.cos(tf)
        w1f[o * m1:(o + 1) * m1, half + o * m1:half + (o + 1) * m1] = -np.sin(tf)
    mats = [jnp.asarray(m, jnp.bfloat16) for m in (w1, w2, w2.T, w4)]
    tabs = [jnp.asarray(m, jnp.float32) for m in (tc, ts, tc.T, ts.T)]
    return mats, tabs, half, jnp.asarray(w1f, jnp.bfloat16)


def _mxu(a, w):
    return jnp.dot(a.astype(jnp.bfloat16), w, preferred_element_type=jnp.float32)


def _fft_forward(x0, w1, w2, tc, ts, half):
    cw = x0.shape[1] // LANE
    a1 = _mxu(x0.T, w1)
    lhs2 = []
    for c in range(cw):
        ar = a1[c * LANE:(c + 1) * LANE, :half]
        ai = a1[c * LANE:(c + 1) * LANE, half:]
        lhs2.append(jnp.concatenate([(ar * tc + ai * ts).T, (ai * tc - ar * ts).T], axis=1))
    return _mxu(jnp.concatenate(lhs2, axis=0), w2)


def _hyconv_kernel(z_ref, g_ref, kre_ref, kim_ref, brep_ref, w1_ref, w2_ref, w2i_ref, w4_ref, tc_ref, ts_ref,
                   tct_ref, tst_ref, o_ref, *, half):
    x0 = z_ref[...]
    cw = x0.shape[1] // LANE
    tct, tst = tct_ref[...], tst_ref[...]
    s = _fft_forward(x0, w1_ref[...], w2_ref[...], tc_ref[...], ts_ref[...], half)
    sr, si = s[:, :LANE], s[:, LANE:]
    m1 = kre_ref.shape[2]
    pairs = half // m1
    shape4 = (cw, pairs, m1, LANE)
    sr, si = sr.reshape(shape4), si.reshape(shape4)
    kr, ki = kre_ref[:, 0][:, None], kim_ref[:, 0][:, None]
    y2 = jnp.concatenate([(sr * kr - si * ki).reshape(cw * half, LANE),
                          (sr * ki + si * kr).reshape(cw * half, LANE)], axis=1)
    bq = _mxu(y2, w2i_ref[...])
    lhs4 = []
    for c in range(cw):
        br = bq[c * half:(c + 1) * half, :LANE]
        bi = bq[c * half:(c + 1) * half, LANE:]
        lhs4.append(jnp.concatenate([(br * tct - bi * tst).T, (bi * tct + br * tst).T], axis=1))
    lhs4 = jnp.concatenate(lhs4, axis=0)
    yt = _mxu(lhs4, w4_ref[...])
    if len(o_ref.shape) == 2:
        o_ref[...] = g_ref[...] * (yt.T + x0 * brep_ref[...])
    else:
        for c in range(cw):
            sl = slice(c * LANE, (c + 1) * LANE)
            o_ref[:, c, :] = g_ref[:, sl] * (yt[sl].T + x0[:, sl] * brep_ref[:, sl])


def hyena_conv(ut, z_src, z_col, g_col, kre, kim, order, bias, bsz, cw, channel_tiles):
    r = ut.shape[0]
    n1 = r // bsz
    c = bias.shape[0]
    mats, tabs, half, _ = _fft_constants(bsz, n1)
    blk = cw * LANE
    zb, gb = z_col // cw, g_col // cw
    brep = jnp.repeat(bias, LANE).reshape(1, c * LANE)
    kspec = pl.BlockSpec((cw, 1, 2 * n1, LANE), lambda j: (j, order, 0, 0))

    def const(a):
        return pl.BlockSpec(a.shape, lambda j: (0, 0))

    return pl.pallas_call(
        functools.partial(_hyconv_kernel, half=half),
        out_shape=jax.ShapeDtypeStruct((r, c, LANE) if channel_tiles else (r, c * LANE), jnp.float32),
        grid=(c // cw,),
        in_specs=[
            pl.BlockSpec((r, blk), lambda j: (0, zb + j)),
            pl.BlockSpec((r, blk), lambda j: (0, gb + j)),
            kspec, kspec,
            pl.BlockSpec((1, blk), lambda j: (0, j)),
        ] + [const(a) for a in mats + tabs],
        out_specs=(pl.BlockSpec((r, cw, LANE), lambda j: (0, j, 0)) if channel_tiles
                   else pl.BlockSpec((r, blk), lambda j: (0, j))),
        compiler_params=_cparams("parallel"),
        name="hyena_conv",
    )(z_src, ut, kre, kim, brep, *mats, *tabs)


_HI = lax.Precision.HIGHEST


def _filtgen_kernel(a0_ref, wc_ref, ws_ref, fb1_ref, freq_ref, fw2t_ref, fb2_ref, fw3t_ref, dl_ref, o_ref, *,
                    seq, bands):
    f32 = jnp.float32
    s = pl.program_id(0)
    t = s * LANE + lax.broadcasted_iota(jnp.int32, (1, LANE), 1)
    pos = jnp.where(t < seq, t, 2 * seq - t).astype(f32)
    tt = pos / (seq - 1.0)
    ang = (2.0 * math.pi / seq) * pos
    j = lax.broadcasted_iota(jnp.int32, (bands, 1), 0).astype(f32)
    fj = 1e-4 + j * ((bands - 1.0 - 1e-4) / (bands - 1.0))
    fa = fj * ang
    freq = freq_ref[...]
    pre = a0_ref[...] * tt + jnp.dot(wc_ref[...], jnp.cos(fa), precision=_HI, preferred_element_type=f32) \
        - jnp.dot(ws_ref[...], jnp.sin(fa), precision=_HI, preferred_element_type=f32) + fb1_ref[...]
    h1 = jnp.sin(freq * pre)
    h2 = jnp.sin(freq * (jnp.dot(fw2t_ref[...], h1, precision=_HI, preferred_element_type=f32) + fb2_ref[...]))
    orders, _, c, hid3 = fw3t_ref.shape
    h2_hi = h2.astype(jnp.bfloat16)
    h2_lo = (h2 - h2_hi.astype(f32)).astype(jnp.bfloat16)
    h = jnp.dot(fw3t_ref[:, 0].reshape(orders * c, hid3), jnp.concatenate([h2_hi, h2_lo, h2_hi], axis=0),
                preferred_element_type=f32)
    h = h * jnp.exp(-tt * dl_ref[...])
    h = jnp.where(t == seq, 0.0, h)
    o_ref[:, 0] = h.reshape(orders, c, LANE)


def hyena_filter_taps(seq, fw1, fb1, freq, fw2, fb2, fw3, hy_w):
    hid = fw2.shape[0]
    bands = (HY_POS_EMB - 1) // 2
    orders = fw3.shape[1] // (2 * hy_w)
    slabs = 2 * seq // LANE
    col = lambda v: v.reshape(-1, 1)
    fw3t = fw3.T.reshape(orders, 2, hy_w, hid)
    w_hi = fw3t.astype(jnp.bfloat16)
    w_lo = (fw3t - w_hi.astype(jnp.float32)).astype(jnp.bfloat16)
    fw3t = jnp.concatenate([w_hi, w_hi, w_lo], axis=-1)
    deltas = jnp.abs(jnp.linspace(math.log(HY_TARGET) / HY_SLOW, math.log(HY_TARGET) / HY_FAST, hy_w,
                                  dtype=jnp.float32))
    dl = jnp.tile(deltas, orders).reshape(-1, 1)

    def whole(a):
        return pl.BlockSpec(a.shape, lambda s: (0,) * a.ndim)

    args = [col(fw1[0]), fw1[1:1 + bands].T, fw1[1 + bands:].T, col(fb1), col(freq), fw2.T, col(fb2)]
    return pl.pallas_call(
        functools.partial(_filtgen_kernel, seq=seq, bands=bands),
        out_shape=jax.ShapeDtypeStruct((orders, slabs, hy_w, LANE), jnp.float32),
        grid=(slabs,),
        in_specs=[whole(a) for a in args] + [
            pl.BlockSpec((orders, 1, hy_w, 3 * hid), lambda s: (0, s // (slabs // 2), 0, 0)),
            whole(dl),
        ],
        out_specs=pl.BlockSpec((orders, 1, hy_w, LANE), lambda s: (0, s, 0, 0)),
        compiler_params=_cparams("parallel"),
        name="hyena_filter_taps",
    )(*args, fw3t, dl)


def _lane_block_abs_norm(x, orders):
    rows_per = x.shape[0] // orders
    cw = x.shape[1] // LANE
    out = []
    for o in range(orders):
        xo = x[o * rows_per:(o + 1) * rows_per]
        a = jnp.sum(jnp.abs(xo), axis=0, keepdims=True)
        inv = [jnp.broadcast_to(1.0 / (jnp.sum(a[:, c * LANE:(c + 1) * LANE], axis=1, keepdims=True) + EPS),
                                (1, LANE)) for c in range(cw)]
        out.append(xo * jnp.concatenate(inv, axis=1))
    return jnp.concatenate(out, axis=0)


def _filtspec_kernel(k_ref, w1_ref, w2_ref, tc_ref, ts_ref, re_ref, im_ref, *, half, orders):
    m1 = half // orders
    x0 = _lane_block_abs_norm(k_ref[...], orders)
    s = _fft_forward(x0, w1_ref[...], w2_ref[...], tc_ref[...], ts_ref[...], half)
    cw = x0.shape[1] // LANE
    re_ref[...] = s[:, :LANE].reshape(cw, orders, m1, LANE)
    im_ref[...] = s[:, LANE:].reshape(cw, orders, m1, LANE)


def hyena_filter_spectrum(taps, bsz, cw):
    orders, m1, c, _ = taps.shape
    mats, tabs, half, w1f = _fft_constants(bsz, m1 // 2)
    assert half == orders * m1
    consts = [w1f, mats[1], tabs[0], tabs[1]]
    out = jax.ShapeDtypeStruct((c, orders, m1, LANE), jnp.float32)
    ospec = pl.BlockSpec((cw, orders, m1, LANE), lambda j: (j, 0, 0, 0))
    return pl.pallas_call(
        functools.partial(_filtspec_kernel, half=half, orders=orders),
        out_shape=(out, out),
        grid=(c // cw,),
        in_specs=[pl.BlockSpec((orders * m1, cw * LANE), lambda j: (0, j))] +
                 [pl.BlockSpec(a.shape, lambda j: (0, 0)) for a in consts],
        out_specs=(ospec, ospec),
        compiler_params=_cparams("parallel"),
        name="hyena_filter_spectrum",
    )(taps.reshape(orders * m1, c * LANE), *consts)


def _dense_dft_constants(seq):
    n = 2 * seq
    t = np.arange(seq)[:, None]
    f = np.arange(n)[None, :]
    ps = 2.0 * np.pi * t * f / n
    fwd = np.block([[np.cos(ps), -np.sin(ps)], [np.sin(ps), np.cos(ps)]])
    inv = fwd.T / n
    tk = np.arange(n)[:, None]
    pk = 2.0 * np.pi * tk * f / n
    fk = np.concatenate([np.cos(pk), -np.sin(pk)], axis=1)
    return [jnp.asarray(m, jnp.bfloat16) for m in (fwd, inv, fk)]


def _ctxspec_kernel(k_ref, fk_ref, re_ref, im_ref):
    orders, slabs = k_ref.shape[0], k_ref.shape[1]
    n = slabs * LANE
    for o in range(orders):
        x = jnp.concatenate([k_ref[o, s] for s in range(slabs)], axis=1)
        x = x / (jnp.sum(jnp.abs(x), axis=1, keepdims=True) + EPS)
        kf = _mxu(x, fk_ref[...])
        re_ref[o] = kf[:, :n]
        im_ref[o] = kf[:, n:]


def ctx_filter_spectrum(taps, cw):
    orders, slabs, c, _ = taps.shape
    n = slabs * LANE
    fk = _dense_dft_constants(n // 2)[2]
    out = jax.ShapeDtypeStruct((orders, c, n), jnp.float32)
    ospec = pl.BlockSpec((orders, cw, n), lambda j: (0, j, 0))
    return pl.pallas_call(
        _ctxspec_kernel,
        out_shape=(out, out),
        grid=(c // cw,),
        in_specs=[pl.BlockSpec((orders, slabs, cw, LANE), lambda j: (0, 0, j, 0)),
                  pl.BlockSpec(fk.shape, lambda j: (0, 0))],
        out_specs=(ospec, ospec),
        compiler_params=_cparams("parallel"),
        name="ctx_filter_spectrum",
    )(taps, fk)


def _ctxconv_kernel(z_ref, g_ref, kre_ref, kim_ref, b_ref, fwd_ref, inv_ref, o_ref):
    bsz, slabs, cw = z_ref.shape[0], z_ref.shape[1], z_ref.shape[2]
    seq = slabs * LANE
    zs = [jnp.concatenate([z_ref[b, s] for s in range(slabs)], axis=1) for b in range(bsz)]
    lhs = jnp.concatenate([jnp.concatenate([zs[2 * p], zs[2 * p + 1]], axis=1) for p in range(bsz // 2)], axis=0)
    s = _mxu(lhs, fwd_ref[...])
    n = 2 * seq
    sr, si = s[:, :n], s[:, n:]
    kr = jnp.concatenate([kre_ref[0]] * (bsz // 2), axis=0)
    ki = jnp.concatenate([kim_ref[0]] * (bsz // 2), axis=0)
    y = _mxu(jnp.concatenate([sr * kr - si * ki, sr * ki + si * kr], axis=1), inv_ref[...])
    bias = jnp.concatenate([b_ref[...]] * slabs, axis=1)
    for b in range(bsz):
        p, m = b // 2, b % 2
        yb = y[p * cw:(p + 1) * cw, m * seq:(m + 1) * seq]
        gate = jnp.concatenate([g_ref[b, s] for s in range(slabs)], axis=1)
        res = gate * (yb + zs[b] * bias)
        for sl in range(slabs):
            o_ref[b, sl] = res[:, sl * LANE:(sl + 1) * LANE]


def ctx_hyena_conv(ut, z_src, z_col, g_col, kre, kim, order, bias, cw):
    bsz, slabs, _, _ = ut.shape
    c = bias.shape[0]
    n = 2 * slabs * LANE
    fwd, inv, _ = _dense_dft_constants(slabs * LANE)
    zb, gb = z_col // cw, g_col // cw
    bb = jnp.broadcast_to(bias[:, None], (c, LANE))
    kspec = pl.BlockSpec((1, cw, n), lambda j: (order, j, 0))
    return pl.pallas_call(
        _ctxconv_kernel,
        out_shape=jax.ShapeDtypeStruct((bsz, slabs, c, LANE), jnp.float32),
        grid=(c // cw,),
        in_specs=[
            pl.BlockSpec((bsz, slabs, cw, LANE), lambda j: (0, 0, zb + j, 0)),
            pl.BlockSpec((bsz, slabs, cw, LANE), lambda j: (0, 0, gb + j, 0)),
            kspec, kspec,
            pl.BlockSpec((cw, LANE), lambda j: (j, 0)),
            pl.BlockSpec(fwd.shape, lambda j: (0, 0)),
            pl.BlockSpec(inv.shape, lambda j: (0, 0)),
        ],
        out_specs=pl.BlockSpec((bsz, slabs, cw, LANE), lambda j: (0, 0, j, 0)),
        compiler_params=_cparams("parallel"),
        name="ctx_hyena_conv",
    )(z_src, ut, kre, kim, bb, fwd, inv)


def _mlp_kernel(x_ref, nw_ref, sc_ref, sh_ref, g_ref, w1_ref, w2_ref, fw_ref, o_ref, h_ref, *, final_norm):
    f = pl.program_id(2)

    @pl.when(f == 0)
    def _():
        _normmod_store(h_ref, x_ref, nw_ref[...], sc_ref[0], sh_ref[0])
        o_ref[...] = jnp.zeros_like(o_ref)

    a = jnp.dot(h_ref[...], w1_ref[...], preferred_element_type=jnp.float32)
    a = jnp.square(jnp.maximum(a, 0.0)).astype(jnp.bfloat16)
    o_ref[0] += jnp.dot(a, w2_ref[...], preferred_element_type=jnp.float32)

    @pl.when(f == pl.num_programs(2) - 1)
    def _():
        y = x_ref[0] + g_ref[0] * o_ref[0]
        if final_norm:
            ms = jnp.mean(y * y, axis=-1, keepdims=True)
            y = y * lax.rsqrt(ms + EPS) * fw_ref[...]
        o_ref[0] = y


def mlp_residual(x, nw, sc, sh, g, w1, w2, layer, fw, tm, tf, final_norm):
    b, l, d = x.shape
    dff = w1.shape[2]
    vec = pl.BlockSpec((1, 1, d), lambda i, j, k: (i, 0, 0))
    row = pl.BlockSpec((1, d), lambda i, j, k: (0, 0))
    return pl.pallas_call(
        functools.partial(_mlp_kernel, final_norm=final_norm),
        out_shape=jax.ShapeDtypeStruct((b, l, d), jnp.float32),
        grid=(b, l // tm, dff // tf),
        in_specs=[
            pl.BlockSpec((1, tm, d), lambda i, j, k: (i, j, 0)),
            row, vec, vec, vec,
            pl.BlockSpec((None, d, tf), lambda i, j, k: (layer, 0, k)),
            pl.BlockSpec((None, tf, d), lambda i, j, k: (layer, k, 0)),
            row,
        ],
        out_specs=pl.BlockSpec((1, tm, d), lambda i, j, k: (i, j, 0)),
        scratch_shapes=[pltpu.VMEM((tm, d), jnp.bfloat16)],
        compiler_params=_cparams("parallel", "parallel", "arbitrary"),
        name="mlp_residual",
    )(x, nw.reshape(1, d), sc.reshape(b, 1, d), sh.reshape(b, 1, d), g.reshape(b, 1, d), w1, w2,
      fw.reshape(1, d))


def _split3(v):
    f32, bf16 = jnp.float32, jnp.bfloat16
    hi = v.astype(bf16)
    r1 = v - hi.astype(f32)
    mid = r1.astype(bf16)
    lo = (r1 - mid.astype(f32)).astype(bf16)
    return jnp.concatenate([hi, mid, lo], axis=1)


def _lane_repeat(v, rep):
    h = v.shape[1]
    row = lax.broadcasted_iota(jnp.int32, (3 * h, h * rep), 0) % h
    col = lax.broadcasted_iota(jnp.int32, (3 * h, h * rep), 1) // rep
    e = (row == col).astype(jnp.bfloat16)
    return jnp.dot(_split3(v), e, preferred_element_type=jnp.float32)


def _silu(v):
    h = 0.5 * v
    return h + h * jnp.tanh(h)


def _softplus(v):
    return jnp.maximum(v, 0.0) + jnp.log1p(jnp.exp(-jnp.abs(v)))


def _token_conv3(u, w, bias, period):
    t, c = u.shape
    pos = lax.broadcasted_iota(jnp.int32, (t, c), 0) % period
    up = jnp.where(pos == 0, 0.0, pltpu.roll(u, 1, 0))
    dn = jnp.where(pos == period - 1, 0.0, pltpu.roll(u, t - 1, 0))
    return bias + w[0:1] * up + w[1:2] * u + w[2:3] * dn


def _ssd_chunk(xs, bm, cm, dtr, s_ref, a_row, dtb_row, reverse, heads, hpg):
    f32, bf16 = jnp.float32, jnp.bfloat16
    q = xs.shape[0]
    p = SSD_HEADDIM
    n = SSD_STATE
    gw = hpg * p
    dt = _softplus(dtr + dtb_row)
    a = dt * (a_row * LOG2E)
    ri = lax.broadcasted_iota(jnp.int32, (q, q), 0)
    ci = lax.broadcasted_iota(jnp.int32, (q, q), 1)
    keep = (ci >= ri) if reverse else (ci <= ri)
    a3 = jnp.dot(keep.astype(bf16), _split3(a), preferred_element_type=f32)
    cs = a3[:, :heads] + a3[:, heads:2 * heads] + a3[:, 2 * heads:]
    dt_rep = _lane_repeat(dt, p)
    cs_rep = _lane_repeat(cs, p)
    cs_wide = _lane_repeat(cs, q)
    end = 0 if reverse else q - 1
    cs_end = cs_rep[end:end + 1]
    xdt = xs * dt_rep
    xw = (xdt * jnp.exp2(cs_end - cs_rep)).astype(bf16)
    ecs = jnp.exp2(cs_rep)
    chunk_decay = jnp.exp2(cs_end)
    xdt_b = xdt.astype(bf16)
    lane = lax.broadcasted_iota(jnp.int32, (q, 2 * p), 1)
    ys = []
    for g in range(SSD_GROUPS):
        bg = bm[:, g * n:(g + 1) * n].astype(bf16)
        cg = cm[:, g * n:(g + 1) * n].astype(bf16)
        cb = lax.dot_general(cg, bg, (((1,), (1,)), ((), ())), preferred_element_type=f32)
        s_old = s_ref[0, g]
        y_off = jnp.dot(cg, s_old.astype(bf16), preferred_element_type=f32) * ecs[:, g * gw:(g + 1) * gw]
        for pr in range(hpg // 2):
            xpair = xdt_b[:, g * gw + pr * 2 * p:g * gw + (pr + 1) * 2 * p]
            gmats, xhs = [], []
            for k in range(2):
                h = g * hpg + pr * 2 + k
                csr = cs_wide[:, h * q:(h + 1) * q]
                seg = csr - csr.T
                gmats.append((cb * jnp.where(keep, jnp.exp2(seg), 0.0)).astype(bf16))
                xhs.append(jnp.where((lane // p) == k, xpair, jnp.zeros_like(xpair)))
            acc = jnp.dot(jnp.concatenate(gmats, axis=1), jnp.concatenate(xhs, axis=0),
                          preferred_element_type=f32)
            lo = pr * 2 * p
            ys.append(acc + y_off[:, lo:lo + 2 * p])
        upd = jnp.dot(bg.T, xw[:, g * gw:(g + 1) * gw], preferred_element_type=f32)
        s_ref[0, g] = s_old * chunk_decay[:, g * gw:(g + 1) * gw] + upd
    return jnp.concatenate(ys, axis=1)


def _ssd_kernel(z_ref, xs_ref, bc_ref, dt_ref, cwx_ref, cbx_ref, cwb_ref, cbb_ref, dtb_ref, alog_ref, drep_ref,
                nw_ref, initf_ref, initb_ref, y_ref, sf_ref, sb_ref, yb_ref, xc_ref, bcc_ref, *, nb, tb, period,
                heads, hpg):
    j = pl.program_id(1)
    q = SSD_CHUNK
    n = SSD_STATE
    gn = SSD_GROUPS * n
    nchunk = tb // q
    ssd_w = heads * SSD_HEADDIM

    @pl.when(j == 0)
    def _():
        sb_ref[...] = initb_ref[...]

    @pl.when(j == nb)
    def _():
        sf_ref[...] = initf_ref[...]

    a_all = -jnp.exp(alog_ref[...])

    def run(reverse):
        d = 1 if reverse else 0
        blk = (nb - 1 - j) if reverse else (j - nb)
        order = range(nchunk - 1, -1, -1) if reverse else range(nchunk)
        rows = pl.ds(pl.multiple_of(blk * tb, tb), tb)
        if reverse:
            xs_all = _silu(_token_conv3(xs_ref[0], cwx_ref[...], cbx_ref[...], period))
            bc_all = _silu(_token_conv3(bc_ref[0], cwb_ref[...], cbb_ref[...], period)).astype(bcc_ref.dtype)
            xc_ref[rows, :] = xs_all
            bcc_ref[rows, :] = bc_all
        else:
            xs_all = xc_ref[rows, :]
            bc_all = bcc_ref[rows, :]
        for ci in order:
            sl = slice(ci * q, (ci + 1) * q)
            y = _ssd_chunk(xs_all[sl], bc_all[sl, :gn], bc_all[sl, gn:], dt_ref[0, sl, d * heads:(d + 1) * heads],
                           sb_ref if reverse else sf_ref, a_all[d:d + 1], dtb_ref[d:d + 1], reverse, heads, hpg)
            row0 = pl.multiple_of(blk * tb + ci * q, q)
            if reverse:
                yb_ref[pl.ds(row0, q), :] = y
            else:
                y = y + yb_ref[pl.ds(row0, q), :] + drep_ref[...] * xs_all[sl]
                y = y * _silu(z_ref[0, sl, :])
                gw = ssd_w // SSD_GROUPS
                outs = []
                for g in range(SSD_GROUPS):
                    yg = y[:, g * gw:(g + 1) * gw]
                    ms = jnp.mean(yg * yg, axis=-1, keepdims=True)
                    outs.append(yg * lax.rsqrt(ms + EPS) * nw_ref[:, g * gw:(g + 1) * gw])
                y_ref[0, sl, :] = jnp.concatenate(outs, axis=1)

    @pl.when(j < nb)
    def _():
        run(True)

    @pl.when(j >= nb)
    def _():
        run(False)


def ssd_mixer(srcs, cols, conv_w_x, conv_b_x, conv_w_bc, conv_b_bc, dt_bias, a_log, ssd_d, norm_w, init_f, init_b,
              period, tb):
    b, l, _ = srcs[0].shape
    heads = dt_bias.shape[1]
    hpg = heads // SSD_GROUPS
    ssd_w = heads * SSD_HEADDIM
    gn = SSD_GROUPS * SSD_STATE
    nb = l // tb
    oz, ox, obc, odt = cols
    f32 = jnp.float32

    def tok(width, off):
        blk_idx = off // width
        return pl.BlockSpec((1, tb, width),
                            lambda i, j: (i, jnp.where(j < nb, nb - 1 - j, j - nb), blk_idx))

    def tok_fwd(width, off):
        blk_idx = off // width
        return pl.BlockSpec((1, tb, width), lambda i, j: (i, jnp.where(j < nb, 0, j - nb), blk_idx))

    def tok_rev(width, off):
        blk_idx = off // width
        return pl.BlockSpec((1, tb, width), lambda i, j: (i, jnp.maximum(nb - 1 - j, 0), blk_idx))

    def whole(shape):
        return pl.BlockSpec(shape, lambda i, j: (0,) * len(shape))

    st_shape = (b, SSD_GROUPS, SSD_STATE, hpg * SSD_HEADDIM)
    st_spec = pl.BlockSpec((1,) + st_shape[1:], lambda i, j: (i, 0, 0, 0))
    drep = jnp.repeat(ssd_d, SSD_HEADDIM).reshape(1, ssd_w)
    kern = functools.partial(_ssd_kernel, nb=nb, tb=tb, period=period, heads=heads, hpg=hpg)
    return pl.pallas_call(
        kern,
        out_shape=(jax.ShapeDtypeStruct((b, l, ssd_w), f32), jax.ShapeDtypeStruct(st_shape, f32),
                   jax.ShapeDtypeStruct(st_shape, f32)),
        grid=(b, 2 * nb),
        in_specs=[
            tok_fwd(ssd_w, oz), tok_rev(ssd_w, ox), tok_rev(2 * gn, obc), tok(LANE, odt),
            whole((3, ssd_w)), whole((1, ssd_w)), whole((3, 2 * gn)), whole((1, 2 * gn)),
            whole((2, heads)), whole((2, heads)), whole((1, ssd_w)), whole((1, ssd_w)),
            st_spec, st_spec,
        ],
        out_specs=(pl.BlockSpec((1, tb, ssd_w), lambda i, j: (i, jnp.where(j < nb, 0, j - nb), 0)),
                   st_spec, st_spec),
        scratch_shapes=[pltpu.VMEM((l, ssd_w), f32), pltpu.VMEM((l, ssd_w), f32),
                        pltpu.VMEM((l, 2 * gn), jnp.bfloat16)],
        compiler_params=_cparams("parallel", "arbitrary"),
        name="ssd_mixer",
    )(*srcs, conv_w_x, conv_b_x.reshape(1, -1), conv_w_bc, conv_b_bc.reshape(1, -1), dt_bias, a_log, drep,
      norm_w.reshape(1, ssd_w), init_f, init_b)


def _pick(l, pref):
    return pref if l % pref == 0 else l


def kernel(x, c, ctx, c_ctx, w_ada, b_ada, norm1_w, w_in, hy_conv_w, hy_conv_b, filt_w1, filt_b1, filt_freq,
           filt_w2, filt_b2, filt_w3, hy_bias, hy_norm_w, ssd_conv_w, ssd_conv_b, dt_bias, a_log, ssd_d,
           ssd_norm_w, w_out, norm2_w, w_mlp1, w_mlp2, final_norm_w):
    depth = w_in.shape[0]
    bsz, seq, d = x.shape
    hy_w = hy_norm_w.shape[1]
    hy_proj = hy_conv_w.shape[2]
    ssd_w = ssd_norm_w.shape[1]
    ssd_xbc = ssd_conv_w.shape[2]
    heads = ssd_w // SSD_HEADDIM
    hpg = heads // SSD_GROUPS
    ssd_dt = 2 * heads
    o0, o1 = hy_proj, hy_proj + ssd_xbc
    o2 = o1 + ssd_dt
    dt_pad = (-ssd_dt) % LANE
    bf16 = jnp.bfloat16
    ssd_cols = (0, o0, o0 + ssd_w, ssd_w)
    tn = o1 // 3

    n1 = seq // LANE
    n1c = ctx.shape[1] // LANE
    zero_state = jnp.zeros((bsz, SSD_GROUPS, SSD_STATE, hpg * SSD_HEADDIM), jnp.float32)

    rows = 8
    c_all = jnp.zeros((rows, d), jnp.float32).at[:bsz].set(c).at[bsz].set(c_ctx)
    mod_all = ada_mod(c_all, w_ada, b_ada)

    w_a_all, w_b_all = cast_in_proj(w_in, o1, o2, w_in.shape[2], ssd_w + ssd_dt + dt_pad)
    w_out_b, w1_b, w2_b = w_out.astype(bf16), w_mlp1.astype(bf16), w_mlp2.astype(bf16)

    h_ctx = ctx
    tm_l = _pick(seq, 512)
    tm_c = _pick(ctx.shape[1], 256)
    for i in range(depth):
        lp = dict(hy_conv_w=hy_conv_w[i], hy_conv_b=hy_conv_b[i], filt_w1=filt_w1[i], filt_b1=filt_b1[i],
                  filt_freq=filt_freq[i], filt_w2=filt_w2[i], filt_b2=filt_b2[i], filt_w3=filt_w3[i],
                  hy_bias=hy_bias[i], hy_norm_w=hy_norm_w[i], ssd_conv_w=ssd_conv_w[i],
                  ssd_conv_b=ssd_conv_b[i], dt_bias=dt_bias[i], a_log=a_log[i], ssd_d=ssd_d[i],
                  ssd_norm_w=ssd_norm_w[i])
        mod = mod_all[i, :bsz].reshape(bsz, N_MOD, d)
        mod_c = jnp.broadcast_to(mod_all[i, bsz].reshape(1, N_MOD, d), (bsz, N_MOD, d))
        def ssd(pa, pb, init_f, init_b, period):
            cw, cb = lp['ssd_conv_w'], lp['ssd_conv_b']
            return ssd_mixer((pb, pa, pa, pb), ssd_cols, cw[:, :ssd_w], cb[:ssd_w], cw[:, ssd_w:], cb[ssd_w:],
                             lp['dt_bias'], lp['a_log'], lp['ssd_d'], lp['ssd_norm_w'], init_f, init_b, period,
                             _pick(pa.shape[1], 256))

        def filt_taps(length):
            return hyena_filter_taps(length, lp['filt_w1'], lp['filt_b1'], lp['filt_freq'], lp['filt_w2'],
                                     lp['filt_b2'], lp['filt_w3'], hy_w)

        pca, pcb = normmod_matmul(h_ctx, norm1_w[i], mod_c[:, 1], mod_c[:, 0], w_a_all, w_b_all, i, tm_c, tn)
        y_ssd, s_f, s_b = ssd(pca, pcb, zero_state, zero_state, pca.shape[1])
        if i < depth - 1:
            utc = hyena_pre(pca, 0, lp['hy_conv_w'], lp['hy_conv_b'], pca.shape[1], pca.shape[1], 512)
            kre, kim = ctx_filter_spectrum(filt_taps(pca.shape[1]), 128)
            zt = utc
            for o in range(HY_ORDER):
                zt = ctx_hyena_conv(utc, zt, 0, (o + 1) * hy_w, kre, kim, o, lp['hy_bias'][o], 128)
            h_ctx = proj_residual(zt, y_ssd, lp['hy_norm_w'], w_out_b, i, h_ctx, mod_c[:, 2], tm_c, HY_GROUPS)
            h_ctx = mlp_residual(h_ctx, norm2_w[i], mod_c[:, 4], mod_c[:, 3], mod_c[:, 5], w1_b, w2_b, i,
                                 final_norm_w, tm_c, 1024, False)
        pla, plb = normmod_matmul(x, norm1_w[i], mod[:, 1], mod[:, 0], w_a_all, w_b_all, i, _pick(seq, 1024), tn)
        ut = hyena_pre_rows(pla, 0, lp['hy_conv_w'], lp['hy_conv_b'], GRID_W, LANE)
        kre, kim = hyena_filter_spectrum(filt_taps(seq), bsz, 32)
        zt = ut
        for o in range(HY_ORDER):
            zt = hyena_conv(ut, zt, 0, (o + 1) * hy_w, kre, kim, o, lp['hy_bias'][o], bsz, 32, o == HY_ORDER - 1)
        zh_t = zt.reshape(bsz, n1, hy_w, LANE)
        y_ssd, _, _ = ssd(pla, plb, s_f, s_b, GRID_W)
        x = proj_residual(zh_t, y_ssd, lp['hy_norm_w'], w_out_b, i, x, mod[:, 2], tm_l, HY_GROUPS)
        x = mlp_residual(x, norm2_w[i], mod[:, 4], mod[:, 3], mod[:, 5], w1_b, w2_b, i, final_norm_w, tm_l, 1024,
                         i == depth - 1)
    return x
```
